```python
import math
import jax, jax.numpy as jnp
from jax import lax
import numpy as np

D_MODEL = 1024
BATCH = 8
SEQ = 16384
DEPTH = 4

D_FF = 2816
HEAD_DIM = 64
N_Q_HEADS = 16
N_KV_HEADS = 2
Q_PER_KV = N_Q_HEADS // N_KV_HEADS
ATTN_WIDTH = N_Q_HEADS * HEAD_DIM
KV_WIDTH = N_KV_HEADS * HEAD_DIM
WINDOW = 128
ATTN_BLOCK = 128
SGU_CHUNK = 128
SGU_GROUPS = 8
SGU_GROUP_CH = 128
SGU_WIDTH = SGU_GROUPS * SGU_GROUP_CH
IN_SPLIT_SIZES = (ATTN_WIDTH, KV_WIDTH, KV_WIDTH, SGU_WIDTH, SGU_WIDTH, D_MODEL, D_MODEL)
IN_WIDTH = sum(IN_SPLIT_SIZES)
IN_SPLIT_POINTS = tuple(int(p) for p in np.cumsum(IN_SPLIT_SIZES)[:-1])

RMS_EPS = 1e-6
LN_EPS = 1e-5
MASK_VALUE = -1e30

kernel_name = "hybrid_swa_sink_gmlp_macaron_sandwich"


def rms_norm(x, g):
    x32 = x.astype(jnp.float32)
    y = x32 * lax.rsqrt(jnp.mean(x32 * x32, axis=-1, keepdims=True) + RMS_EPS)
    return (y * g.astype(jnp.float32)).astype(x.dtype)


def layer_norm(x, g, b):
    x32 = x.astype(jnp.float32)
    mu = jnp.mean(x32, axis=-1, keepdims=True)
    xc = x32 - mu
    y = xc * lax.rsqrt(jnp.mean(xc * xc, axis=-1, keepdims=True) + LN_EPS)
    return (y * g.astype(jnp.float32) + b.astype(jnp.float32)).astype(x.dtype)


def swiglu(x, w1, w2):
    g, u = jnp.split(x @ w1, 2, axis=-1)
    return (jax.nn.silu(g) * u) @ w2


def sliding_window_attention(q, k, v, sinks):
    B, S, _ = q.shape
    nb = S // ATTN_BLOCK
    qb = q.reshape(B, nb, ATTN_BLOCK, N_KV_HEADS, Q_PER_KV, HEAD_DIM).astype(jnp.float32)
    kb = k.reshape(B, nb, ATTN_BLOCK, N_KV_HEADS, HEAD_DIM)
    vb = v.reshape(B, nb, ATTN_BLOCK, N_KV_HEADS, HEAD_DIM)
    kpad = jnp.zeros_like(kb[:, :1])
    vpad = jnp.zeros_like(vb[:, :1])
    k2 = jnp.concatenate([jnp.concatenate([kpad, kb[:, :-1]], axis=1), kb], axis=2).astype(jnp.float32)
    v2 = jnp.concatenate([jnp.concatenate([vpad, vb[:, :-1]], axis=1), vb], axis=2).astype(jnp.float32)
    scale = 1.0 / math.sqrt(HEAD_DIM)
    scores = jnp.einsum('bnqgrd,bnkgd->bngrqk', qb, k2) * scale
    qi = jnp.arange(ATTN_BLOCK)[:, None]
    kj = jnp.arange(2 * ATTN_BLOCK)[None, :]
    rel = qi + ATTN_BLOCK - kj
    band = (rel >= 0) & (rel < WINDOW)
    blk = jnp.arange(nb)[:, None, None]
    valid = band[None] & ((blk > 0) | (kj[None] >= ATTN_BLOCK))
    scores = jnp.where(valid[None, :, None, None], scores, MASK_VALUE)
    sink = sinks.astype(jnp.float32).reshape(1, 1, N_KV_HEADS, Q_PER_KV, 1)
    m = jnp.maximum(scores.max(axis=-1), sink)
    p = jnp.exp(scores - m[..., None])
    probs = p / (p.sum(axis=-1) + jnp.exp(sink - m))[..., None]
    out = jnp.einsum('bngrqk,bnkgd->bnqgrd', probs, v2)
    return out.reshape(B, S, ATTN_WIDTH).astype(q.dtype)


def spatial_gating(u, v, ln_g, ln_b, w_s, b_s):
    B, S, _ = u.shape
    nc = S // SGU_CHUNK
    vn = layer_norm(v, ln_g, ln_b).reshape(B, nc, SGU_CHUNK, SGU_GROUPS, SGU_GROUP_CH)
    causal = jnp.tril(jnp.ones((SGU_CHUNK, SGU_CHUNK), dtype=bool))
    w = jnp.where(causal[None], w_s, jnp.zeros_like(w_s))
    s = jnp.einsum('gts,bnsgc->bntgc', w, vn) + b_s.T[None, None, :, :, None]
    return u * s.reshape(B, S, SGU_WIDTH)


def _fwd_setup_inputs(seed: int = 0) -> dict:
    key = jax.random.key(seed)
    ks = iter(jax.random.split(key, 32))
    f32 = jnp.float32

    def nrm(shape, fan_in, scale=1.0):
        return jax.random.normal(next(ks), shape, f32) * (scale * fan_in ** -0.5)

    def gain(shape):
        return 1.0 + 0.05 * jax.random.normal(next(ks), shape, f32)

    L, D = DEPTH, D_MODEL
    return {
        "x": jax.random.normal(next(ks), (BATCH, SEQ, D), f32),
        "ffn1_pre_g": gain((L, D)),
        "ffn1_w1": nrm((L, D, 2 * D_FF), D),
        "ffn1_w2": nrm((L, D_FF, D), D_FF),
        "ffn1_post_g": gain((L, D)),
        "mix_pre_g": gain((L, D)),
        "w_in": nrm((L, D, IN_WIDTH), D),
        "attn_sinks": 0.5 * jax.random.normal(next(ks), (L, N_Q_HEADS), f32),
        "sgu_ln_g": gain((L, SGU_WIDTH)),
        "sgu_ln_b": 0.02 * jax.random.normal(next(ks), (L, SGU_WIDTH), f32),
        "sgu_w": nrm((L, SGU_GROUPS, SGU_CHUNK, SGU_CHUNK), SGU_CHUNK, 0.5),
        "sgu_b": gain((L, SGU_GROUPS, SGU_CHUNK)),
        "w_attn_branch": nrm((L, ATTN_WIDTH, D), ATTN_WIDTH),
        "w_sgu_branch": nrm((L, SGU_WIDTH, D), SGU_WIDTH),
        "w_out": nrm((L, D, D), D),
        "mix_post_g": gain((L, D)),
        "ffn2_pre_g": gain((L, D)),
        "ffn2_w1": nrm((L, D, 2 * D_FF), D),
        "ffn2_w2": nrm((L, D_FF, D), D_FF),
        "ffn2_post_g": gain((L, D)),
    }


def _fwd_reference(x, ffn1_pre_g, ffn1_w1, ffn1_w2, ffn1_post_g, mix_pre_g, w_in, attn_sinks,
              sgu_ln_g, sgu_ln_b, sgu_w, sgu_b, w_attn_branch, w_sgu_branch, w_out, mix_post_g,
              ffn2_pre_g, ffn2_w1, ffn2_w2, ffn2_post_g):
    for l in range(DEPTH):
        h = rms_norm(x, ffn1_pre_g[l])
        x = x + 0.5 * rms_norm(swiglu(h, ffn1_w1[l], ffn1_w2[l]), ffn1_post_g[l])

        h = rms_norm(x, mix_pre_g[l])
        z = h @ w_in[l]
        q, k, v, u_s, v_s, g_a, g_b = jnp.split(z, IN_SPLIT_POINTS, axis=-1)
        y_attn = sliding_window_attention(q, k, v, attn_sinks[l])
        y_sgu = spatial_gating(jax.nn.gelu(u_s, approximate=False), jax.nn.gelu(v_s, approximate=False),
                               sgu_ln_g[l], sgu_ln_b[l], sgu_w[l], sgu_b[l])
        merged = (jax.nn.sigmoid(g_a) * (y_attn @ w_attn_branch[l])
                  + jax.nn.sigmoid(g_b) * (y_sgu @ w_sgu_branch[l]))
        x = x + rms_norm(merged @ w_out[l], mix_post_g[l])

        h = rms_norm(x, ffn2_pre_g[l])
        x = x + 0.5 * rms_norm(swiglu(h, ffn2_w1[l], ffn2_w2[l]), ffn2_post_g[l])
    return x


import jax as _jax
import jax.numpy as _jnp

TWIN_FORMAT = 'train_step'
FWD_PARAMS = ['x', 'ffn1_pre_g', 'ffn1_w1', 'ffn1_w2', 'ffn1_post_g', 'mix_pre_g', 'w_in', 'attn_sinks', 'sgu_ln_g', 'sgu_ln_b', 'sgu_w', 'sgu_b', 'w_attn_branch', 'w_sgu_branch', 'w_out', 'mix_post_g', 'ffn2_pre_g', 'ffn2_w1', 'ffn2_w2', 'ffn2_post_g']
TWIN_WEIGHTS = ['ffn1_pre_g', 'ffn1_w1', 'ffn1_w2', 'ffn1_post_g', 'mix_pre_g', 'w_in', 'attn_sinks', 'sgu_ln_g', 'sgu_ln_b', 'sgu_w', 'sgu_b', 'w_attn_branch', 'w_sgu_branch', 'w_out', 'mix_post_g', 'ffn2_pre_g', 'ffn2_w1', 'ffn2_w2', 'ffn2_post_g']
TWIN_DIFF_INPUT = 'x'
TWIN_INPUTS = ['x', 'ffn1_pre_g', 'ffn1_w1', 'ffn1_w2', 'ffn1_post_g', 'mix_pre_g', 'w_in', 'attn_sinks', 'sgu_ln_g', 'sgu_ln_b', 'sgu_w', 'sgu_b', 'w_attn_branch', 'w_sgu_branch', 'w_out', 'mix_post_g', 'ffn2_pre_g', 'ffn2_w1', 'ffn2_w2', 'ffn2_post_g', 'loss_target', 'm_ffn1_pre_g', 'm_ffn1_w1', 'm_ffn1_w2', 'm_ffn1_post_g', 'm_mix_pre_g', 'm_w_in', 'm_attn_sinks', 'm_sgu_ln_g', 'm_sgu_ln_b', 'm_sgu_w', 'm_sgu_b', 'm_w_attn_branch', 'm_w_sgu_branch', 'm_w_out', 'm_mix_post_g', 'm_ffn2_pre_g', 'm_ffn2_w1', 'm_ffn2_w2', 'm_ffn2_post_g', 'v_ffn1_pre_g', 'v_ffn1_w1', 'v_ffn1_w2', 'v_ffn1_post_g', 'v_mix_pre_g', 'v_w_in', 'v_attn_sinks', 'v_sgu_ln_g', 'v_sgu_ln_b', 'v_sgu_w', 'v_sgu_b', 'v_w_attn_branch', 'v_w_sgu_branch', 'v_w_out', 'v_mix_post_g', 'v_ffn2_pre_g', 'v_ffn2_w1', 'v_ffn2_w2', 'v_ffn2_post_g']
TWIN_OUTPUTS = ['loss', 'grad_x', 'grad_ffn1_pre_g', 'grad_ffn1_w1', 'grad_ffn1_w2', 'grad_ffn1_post_g', 'grad_mix_pre_g', 'grad_w_in', 'grad_attn_sinks', 'grad_sgu_ln_g', 'grad_sgu_ln_b', 'grad_sgu_w', 'grad_sgu_b', 'grad_w_attn_branch', 'grad_w_sgu_branch', 'grad_w_out', 'grad_mix_post_g', 'grad_ffn2_pre_g', 'grad_ffn2_w1', 'grad_ffn2_w2', 'grad_ffn2_post_g', 'delta_ffn1_pre_g', 'delta_ffn1_w1', 'delta_ffn1_w2', 'delta_ffn1_post_g', 'delta_mix_pre_g', 'delta_w_in', 'delta_attn_sinks', 'delta_sgu_ln_g', 'delta_sgu_ln_b', 'delta_sgu_w', 'delta_sgu_b', 'delta_w_attn_branch', 'delta_w_sgu_branch', 'delta_w_out', 'delta_mix_post_g', 'delta_ffn2_pre_g', 'delta_ffn2_w1', 'delta_ffn2_w2', 'delta_ffn2_post_g', 'new_m_ffn1_pre_g', 'new_m_ffn1_w1', 'new_m_ffn1_w2', 'new_m_ffn1_post_g', 'new_m_mix_pre_g', 'new_m_w_in', 'new_m_attn_sinks', 'new_m_sgu_ln_g', 'new_m_sgu_ln_b', 'new_m_sgu_w', 'new_m_sgu_b', 'new_m_w_attn_branch', 'new_m_w_sgu_branch', 'new_m_w_out', 'new_m_mix_post_g', 'new_m_ffn2_pre_g', 'new_m_ffn2_w1', 'new_m_ffn2_w2', 'new_m_ffn2_post_g', 'new_v_ffn1_pre_g', 'new_v_ffn1_w1', 'new_v_ffn1_w2', 'new_v_ffn1_post_g', 'new_v_mix_pre_g', 'new_v_w_in', 'new_v_attn_sinks', 'new_v_sgu_ln_g', 'new_v_sgu_ln_b', 'new_v_sgu_w', 'new_v_sgu_b', 'new_v_w_attn_branch', 'new_v_w_sgu_branch', 'new_v_w_out', 'new_v_mix_post_g', 'new_v_ffn2_pre_g', 'new_v_ffn2_w1', 'new_v_ffn2_w2', 'new_v_ffn2_post_g']
TWIN_LEAF_KINDS = {'loss': 'loss', 'grad_x': 'grad_x', 'grad_ffn1_pre_g': 'grad_w', 'grad_ffn1_w1': 'grad_w', 'grad_ffn1_w2': 'grad_w', 'grad_ffn1_post_g': 'grad_w', 'grad_mix_pre_g': 'grad_w', 'grad_w_in': 'grad_w', 'grad_attn_sinks': 'grad_w', 'grad_sgu_ln_g': 'grad_w', 'grad_sgu_ln_b': 'grad_w', 'grad_sgu_w': 'grad_w', 'grad_sgu_b': 'grad_w', 'grad_w_attn_branch': 'grad_w', 'grad_w_sgu_branch': 'grad_w', 'grad_w_out': 'grad_w', 'grad_mix_post_g': 'grad_w', 'grad_ffn2_pre_g': 'grad_w', 'grad_ffn2_w1': 'grad_w', 'grad_ffn2_w2': 'grad_w', 'grad_ffn2_post_g': 'grad_w', 'delta_ffn1_pre_g': 'delta_w', 'delta_ffn1_w1': 'delta_w', 'delta_ffn1_w2': 'delta_w', 'delta_ffn1_post_g': 'delta_w', 'delta_mix_pre_g': 'delta_w', 'delta_w_in': 'delta_w', 'delta_attn_sinks': 'delta_w', 'delta_sgu_ln_g': 'delta_w', 'delta_sgu_ln_b': 'delta_w', 'delta_sgu_w': 'delta_w', 'delta_sgu_b': 'delta_w', 'delta_w_attn_branch': 'delta_w', 'delta_w_sgu_branch': 'delta_w', 'delta_w_out': 'delta_w', 'delta_mix_post_g': 'delta_w', 'delta_ffn2_pre_g': 'delta_w', 'delta_ffn2_w1': 'delta_w', 'delta_ffn2_w2': 'delta_w', 'delta_ffn2_post_g': 'delta_w', 'new_m_ffn1_pre_g': 'new_m', 'new_m_ffn1_w1': 'new_m', 'new_m_ffn1_w2': 'new_m', 'new_m_ffn1_post_g': 'new_m', 'new_m_mix_pre_g': 'new_m', 'new_m_w_in': 'new_m', 'new_m_attn_sinks': 'new_m', 'new_m_sgu_ln_g': 'new_m', 'new_m_sgu_ln_b': 'new_m', 'new_m_sgu_w': 'new_m', 'new_m_sgu_b': 'new_m', 'new_m_w_attn_branch': 'new_m', 'new_m_w_sgu_branch': 'new_m', 'new_m_w_out': 'new_m', 'new_m_mix_post_g': 'new_m', 'new_m_ffn2_pre_g': 'new_m', 'new_m_ffn2_w1': 'new_m', 'new_m_ffn2_w2': 'new_m', 'new_m_ffn2_post_g': 'new_m', 'new_v_ffn1_pre_g': 'new_v', 'new_v_ffn1_w1': 'new_v', 'new_v_ffn1_w2': 'new_v', 'new_v_ffn1_post_g': 'new_v', 'new_v_mix_pre_g': 'new_v', 'new_v_w_in': 'new_v', 'new_v_attn_sinks': 'new_v', 'new_v_sgu_ln_g': 'new_v', 'new_v_sgu_ln_b': 'new_v', 'new_v_sgu_w': 'new_v', 'new_v_sgu_b': 'new_v', 'new_v_w_attn_branch': 'new_v', 'new_v_w_sgu_branch': 'new_v', 'new_v_w_out': 'new_v', 'new_v_mix_post_g': 'new_v', 'new_v_ffn2_pre_g': 'new_v', 'new_v_ffn2_w1': 'new_v', 'new_v_ffn2_w2': 'new_v', 'new_v_ffn2_post_g': 'new_v'}


def _forward(args):
    return _fwd_reference(*[args[k] for k in FWD_PARAMS])


def _output_shape():
    def fwd():
        inp = _fwd_setup_inputs(0)
        return _fwd_reference(*[inp[k] for k in FWD_PARAMS])
    out = _jax.eval_shape(fwd)
    return out.shape, out.dtype

N_MICROBATCH = 1
ADAM_LR = 0.001
ADAM_B1 = 0.9
ADAM_B2 = 0.999
ADAM_EPS = 1e-08
ADAM_WD = 0.01
ADAM_STEP = 10
PER_EXAMPLE_BATCH_AXIS = {'x': 0, 'loss_target': 0}
SHARED_INPUTS = []
_WEIGHT_DTYPES = {'ffn1_pre_g': _jnp.float32, 'ffn1_w1': _jnp.float32, 'ffn1_w2': _jnp.float32, 'ffn1_post_g': _jnp.float32, 'mix_pre_g': _jnp.float32, 'w_in': _jnp.float32, 'attn_sinks': _jnp.float32, 'sgu_ln_g': _jnp.float32, 'sgu_ln_b': _jnp.float32, 'sgu_w': _jnp.float32, 'sgu_b': _jnp.float32, 'w_attn_branch': _jnp.float32, 'w_sgu_branch': _jnp.float32, 'w_out': _jnp.float32, 'mix_post_g': _jnp.float32, 'ffn2_pre_g': _jnp.float32, 'ffn2_w1': _jnp.float32, 'ffn2_w2': _jnp.float32, 'ffn2_post_g': _jnp.float32}
MOMENT_SCALE = {'ffn1_pre_g': 8.493272e+00, 'ffn1_w1': 3.473554e+00, 'ffn1_w2': 7.508946e+00, 'ffn1_post_g': 3.226286e+01, 'mix_pre_g': 4.109513e+01, 'w_in': 1.811329e+01, 'attn_sinks': 8.272731e-01, 'sgu_ln_g': 5.560813e-01, 'sgu_ln_b': 9.423987e-01, 'sgu_w': 1.064810e+00, 'sgu_b': 2.283076e+00, 'w_attn_branch': 3.473101e+01, 'w_sgu_branch': 5.267111e+01, 'w_out': 6.475811e+01, 'mix_post_g': 1.474387e+02, 'ffn2_pre_g': 1.240845e+01, 'ffn2_w1': 5.239472e+00, 'ffn2_w2': 1.056083e+01, 'ffn2_post_g': 3.417586e+01}


def _to_microbatches(a, axis):
    t = _jnp.moveaxis(a, axis, 0)
    t = t.reshape((N_MICROBATCH, t.shape[0] // N_MICROBATCH) + t.shape[1:])
    return _jnp.moveaxis(t, 1, axis + 1)


def setup_inputs(seed: int = 0) -> dict:
    inp = _fwd_setup_inputs(seed)
    key = _jax.random.fold_in(_jax.random.key(seed), 7919)
    shape, _ = _output_shape()
    out = dict(inp)
    out["loss_target"] = _jax.random.normal(_jax.random.fold_in(key, 0), shape, _jnp.float32)
    for i, name in enumerate(TWIN_WEIGHTS):
        w = inp[name].astype(_jnp.float32)
        if MOMENT_SCALE is None:
            s = _jnp.sqrt(_jnp.mean(_jnp.square(w)) + 1e-30)
        else:
            s = MOMENT_SCALE[name]
        km, kv = _jax.random.split(_jax.random.fold_in(key, i + 1))
        out[name] = w
        out["m_" + name] = s * _jax.random.normal(km, w.shape, _jnp.float32)
        out["v_" + name] = (s * s) * _jax.random.uniform(kv, w.shape, _jnp.float32, 0.5, 1.5)
    if N_MICROBATCH > 1:
        for name, axis in PER_EXAMPLE_BATCH_AXIS.items():
            out[name] = _to_microbatches(out[name], axis)
    return {'x': out['x'], 'ffn1_pre_g': out['ffn1_pre_g'], 'ffn1_w1': out['ffn1_w1'], 'ffn1_w2': out['ffn1_w2'], 'ffn1_post_g': out['ffn1_post_g'], 'mix_pre_g': out['mix_pre_g'], 'w_in': out['w_in'], 'attn_sinks': out['attn_sinks'], 'sgu_ln_g': out['sgu_ln_g'], 'sgu_ln_b': out['sgu_ln_b'], 'sgu_w': out['sgu_w'], 'sgu_b': out['sgu_b'], 'w_attn_branch': out['w_attn_branch'], 'w_sgu_branch': out['w_sgu_branch'], 'w_out': out['w_out'], 'mix_post_g': out['mix_post_g'], 'ffn2_pre_g': out['ffn2_pre_g'], 'ffn2_w1': out['ffn2_w1'], 'ffn2_w2': out['ffn2_w2'], 'ffn2_post_g': out['ffn2_post_g'], 'loss_target': out['loss_target'], 'm_ffn1_pre_g': out['m_ffn1_pre_g'], 'm_ffn1_w1': out['m_ffn1_w1'], 'm_ffn1_w2': out['m_ffn1_w2'], 'm_ffn1_post_g': out['m_ffn1_post_g'], 'm_mix_pre_g': out['m_mix_pre_g'], 'm_w_in': out['m_w_in'], 'm_attn_sinks': out['m_attn_sinks'], 'm_sgu_ln_g': out['m_sgu_ln_g'], 'm_sgu_ln_b': out['m_sgu_ln_b'], 'm_sgu_w': out['m_sgu_w'], 'm_sgu_b': out['m_sgu_b'], 'm_w_attn_branch': out['m_w_attn_branch'], 'm_w_sgu_branch': out['m_w_sgu_branch'], 'm_w_out': out['m_w_out'], 'm_mix_post_g': out['m_mix_post_g'], 'm_ffn2_pre_g': out['m_ffn2_pre_g'], 'm_ffn2_w1': out['m_ffn2_w1'], 'm_ffn2_w2': out['m_ffn2_w2'], 'm_ffn2_post_g': out['m_ffn2_post_g'], 'v_ffn1_pre_g': out['v_ffn1_pre_g'], 'v_ffn1_w1': out['v_ffn1_w1'], 'v_ffn1_w2': out['v_ffn1_w2'], 'v_ffn1_post_g': out['v_ffn1_post_g'], 'v_mix_pre_g': out['v_mix_pre_g'], 'v_w_in': out['v_w_in'], 'v_attn_sinks': out['v_attn_sinks'], 'v_sgu_ln_g': out['v_sgu_ln_g'], 'v_sgu_ln_b': out['v_sgu_ln_b'], 'v_sgu_w': out['v_sgu_w'], 'v_sgu_b': out['v_sgu_b'], 'v_w_attn_branch': out['v_w_attn_branch'], 'v_w_sgu_branch': out['v_w_sgu_branch'], 'v_w_out': out['v_w_out'], 'v_mix_post_g': out['v_mix_post_g'], 'v_ffn2_pre_g': out['v_ffn2_pre_g'], 'v_ffn2_w1': out['v_ffn2_w1'], 'v_ffn2_w2': out['v_ffn2_w2'], 'v_ffn2_post_g': out['v_ffn2_post_g']}


def _loss(weights, diff, rest, loss_target):
    with _jax.named_scope("forward"):
        args = {**rest, TWIN_DIFF_INPUT: diff, **{k: w.astype(_WEIGHT_DTYPES[k]) for k, w in weights.items()}}
        y = _forward(args)
    with _jax.named_scope("loss_head"):
        err = _jnp.square(y.astype(_jnp.float32) - loss_target)
        return 0.5 * _jnp.sum(_jnp.mean(err, axis=-1)) if err.ndim else 0.5 * err


def _adamw(w, g, m, v):
    m = ADAM_B1 * m + (1.0 - ADAM_B1) * g
    v = ADAM_B2 * v + (1.0 - ADAM_B2) * _jnp.square(g)
    m_hat = m / (1.0 - ADAM_B1 ** ADAM_STEP)
    v_hat = v / (1.0 - ADAM_B2 ** ADAM_STEP)
    delta = -ADAM_LR * (m_hat / (_jnp.sqrt(v_hat) + ADAM_EPS) + ADAM_WD * w)
    return delta, m, v


def reference(x, ffn1_pre_g, ffn1_w1, ffn1_w2, ffn1_post_g, mix_pre_g, w_in, attn_sinks, sgu_ln_g, sgu_ln_b, sgu_w, sgu_b, w_attn_branch, w_sgu_branch, w_out, mix_post_g, ffn2_pre_g, ffn2_w1, ffn2_w2, ffn2_post_g, loss_target, m_ffn1_pre_g, m_ffn1_w1, m_ffn1_w2, m_ffn1_post_g, m_mix_pre_g, m_w_in, m_attn_sinks, m_sgu_ln_g, m_sgu_ln_b, m_sgu_w, m_sgu_b, m_w_attn_branch, m_w_sgu_branch, m_w_out, m_mix_post_g, m_ffn2_pre_g, m_ffn2_w1, m_ffn2_w2, m_ffn2_post_g, v_ffn1_pre_g, v_ffn1_w1, v_ffn1_w2, v_ffn1_post_g, v_mix_pre_g, v_w_in, v_attn_sinks, v_sgu_ln_g, v_sgu_ln_b, v_sgu_w, v_sgu_b, v_w_attn_branch, v_w_sgu_branch, v_w_out, v_mix_post_g, v_ffn2_pre_g, v_ffn2_w1, v_ffn2_w2, v_ffn2_post_g):
    given = dict(x=x, ffn1_pre_g=ffn1_pre_g, ffn1_w1=ffn1_w1, ffn1_w2=ffn1_w2, ffn1_post_g=ffn1_post_g, mix_pre_g=mix_pre_g, w_in=w_in, attn_sinks=attn_sinks, sgu_ln_g=sgu_ln_g, sgu_ln_b=sgu_ln_b, sgu_w=sgu_w, sgu_b=sgu_b, w_attn_branch=w_attn_branch, w_sgu_branch=w_sgu_branch, w_out=w_out, mix_post_g=mix_post_g, ffn2_pre_g=ffn2_pre_g, ffn2_w1=ffn2_w1, ffn2_w2=ffn2_w2, ffn2_post_g=ffn2_post_g, loss_target=loss_target, m_ffn1_pre_g=m_ffn1_pre_g, m_ffn1_w1=m_ffn1_w1, m_ffn1_w2=m_ffn1_w2, m_ffn1_post_g=m_ffn1_post_g, m_mix_pre_g=m_mix_pre_g, m_w_in=m_w_in, m_attn_sinks=m_attn_sinks, m_sgu_ln_g=m_sgu_ln_g, m_sgu_ln_b=m_sgu_ln_b, m_sgu_w=m_sgu_w, m_sgu_b=m_sgu_b, m_w_attn_branch=m_w_attn_branch, m_w_sgu_branch=m_w_sgu_branch, m_w_out=m_w_out, m_mix_post_g=m_mix_post_g, m_ffn2_pre_g=m_ffn2_pre_g, m_ffn2_w1=m_ffn2_w1, m_ffn2_w2=m_ffn2_w2, m_ffn2_post_g=m_ffn2_post_g, v_ffn1_pre_g=v_ffn1_pre_g, v_ffn1_w1=v_ffn1_w1, v_ffn1_w2=v_ffn1_w2, v_ffn1_post_g=v_ffn1_post_g, v_mix_pre_g=v_mix_pre_g, v_w_in=v_w_in, v_attn_sinks=v_attn_sinks, v_sgu_ln_g=v_sgu_ln_g, v_sgu_ln_b=v_sgu_ln_b, v_sgu_w=v_sgu_w, v_sgu_b=v_sgu_b, v_w_attn_branch=v_w_attn_branch, v_w_sgu_branch=v_w_sgu_branch, v_w_out=v_w_out, v_mix_post_g=v_mix_post_g, v_ffn2_pre_g=v_ffn2_pre_g, v_ffn2_w1=v_ffn2_w1, v_ffn2_w2=v_ffn2_w2, v_ffn2_post_g=v_ffn2_post_g)
    weights = {n: given[n] for n in TWIN_WEIGHTS}
    shared = {n: given[n] for n in SHARED_INPUTS}
    per_example = {n: given[n] for n in ['x']}
    grad_fn = _jax.value_and_grad(_loss, argnums=(0, 1))

    def one_microbatch(ex, loss_target):
        ex = dict(ex)
        diff = ex.pop(TWIN_DIFF_INPUT)
        return grad_fn(weights, diff, {**shared, **ex}, loss_target)

    if N_MICROBATCH == 1:
        loss, (grad_w, grad_x) = one_microbatch(per_example, given["loss_target"])
    else:
        def body(carry, xs):
            loss_sum, grad_sum = carry
            l_k, (gw_k, gx_k) = one_microbatch(xs[0], xs[1])
            with _jax.named_scope("update"):
                return (loss_sum + l_k, _jax.tree.map(_jnp.add, grad_sum, gw_k)), gx_k

        init = (_jnp.zeros((), _jnp.float32), _jax.tree.map(_jnp.zeros_like, weights))
        (loss, grad_w), grad_x = _jax.lax.scan(body, init, (per_example, given["loss_target"]))
    with _jax.named_scope("update"):
        delta_w, new_m, new_v = {}, {}, {}
        for n in TWIN_WEIGHTS:
            delta_w[n], new_m[n], new_v[n] = _adamw(weights[n], grad_w[n], given["m_" + n], given["v_" + n])
    return (loss, grad_x, *[grad_w[n] for n in TWIN_WEIGHTS], *[delta_w[n] for n in TWIN_WEIGHTS],
            *[new_m[n] for n in TWIN_WEIGHTS], *[new_v[n] for n in TWIN_WEIGHTS])
```

```python
import math

import jax
import jax.numpy as jnp
from jax import lax
from jax.experimental import pallas as pl
from jax.experimental.pallas import tpu as pltpu

F32 = jnp.float32
BF16 = jnp.bfloat16

N_DEV = 8
D = 1024
FF = 2816
DEPTH = 4
HEAD_DIM = 64
N_Q_HEADS = 16
N_KV_HEADS = 2
Q_PER_KV = N_Q_HEADS // N_KV_HEADS
BLK = 128
SGU_GROUPS = 8
IN_WIDTH = 5376
W1_SHARD = 2 * FF // N_DEV
WIN_SHARD = IN_WIDTH // N_DEV
W2_SHARD = FF // N_DEV
SQ_SHARD = D // N_DEV
WC_WIDTH = 2 * W1_SHARD + WIN_SHARD

RMS_EPS = 1e-6
LN_EPS = 1e-5
MASK_VALUE = -1e30
ATTN_SCALE = 1.0 / math.sqrt(HEAD_DIM)

ADAM_LR = 0.001
ADAM_B1 = 0.9
ADAM_B2 = 0.999
ADAM_EPS = 1e-08
ADAM_WD = 0.01
ADAM_STEP = 10

VMEM_LIMIT_V7X = 56 * 1024 * 1024

WIN_SEGMENTS = ((0, 1024, 0), (1024, 128, 5120), (1152, 128, 5248), (1280, 1024, 1024),
                (2304, 1024, 2048), (3328, 1024, 3072), (4352, 1024, 4096))
ZQ, ZU, ZV, ZGA, ZGB, ZKV = 0, 1024, 2048, 3072, 4096, 5120

FF_CHUNKS = ((0, 1024), (1024, 1024), (2048, 768))
WIN_CHUNKS = ((0, 1792), (1792, 1792), (3584, 1792))

SMALL_ROWS = 144
SMALL_VEC = ("ffn1_pre_g", "ffn1_post_g", "mix_pre_g", "mix_post_g", "ffn2_pre_g", "ffn2_post_g",
             "sgu_ln_g", "sgu_ln_b")

TM_FFN_FWD = 512
TM_FFN_BWD = 256
TM_MIX = 256
TQ_ATTN = 512
TK_WGRAD = 512
TR_ADAM = 256


def _shard_pieces(shard_width, segments):
    out = []
    for j in range(N_DEV):
        lo, hi = j * shard_width, (j + 1) * shard_width
        pieces = []
        for ns, ln, ps in segments:
            a, b = max(lo, ns), min(hi, ns + ln)
            if a < b:
                pieces.append((a - lo, b - a, ps + a - ns))
        out.append(tuple(pieces))
    return tuple(out)


W1_PIECES = _shard_pieces(W1_SHARD, ((0, 2 * FF, 0),))
WIN_PIECES = _shard_pieces(WIN_SHARD, WIN_SEGMENTS)


def _params(n_grid, vmem=VMEM_LIMIT_V7X):
    return pltpu.CompilerParams(dimension_semantics=("arbitrary",) * n_grid, vmem_limit_bytes=vmem)


def _dot(a, b):
    return jnp.dot(a, b, preferred_element_type=F32)


def _dot_nt(a, b):
    return lax.dot_general(a, b, (((1,), (1,)), ((), ())), preferred_element_type=F32)


def _dot_tn(a, b):
    return lax.dot_general(a, b, (((0,), (0,)), ((), ())), preferred_element_type=F32)


def _rms(x, g):
    r = lax.rsqrt(jnp.mean(x * x, axis=-1, keepdims=True) + RMS_EPS)
    n = x * r
    return n * g, n, r


def _rms_bwd(n, r, g, dy):
    dn = dy * g
    return r * (dn - n * jnp.mean(dn * n, axis=-1, keepdims=True))


def _colsum(v):
    return jnp.sum(v, axis=0, keepdims=True)


def _sigmoid(v):
    return 1.0 / (1.0 + jnp.exp(-v))


def _gelu_parts(v):
    cdf = 0.5 * (1.0 + lax.erf(v * (1.0 / math.sqrt(2.0))))
    return cdf, jnp.exp(-0.5 * v * v) * (1.0 / math.sqrt(2.0 * math.pi))


def _row_tile(rows, cap):
    return max(t for t in range(16, min(rows, cap) + 1, 16) if rows % t == 0)


def _row_spec(tm, width, col_block=0):
    return pl.BlockSpec((tm, width), lambda i, cb=col_block: (i, cb))


def _full_spec(shape):
    nd = len(shape)
    return pl.BlockSpec(tuple(shape), lambda *_: (0,) * nd)


def _layer_spec(arr, l):
    nd = arr.ndim - 1
    return pl.BlockSpec((None,) + tuple(arr.shape[1:]), lambda *_: (l,) + (0,) * nd,
                        pipeline_mode=pl.Buffered(1))


def cast_rows(w2d):
    rows, cols = w2d.shape
    tr = _row_tile(rows, 256)

    def body(w_ref, o_ref):
        o_ref[...] = w_ref[...].astype(BF16)

    return pl.pallas_call(
        body, grid=(rows // tr,), in_specs=[_row_spec(tr, cols)], out_specs=_row_spec(tr, cols),
        out_shape=jax.ShapeDtypeStruct((rows, cols), BF16), name="cast_rows", compiler_params=_params(1),
    )(w2d)


def cast_cols(w1a, w1b, win, tr=256):
    rows = w1a.shape[0]

    def body(a_ref, b_ref, c_ref, o_ref):
        o_ref[:, 0:W1_SHARD] = a_ref[...].astype(BF16)
        o_ref[:, W1_SHARD:2 * W1_SHARD] = b_ref[...].astype(BF16)
        o_ref[:, 2 * W1_SHARD:WC_WIDTH] = c_ref[...].astype(BF16)

    return pl.pallas_call(
        body, grid=(rows // tr,),
        in_specs=[_row_spec(tr, W1_SHARD), _row_spec(tr, W1_SHARD), _row_spec(tr, WIN_SHARD)],
        out_specs=_row_spec(tr, WC_WIDTH),
        out_shape=jax.ShapeDtypeStruct((rows, WC_WIDTH), BF16), name="cast_cols", compiler_params=_params(1),
    )(w1a, w1b, win)


def relayout_cols(gc, tr=256):
    def body(g_ref, o1_ref, o2_ref, o3_ref):
        for j in range(N_DEV):
            blk = g_ref[j]
            o1_ref[:, j * W1_SHARD:(j + 1) * W1_SHARD] = blk[:, 0:W1_SHARD]
            o2_ref[:, j * W1_SHARD:(j + 1) * W1_SHARD] = blk[:, W1_SHARD:2 * W1_SHARD]
            for off, w, ps in WIN_PIECES[j]:
                o3_ref[:, ps:ps + w] = blk[:, 2 * W1_SHARD + off:2 * W1_SHARD + off + w]

    out_spec = lambda width: pl.BlockSpec((None, tr, width), lambda l, i: (l, i, 0))
    return pl.pallas_call(
        body, grid=(DEPTH, D // tr),
        in_specs=[pl.BlockSpec((N_DEV, None, tr, WC_WIDTH), lambda l, i: (0, l, i, 0))],
        out_specs=[out_spec(2 * FF), out_spec(2 * FF), out_spec(IN_WIDTH)],
        out_shape=[jax.ShapeDtypeStruct((DEPTH, D, 2 * FF), BF16), jax.ShapeDtypeStruct((DEPTH, D, 2 * FF), BF16),
                   jax.ShapeDtypeStruct((DEPTH, D, IN_WIDTH), BF16)],
        name="relayout_cols", compiler_params=_params(2),
    )(gc)


HBM_SPEC = pl.BlockSpec(memory_space=pltpu.HBM)
MESH_ID = pl.DeviceIdType.MESH


def all_gather_weights(col_shards, row_shards):
    shards = list(col_shards) + list(row_shards)
    n, n_col = len(shards), len(col_shards)

    def body(*refs):
        ins, outs = refs[:n], refs[n:2 * n]
        send_sems, recv_sems, local_sems = refs[2 * n:]
        x, y, c = lax.axis_index("x"), lax.axis_index("y"), lax.axis_index("c")
        me, sibling = (x, y, c), (x, y, 1 - c)
        chips = [(1 - x, y), (x, 1 - y), (1 - x, 1 - y)]

        def slot(a, owner):
            d = 4 * owner[0] + 2 * owner[1] + owner[2]
            return outs[a].at[d] if a < n_col else outs[a].at[:, d]

        def copy(a, k, owner, to, src=None):
            return pltpu.make_async_remote_copy(
                src_ref=slot(a, owner) if src is None else src, dst_ref=slot(a, owner),
                send_sem=send_sems.at[a * 7 + k], recv_sem=recv_sems.at[a * 7 + k],
                device_id=to, device_id_type=MESH_ID)

        mine = [pltpu.make_async_copy(ins[a], slot(a, me), local_sems.at[a]) for a in range(n)]
        for cp in mine:
            cp.start()
        first = []
        for a in range(n):
            first.append(copy(a, 0, me, sibling, src=ins[a]))
            first += [copy(a, 1 + j, me, (*chip, c), src=ins[a]) for j, chip in enumerate(chips)]
        for cp in first:
            cp.start()
        passed = []
        for j, chip in enumerate(chips):
            for a in range(n):
                copy(a, 1 + j, (*chip, c), me).wait_recv()
                fwd = copy(a, 4 + j, (*chip, c), sibling)
                fwd.start()
                passed.append(fwd)
        for a in range(n):
            copy(a, 0, sibling, me).wait_recv()
            for j, chip in enumerate(chips):
                copy(a, 4 + j, (*chip, 1 - c), me).wait_recv()
        for cp in first + passed:
            cp.wait_send()
        for cp in mine:
            cp.wait()

    out_shape = [jax.ShapeDtypeStruct((N_DEV,) + s.shape, s.dtype) for s in col_shards]
    out_shape += [jax.ShapeDtypeStruct((s.shape[0], N_DEV) + s.shape[1:], s.dtype) for s in row_shards]
    return pl.pallas_call(
        body, in_specs=[HBM_SPEC] * n, out_specs=[HBM_SPEC] * n, out_shape=out_shape,
        scratch_shapes=[pltpu.SemaphoreType.DMA((7 * n,)), pltpu.SemaphoreType.DMA((7 * n,)),
                        pltpu.SemaphoreType.DMA((n,))],
        name="all_gather_weights",
    )(*shards)


RELATIONS = tuple((rx, ry, rc) for rx in (0, 1) for ry in (0, 1) for rc in (0, 1))[1:]


def exchange_grads(col_parts, row_parts, small):
    parts = list(col_parts) + list(row_parts) + [small]
    n, n_col, n_row = len(parts), len(col_parts), len(row_parts)

    def body(*refs):
        ins, outs = refs[:n], refs[n:2 * n]
        send_sems, recv_sems, local_sems = refs[2 * n:]
        x, y, c = lax.axis_index("x"), lax.axis_index("y"), lax.axis_index("c")
        me = 4 * x + 2 * y + c

        def src(a, d):
            if a < n_col:
                return ins[a].at[d]
            if a < n_col + n_row:
                return ins[a].at[:, d]
            return ins[a]

        mine = [pltpu.make_async_copy(src(a, me), outs[a].at[me], local_sems.at[a]) for a in range(n)]
        for cp in mine:
            cp.start()
        sends = []
        for k, (rx, ry, rc) in enumerate(RELATIONS):
            tx, ty, tc = (1 - x if rx else x), (1 - y if ry else y), (1 - c if rc else c)
            peer = 4 * tx + 2 * ty + tc
            for a in range(n):
                sends.append(pltpu.make_async_remote_copy(
                    src_ref=src(a, peer), dst_ref=outs[a].at[me],
                    send_sem=send_sems.at[a * 7 + k], recv_sem=recv_sems.at[a * 7 + k],
                    device_id=(tx, ty, tc), device_id_type=MESH_ID))
        for cp in sends:
            cp.start()
        for k, (rx, ry, rc) in enumerate(RELATIONS):
            tx, ty, tc = (1 - x if rx else x), (1 - y if ry else y), (1 - c if rc else c)
            peer = 4 * tx + 2 * ty + tc
            for a in range(n):
                pltpu.make_async_remote_copy(
                    src_ref=src(a, me), dst_ref=outs[a].at[peer],
                    send_sem=send_sems.at[a * 7 + k], recv_sem=recv_sems.at[a * 7 + k],
                    device_id=(tx, ty, tc), device_id_type=MESH_ID).wait_recv()
        for cp in sends:
            cp.wait_send()
        for cp in mine:
            cp.wait()

    out_shape = [jax.ShapeDtypeStruct(p.shape, p.dtype) for p in col_parts]
    out_shape += [jax.ShapeDtypeStruct((N_DEV, p.shape[0]) + p.shape[2:], p.dtype) for p in row_parts]
    out_shape += [jax.ShapeDtypeStruct((N_DEV,) + small.shape, small.dtype)]
    return pl.pallas_call(
        body, in_specs=[HBM_SPEC] * n, out_specs=[HBM_SPEC] * n, out_shape=out_shape,
        scratch_shapes=[pltpu.SemaphoreType.DMA((7 * n,)), pltpu.SemaphoreType.DMA((7 * n,)),
                        pltpu.SemaphoreType.DMA((n,))],
        name="exchange_grads",
    )(*parts)


def ffn_fwd(x, pre_g, post_g, w1, w2, l, name):
    T = x.shape[0]
    tm = min(TM_FFN_FWD, T)

    def body(x_ref, pg_ref, qg_ref, w1_ref, w2_ref, xo_ref, a_ref, y_ref):
        xv = x_ref[...]
        h, _, _ = _rms(xv, pg_ref[...])
        hb = h.astype(BF16)
        acc = jnp.zeros((tm, D), F32)
        for c0, cw in FF_CHUNKS:
            g = _dot(hb, w1_ref[:, c0:c0 + cw])
            u = _dot(hb, w1_ref[:, FF + c0:FF + c0 + cw])
            a_ref[:, c0:c0 + cw] = g.astype(BF16)
            a_ref[:, FF + c0:FF + c0 + cw] = u.astype(BF16)
            s = (g * _sigmoid(g) * u).astype(BF16)
            acc = acc + _dot(s, w2_ref[c0:c0 + cw, :])
        y_ref[...] = acc.astype(BF16)
        o, _, _ = _rms(acc, qg_ref[...])
        xo_ref[...] = xv + 0.5 * o

    return pl.pallas_call(
        body, grid=(T // tm,),
        in_specs=[_row_spec(tm, D), _full_spec((1, D)), _full_spec((1, D)), _layer_spec(w1, l), _layer_spec(w2, l)],
        out_specs=[_row_spec(tm, D), _row_spec(tm, 2 * FF), _row_spec(tm, D)],
        out_shape=[jax.ShapeDtypeStruct((T, D), F32), jax.ShapeDtypeStruct((T, 2 * FF), BF16),
                   jax.ShapeDtypeStruct((T, D), BF16)],
        name=name, compiler_params=_params(1),
    )(x, pre_g, post_g, w1, w2)


def ffn_bwd(dxo, x, y, a, pre_g, post_g, w1, w2, l, name):
    T = x.shape[0]
    tm = min(TM_FFN_BWD, T)

    def body(dxo_ref, x_ref, y_ref, a_ref, pg_ref, qg_ref, w1_ref, w2_ref,
             dx_ref, da_ref, s_ref, dy_ref, hb_ref, dpg_ref, dqg_ref):
        @pl.when(pl.program_id(0) == 0)
        def _():
            dpg_ref[...] = jnp.zeros_like(dpg_ref)
            dqg_ref[...] = jnp.zeros_like(dqg_ref)

        dxo = dxo_ref[...]
        qg = qg_ref[...]
        _, ny, ry = _rms(y_ref[...].astype(F32), qg)
        dn = 0.5 * dxo
        dqg_ref[...] += _colsum(dn * ny)
        dyb = _rms_bwd(ny, ry, qg, dn).astype(BF16)
        dy_ref[...] = dyb
        pg = pg_ref[...]
        h, nx, rx = _rms(x_ref[...], pg)
        hb_ref[...] = h.astype(BF16)
        dh = jnp.zeros((tm, D), F32)
        for c0, cw in FF_CHUNKS:
            ds = _dot_nt(dyb, w2_ref[c0:c0 + cw, :])
            g = a_ref[:, c0:c0 + cw].astype(F32)
            u = a_ref[:, FF + c0:FF + c0 + cw].astype(F32)
            sg = _sigmoid(g)
            si = g * sg
            s_ref[:, c0:c0 + cw] = (si * u).astype(BF16)
            dg = (ds * u * (sg * (1.0 + g * (1.0 - sg)))).astype(BF16)
            du = (ds * si).astype(BF16)
            da_ref[:, c0:c0 + cw] = dg
            da_ref[:, FF + c0:FF + c0 + cw] = du
            dh = dh + _dot_nt(dg, w1_ref[:, c0:c0 + cw]) + _dot_nt(du, w1_ref[:, FF + c0:FF + c0 + cw])
        dpg_ref[...] += _colsum(dh * nx)
        dx_ref[...] = dxo + _rms_bwd(nx, rx, pg, dh)

    return pl.pallas_call(
        body, grid=(T // tm,),
        in_specs=[_row_spec(tm, D), _row_spec(tm, D), _row_spec(tm, D), _row_spec(tm, 2 * FF),
                  _full_spec((1, D)), _full_spec((1, D)), _layer_spec(w1, l), _layer_spec(w2, l)],
        out_specs=[_row_spec(tm, D), _row_spec(tm, 2 * FF), _row_spec(tm, FF), _row_spec(tm, D), _row_spec(tm, D),
                   _full_spec((1, D)), _full_spec((1, D))],
        out_shape=[jax.ShapeDtypeStruct((T, D), F32), jax.ShapeDtypeStruct((T, 2 * FF), BF16),
                   jax.ShapeDtypeStruct((T, FF), BF16), jax.ShapeDtypeStruct((T, D), BF16),
                   jax.ShapeDtypeStruct((T, D), BF16), jax.ShapeDtypeStruct((1, D), F32),
                   jax.ShapeDtypeStruct((1, D), F32)],
        name=name, compiler_params=_params(1),
    )(dxo, x, y, a, pre_g, post_g, w1, w2)


def mix_in_fwd(x, pre_g, win, l):
    T = x.shape[0]
    tm = min(TM_FFN_FWD, T)

    def body(x_ref, pg_ref, w_ref, z_ref):
        h, _, _ = _rms(x_ref[...], pg_ref[...])
        hb = h.astype(BF16)
        for c0, cw in WIN_CHUNKS:
            z_ref[:, c0:c0 + cw] = _dot(hb, w_ref[:, c0:c0 + cw]).astype(BF16)

    return pl.pallas_call(
        body, grid=(T // tm,),
        in_specs=[_row_spec(tm, D), _full_spec((1, D)), _layer_spec(win, l)],
        out_specs=_row_spec(tm, IN_WIDTH),
        out_shape=jax.ShapeDtypeStruct((T, IN_WIDTH), BF16),
        name="mix_in_fwd", compiler_params=_params(1),
    )(x, pre_g, win)


def _band_mask(first):
    qi = lax.broadcasted_iota(jnp.int32, (BLK, 2 * BLK), 0)
    kj = lax.broadcasted_iota(jnp.int32, (BLK, 2 * BLK), 1)
    band = (kj > qi) & (kj <= qi + BLK)
    if first is None:
        return band
    return band & (kj >= BLK * first.astype(jnp.int32))


def _attn_probs(qh, kg, sink, mask):
    s = _dot_nt(qh, kg) * ATTN_SCALE
    s = jnp.where(mask, s, MASK_VALUE)
    m = jnp.maximum(jnp.max(s, axis=-1, keepdims=True), sink)
    p = jnp.exp(s - m)
    e_sink = jnp.exp(sink - m)
    inv = 1.0 / (jnp.sum(p, axis=-1, keepdims=True) + e_sink)
    return p * inv, e_sink * inv


def _kv_specs(tq, nb):
    kv_blk = ZKV // (2 * BLK)
    return [pl.BlockSpec((tq, 2 * BLK), lambda i: (i, kv_blk)),
            pl.BlockSpec((BLK, 2 * BLK), lambda i: (jnp.maximum(i * nb - 1, 0), kv_blk))]


def attn_fwd(z, sinks):
    T = z.shape[0]
    tq = min(TQ_ATTN, T)
    nb = tq // BLK

    def body(sink_ref, q_ref, kv_ref, kvh_ref, o_ref):
        i = pl.program_id(0)
        for b in range(nb):
            qb = q_ref[b * BLK:(b + 1) * BLK, :]
            kvp = kvh_ref[...] if b == 0 else kv_ref[(b - 1) * BLK:b * BLK, :]
            kv2 = jnp.concatenate([kvp, kv_ref[b * BLK:(b + 1) * BLK, :]], axis=0)
            mask = _band_mask(i == 0) if b == 0 else _band_mask(None)
            outs = []
            for h in range(N_Q_HEADS):
                g = h // Q_PER_KV
                kg = kv2[:, g * HEAD_DIM:(g + 1) * HEAD_DIM]
                vg = kv2[:, BLK + g * HEAD_DIM:BLK + (g + 1) * HEAD_DIM]
                pn, _ = _attn_probs(qb[:, h * HEAD_DIM:(h + 1) * HEAD_DIM], kg, sink_ref[h], mask)
                outs.append(_dot(pn.astype(BF16), vg))
            o_ref[b * BLK:(b + 1) * BLK, :] = jnp.concatenate(outs, axis=1).astype(BF16)

    return pl.pallas_call(
        body, grid=(T // tq,),
        in_specs=[pl.BlockSpec(memory_space=pltpu.SMEM), _row_spec(tq, D)] + _kv_specs(tq, nb),
        out_specs=_row_spec(tq, D),
        out_shape=jax.ShapeDtypeStruct((T, D), BF16),
        name="attn_fwd", compiler_params=_params(1),
    )(sinks, z, z, z)


def attn_bwd(z, dya, dzb, sinks):
    T = z.shape[0]
    tq = min(TQ_ATTN, T)
    nb = tq // BLK
    nt = T // tq
    mid = ZKV - ZU

    def body(sink_ref, q_ref, kv_ref, kvh_ref, dy_ref, dzb_ref, dz_ref, dsink_ref, acc_ref, carry_ref):
        i = pl.program_id(0)
        first_tile = i == nt - 1

        @pl.when(i == 0)
        def _():
            carry_ref[...] = jnp.zeros_like(carry_ref)
            dsink_ref[...] = jnp.zeros_like(dsink_ref)

        acc_ref[...] = jnp.zeros_like(acc_ref)
        lane = lax.broadcasted_iota(jnp.int32, (1, BLK), 1)
        dsink = jnp.zeros((1, BLK), F32)
        for b in range(nb):
            qb = q_ref[b * BLK:(b + 1) * BLK, :]
            dyb = dy_ref[b * BLK:(b + 1) * BLK, :]
            kvp = kvh_ref[...] if b == 0 else kv_ref[(b - 1) * BLK:b * BLK, :]
            kv2 = jnp.concatenate([kvp, kv_ref[b * BLK:(b + 1) * BLK, :]], axis=0)
            mask = _band_mask(first_tile) if b == 0 else _band_mask(None)
            dqs, dks, dvs = [], [], []
            for g in range(N_KV_HEADS):
                kg = kv2[:, g * HEAD_DIM:(g + 1) * HEAD_DIM]
                vg = kv2[:, BLK + g * HEAD_DIM:BLK + (g + 1) * HEAD_DIM]
                ps, dss, qs, dys = [], [], [], []
                for r in range(Q_PER_KV):
                    h = g * Q_PER_KV + r
                    qh = qb[:, h * HEAD_DIM:(h + 1) * HEAD_DIM]
                    dyh = dyb[:, h * HEAD_DIM:(h + 1) * HEAD_DIM]
                    pn, p_sink = _attn_probs(qh, kg, sink_ref[h], mask)
                    dp = _dot_nt(dyh, vg)
                    delta = jnp.sum(pn * dp, axis=-1, keepdims=True)
                    ds = (pn * (dp - delta)).astype(BF16)
                    dsink = dsink + jnp.where(lane == h, -_colsum(p_sink * delta), 0.0)
                    dqs.append(_dot(ds, kg) * ATTN_SCALE)
                    ps.append(pn.astype(BF16))
                    dss.append(ds)
                    qs.append(qh)
                    dys.append(dyh)
                dks.append(_dot_tn(jnp.concatenate(dss, axis=0), jnp.concatenate(qs, axis=0)) * ATTN_SCALE)
                dvs.append(_dot_tn(jnp.concatenate(ps, axis=0), jnp.concatenate(dys, axis=0)))
            acc_ref[b * BLK:(b + 2) * BLK, :] += jnp.concatenate(dks + dvs, axis=1)
            dz_ref[b * BLK:(b + 1) * BLK, ZQ:ZQ + D] = jnp.concatenate(dqs, axis=1).astype(BF16)
        dsink_ref[...] += dsink
        dz_ref[:, ZU:ZKV] = dzb_ref[...]
        if nb > 1:
            dz_ref[0:tq - BLK, ZKV:IN_WIDTH] = acc_ref[BLK:tq, :].astype(BF16)
        dz_ref[tq - BLK:tq, ZKV:IN_WIDTH] = (acc_ref[tq:tq + BLK, :] + carry_ref[...]).astype(BF16)
        carry_ref[...] = acc_ref[0:BLK, :]

    kv_blk = ZKV // (2 * BLK)
    return pl.pallas_call(
        body, grid=(nt,),
        in_specs=[pl.BlockSpec(memory_space=pltpu.SMEM),
                  pl.BlockSpec((tq, D), lambda i: (nt - 1 - i, 0)),
                  pl.BlockSpec((tq, 2 * BLK), lambda i: (nt - 1 - i, kv_blk)),
                  pl.BlockSpec((BLK, 2 * BLK), lambda i: (jnp.maximum((nt - 1 - i) * nb - 1, 0), kv_blk)),
                  pl.BlockSpec((tq, D), lambda i: (nt - 1 - i, 0)),
                  pl.BlockSpec((tq, mid), lambda i: (nt - 1 - i, 0))],
        out_specs=[pl.BlockSpec((tq, IN_WIDTH), lambda i: (nt - 1 - i, 0)), _full_spec((1, BLK))],
        out_shape=[jax.ShapeDtypeStruct((T, IN_WIDTH), BF16), jax.ShapeDtypeStruct((1, BLK), F32)],
        scratch_shapes=[pltpu.VMEM((tq + BLK, 2 * BLK), F32), pltpu.VMEM((BLK, 2 * BLK), F32)],
        name="attn_bwd", compiler_params=_params(1),
    )(sinks, z, z, z, dya, dzb)


def _to_group_lanes(v, g, nch):
    return jnp.concatenate([v[n * BLK:(n + 1) * BLK, g * BLK:(g + 1) * BLK] for n in range(nch)], axis=1)


def _from_group_lanes(per_group, nch):
    rows = [jnp.concatenate([per_group[g][:, n * BLK:(n + 1) * BLK] for g in range(SGU_GROUPS)], axis=1)
            for n in range(nch)]
    return jnp.concatenate(rows, axis=0)


def _tril_bf16(w_ref, g):
    t = lax.broadcasted_iota(jnp.int32, (BLK, BLK), 0)
    s = lax.broadcasted_iota(jnp.int32, (BLK, BLK), 1)
    return jnp.where(t >= s, w_ref[g], 0.0).astype(BF16)


def _sgu_norm(v_s, ln_g, ln_b):
    cdf, pdf = _gelu_parts(v_s)
    gv = v_s * cdf
    xc = gv - jnp.mean(gv, axis=-1, keepdims=True)
    rstd = lax.rsqrt(jnp.mean(xc * xc, axis=-1, keepdims=True) + LN_EPS)
    nhat = xc * rstd
    return nhat * ln_g + ln_b, nhat, rstd, cdf + v_s * pdf


def _sgu_gate(vn, w_ref, bb_ref, nch):
    vnb = vn.astype(BF16)
    return _from_group_lanes(
        [_dot(_tril_bf16(w_ref, g), _to_group_lanes(vnb, g, nch)) + jnp.tile(bb_ref[g], (1, nch))
         for g in range(SGU_GROUPS)], nch)


def mix_fwd_out(x, z, ya, post_g, ln_g, ln_b, sgu_w, sgu_bb, wa, ws, wo, l):
    T = x.shape[0]
    tm = min(TM_MIX, T)
    nch = tm // BLK

    def body(x_ref, us_ref, vs_ref, ga_ref, gb_ref, ya_ref, qg_ref, lg_ref, lb_ref, w_ref, bb_ref,
             wa_ref, ws_ref, wo_ref, xo_ref, ysg_ref, pa_ref, pb_ref, o_ref):
        vn, _, _, _ = _sgu_norm(vs_ref[...].astype(F32), lg_ref[...], lb_ref[...])
        gate = _sgu_gate(vn, w_ref, bb_ref, nch)
        us = us_ref[...].astype(F32)
        cdf, _ = _gelu_parts(us)
        ysg = (us * cdf * gate).astype(BF16)
        ysg_ref[...] = ysg
        pa = _dot(ya_ref[...], wa_ref[...])
        pb = _dot(ysg, ws_ref[...])
        pa_ref[...] = pa.astype(BF16)
        pb_ref[...] = pb.astype(BF16)
        merged = _sigmoid(ga_ref[...].astype(F32)) * pa + _sigmoid(gb_ref[...].astype(F32)) * pb
        o = _dot(merged.astype(BF16), wo_ref[...])
        o_ref[...] = o.astype(BF16)
        on, _, _ = _rms(o, qg_ref[...])
        xo_ref[...] = x_ref[...] + on

    zspec = lambda start: _row_spec(tm, D, start // D)
    act = jax.ShapeDtypeStruct((T, D), BF16)
    return pl.pallas_call(
        body, grid=(T // tm,),
        in_specs=[_row_spec(tm, D), zspec(ZU), zspec(ZV), zspec(ZGA), zspec(ZGB), _row_spec(tm, D),
                  _full_spec((1, D)), _full_spec((1, D)), _full_spec((1, D)),
                  _full_spec((SGU_GROUPS, BLK, BLK)), _full_spec((SGU_GROUPS, BLK, BLK)),
                  _layer_spec(wa, l), _layer_spec(ws, l), _layer_spec(wo, l)],
        out_specs=[_row_spec(tm, D)] * 5,
        out_shape=[jax.ShapeDtypeStruct((T, D), F32), act, act, act, act],
        name="mix_fwd_out", compiler_params=_params(1),
    )(x, z, z, z, z, ya, post_g, ln_g, ln_b, sgu_w, sgu_bb, wa, ws, wo)


def mix_bwd_out(dxo, z, o, pa, pb, post_g, ln_g, ln_b, sgu_w, sgu_bb, wa, ws, wo, l):
    T = dxo.shape[0]
    tm = min(TM_MIX, T)
    nch = tm // BLK
    mid = ZKV - ZU

    def body(dxo_ref, us_ref, vs_ref, ga_ref, gb_ref, o_ref, pa_ref, pb_ref, qg_ref, lg_ref, lb_ref, w_ref, bb_ref,
             wa_ref, ws_ref, wo_ref,
             dzb_ref, dya_ref, mg_ref, do_ref, dpa_ref, dpb_ref, dqg_ref, dlg_ref, dlb_ref, dw_ref, dbb_ref):
        @pl.when(pl.program_id(0) == 0)
        def _():
            for r in (dqg_ref, dlg_ref, dlb_ref, dw_ref, dbb_ref):
                r[...] = jnp.zeros_like(r)

        qg = qg_ref[...]
        dxo = dxo_ref[...]
        _, no, ro = _rms(o_ref[...].astype(F32), qg)
        dqg_ref[...] += _colsum(dxo * no)
        dob = _rms_bwd(no, ro, qg, dxo).astype(BF16)
        do_ref[...] = dob
        dmerged = _dot_nt(dob, wo_ref[...])
        sa = _sigmoid(ga_ref[...].astype(F32))
        sb = _sigmoid(gb_ref[...].astype(F32))
        pa = pa_ref[...].astype(F32)
        pb = pb_ref[...].astype(F32)
        mg_ref[...] = (sa * pa + sb * pb).astype(BF16)
        dpa = (dmerged * sa).astype(BF16)
        dpb = (dmerged * sb).astype(BF16)
        dpa_ref[...] = dpa
        dpb_ref[...] = dpb
        dzb_ref[:, ZGA - ZU:ZGA - ZU + D] = (dmerged * pa * sa * (1.0 - sa)).astype(BF16)
        dzb_ref[:, ZGB - ZU:ZGB - ZU + D] = (dmerged * pb * sb * (1.0 - sb)).astype(BF16)
        dya_ref[...] = _dot_nt(dpa, wa_ref[...]).astype(BF16)
        dysg = _dot_nt(dpb, ws_ref[...])

        lg = lg_ref[...]
        vn, nhat, rstd, dgelu_v = _sgu_norm(vs_ref[...].astype(F32), lg, lb_ref[...])
        gate = _sgu_gate(vn, w_ref, bb_ref, nch)
        us = us_ref[...].astype(F32)
        cdf, pdf = _gelu_parts(us)
        dzb_ref[:, 0:D] = (dysg * gate * (cdf + us * pdf)).astype(BF16)
        dgate = (dysg * (us * cdf)).astype(BF16)
        vnb = vn.astype(BF16)
        t = lax.broadcasted_iota(jnp.int32, (BLK, BLK), 0)
        s = lax.broadcasted_iota(jnp.int32, (BLK, BLK), 1)
        dvn_groups = []
        for g in range(SGU_GROUPS):
            dgl = _to_group_lanes(dgate, g, nch)
            dbb_ref[g] += jnp.broadcast_to(jnp.sum(dgl.astype(F32), axis=-1, keepdims=True), (BLK, BLK))
            dw_ref[g] += jnp.where(t >= s, _dot_nt(dgl, _to_group_lanes(vnb, g, nch)), 0.0)
            dvn_groups.append(_dot_tn(_tril_bf16(w_ref, g), dgl))
        dvn = _from_group_lanes(dvn_groups, nch)
        dlg_ref[...] += _colsum(dvn * nhat)
        dlb_ref[...] += _colsum(dvn)
        dnh = dvn * lg
        dgv = rstd * (dnh - jnp.mean(dnh, axis=-1, keepdims=True) - nhat * jnp.mean(dnh * nhat, axis=-1, keepdims=True))
        dzb_ref[:, ZV - ZU:ZV - ZU + D] = (dgv * dgelu_v).astype(BF16)

    zspec = lambda start: _row_spec(tm, D, start // D)
    act = jax.ShapeDtypeStruct((T, D), BF16)
    grp = jax.ShapeDtypeStruct((SGU_GROUPS, BLK, BLK), F32)
    vec = jax.ShapeDtypeStruct((1, D), F32)
    return pl.pallas_call(
        body, grid=(T // tm,),
        in_specs=[_row_spec(tm, D), zspec(ZU), zspec(ZV), zspec(ZGA), zspec(ZGB),
                  _row_spec(tm, D), _row_spec(tm, D), _row_spec(tm, D),
                  _full_spec((1, D)), _full_spec((1, D)), _full_spec((1, D)),
                  _full_spec((SGU_GROUPS, BLK, BLK)), _full_spec((SGU_GROUPS, BLK, BLK)),
                  _layer_spec(wa, l), _layer_spec(ws, l), _layer_spec(wo, l)],
        out_specs=[_row_spec(tm, mid)] + [_row_spec(tm, D)] * 5 + [_full_spec((1, D))] * 3
                  + [_full_spec((SGU_GROUPS, BLK, BLK))] * 2,
        out_shape=[jax.ShapeDtypeStruct((T, mid), BF16), act, act, act, act, act, vec, vec, vec, grp, grp],
        name="mix_bwd_out", compiler_params=_params(1),
    )(dxo, z, z, z, z, o, pa, pb, post_g, ln_g, ln_b, sgu_w, sgu_bb, wa, ws, wo)


def mix_in_bwd(dxo, x, dz, pre_g, win, l):
    T = x.shape[0]
    tm = min(TM_MIX, T)

    def body(dxo_ref, x_ref, dz_ref, pg_ref, w_ref, dx_ref, hb_ref, dpg_ref):
        @pl.when(pl.program_id(0) == 0)
        def _():
            dpg_ref[...] = jnp.zeros_like(dpg_ref)

        pg = pg_ref[...]
        h, nx, rx = _rms(x_ref[...], pg)
        hb_ref[...] = h.astype(BF16)
        dh = jnp.zeros((tm, D), F32)
        for c0, cw in WIN_CHUNKS:
            dh = dh + _dot_nt(dz_ref[:, c0:c0 + cw], w_ref[:, c0:c0 + cw])
        dpg_ref[...] += _colsum(dh * nx)
        dx_ref[...] = dxo_ref[...] + _rms_bwd(nx, rx, pg, dh)

    return pl.pallas_call(
        body, grid=(T // tm,),
        in_specs=[_row_spec(tm, D), _row_spec(tm, D), _row_spec(tm, IN_WIDTH), _full_spec((1, D)), _layer_spec(win, l)],
        out_specs=[_row_spec(tm, D), _row_spec(tm, D), _full_spec((1, D))],
        out_shape=[jax.ShapeDtypeStruct((T, D), F32), jax.ShapeDtypeStruct((T, D), BF16),
                   jax.ShapeDtypeStruct((1, D), F32)],
        name="mix_in_bwd", compiler_params=_params(1),
    )(dxo, x, dz, pre_g, win)


def wgrad_cols(a, b, pieces, shard_width, name, tmo=512):
    T, M = a.shape
    N = b.shape[1]
    tk = min(TK_WGRAD, T)
    nk = T // tk

    def body(a_ref, b_ref, o_ref, acc_ref):
        k = pl.program_id(1)

        @pl.when(k == 0)
        def _():
            acc_ref[...] = jnp.zeros_like(acc_ref)

        acc_ref[...] += _dot_tn(a_ref[...], b_ref[...])

        @pl.when(k == nk - 1)
        def _():
            for j in range(N_DEV):
                for off, w, ps in pieces[j]:
                    o_ref[j, :, off:off + w] = acc_ref[:, ps:ps + w].astype(BF16)

    return pl.pallas_call(
        body, grid=(M // tmo, nk),
        in_specs=[pl.BlockSpec((tk, tmo), lambda m, k: (k, m)), pl.BlockSpec((tk, N), lambda m, k: (k, 0))],
        out_specs=pl.BlockSpec((N_DEV, tmo, shard_width), lambda m, k: (0, m, 0)),
        out_shape=jax.ShapeDtypeStruct((N_DEV, M, shard_width), BF16),
        scratch_shapes=[pltpu.VMEM((tmo, N), F32)],
        name=name, compiler_params=_params(2),
    )(a, b)


def wgrad_rows(a, b, name, tmo):
    T, M = a.shape
    N = b.shape[1]
    tk = min(TK_WGRAD, T)
    nk = T // tk

    def body(a_ref, b_ref, o_ref, acc_ref):
        k = pl.program_id(1)

        @pl.when(k == 0)
        def _():
            acc_ref[...] = jnp.zeros_like(acc_ref)

        acc_ref[...] += _dot_tn(a_ref[...], b_ref[...])

        @pl.when(k == nk - 1)
        def _():
            o_ref[...] = acc_ref[...].astype(BF16)

    return pl.pallas_call(
        body, grid=(M // tmo, nk),
        in_specs=[pl.BlockSpec((tk, tmo), lambda m, k: (k, m)), pl.BlockSpec((tk, N), lambda m, k: (k, 0))],
        out_specs=pl.BlockSpec((tmo, N), lambda m, k: (m, 0)),
        out_shape=jax.ShapeDtypeStruct((M, N), BF16),
        scratch_shapes=[pltpu.VMEM((tmo, N), F32)],
        name=name, compiler_params=_params(2),
    )(a, b)


def loss_head(y, target):
    T = y.shape[0]
    tm = min(TM_FFN_FWD, T)

    def body(y_ref, t_ref, sq_ref, dy_ref):
        @pl.when(pl.program_id(0) == 0)
        def _():
            sq_ref[...] = jnp.zeros_like(sq_ref)

        err = y_ref[...] - t_ref[...]
        sq_ref[...] += _colsum(err * err)
        dy_ref[...] = err * (1.0 / D)

    return pl.pallas_call(
        body, grid=(T // tm,), in_specs=[_row_spec(tm, D), _row_spec(tm, D)],
        out_specs=[_full_spec((1, D)), _row_spec(tm, D)],
        out_shape=[jax.ShapeDtypeStruct((1, D), F32), jax.ShapeDtypeStruct((T, D), F32)],
        name="loss_head", compiler_params=_params(1),
    )(y, target)


def adamw_sum(parts, w, m, v, name):
    rows, cols = w.shape
    tr = _row_tile(rows, TR_ADAM)
    c1 =1.0 - ADAM_B1 ** ADAM_STEP
    c2 = 1.0 - ADAM_B2 ** ADAM_STEP

    def body(p_ref, w_ref, m_ref, v_ref, g_ref, d_ref, nm_ref, nv_ref):
        g = p_ref[0].astype(F32)
        for j in range(1, N_DEV):
            g = g + p_ref[j].astype(F32)
        nm = ADAM_B1 * m_ref[...] + (1.0 - ADAM_B1) * g
        nv = ADAM_B2 * v_ref[...] + (1.0 - ADAM_B2) * (g * g)
        g_ref[...] = g
        nm_ref[...] = nm
        nv_ref[...] = nv
        d_ref[...] = -ADAM_LR * ((nm / c1) / (jnp.sqrt(nv / c2) + ADAM_EPS) + ADAM_WD * w_ref[...])

    spec = _row_spec(tr, cols)
    out = jax.ShapeDtypeStruct((rows, cols), F32)
    return pl.pallas_call(
        body, grid=(rows // tr,),
        in_specs=[pl.BlockSpec((N_DEV, tr, cols), lambda i: (0, i, 0)), spec, spec, spec],
        out_specs=[spec] * 4, out_shape=[out] * 4, name=name, compiler_params=_params(1),
    )(parts, w, m, v)


def _pack_small(p):
    layers = []
    for l in range(DEPTH):
        rows = [p[n][l].reshape(1, D) for n in SMALL_VEC]
        rows.append(p["sgu_b"][l].reshape(1, D))
        rows.append(jnp.pad(p["attn_sinks"][l].reshape(1, N_Q_HEADS), ((0, 0), (0, D - N_Q_HEADS))))
        rows.append(p["sgu_w"][l].reshape(BLK, D))
        used = len(SMALL_VEC) + 2 + BLK
        rows.append(jnp.zeros((SMALL_ROWS - used, D), F32))
        layers.append(jnp.concatenate(rows, axis=0))
    return jnp.concatenate(layers, axis=0)


def _unpack_small(packed):
    a = packed.reshape(DEPTH, SMALL_ROWS, D)
    out = {n: a[:, i, :] for i, n in enumerate(SMALL_VEC)}
    k = len(SMALL_VEC)
    out["sgu_b"] = a[:, k, :].reshape(DEPTH, SGU_GROUPS, BLK)
    out["attn_sinks"] = a[:, k + 1, :N_Q_HEADS]
    out["sgu_w"] = a[:, k + 2:k + 2 + BLK, :].reshape(DEPTH, SGU_GROUPS, BLK, BLK)
    return out


WEIGHT_NAMES = ("ffn1_pre_g", "ffn1_w1", "ffn1_w2", "ffn1_post_g", "mix_pre_g", "w_in", "attn_sinks", "sgu_ln_g",
                "sgu_ln_b", "sgu_w", "sgu_b", "w_attn_branch", "w_sgu_branch", "w_out", "mix_post_g", "ffn2_pre_g",
                "ffn2_w1", "ffn2_w2", "ffn2_post_g")
COL_SHARDED = ("ffn1_w1", "ffn2_w1", "w_in")
ROW_SHARDED = ("ffn1_w2", "ffn2_w2", "w_attn_branch", "w_sgu_branch", "w_out")


def kernel(x, ffn1_pre_g, ffn1_w1, ffn1_w2, ffn1_post_g, mix_pre_g, w_in, attn_sinks, sgu_ln_g, sgu_ln_b, sgu_w, sgu_b, w_attn_branch, w_sgu_branch, w_out, mix_post_g, ffn2_pre_g, ffn2_w1, ffn2_w2, ffn2_post_g, loss_target, m_ffn1_pre_g, m_ffn1_w1, m_ffn1_w2, m_ffn1_post_g, m_mix_pre_g, m_w_in, m_attn_sinks, m_sgu_ln_g, m_sgu_ln_b, m_sgu_w, m_sgu_b, m_w_attn_branch, m_w_sgu_branch, m_w_out, m_mix_post_g, m_ffn2_pre_g, m_ffn2_w1, m_ffn2_w2, m_ffn2_post_g, v_ffn1_pre_g, v_ffn1_w1, v_ffn1_w2, v_ffn1_post_g, v_mix_pre_g, v_w_in, v_attn_sinks, v_sgu_ln_g, v_sgu_ln_b, v_sgu_w, v_sgu_b, v_w_attn_branch, v_w_sgu_branch, v_w_out, v_mix_post_g, v_ffn2_pre_g, v_ffn2_w1, v_ffn2_w2, v_ffn2_post_g):
    w = dict(zip(WEIGHT_NAMES, (ffn1_pre_g, ffn1_w1, ffn1_w2, ffn1_post_g, mix_pre_g, w_in, attn_sinks, sgu_ln_g,
                                sgu_ln_b, sgu_w, sgu_b, w_attn_branch, w_sgu_branch, w_out, mix_post_g, ffn2_pre_g,
                                ffn2_w1, ffn2_w2, ffn2_post_g)))
    mom = dict(zip(WEIGHT_NAMES, (m_ffn1_pre_g, m_ffn1_w1, m_ffn1_w2, m_ffn1_post_g, m_mix_pre_g, m_w_in,
                                  m_attn_sinks, m_sgu_ln_g, m_sgu_ln_b, m_sgu_w, m_sgu_b, m_w_attn_branch,
                                  m_w_sgu_branch, m_w_out, m_mix_post_g, m_ffn2_pre_g, m_ffn2_w1, m_ffn2_w2,
                                  m_ffn2_post_g)))
    var = dict(zip(WEIGHT_NAMES, (v_ffn1_pre_g, v_ffn1_w1, v_ffn1_w2, v_ffn1_post_g, v_mix_pre_g, v_w_in,
                                  v_attn_sinks, v_sgu_ln_g, v_sgu_ln_b, v_sgu_w, v_sgu_b, v_w_attn_branch,
                                  v_w_sgu_branch, v_w_out, v_mix_post_g, v_ffn2_pre_g, v_ffn2_w1, v_ffn2_w2,
                                  v_ffn2_post_g)))
    T = x.shape[1]
    xs = x.reshape(T, D)
    target = loss_target.reshape(T, D)

    wc = cast_cols(ffn1_w1.reshape(DEPTH * D, W1_SHARD), ffn2_w1.reshape(DEPTH * D, W1_SHARD),
                   w_in.reshape(DEPTH * D, WIN_SHARD)).reshape(DEPTH, D, WC_WIDTH)
    row_bf16 = [cast_rows(w[n].reshape(-1, D)).reshape(w[n].shape) for n in ROW_SHARDED]
    gathered = all_gather_weights([wc], row_bf16)
    w1a, w1b, win = relayout_cols(gathered[0])
    w2a, w2b, wat, wsg, wou = [g.reshape(DEPTH, -1, D) for g in gathered[1:]]

    vec = lambda n, l: w[n][l].reshape(1, D)
    sgu_bb = [jnp.broadcast_to(w["sgu_b"][l][:, :, None], (SGU_GROUPS, BLK, BLK)) for l in range(DEPTH)]

    saved = []
    h = xs
    for l in range(DEPTH):
        x0 = h
        x1, a1, y1 = ffn_fwd(x0, vec("ffn1_pre_g", l), vec("ffn1_post_g", l), w1a, w2a, l, "ffn1_fwd")
        z = mix_in_fwd(x1, vec("mix_pre_g", l), win, l)
        ya = attn_fwd(z, w["attn_sinks"][l])
        x2, ysg, pa, pb, o = mix_fwd_out(x1, z, ya, vec("mix_post_g", l), vec("sgu_ln_g", l), vec("sgu_ln_b", l),
                                         w["sgu_w"][l], sgu_bb[l], wat, wsg, wou, l)
        h, a2, y2 = ffn_fwd(x2, vec("ffn2_pre_g", l), vec("ffn2_post_g", l), w1b, w2b, l, "ffn2_fwd")
        saved.append((x0, a1, y1, x1, z, ya, ysg, pa, pb, o, x2, a2, y2))

    sq, dx = loss_head(h, target)
    loss = lax.psum(0.5 / D * jnp.sum(sq), ("x", "y", "c"))

    col_parts = {n: [None] * DEPTH for n in COL_SHARDED}
    row_parts = {n: [None] * DEPTH for n in ROW_SHARDED}
    small = {n: [None] * DEPTH for n in WEIGHT_NAMES if n not in COL_SHARDED + ROW_SHARDED}
    for l in reversed(range(DEPTH)):
        x0, a1, y1, x1, z, ya, ysg, pa, pb, o, x2, a2, y2 = saved[l]
        dx, da, s, dy, hb, dpg, dqg = ffn_bwd(dx, x2, y2, a2, vec("ffn2_pre_g", l), vec("ffn2_post_g", l),
                                              w1b, w2b, l, "ffn2_bwd")
        small["ffn2_pre_g"][l], small["ffn2_post_g"][l] = dpg, dqg
        col_parts["ffn2_w1"][l] = wgrad_cols(hb, da, W1_PIECES, W1_SHARD, "wgrad_ffn_w1")
        row_parts["ffn2_w2"][l] = wgrad_rows(s, dy, "wgrad_ffn_w2", FF // 2)

        dzb, dya, mg, dob, dpa, dpb, dqg, dlg, dlb, dsw, dsb = mix_bwd_out(
            dx, z, o, pa, pb, vec("mix_post_g", l), vec("sgu_ln_g", l), vec("sgu_ln_b", l), w["sgu_w"][l], sgu_bb[l],
            wat, wsg, wou, l)
        dz, dsink = attn_bwd(z, dya, dzb, w["attn_sinks"][l])
        dx, hb, dpg = mix_in_bwd(dx, x1, dz, vec("mix_pre_g", l), win, l)
        small["mix_pre_g"][l], small["mix_post_g"][l] = dpg, dqg
        small["sgu_ln_g"][l], small["sgu_ln_b"][l] = dlg, dlb
        small["sgu_w"][l], small["sgu_b"][l] = dsw, dsb[:, :, 0]
        small["attn_sinks"][l] = dsink[0, :N_Q_HEADS]
        col_parts["w_in"][l] = wgrad_cols(hb, dz, WIN_PIECES, WIN_SHARD, "wgrad_w_in")
        row_parts["w_attn_branch"][l] = wgrad_rows(ya, dpa, "wgrad_square", D)
        row_parts["w_sgu_branch"][l] = wgrad_rows(ysg, dpb, "wgrad_square", D)
        row_parts["w_out"][l] = wgrad_rows(mg, dob, "wgrad_square", D)

        dx, da, s, dy, hb, dpg, dqg = ffn_bwd(dx, x0, y1, a1, vec("ffn1_pre_g", l), vec("ffn1_post_g", l),
                                              w1a, w2a, l, "ffn1_bwd")
        small["ffn1_pre_g"][l], small["ffn1_post_g"][l] = dpg, dqg
        col_parts["ffn1_w1"][l] = wgrad_cols(hb, da, W1_PIECES, W1_SHARD, "wgrad_ffn_w1")
        row_parts["ffn1_w2"][l] = wgrad_rows(s, dy, "wgrad_ffn_w2", FF // 2)
    grad_x = dx.reshape(x.shape)

    col_send = [jnp.stack(col_parts[n], axis=1) for n in COL_SHARDED]
    row_send = [jnp.stack(row_parts[n], axis=0).reshape(DEPTH, N_DEV, -1, D) for n in ROW_SHARDED]
    small_send = _pack_small({n: jnp.stack(v, axis=0) for n, v in small.items()})
    received = exchange_grads(col_send, row_send, small_send)

    grads, deltas, new_m, new_v = {}, {}, {}, {}
    for n, parts in zip(COL_SHARDED + ROW_SHARDED, received[:-1]):
        shape = w[n].shape
        cols = shape[-1]
        res = adamw_sum(parts.reshape(N_DEV, -1, cols), w[n].reshape(-1, cols), mom[n].reshape(-1, cols),
                        var[n].reshape(-1, cols), "adamw_" + n)
        grads[n], deltas[n], new_m[n], new_v[n] = [r.reshape(shape) for r in res]
    res = adamw_sum(received[-1], _pack_small(w), _pack_small(mom), _pack_small(var), "adamw_small")
    for out, packed in zip((grads, deltas, new_m, new_v), res):
        out.update(_unpack_small(packed))

    return (loss, grad_x, *[grads[n] for n in WEIGHT_NAMES], *[deltas[n] for n in WEIGHT_NAMES],
            *[new_m[n] for n in WEIGHT_NAMES], *[new_v[n] for n in WEIGHT_NAMES])
```

```python
import math

import jax
import jax.numpy as jnp
from jax import lax
from jax.experimental import pallas as pl
from jax.experimental.pallas import tpu as pltpu

F32 = jnp.float32
BF16 = jnp.bfloat16

N_DEV = 8
D = 1024
FF = 2816
DEPTH = 4
HEAD_DIM = 64
N_Q_HEADS = 16
N_KV_HEADS = 2
Q_PER_KV = N_Q_HEADS // N_KV_HEADS
BLK = 128
SGU_GROUPS = 8
IN_WIDTH = 5376
W1_SHARD = 2 * FF // N_DEV
WIN_SHARD = IN_WIDTH // N_DEV
W2_SHARD = FF // N_DEV
SQ_SHARD = D // N_DEV
WC_WIDTH = 2 * W1_SHARD + WIN_SHARD

RMS_EPS = 1e-6
LN_EPS = 1e-5
MASK_VALUE = -1e30
ATTN_SCALE = 1.0 / math.sqrt(HEAD_DIM)

ADAM_LR = 0.001
ADAM_B1 = 0.9
ADAM_B2 = 0.999
ADAM_EPS = 1e-08
ADAM_WD = 0.01
ADAM_STEP = 10

VMEM_LIMIT_V7X = 56 * 1024 * 1024

WIN_SEGMENTS = ((0, 1024, 0), (1024, 128, 5120), (1152, 128, 5248), (1280, 1024, 1024),
                (2304, 1024, 2048), (3328, 1024, 3072), (4352, 1024, 4096))
ZQ, ZU, ZV, ZGA, ZGB, ZKV = 0, 1024, 2048, 3072, 4096, 5120

FF_CHUNKS = ((0, 1024), (1024, 1024), (2048, 768))
WIN_CHUNKS = ((0, 1792), (1792, 1792), (3584, 1792))

SMALL_ROWS = 144
SMALL_VEC = ("ffn1_pre_g", "ffn1_post_g", "mix_pre_g", "mix_post_g", "ffn2_pre_g", "ffn2_post_g",
             "sgu_ln_g", "sgu_ln_b")

TM_FFN_FWD = 512
TM_FFN_BWD = 256
TM_MIX = 512
TM_MIX_BWD = 256
TQ_ATTN = 512
TK_WGRAD = 512
TR_ADAM = 256


def _shard_pieces(shard_width, segments):
    out = []
    for j in range(N_DEV):
        lo, hi = j * shard_width, (j + 1) * shard_width
        pieces = []
        for ns, ln, ps in segments:
            a, b = max(lo, ns), min(hi, ns + ln)
            if a < b:
                pieces.append((a - lo, b - a, ps + a - ns))
        out.append(tuple(pieces))
    return tuple(out)


W1_PIECES = _shard_pieces(W1_SHARD, ((0, 2 * FF, 0),))
WIN_PIECES = _shard_pieces(WIN_SHARD, WIN_SEGMENTS)


def _params(n_grid, vmem=VMEM_LIMIT_V7X):
    return pltpu.CompilerParams(dimension_semantics=("arbitrary",) * n_grid, vmem_limit_bytes=vmem)


def _dot(a, b):
    return jnp.dot(a, b, preferred_element_type=F32)


def _dot_nt(a, b):
    return lax.dot_general(a, b, (((1,), (1,)), ((), ())), preferred_element_type=F32)


def _dot_tn(a, b):
    return lax.dot_general(a, b, (((0,), (0,)), ((), ())), preferred_element_type=F32)


def _rms(x, g):
    r = lax.rsqrt(jnp.mean(x * x, axis=-1, keepdims=True) + RMS_EPS)
    n = x * r
    return n * g, n, r


def _rms_bwd(n, r, g, dy):
    dn = dy * g
    return r * (dn - n * jnp.mean(dn * n, axis=-1, keepdims=True))


def _colsum(v):
    return jnp.sum(v, axis=0, keepdims=True)


def _sigmoid(v):
    return 1.0 / (1.0 + jnp.exp(-v))


def _gelu_parts(v):
    cdf = 0.5 * (1.0 + lax.erf(v * (1.0 / math.sqrt(2.0))))
    return cdf, jnp.exp(-0.5 * v * v) * (1.0 / math.sqrt(2.0 * math.pi))


def _row_tile(rows, cap):
    return max(t for t in range(16, min(rows, cap) + 1, 16) if rows % t == 0)


def _row_spec(tm, width, col_block=0):
    return pl.BlockSpec((tm, width), lambda i, cb=col_block: (i, cb))


def _full_spec(shape):
    nd = len(shape)
    return pl.BlockSpec(tuple(shape), lambda *_: (0,) * nd)


def _layer_spec(arr, l):
    nd = arr.ndim - 1
    return pl.BlockSpec((None,) + tuple(arr.shape[1:]), lambda *_: (l,) + (0,) * nd,
                        pipeline_mode=pl.Buffered(1))


def cast_rows(w2d):
    rows, cols = w2d.shape
    tr = _row_tile(rows, 256)

    def body(w_ref, o_ref):
        o_ref[...] = w_ref[...].astype(BF16)

    return pl.pallas_call(
        body, grid=(rows // tr,), in_specs=[_row_spec(tr, cols)], out_specs=_row_spec(tr, cols),
        out_shape=jax.ShapeDtypeStruct((rows, cols), BF16), name="cast_rows", compiler_params=_params(1),
    )(w2d)


def cast_cols(w1a, w1b, win, tr=256):
    rows = w1a.shape[0]

    def body(a_ref, b_ref, c_ref, o_ref):
        o_ref[:, 0:W1_SHARD] = a_ref[...].astype(BF16)
        o_ref[:, W1_SHARD:2 * W1_SHARD] = b_ref[...].astype(BF16)
        o_ref[:, 2 * W1_SHARD:WC_WIDTH] = c_ref[...].astype(BF16)

    return pl.pallas_call(
        body, grid=(rows // tr,),
        in_specs=[_row_spec(tr, W1_SHARD), _row_spec(tr, W1_SHARD), _row_spec(tr, WIN_SHARD)],
        out_specs=_row_spec(tr, WC_WIDTH),
        out_shape=jax.ShapeDtypeStruct((rows, WC_WIDTH), BF16), name="cast_cols", compiler_params=_params(1),
    )(w1a, w1b, win)


def relayout_cols(gc, tr=256):
    def body(g_ref, o1_ref, o2_ref, o3_ref):
        for j in range(N_DEV):
            blk = g_ref[j]
            o1_ref[:, j * W1_SHARD:(j + 1) * W1_SHARD] = blk[:, 0:W1_SHARD]
            o2_ref[:, j * W1_SHARD:(j + 1) * W1_SHARD] = blk[:, W1_SHARD:2 * W1_SHARD]
            for off, w, ps in WIN_PIECES[j]:
                o3_ref[:, ps:ps + w] = blk[:, 2 * W1_SHARD + off:2 * W1_SHARD + off + w]

    out_spec = lambda width: pl.BlockSpec((None, tr, width), lambda l, i: (l, i, 0))
    return pl.pallas_call(
        body, grid=(DEPTH, D // tr),
        in_specs=[pl.BlockSpec((N_DEV, None, tr, WC_WIDTH), lambda l, i: (0, l, i, 0))],
        out_specs=[out_spec(2 * FF), out_spec(2 * FF), out_spec(IN_WIDTH)],
        out_shape=[jax.ShapeDtypeStruct((DEPTH, D, 2 * FF), BF16), jax.ShapeDtypeStruct((DEPTH, D, 2 * FF), BF16),
                   jax.ShapeDtypeStruct((DEPTH, D, IN_WIDTH), BF16)],
        name="relayout_cols", compiler_params=_params(2),
    )(gc)


HBM_SPEC = pl.BlockSpec(memory_space=pltpu.HBM)
MESH_ID = pl.DeviceIdType.MESH


def all_gather_weights(col_shards, row_shards):
    shards = list(col_shards) + list(row_shards)
    n, n_col = len(shards), len(col_shards)

    def body(*refs):
        ins, outs = refs[:n], refs[n:2 * n]
        send_sems, recv_sems, local_sems = refs[2 * n:]
        x, y, c = lax.axis_index("x"), lax.axis_index("y"), lax.axis_index("c")
        me, sibling = (x, y, c), (x, y, 1 - c)
        chips = [(1 - x, y), (x, 1 - y), (1 - x, 1 - y)]

        def slot(a, owner):
            d = 4 * owner[0] + 2 * owner[1] + owner[2]
            return outs[a].at[d] if a < n_col else outs[a].at[:, d]

        def copy(a, k, owner, to, src=None):
            return pltpu.make_async_remote_copy(
                src_ref=slot(a, owner) if src is None else src, dst_ref=slot(a, owner),
                send_sem=send_sems.at[a * 7 + k], recv_sem=recv_sems.at[a * 7 + k],
                device_id=to, device_id_type=MESH_ID)

        mine = [pltpu.make_async_copy(ins[a], slot(a, me), local_sems.at[a]) for a in range(n)]
        for cp in mine:
            cp.start()
        first = []
        for a in range(n):
            first.append(copy(a, 0, me, sibling, src=ins[a]))
            first += [copy(a, 1 + j, me, (*chip, c), src=ins[a]) for j, chip in enumerate(chips)]
        for cp in first:
            cp.start()
        passed = []
        for j, chip in enumerate(chips):
            for a in range(n):
                copy(a, 1 + j, (*chip, c), me).wait_recv()
                fwd = copy(a, 4 + j, (*chip, c), sibling)
                fwd.start()
                passed.append(fwd)
        for a in range(n):
            copy(a, 0, sibling, me).wait_recv()
            for j, chip in enumerate(chips):
                copy(a, 4 + j, (*chip, 1 - c), me).wait_recv()
        for cp in first + passed:
            cp.wait_send()
        for cp in mine:
            cp.wait()

    out_shape = [jax.ShapeDtypeStruct((N_DEV,) + s.shape, s.dtype) for s in col_shards]
    out_shape += [jax.ShapeDtypeStruct((s.shape[0], N_DEV) + s.shape[1:], s.dtype) for s in row_shards]
    return pl.pallas_call(
        body, in_specs=[HBM_SPEC] * n, out_specs=[HBM_SPEC] * n, out_shape=out_shape,
        scratch_shapes=[pltpu.SemaphoreType.DMA((7 * n,)), pltpu.SemaphoreType.DMA((7 * n,)),
                        pltpu.SemaphoreType.DMA((n,))],
        name="all_gather_weights",
    )(*shards)


RELATIONS = tuple((rx, ry, rc) for rx in (0, 1) for ry in (0, 1) for rc in (0, 1))[1:]


def exchange_grads(col_parts, row_parts, small):
    parts = list(col_parts) + list(row_parts) + [small]
    n, n_col, n_row = len(parts), len(col_parts), len(row_parts)

    def body(*refs):
        ins, outs = refs[:n], refs[n:2 * n]
        send_sems, recv_sems, local_sems = refs[2 * n:]
        x, y, c = lax.axis_index("x"), lax.axis_index("y"), lax.axis_index("c")
        me = 4 * x + 2 * y + c

        def src(a, d):
            if a < n_col:
                return ins[a].at[d]
            if a < n_col + n_row:
                return ins[a].at[:, d]
            return ins[a]

        mine = [pltpu.make_async_copy(src(a, me), outs[a].at[me], local_sems.at[a]) for a in range(n)]
        for cp in mine:
            cp.start()
        sends = []
        for k, (rx, ry, rc) in enumerate(RELATIONS):
            tx, ty, tc = (1 - x if rx else x), (1 - y if ry else y), (1 - c if rc else c)
            peer = 4 * tx + 2 * ty + tc
            for a in range(n):
                sends.append(pltpu.make_async_remote_copy(
                    src_ref=src(a, peer), dst_ref=outs[a].at[me],
                    send_sem=send_sems.at[a * 7 + k], recv_sem=recv_sems.at[a * 7 + k],
                    device_id=(tx, ty, tc), device_id_type=MESH_ID))
        for cp in sends:
            cp.start()
        for k, (rx, ry, rc) in enumerate(RELATIONS):
            tx, ty, tc = (1 - x if rx else x), (1 - y if ry else y), (1 - c if rc else c)
            peer = 4 * tx + 2 * ty + tc
            for a in range(n):
                pltpu.make_async_remote_copy(
                    src_ref=src(a, me), dst_ref=outs[a].at[peer],
                    send_sem=send_sems.at[a * 7 + k], recv_sem=recv_sems.at[a * 7 + k],
                    device_id=(tx, ty, tc), device_id_type=MESH_ID).wait_recv()
        for cp in sends:
            cp.wait_send()
        for cp in mine:
            cp.wait()

    out_shape = [jax.ShapeDtypeStruct(p.shape, p.dtype) for p in col_parts]
    out_shape += [jax.ShapeDtypeStruct((N_DEV, p.shape[0]) + p.shape[2:], p.dtype) for p in row_parts]
    out_shape += [jax.ShapeDtypeStruct((N_DEV,) + small.shape, small.dtype)]
    return pl.pallas_call(
        body, in_specs=[HBM_SPEC] * n, out_specs=[HBM_SPEC] * n, out_shape=out_shape,
        scratch_shapes=[pltpu.SemaphoreType.DMA((7 * n,)), pltpu.SemaphoreType.DMA((7 * n,)),
                        pltpu.SemaphoreType.DMA((n,))],
        name="exchange_grads",
    )(*parts)


def ffn_fwd(x, pre_g, post_g, w1, w2, l, name):
    T = x.shape[0]
    tm = min(TM_FFN_FWD, T)

    def body(x_ref, pg_ref, qg_ref, w1_ref, w2_ref, xo_ref, a_ref, y_ref):
        xv = x_ref[...]
        h, _, _ = _rms(xv, pg_ref[...])
        hb = h.astype(BF16)
        acc = jnp.zeros((tm, D), F32)
        for c0, cw in FF_CHUNKS:
            g = _dot(hb, w1_ref[:, c0:c0 + cw])
            u = _dot(hb, w1_ref[:, FF + c0:FF + c0 + cw])
            a_ref[:, c0:c0 + cw] = g.astype(BF16)
            a_ref[:, FF + c0:FF + c0 + cw] = u.astype(BF16)
            s = (g * _sigmoid(g) * u).astype(BF16)
            acc = acc + _dot(s, w2_ref[c0:c0 + cw, :])
        y_ref[...] = acc.astype(BF16)
        o, _, _ = _rms(acc, qg_ref[...])
        xo_ref[...] = xv + 0.5 * o

    return pl.pallas_call(
        body, grid=(T // tm,),
        in_specs=[_row_spec(tm, D), _full_spec((1, D)), _full_spec((1, D)), _layer_spec(w1, l), _layer_spec(w2, l)],
        out_specs=[_row_spec(tm, D), _row_spec(tm, 2 * FF), _row_spec(tm, D)],
        out_shape=[jax.ShapeDtypeStruct((T, D), F32), jax.ShapeDtypeStruct((T, 2 * FF), BF16),
                   jax.ShapeDtypeStruct((T, D), BF16)],
        name=name, compiler_params=_params(1),
    )(x, pre_g, post_g, w1, w2)


def ffn_bwd(dxo, x, y, a, pre_g, post_g, w1, w2, l, name):
    T = x.shape[0]
    tm = min(TM_FFN_BWD, T)

    def body(dxo_ref, x_ref, y_ref, a_ref, pg_ref, qg_ref, w1_ref, w2_ref,
             dx_ref, da_ref, s_ref, dy_ref, hb_ref, dpg_ref, dqg_ref):
        @pl.when(pl.program_id(0) == 0)
        def _():
            dpg_ref[...] = jnp.zeros_like(dpg_ref)
            dqg_ref[...] = jnp.zeros_like(dqg_ref)

        dxo = dxo_ref[...]
        qg = qg_ref[...]
        _, ny, ry = _rms(y_ref[...].astype(F32), qg)
        dn = 0.5 * dxo
        dqg_ref[...] += _colsum(dn * ny)
        dyb = _rms_bwd(ny, ry, qg, dn).astype(BF16)
        dy_ref[...] = dyb
        pg = pg_ref[...]
        h, nx, rx = _rms(x_ref[...], pg)
        hb_ref[...] = h.astype(BF16)
        dh = jnp.zeros((tm, D), F32)
        for c0, cw in FF_CHUNKS:
            ds = _dot_nt(dyb, w2_ref[c0:c0 + cw, :])
            g = a_ref[:, c0:c0 + cw].astype(F32)
            u = a_ref[:, FF + c0:FF + c0 + cw].astype(F32)
            sg = _sigmoid(g)
            si = g * sg
            s_ref[:, c0:c0 + cw] = (si * u).astype(BF16)
            dg = (ds * u * (sg * (1.0 + g * (1.0 - sg)))).astype(BF16)
            du = (ds * si).astype(BF16)
            da_ref[:, c0:c0 + cw] = dg
            da_ref[:, FF + c0:FF + c0 + cw] = du
            dh = dh + _dot_nt(dg, w1_ref[:, c0:c0 + cw]) + _dot_nt(du, w1_ref[:, FF + c0:FF + c0 + cw])
        dpg_ref[...] += _colsum(dh * nx)
        dx_ref[...] = dxo + _rms_bwd(nx, rx, pg, dh)

    return pl.pallas_call(
        body, grid=(T // tm,),
        in_specs=[_row_spec(tm, D), _row_spec(tm, D), _row_spec(tm, D), _row_spec(tm, 2 * FF),
                  _full_spec((1, D)), _full_spec((1, D)), _layer_spec(w1, l), _layer_spec(w2, l)],
        out_specs=[_row_spec(tm, D), _row_spec(tm, 2 * FF), _row_spec(tm, FF), _row_spec(tm, D), _row_spec(tm, D),
                   _full_spec((1, D)), _full_spec((1, D))],
        out_shape=[jax.ShapeDtypeStruct((T, D), F32), jax.ShapeDtypeStruct((T, 2 * FF), BF16),
                   jax.ShapeDtypeStruct((T, FF), BF16), jax.ShapeDtypeStruct((T, D), BF16),
                   jax.ShapeDtypeStruct((T, D), BF16), jax.ShapeDtypeStruct((1, D), F32),
                   jax.ShapeDtypeStruct((1, D), F32)],
        name=name, compiler_params=_params(1),
    )(dxo, x, y, a, pre_g, post_g, w1, w2)


def mix_in_fwd(x, pre_g, win, l):
    T = x.shape[0]
    tm = min(TM_FFN_FWD, T)

    def body(x_ref, pg_ref, w_ref, z_ref):
        h, _, _ = _rms(x_ref[...], pg_ref[...])
        hb = h.astype(BF16)
        for c0, cw in WIN_CHUNKS:
            z_ref[:, c0:c0 + cw] = _dot(hb, w_ref[:, c0:c0 + cw]).astype(BF16)

    return pl.pallas_call(
        body, grid=(T // tm,),
        in_specs=[_row_spec(tm, D), _full_spec((1, D)), _layer_spec(win, l)],
        out_specs=_row_spec(tm, IN_WIDTH),
        out_shape=jax.ShapeDtypeStruct((T, IN_WIDTH), BF16),
        name="mix_in_fwd", compiler_params=_params(1),
    )(x, pre_g, win)


STACK = Q_PER_KV // 2
SROWS = STACK * BLK


def _band_mask(first):
    qi = lax.broadcasted_iota(jnp.int32, (SROWS, 2 * BLK), 0) & (BLK - 1)
    kj = lax.broadcasted_iota(jnp.int32, (SROWS, 2 * BLK), 1)
    band = (kj > qi) & (kj <= qi + BLK)
    if first is None:
        return band
    return band & (kj >= BLK * first.astype(jnp.int32))


def _lane_half(rows):
    return lax.broadcasted_iota(jnp.int32, (rows, BLK), 1) // HEAD_DIM


def _stack(ref, b, g):
    return jnp.concatenate([ref[b * BLK:(b + 1) * BLK, (STACK * g + j) * BLK:(STACK * g + j + 1) * BLK]
                            for j in range(STACK)], axis=0)


def _placed(pair, g, hp, fill):
    src = pair if hp == g else pltpu.roll(pair, HEAD_DIM, 1)
    return jnp.where(_lane_half(2 * BLK) == hp, src, fill).astype(BF16)


def _sink_column(sink_ref, g, hp):
    rb = lax.broadcasted_iota(jnp.int32, (SROWS, 1), 0) // BLK
    col = jnp.full((SROWS, 1), sink_ref[Q_PER_KV * g + hp], F32)
    for j in range(1, STACK):
        col = jnp.where(rb == j, sink_ref[Q_PER_KV * g + 2 * j + hp], col)
    return col


def _attn_scores(qs, kz, sink_col, mask):
    s = jnp.where(mask, _dot_nt(qs, kz) * ATTN_SCALE, MASK_VALUE)
    m = jnp.maximum(jnp.max(s, axis=-1, keepdims=True), sink_col)
    return jnp.exp(s - m), jnp.exp(sink_col - m)


def _kv_specs(tq, nb):
    kv_blk = ZKV // (2 * BLK)
    return [pl.BlockSpec((tq, 2 * BLK), lambda i: (i, kv_blk)),
            pl.BlockSpec((BLK, 2 * BLK), lambda i: (jnp.maximum(i * nb - 1, 0), kv_blk))]


def attn_fwd(z, sinks):
    T = z.shape[0]
    tq = min(TQ_ATTN, T)
    nb = tq // BLK

    def body(sink_ref, q_ref, kv_ref, kvh_ref, o_ref):
        i = pl.program_id(0)
        low = _lane_half(SROWS) == 0
        for b in range(nb):
            kvp = kvh_ref[...] if b == 0 else kv_ref[(b - 1) * BLK:b * BLK, :]
            kv2 = jnp.concatenate([kvp, kv_ref[b * BLK:(b + 1) * BLK, :]], axis=0).astype(F32)
            mask = _band_mask(i == 0) if b == 0 else _band_mask(None)
            for g in range(N_KV_HEADS):
                qs = _stack(q_ref, b, g)
                r, e = [], []
                for hp in range(2):
                    p, e_sink = _attn_scores(qs, _placed(kv2[:, 0:BLK], g, hp, 0.0), _sink_column(sink_ref, g, hp), mask)
                    r.append(_dot(p.astype(BF16), _placed(kv2[:, BLK:2 * BLK], g, hp, 1.0)))
                    e.append(e_sink)
                den = pltpu.roll(jnp.where(low, r[1], r[0]), HEAD_DIM, 1) + jnp.where(low, e[0], e[1])
                out = (jnp.where(low, r[0], r[1]) * (1.0 / den)).astype(BF16)
                for j in range(STACK):
                    o_ref[b * BLK:(b + 1) * BLK, (STACK * g + j) * BLK:(STACK * g + j + 1) * BLK] = \
                        out[j * BLK:(j + 1) * BLK, :]

    return pl.pallas_call(
        body, grid=(T // tq,),
        in_specs=[pl.BlockSpec(memory_space=pltpu.SMEM), _row_spec(tq, D)] + _kv_specs(tq, nb),
        out_specs=_row_spec(tq, D),
        out_shape=jax.ShapeDtypeStruct((T, D), BF16),
        name="attn_fwd", compiler_params=_params(1),
    )(sinks, z, z, z)


def attn_bwd(z, ya, dya, dzb, sinks):
    T = z.shape[0]
    tq = min(TQ_ATTN, T)
    nb = tq // BLK
    nt = T // tq
    mid = ZKV - ZU

    def body(sink_ref, q_ref, kv_ref, kvh_ref, y_ref, dy_ref, dzb_ref, dz_ref, dsink_ref, acc_ref, carry_ref):
        i = pl.program_id(0)
        first_tile = i == nt - 1

        @pl.when(i == 0)
        def _():
            carry_ref[...] = jnp.zeros_like(carry_ref)
            dsink_ref[...] = jnp.zeros_like(dsink_ref)

        acc_ref[...] = jnp.zeros_like(acc_ref)
        lane = lax.broadcasted_iota(jnp.int32, (1, BLK), 1)
        dsink = jnp.zeros((1, BLK), F32)
        half = _lane_half(SROWS)
        low_kv = _lane_half(2 * BLK) == 0
        for b in range(nb):
            kvp = kvh_ref[...] if b == 0 else kv_ref[(b - 1) * BLK:b * BLK, :]
            kv2 = jnp.concatenate([kvp, kv_ref[b * BLK:(b + 1) * BLK, :]], axis=0).astype(F32)
            mask = _band_mask(first_tile) if b == 0 else _band_mask(None)
            dk_groups, dv_groups = [], []
            for g in range(N_KV_HEADS):
                qs = _stack(q_ref, b, g)
                dys = _stack(dy_ref, b, g)
                dyy = dys.astype(F32) * _stack(y_ref, b, g).astype(F32)
                dq = jnp.zeros((SROWS, BLK), F32)
                dk_raw, dv_raw = [], []
                for hp in range(2):
                    kz = _placed(kv2[:, 0:BLK], g, hp, 0.0)
                    p, e_sink = _attn_scores(qs, kz, _sink_column(sink_ref, g, hp), mask)
                    inv = 1.0 / (jnp.sum(p, axis=-1, keepdims=True) + e_sink)
                    pn = p * inv
                    delta = jnp.sum(jnp.where(half == hp, dyy, 0.0), axis=-1, keepdims=True)
                    dp = _dot_nt(dys, _placed(kv2[:, BLK:2 * BLK], g, hp, 0.0))
                    ds = (pn * (dp - delta)).astype(BF16)
                    sink_term = e_sink * inv * delta
                    for j in range(STACK):
                        dsink = dsink + jnp.where(lane == Q_PER_KV * g + 2 * j + hp,
                                                  -_colsum(sink_term[j * BLK:(j + 1) * BLK, :]), 0.0)
                    dq = dq + _dot(ds, kz)
                    dk_raw.append(_dot_tn(ds, qs))
                    dv_raw.append(_dot_tn(pn.astype(BF16), dys))
                tk = jnp.where(low_kv, dk_raw[0], dk_raw[1])
                tv = jnp.where(low_kv, dv_raw[0], dv_raw[1])
                dk_groups.append(tk + pltpu.roll(tk, HEAD_DIM, 1))
                dv_groups.append(tv + pltpu.roll(tv, HEAD_DIM, 1))
                dqb = (dq * ATTN_SCALE).astype(BF16)
                for j in range(STACK):
                    dz_ref[b * BLK:(b + 1) * BLK, ZQ + (STACK * g + j) * BLK:ZQ + (STACK * g + j + 1) * BLK] = \
                        dqb[j * BLK:(j + 1) * BLK, :]
            acc_ref[b * BLK:(b + 2) * BLK, 0:BLK] += jnp.where(low_kv, dk_groups[0], dk_groups[1]) * ATTN_SCALE
            acc_ref[b * BLK:(b + 2) * BLK, BLK:2 * BLK] += jnp.where(low_kv, dv_groups[0], dv_groups[1])
        dsink_ref[...] += dsink
        dz_ref[:, ZU:ZKV] = dzb_ref[...]
        if nb > 1:
            dz_ref[0:tq - BLK, ZKV:IN_WIDTH] = acc_ref[BLK:tq, :].astype(BF16)
        dz_ref[tq - BLK:tq, ZKV:IN_WIDTH] = (acc_ref[tq:tq + BLK, :] + carry_ref[...]).astype(BF16)
        carry_ref[...] = acc_ref[0:BLK, :]

    kv_blk = ZKV // (2 * BLK)
    return pl.pallas_call(
        body, grid=(nt,),
        in_specs=[pl.BlockSpec(memory_space=pltpu.SMEM),
                  pl.BlockSpec((tq, D), lambda i: (nt - 1 - i, 0)),
                  pl.BlockSpec((tq, 2 * BLK), lambda i: (nt - 1 - i, kv_blk)),
                  pl.BlockSpec((BLK, 2 * BLK), lambda i: (jnp.maximum((nt - 1 - i) * nb - 1, 0), kv_blk)),
                  pl.BlockSpec((tq, D), lambda i: (nt - 1 - i, 0)),
                  pl.BlockSpec((tq, D), lambda i: (nt - 1 - i, 0)),
                  pl.BlockSpec((tq, mid), lambda i: (nt - 1 - i, 0))],
        out_specs=[pl.BlockSpec((tq, IN_WIDTH), lambda i: (nt - 1 - i, 0)), _full_spec((1, BLK))],
        out_shape=[jax.ShapeDtypeStruct((T, IN_WIDTH), BF16), jax.ShapeDtypeStruct((1, BLK), F32)],
        scratch_shapes=[pltpu.VMEM((tq + BLK, 2 * BLK), F32), pltpu.VMEM((BLK, 2 * BLK), F32)],
        name="attn_bwd", compiler_params=_params(1),
    )(sinks, z, z, z, ya, dya, dzb)


def _to_group_lanes(v, g, nch):
    return jnp.concatenate([v[n * BLK:(n + 1) * BLK, g * BLK:(g + 1) * BLK] for n in range(nch)], axis=1)


def _from_group_lanes(per_group, nch):
    rows = [jnp.concatenate([per_group[g][:, n * BLK:(n + 1) * BLK] for g in range(SGU_GROUPS)], axis=1)
            for n in range(nch)]
    return jnp.concatenate(rows, axis=0)


def _tril_bf16(w_ref, g):
    t = lax.broadcasted_iota(jnp.int32, (BLK, BLK), 0)
    s = lax.broadcasted_iota(jnp.int32, (BLK, BLK), 1)
    return jnp.where(t >= s, w_ref[g], 0.0).astype(BF16)


def _sgu_norm(v_s, ln_g, ln_b):
    cdf, pdf = _gelu_parts(v_s)
    gv = v_s * cdf
    xc = gv - jnp.mean(gv, axis=-1, keepdims=True)
    rstd = lax.rsqrt(jnp.mean(xc * xc, axis=-1, keepdims=True) + LN_EPS)
    nhat = xc * rstd
    return nhat * ln_g + ln_b, nhat, rstd, cdf + v_s * pdf


def _sgu_gate(vn, w_ref, bb_ref, nch):
    vnb = vn.astype(BF16)
    return _from_group_lanes(
        [_dot(_tril_bf16(w_ref, g), _to_group_lanes(vnb, g, nch)) + jnp.tile(bb_ref[g], (1, nch))
         for g in range(SGU_GROUPS)], nch)


def mix_fwd_out(x, z, ya, post_g, ln_g, ln_b, sgu_w, sgu_bb, wa, ws, wo, l):
    T = x.shape[0]
    tm = min(TM_MIX, T)
    nch = tm // BLK

    def body(x_ref, us_ref, vs_ref, ga_ref, gb_ref, ya_ref, qg_ref, lg_ref, lb_ref, w_ref, bb_ref,
             wa_ref, ws_ref, wo_ref, xo_ref, ysg_ref, pa_ref, pb_ref, o_ref):
        vn, _, _, _ = _sgu_norm(vs_ref[...].astype(F32), lg_ref[...], lb_ref[...])
        gate = _sgu_gate(vn, w_ref, bb_ref, nch)
        us = us_ref[...].astype(F32)
        cdf, _ = _gelu_parts(us)
        ysg = (us * cdf * gate).astype(BF16)
        ysg_ref[...] = ysg
        pa = _dot(ya_ref[...], wa_ref[...])
        pb = _dot(ysg, ws_ref[...])
        pa_ref[...] = pa.astype(BF16)
        pb_ref[...] = pb.astype(BF16)
        merged = _sigmoid(ga_ref[...].astype(F32)) * pa + _sigmoid(gb_ref[...].astype(F32)) * pb
        o = _dot(merged.astype(BF16), wo_ref[...])
        o_ref[...] = o.astype(BF16)
        on, _, _ = _rms(o, qg_ref[...])
        xo_ref[...] = x_ref[...] + on

    zspec = lambda start: _row_spec(tm, D, start // D)
    act = jax.ShapeDtypeStruct((T, D), BF16)
    return pl.pallas_call(
        body, grid=(T // tm,),
        in_specs=[_row_spec(tm, D), zspec(ZU), zspec(ZV), zspec(ZGA), zspec(ZGB), _row_spec(tm, D),
                  _full_spec((1, D)), _full_spec((1, D)), _full_spec((1, D)),
                  _full_spec((SGU_GROUPS, BLK, BLK)), _full_spec((SGU_GROUPS, BLK, BLK)),
                  _layer_spec(wa, l), _layer_spec(ws, l), _layer_spec(wo, l)],
        out_specs=[_row_spec(tm, D)] * 5,
        out_shape=[jax.ShapeDtypeStruct((T, D), F32), act, act, act, act],
        name="mix_fwd_out", compiler_params=_params(1),
    )(x, z, z, z, z, ya, post_g, ln_g, ln_b, sgu_w, sgu_bb, wa, ws, wo)


def mix_bwd_out(dxo, z, o, pa, pb, post_g, ln_g, ln_b, sgu_w, sgu_bb, wa, ws, wo, l):
    T = dxo.shape[0]
    tm = min(TM_MIX_BWD, T)
    nch = tm // BLK
    mid = ZKV - ZU

    def body(dxo_ref, us_ref, vs_ref, ga_ref, gb_ref, o_ref, pa_ref, pb_ref, qg_ref, lg_ref, lb_ref, w_ref, bb_ref,
             wa_ref, ws_ref, wo_ref,
             dzb_ref, dya_ref, mg_ref, do_ref, dpa_ref, dpb_ref, dqg_ref, dlg_ref, dlb_ref, dw_ref, dbb_ref):
        @pl.when(pl.program_id(0) == 0)
        def _():
            for r in (dqg_ref, dlg_ref, dlb_ref, dw_ref, dbb_ref):
                r[...] = jnp.zeros_like(r)

        qg = qg_ref[...]
        dxo = dxo_ref[...]
        _, no, ro = _rms(o_ref[...].astype(F32), qg)
        dqg_ref[...] += _colsum(dxo * no)
        dob = _rms_bwd(no, ro, qg, dxo).astype(BF16)
        do_ref[...] = dob
        dmerged = _dot_nt(dob, wo_ref[...])
        sa = _sigmoid(ga_ref[...].astype(F32))
        sb = _sigmoid(gb_ref[...].astype(F32))
        pa = pa_ref[...].astype(F32)
        pb = pb_ref[...].astype(F32)
        mg_ref[...] = (sa * pa + sb * pb).astype(BF16)
        dpa = (dmerged * sa).astype(BF16)
        dpb = (dmerged * sb).astype(BF16)
        dpa_ref[...] = dpa
        dpb_ref[...] = dpb
        dzb_ref[:, ZGA - ZU:ZGA - ZU + D] = (dmerged * pa * sa * (1.0 - sa)).astype(BF16)
        dzb_ref[:, ZGB - ZU:ZGB - ZU + D] = (dmerged * pb * sb * (1.0 - sb)).astype(BF16)
        dya_ref[...] = _dot_nt(dpa, wa_ref[...]).astype(BF16)
        dysg = _dot_nt(dpb, ws_ref[...])

        lg = lg_ref[...]
        vn, nhat, rstd, dgelu_v = _sgu_norm(vs_ref[...].astype(F32), lg, lb_ref[...])
        gate = _sgu_gate(vn, w_ref, bb_ref, nch)
        us = us_ref[...].astype(F32)
        cdf, pdf = _gelu_parts(us)
        dzb_ref[:, 0:D] = (dysg * gate * (cdf + us * pdf)).astype(BF16)
        dgate = (dysg * (us * cdf)).astype(BF16)
        vnb = vn.astype(BF16)
        t = lax.broadcasted_iota(jnp.int32, (BLK, BLK), 0)
        s = lax.broadcasted_iota(jnp.int32, (BLK, BLK), 1)
        dvn_groups = []
        for g in range(SGU_GROUPS):
            dgl = _to_group_lanes(dgate, g, nch)
            dbb_ref[g] += jnp.broadcast_to(jnp.sum(dgl.astype(F32), axis=-1, keepdims=True), (BLK, BLK))
            dw_ref[g] += jnp.where(t >= s, _dot_nt(dgl, _to_group_lanes(vnb, g, nch)), 0.0)
            dvn_groups.append(_dot_tn(_tril_bf16(w_ref, g), dgl))
        dvn = _from_group_lanes(dvn_groups, nch)
        dlg_ref[...] += _colsum(dvn * nhat)
        dlb_ref[...] += _colsum(dvn)
        dnh = dvn * lg
        dgv = rstd * (dnh - jnp.mean(dnh, axis=-1, keepdims=True) - nhat * jnp.mean(dnh * nhat, axis=-1, keepdims=True))
        dzb_ref[:, ZV - ZU:ZV - ZU + D] = (dgv * dgelu_v).astype(BF16)

    zspec = lambda start: _row_spec(tm, D, start // D)
    act = jax.ShapeDtypeStruct((T, D), BF16)
    grp = jax.ShapeDtypeStruct((SGU_GROUPS, BLK, BLK), F32)
    vec = jax.ShapeDtypeStruct((1, D), F32)
    return pl.pallas_call(
        body, grid=(T // tm,),
        in_specs=[_row_spec(tm, D), zspec(ZU), zspec(ZV), zspec(ZGA), zspec(ZGB),
                  _row_spec(tm, D), _row_spec(tm, D), _row_spec(tm, D),
                  _full_spec((1, D)), _full_spec((1, D)), _full_spec((1, D)),
                  _full_spec((SGU_GROUPS, BLK, BLK)), _full_spec((SGU_GROUPS, BLK, BLK)),
                  _layer_spec(wa, l), _layer_spec(ws, l), _layer_spec(wo, l)],
        out_specs=[_row_spec(tm, mid)] + [_row_spec(tm, D)] * 5 + [_full_spec((1, D))] * 3
                  + [_full_spec((SGU_GROUPS, BLK, BLK))] * 2,
        out_shape=[jax.ShapeDtypeStruct((T, mid), BF16), act, act, act, act, act, vec, vec, vec, grp, grp],
        name="mix_bwd_out", compiler_params=_params(1),
    )(dxo, z, z, z, z, o, pa, pb, post_g, ln_g, ln_b, sgu_w, sgu_bb, wa, ws, wo)


def mix_in_bwd(dxo, x, dz, pre_g, win, l):
    T = x.shape[0]
    tm = min(TM_MIX, T)

    def body(dxo_ref, x_ref, dz_ref, pg_ref, w_ref, dx_ref, hb_ref, dpg_ref):
        @pl.when(pl.program_id(0) == 0)
        def _():
            dpg_ref[...] = jnp.zeros_like(dpg_ref)

        pg = pg_ref[...]
        h, nx, rx = _rms(x_ref[...], pg)
        hb_ref[...] = h.astype(BF16)
        dh = jnp.zeros((tm, D), F32)
        for c0, cw in WIN_CHUNKS:
            dh = dh + _dot_nt(dz_ref[:, c0:c0 + cw], w_ref[:, c0:c0 + cw])
        dpg_ref[...] += _colsum(dh * nx)
        dx_ref[...] = dxo_ref[...] + _rms_bwd(nx, rx, pg, dh)

    return pl.pallas_call(
        body, grid=(T // tm,),
        in_specs=[_row_spec(tm, D), _row_spec(tm, D), _row_spec(tm, IN_WIDTH), _full_spec((1, D)), _layer_spec(win, l)],
        out_specs=[_row_spec(tm, D), _row_spec(tm, D), _full_spec((1, D))],
        out_shape=[jax.ShapeDtypeStruct((T, D), F32), jax.ShapeDtypeStruct((T, D), BF16),
                   jax.ShapeDtypeStruct((1, D), F32)],
        name="mix_in_bwd", compiler_params=_params(1),
    )(dxo, x, dz, pre_g, win)


def wgrad_cols(a, b, pieces, shard_width, name, tmo=512):
    T, M = a.shape
    N = b.shape[1]
    tk = min(TK_WGRAD, T)
    nk = T // tk

    def body(a_ref, b_ref, o_ref, acc_ref):
        k = pl.program_id(1)

        @pl.when(k == 0)
        def _():
            acc_ref[...] = jnp.zeros_like(acc_ref)

        acc_ref[...] += _dot_tn(a_ref[...], b_ref[...])

        @pl.when(k == nk - 1)
        def _():
            for j in range(N_DEV):
                for off, w, ps in pieces[j]:
                    o_ref[j, :, off:off + w] = acc_ref[:, ps:ps + w].astype(BF16)

    return pl.pallas_call(
        body, grid=(M // tmo, nk),
        in_specs=[pl.BlockSpec((tk, tmo), lambda m, k: (k, m)), pl.BlockSpec((tk, N), lambda m, k: (k, 0))],
        out_specs=pl.BlockSpec((N_DEV, tmo, shard_width), lambda m, k: (0, m, 0)),
        out_shape=jax.ShapeDtypeStruct((N_DEV, M, shard_width), BF16),
        scratch_shapes=[pltpu.VMEM((tmo, N), F32)],
        name=name, compiler_params=_params(2),
    )(a, b)


def wgrad_rows(a, b, name, tmo):
    T, M = a.shape
    N = b.shape[1]
    tk = min(TK_WGRAD, T)
    nk = T // tk

    def body(a_ref, b_ref, o_ref, acc_ref):
        k = pl.program_id(1)

        @pl.when(k == 0)
        def _():
            acc_ref[...] = jnp.zeros_like(acc_ref)

        acc_ref[...] += _dot_tn(a_ref[...], b_ref[...])

        @pl.when(k == nk - 1)
        def _():
            o_ref[...] = acc_ref[...].astype(BF16)

    return pl.pallas_call(
        body, grid=(M // tmo, nk),
        in_specs=[pl.BlockSpec((tk, tmo), lambda m, k: (k, m)), pl.BlockSpec((tk, N), lambda m, k: (k, 0))],
        out_specs=pl.BlockSpec((tmo, N), lambda m, k: (m, 0)),
        out_shape=jax.ShapeDtypeStruct((M, N), BF16),
        scratch_shapes=[pltpu.VMEM((tmo, N), F32)],
        name=name, compiler_params=_params(2),
    )(a, b)


def loss_head(y, target):
    T = y.shape[0]
    tm = min(TM_FFN_FWD, T)

    def body(y_ref, t_ref, sq_ref, dy_ref):
        @pl.when(pl.program_id(0) == 0)
        def _():
            sq_ref[...] = jnp.zeros_like(sq_ref)

        err = y_ref[...] - t_ref[...]
        sq_ref[...] += _colsum(err * err)
        dy_ref[...] = err * (1.0 / D)

    return pl.pallas_call(
        body, grid=(T // tm,), in_specs=[_row_spec(tm, D), _row_spec(tm, D)],
        out_specs=[_full_spec((1, D)), _row_spec(tm, D)],
        out_shape=[jax.ShapeDtypeStruct((1, D), F32), jax.ShapeDtypeStruct((T, D), F32)],
        name="loss_head", compiler_params=_params(1),
    )(y, target)


def adamw_sum(parts, w, m, v, name):
    rows, cols = w.shape
    tr = _row_tile(rows, TR_ADAM)
    c1 =1.0 - ADAM_B1 ** ADAM_STEP
    c2 = 1.0 - ADAM_B2 ** ADAM_STEP

    def body(p_ref, w_ref, m_ref, v_ref, g_ref, d_ref, nm_ref, nv_ref):
        g = p_ref[0].astype(F32)
        for j in range(1, N_DEV):
            g = g + p_ref[j].astype(F32)
        nm = ADAM_B1 * m_ref[...] + (1.0 - ADAM_B1) * g
        nv = ADAM_B2 * v_ref[...] + (1.0 - ADAM_B2) * (g * g)
        g_ref[...] = g
        nm_ref[...] = nm
        nv_ref[...] = nv
        d_ref[...] = -ADAM_LR * ((nm / c1) / (jnp.sqrt(nv / c2) + ADAM_EPS) + ADAM_WD * w_ref[...])

    spec = _row_spec(tr, cols)
    out = jax.ShapeDtypeStruct((rows, cols), F32)
    return pl.pallas_call(
        body, grid=(rows // tr,),
        in_specs=[pl.BlockSpec((N_DEV, tr, cols), lambda i: (0, i, 0)), spec, spec, spec],
        out_specs=[spec] * 4, out_shape=[out] * 4, name=name, compiler_params=_params(1),
    )(parts, w, m, v)


def _pack_small(p):
    layers = []
    for l in range(DEPTH):
        rows = [p[n][l].reshape(1, D) for n in SMALL_VEC]
        rows.append(p["sgu_b"][l].reshape(1, D))
        rows.append(jnp.pad(p["attn_sinks"][l].reshape(1, N_Q_HEADS), ((0, 0), (0, D - N_Q_HEADS))))
        rows.append(p["sgu_w"][l].reshape(BLK, D))
        used = len(SMALL_VEC) + 2 + BLK
        rows.append(jnp.zeros((SMALL_ROWS - used, D), F32))
        layers.append(jnp.concatenate(rows, axis=0))
    return jnp.concatenate(layers, axis=0)


def _unpack_small(packed):
    a = packed.reshape(DEPTH, SMALL_ROWS, D)
    out = {n: a[:, i, :] for i, n in enumerate(SMALL_VEC)}
    k = len(SMALL_VEC)
    out["sgu_b"] = a[:, k, :].reshape(DEPTH, SGU_GROUPS, BLK)
    out["attn_sinks"] = a[:, k + 1, :N_Q_HEADS]
    out["sgu_w"] = a[:, k + 2:k + 2 + BLK, :].reshape(DEPTH, SGU_GROUPS, BLK, BLK)
    return out


WEIGHT_NAMES = ("ffn1_pre_g", "ffn1_w1", "ffn1_w2", "ffn1_post_g", "mix_pre_g", "w_in", "attn_sinks", "sgu_ln_g",
                "sgu_ln_b", "sgu_w", "sgu_b", "w_attn_branch", "w_sgu_branch", "w_out", "mix_post_g", "ffn2_pre_g",
                "ffn2_w1", "ffn2_w2", "ffn2_post_g")
COL_SHARDED = ("ffn1_w1", "ffn2_w1", "w_in")
ROW_SHARDED = ("ffn1_w2", "ffn2_w2", "w_attn_branch", "w_sgu_branch", "w_out")


def kernel(x, ffn1_pre_g, ffn1_w1, ffn1_w2, ffn1_post_g, mix_pre_g, w_in, attn_sinks, sgu_ln_g, sgu_ln_b, sgu_w, sgu_b, w_attn_branch, w_sgu_branch, w_out, mix_post_g, ffn2_pre_g, ffn2_w1, ffn2_w2, ffn2_post_g, loss_target, m_ffn1_pre_g, m_ffn1_w1, m_ffn1_w2, m_ffn1_post_g, m_mix_pre_g, m_w_in, m_attn_sinks, m_sgu_ln_g, m_sgu_ln_b, m_sgu_w, m_sgu_b, m_w_attn_branch, m_w_sgu_branch, m_w_out, m_mix_post_g, m_ffn2_pre_g, m_ffn2_w1, m_ffn2_w2, m_ffn2_post_g, v_ffn1_pre_g, v_ffn1_w1, v_ffn1_w2, v_ffn1_post_g, v_mix_pre_g, v_w_in, v_attn_sinks, v_sgu_ln_g, v_sgu_ln_b, v_sgu_w, v_sgu_b, v_w_attn_branch, v_w_sgu_branch, v_w_out, v_mix_post_g, v_ffn2_pre_g, v_ffn2_w1, v_ffn2_w2, v_ffn2_post_g):
    w = dict(zip(WEIGHT_NAMES, (ffn1_pre_g, ffn1_w1, ffn1_w2, ffn1_post_g, mix_pre_g, w_in, attn_sinks, sgu_ln_g,
                                sgu_ln_b, sgu_w, sgu_b, w_attn_branch, w_sgu_branch, w_out, mix_post_g, ffn2_pre_g,
                                ffn2_w1, ffn2_w2, ffn2_post_g)))
    mom = dict(zip(WEIGHT_NAMES, (m_ffn1_pre_g, m_ffn1_w1, m_ffn1_w2, m_ffn1_post_g, m_mix_pre_g, m_w_in,
                                  m_attn_sinks, m_sgu_ln_g, m_sgu_ln_b, m_sgu_w, m_sgu_b, m_w_attn_branch,
                                  m_w_sgu_branch, m_w_out, m_mix_post_g, m_ffn2_pre_g, m_ffn2_w1, m_ffn2_w2,
                                  m_ffn2_post_g)))
    var = dict(zip(WEIGHT_NAMES, (v_ffn1_pre_g, v_ffn1_w1, v_ffn1_w2, v_ffn1_post_g, v_mix_pre_g, v_w_in,
                                  v_attn_sinks, v_sgu_ln_g, v_sgu_ln_b, v_sgu_w, v_sgu_b, v_w_attn_branch,
                                  v_w_sgu_branch, v_w_out, v_mix_post_g, v_ffn2_pre_g, v_ffn2_w1, v_ffn2_w2,
                                  v_ffn2_post_g)))
    T = x.shape[1]
    xs = x.reshape(T, D)
    target = loss_target.reshape(T, D)

    wc = cast_cols(ffn1_w1.reshape(DEPTH * D, W1_SHARD), ffn2_w1.reshape(DEPTH * D, W1_SHARD),
                   w_in.reshape(DEPTH * D, WIN_SHARD)).reshape(DEPTH, D, WC_WIDTH)
    row_bf16 = [cast_rows(w[n].reshape(-1, D)).reshape(w[n].shape) for n in ROW_SHARDED]
    gathered = all_gather_weights([wc], row_bf16)
    w1a, w1b, win = relayout_cols(gathered[0])
    w2a, w2b, wat, wsg, wou = [g.reshape(DEPTH, -1, D) for g in gathered[1:]]

    vec = lambda n, l: w[n][l].reshape(1, D)
    sgu_bb = [jnp.broadcast_to(w["sgu_b"][l][:, :, None], (SGU_GROUPS, BLK, BLK)) for l in range(DEPTH)]

    saved = []
    h = xs
    for l in range(DEPTH):
        x0 = h
        x1, a1, y1 = ffn_fwd(x0, vec("ffn1_pre_g", l), vec("ffn1_post_g", l), w1a, w2a, l, "ffn1_fwd")
        z = mix_in_fwd(x1, vec("mix_pre_g", l), win, l)
        ya = attn_fwd(z, w["attn_sinks"][l])
        x2, ysg, pa, pb, o = mix_fwd_out(x1, z, ya, vec("mix_post_g", l), vec("sgu_ln_g", l), vec("sgu_ln_b", l),
                                         w["sgu_w"][l], sgu_bb[l], wat, wsg, wou, l)
        h, a2, y2 = ffn_fwd(x2, vec("ffn2_pre_g", l), vec("ffn2_post_g", l), w1b, w2b, l, "ffn2_fwd")
        saved.append((x0, a1, y1, x1, z, ya, ysg, pa, pb, o, x2, a2, y2))

    sq, dx = loss_head(h, target)
    loss = lax.psum(0.5 / D * jnp.sum(sq), ("x", "y", "c"))

    col_parts = {n: [None] * DEPTH for n in COL_SHARDED}
    row_parts = {n: [None] * DEPTH for n in ROW_SHARDED}
    small = {n: [None] * DEPTH for n in WEIGHT_NAMES if n not in COL_SHARDED + ROW_SHARDED}
    for l in reversed(range(DEPTH)):
        x0, a1, y1, x1, z, ya, ysg, pa, pb, o, x2, a2, y2 = saved[l]
        dx, da, s, dy, hb, dpg, dqg = ffn_bwd(dx, x2, y2, a2, vec("ffn2_pre_g", l), vec("ffn2_post_g", l),
                                              w1b, w2b, l, "ffn2_bwd")
        small["ffn2_pre_g"][l], small["ffn2_post_g"][l] = dpg, dqg
        col_parts["ffn2_w1"][l] = wgrad_cols(hb, da, W1_PIECES, W1_SHARD, "wgrad_ffn_w1")
        row_parts["ffn2_w2"][l] = wgrad_rows(s, dy, "wgrad_ffn_w2", FF // 2)

        dzb, dya, mg, dob, dpa, dpb, dqg, dlg, dlb, dsw, dsb = mix_bwd_out(
            dx, z, o, pa, pb, vec("mix_post_g", l), vec("sgu_ln_g", l), vec("sgu_ln_b", l), w["sgu_w"][l], sgu_bb[l],
            wat, wsg, wou, l)
        dz, dsink = attn_bwd(z, ya, dya, dzb, w["attn_sinks"][l])
        dx, hb, dpg = mix_in_bwd(dx, x1, dz, vec("mix_pre_g", l), win, l)
        small["mix_pre_g"][l], small["mix_post_g"][l] = dpg, dqg
        small["sgu_ln_g"][l], small["sgu_ln_b"][l] = dlg, dlb
        small["sgu_w"][l], small["sgu_b"][l] = dsw, dsb[:, :, 0]
        small["attn_sinks"][l] = dsink[0, :N_Q_HEADS]
        col_parts["w_in"][l] = wgrad_cols(hb, dz, WIN_PIECES, WIN_SHARD, "wgrad_w_in")
        row_parts["w_attn_branch"][l] = wgrad_rows(ya, dpa, "wgrad_square", D)
        row_parts["w_sgu_branch"][l] = wgrad_rows(ysg, dpb, "wgrad_square", D)
        row_parts["w_out"][l] = wgrad_rows(mg, dob, "wgrad_square", D)

        dx, da, s, dy, hb, dpg, dqg = ffn_bwd(dx, x0, y1, a1, vec("ffn1_pre_g", l), vec("ffn1_post_g", l),
                                              w1a, w2a, l, "ffn1_bwd")
        small["ffn1_pre_g"][l], small["ffn1_post_g"][l] = dpg, dqg
        col_parts["ffn1_w1"][l] = wgrad_cols(hb, da, W1_PIECES, W1_SHARD, "wgrad_ffn_w1")
        row_parts["ffn1_w2"][l] = wgrad_rows(s, dy, "wgrad_ffn_w2", FF // 2)
    grad_x = dx.reshape(x.shape)

    col_send = [jnp.stack(col_parts[n], axis=1) for n in COL_SHARDED]
    row_send = [jnp.stack(row_parts[n], axis=0).reshape(DEPTH, N_DEV, -1, D) for n in ROW_SHARDED]
    small_send = _pack_small({n: jnp.stack(v, axis=0) for n, v in small.items()})
    received = exchange_grads(col_send, row_send, small_send)

    grads, deltas, new_m, new_v = {}, {}, {}, {}
    for n, parts in zip(COL_SHARDED + ROW_SHARDED, received[:-1]):
        shape = w[n].shape
        cols = shape[-1]
        res = adamw_sum(parts.reshape(N_DEV, -1, cols), w[n].reshape(-1, cols), mom[n].reshape(-1, cols),
                        var[n].reshape(-1, cols), "adamw_" + n)
        grads[n], deltas[n], new_m[n], new_v[n] = [r.reshape(shape) for r in res]
    res = adamw_sum(received[-1], _pack_small(w), _pack_small(mom), _pack_small(var), "adamw_small")
    for out, packed in zip((grads, deltas, new_m, new_v), res):
        out.update(_unpack_small(packed))

    return (loss, grad_x, *[grads[n] for n in WEIGHT_NAMES], *[deltas[n] for n in WEIGHT_NAMES],
            *[new_m[n] for n in WEIGHT_NAMES], *[new_v[n] for n in WEIGHT_NAMES])
```

```python
import math

import jax
import jax.numpy as jnp
from jax import lax
from jax.experimental import pallas as pl
from jax.experimental.pallas import tpu as pltpu

F32 = jnp.float32
BF16 = jnp.bfloat16

N_DEV = 8
D = 1024
FF = 2816
DEPTH = 4
HEAD_DIM = 64
N_Q_HEADS = 16
N_KV_HEADS = 2
Q_PER_KV = N_Q_HEADS // N_KV_HEADS
BLK = 128
SGU_GROUPS = 8
IN_WIDTH = 5376
W1_SHARD = 2 * FF // N_DEV
WIN_SHARD = IN_WIDTH // N_DEV
W2_SHARD = FF // N_DEV
SQ_SHARD = D // N_DEV

RMS_EPS = 1e-6
LN_EPS = 1e-5
MASK_VALUE = -1e30
ATTN_SCALE = 1.0 / math.sqrt(HEAD_DIM)

ADAM_LR = 0.001
ADAM_B1 = 0.9
ADAM_B2 = 0.999
ADAM_EPS = 1e-08
ADAM_WD = 0.01
ADAM_STEP = 10

VMEM_LIMIT_V7X = 56 * 1024 * 1024

WIN_SEGMENTS = ((0, 1024, 0), (1024, 128, 5120), (1152, 128, 5248), (1280, 1024, 1024),
                (2304, 1024, 2048), (3328, 1024, 3072), (4352, 1024, 4096))
ZQ, ZU, ZV, ZGA, ZGB, ZKV = 0, 1024, 2048, 3072, 4096, 5120

FF_CHUNKS = ((0, 1024), (1024, 1024), (2048, 768))
WIN_CHUNKS = ((0, 1792), (1792, 1792), (3584, 1792))

SMALL_ROWS = 144
SMALL_VEC = ("ffn1_pre_g", "ffn1_post_g", "mix_pre_g", "mix_post_g", "ffn2_pre_g", "ffn2_post_g",
             "sgu_ln_g", "sgu_ln_b")

TM_FFN_FWD = 512
TM_FFN_BWD = 256
TM_MIX = 512
TM_MIX_BWD = 256
TQ_ATTN = 512
TK_WGRAD = 512
TR_ADAM = 256


def _shard_pieces(shard_width, segments):
    out = []
    for j in range(N_DEV):
        lo, hi = j * shard_width, (j + 1) * shard_width
        pieces = []
        for ns, ln, ps in segments:
            a, b = max(lo, ns), min(hi, ns + ln)
            if a < b:
                pieces.append((a - lo, b - a, ps + a - ns))
        out.append(tuple(pieces))
    return tuple(out)


W1_PIECES = _shard_pieces(W1_SHARD, ((0, 2 * FF, 0),))
WIN_PIECES = _shard_pieces(WIN_SHARD, WIN_SEGMENTS)


def _params(n_grid, vmem=VMEM_LIMIT_V7X):
    return pltpu.CompilerParams(dimension_semantics=("arbitrary",) * n_grid, vmem_limit_bytes=vmem)


def _dot(a, b):
    return jnp.dot(a, b, preferred_element_type=F32)


def _dot_nt(a, b):
    return lax.dot_general(a, b, (((1,), (1,)), ((), ())), preferred_element_type=F32)


def _dot_tn(a, b):
    return lax.dot_general(a, b, (((0,), (0,)), ((), ())), preferred_element_type=F32)


def _rms(x, g):
    r = lax.rsqrt(jnp.mean(x * x, axis=-1, keepdims=True) + RMS_EPS)
    n = x * r
    return n * g, n, r


def _rms_bwd(n, r, g, dy):
    dn = dy * g
    return r * (dn - n * jnp.mean(dn * n, axis=-1, keepdims=True))


def _colsum(v):
    return jnp.sum(v, axis=0, keepdims=True)


def _sigmoid(v):
    return 1.0 / (1.0 + jnp.exp(-v))


def _gelu_parts(v):
    cdf = 0.5 * (1.0 + lax.erf(v * (1.0 / math.sqrt(2.0))))
    return cdf, jnp.exp(-0.5 * v * v) * (1.0 / math.sqrt(2.0 * math.pi))


def _row_tile(rows, cap):
    return max(t for t in range(16, min(rows, cap) + 1, 16) if rows % t == 0)


def _row_spec(tm, width, col_block=0):
    return pl.BlockSpec((tm, width), lambda i, cb=col_block: (i, cb))


def _full_spec(shape):
    nd = len(shape)
    return pl.BlockSpec(tuple(shape), lambda *_: (0,) * nd)


def _whole_spec(arr):
    nd = arr.ndim
    return pl.BlockSpec(tuple(arr.shape), lambda *_: (0,) * nd, pipeline_mode=pl.Buffered(1))


HBM_SPEC = pl.BlockSpec(memory_space=pltpu.HBM)
MESH_ID = pl.DeviceIdType.MESH
RELATIONS = tuple((rx, ry, rc) for rx in (0, 1) for ry in (0, 1) for rc in (0, 1))[1:]


class PeerCopies:
    def __init__(self, jobs):
        self.kinds = [k for k, _ in jobs]
        self.arrays = [a for _, a in jobs]
        self.n = len(jobs)
        self.out_shape = [jax.ShapeDtypeStruct((N_DEV,) + a.shape if k == "all" else a.shape, a.dtype)
                          for k, a in jobs]
        self.scratch = [pltpu.SemaphoreType.DMA((7 * self.n,)), pltpu.SemaphoreType.DMA((7 * self.n,)),
                        pltpu.SemaphoreType.DMA((self.n,))]

    def _copies(self, ins, outs, sems):
        send_sems, recv_sems, local_sems = sems
        x, y, c = lax.axis_index("x"), lax.axis_index("y"), lax.axis_index("c")
        me = 4 * x + 2 * y + c
        src = lambda a, d: ins[a] if self.kinds[a] == "all" else ins[a].at[d]
        local = [pltpu.make_async_copy(src(a, me), outs[a].at[me], local_sems.at[a]) for a in range(self.n)]
        sends, arrivals = [], []
        for k, (rx, ry, rc) in enumerate(RELATIONS):
            tx, ty, tc = (1 - x if rx else x), (1 - y if ry else y), (1 - c if rc else c)
            peer = 4 * tx + 2 * ty + tc
            for a in range(self.n):
                pair = dict(send_sem=send_sems.at[a * 7 + k], recv_sem=recv_sems.at[a * 7 + k],
                            device_id=(tx, ty, tc), device_id_type=MESH_ID)
                sends.append(pltpu.make_async_remote_copy(src_ref=src(a, peer), dst_ref=outs[a].at[me], **pair))
                arrivals.append(pltpu.make_async_remote_copy(src_ref=src(a, me), dst_ref=outs[a].at[peer], **pair))
        return local, sends, arrivals

    def start(self, ins, outs, sems):
        local, sends, _ = self._copies(ins, outs, sems)
        for cp in local + sends:
            cp.start()

    def wait(self, ins, outs, sems):
        local, sends, arrivals = self._copies(ins, outs, sems)
        for cp in arrivals:
            cp.wait_recv()
        for cp in sends:
            cp.wait_send()
        for cp in local:
            cp.wait()


def _call(body, grid, in_specs, args, out_specs, out_shape, name, scratch=(), copies=None):
    if copies is None:
        outs = pl.pallas_call(body, grid=grid, in_specs=list(in_specs), out_specs=list(out_specs),
                              out_shape=list(out_shape), scratch_shapes=list(scratch), name=name,
                              compiler_params=_params(len(grid)))(*args)
        return outs, []
    n_in, n_out, n_scr, nc = len(in_specs), len(out_specs), len(scratch), copies.n
    last = grid[0] - 1

    def hosted(*refs):
        ins, refs = refs[:n_in], refs[n_in:]
        c_in, refs = refs[:nc], refs[nc:]
        outs, refs = refs[:n_out], refs[n_out:]
        c_out, refs = refs[:nc], refs[nc:]
        scr, sems = refs[:n_scr], refs[n_scr:]

        @pl.when(pl.program_id(0) == 0)
        def _():
            copies.start(c_in, c_out, sems)

        body(*ins, *outs, *scr)

        @pl.when(pl.program_id(0) == last)
        def _():
            copies.wait(c_in, c_out, sems)

    outs = pl.pallas_call(hosted, grid=grid, in_specs=list(in_specs) + [HBM_SPEC] * nc,
                          out_specs=list(out_specs) + [HBM_SPEC] * nc,
                          out_shape=list(out_shape) + copies.out_shape,
                          scratch_shapes=list(scratch) + copies.scratch, name=name,
                          compiler_params=_params(len(grid)))(*args, *copies.arrays)
    return outs[:n_out], outs[n_out:]


def cast_layer(w, l):
    _, rows, cols = w.shape
    tr = _row_tile(rows, 256)

    def body(w_ref, o_ref):
        o_ref[...] = w_ref[...].astype(BF16)

    return pl.pallas_call(
        body, grid=(rows // tr,), in_specs=[pl.BlockSpec((None, tr, cols), lambda i: (l, i, 0))],
        out_specs=_row_spec(tr, cols), out_shape=jax.ShapeDtypeStruct((rows, cols), BF16),
        name="cast_layer", compiler_params=_params(1),
    )(w)


def relayout_cols(g, pieces, tr=256):
    _, rows, shard = g.shape

    def body(g_ref, o_ref):
        for j in range(N_DEV):
            blk = g_ref[j]
            for off, w, ps in pieces[j]:
                o_ref[:, ps:ps + w] = blk[:, off:off + w]

    return pl.pallas_call(
        body, grid=(rows // tr,), in_specs=[pl.BlockSpec((N_DEV, tr, shard), lambda i: (0, i, 0))],
        out_specs=_row_spec(tr, N_DEV * shard), out_shape=jax.ShapeDtypeStruct((rows, N_DEV * shard), BF16),
        name="relayout_cols", compiler_params=_params(1),
    )(g)


def all_gather_weights(shards):
    n = len(shards)

    def body(*refs):
        ins, outs = refs[:n], refs[n:2 * n]
        send_sems, recv_sems, local_sems = refs[2 * n:]
        x, y, c = lax.axis_index("x"), lax.axis_index("y"), lax.axis_index("c")
        me, sibling = (x, y, c), (x, y, 1 - c)
        chips = [(1 - x, y), (x, 1 - y), (1 - x, 1 - y)]

        def slot(a, owner):
            return outs[a].at[4 * owner[0] + 2 * owner[1] + owner[2]]

        def copy(a, k, owner, to, src=None):
            return pltpu.make_async_remote_copy(
                src_ref=slot(a, owner) if src is None else src, dst_ref=slot(a, owner),
                send_sem=send_sems.at[a * 7 + k], recv_sem=recv_sems.at[a * 7 + k],
                device_id=to, device_id_type=MESH_ID)

        mine = [pltpu.make_async_copy(ins[a], slot(a, me), local_sems.at[a]) for a in range(n)]
        for cp in mine:
            cp.start()
        first = []
        for a in range(n):
            first.append(copy(a, 0, me, sibling, src=ins[a]))
            first += [copy(a, 1 + j, me, (*chip, c), src=ins[a]) for j, chip in enumerate(chips)]
        for cp in first:
            cp.start()
        passed = []
        for j, chip in enumerate(chips):
            for a in range(n):
                copy(a, 1 + j, (*chip, c), me).wait_recv()
                fwd = copy(a, 4 + j, (*chip, c), sibling)
                fwd.start()
                passed.append(fwd)
        for a in range(n):
            copy(a, 0, sibling, me).wait_recv()
            for j, chip in enumerate(chips):
                copy(a, 4 + j, (*chip, 1 - c), me).wait_recv()
        for cp in first + passed:
            cp.wait_send()
        for cp in mine:
            cp.wait()

    out_shape = [jax.ShapeDtypeStruct((N_DEV,) + s.shape, s.dtype) for s in shards]
    return pl.pallas_call(
        body, in_specs=[HBM_SPEC] * n, out_specs=[HBM_SPEC] * n, out_shape=out_shape,
        scratch_shapes=[pltpu.SemaphoreType.DMA((7 * n,)), pltpu.SemaphoreType.DMA((7 * n,)),
                        pltpu.SemaphoreType.DMA((n,))],
        name="all_gather_weights",
    )(*shards)


def exchange_last(jobs):
    copies = PeerCopies(jobs)
    n = copies.n

    def body(*refs):
        ins, outs, sems = refs[:n], refs[n:2 * n], refs[2 * n:]
        copies.start(ins, outs, sems)
        copies.wait(ins, outs, sems)

    return pl.pallas_call(
        body, in_specs=[HBM_SPEC] * n, out_specs=[HBM_SPEC] * n, out_shape=copies.out_shape,
        scratch_shapes=copies.scratch, name="exchange_last",
    )(*copies.arrays)


def ffn_fwd(x, pre_g, post_g, w1, w2, copies=None):
    T = x.shape[0]
    tm = min(TM_FFN_FWD, T)

    def body(x_ref, pg_ref, qg_ref, w1_ref, w2_ref, xo_ref, a_ref, y_ref):
        xv = x_ref[...]
        h, _, _ = _rms(xv, pg_ref[...])
        hb = h.astype(BF16)
        acc = jnp.zeros((tm, D), F32)
        for c0, cw in FF_CHUNKS:
            g = _dot(hb, w1_ref[:, c0:c0 + cw])
            u = _dot(hb, w1_ref[:, FF + c0:FF + c0 + cw])
            a_ref[:, c0:c0 + cw] = g.astype(BF16)
            a_ref[:, FF + c0:FF + c0 + cw] = u.astype(BF16)
            s = (g * _sigmoid(g) * u).astype(BF16)
            acc = acc + _dot(s, w2_ref[c0:c0 + cw, :])
        y_ref[...] = acc.astype(BF16)
        o, _, _ = _rms(acc, qg_ref[...])
        xo_ref[...] = xv + 0.5 * o

    return _call(
        body, (T // tm,),
        [_row_spec(tm, D), _full_spec((1, D)), _full_spec((1, D)), _whole_spec(w1), _whole_spec(w2)],
        (x, pre_g, post_g, w1, w2),
        [_row_spec(tm, D), _row_spec(tm, 2 * FF), _row_spec(tm, D)],
        [jax.ShapeDtypeStruct((T, D), F32), jax.ShapeDtypeStruct((T, 2 * FF), BF16),
         jax.ShapeDtypeStruct((T, D), BF16)],
        "ffn_fwd", copies=copies)


def ffn_bwd(dxo, x, y, a, pre_g, post_g, w1, w2, copies=None):
    T = x.shape[0]
    tm = min(TM_FFN_BWD, T)

    def body(dxo_ref, x_ref, y_ref, a_ref, pg_ref, qg_ref, w1_ref, w2_ref,
             dx_ref, da_ref, s_ref, dy_ref, hb_ref, dpg_ref, dqg_ref):
        @pl.when(pl.program_id(0) == 0)
        def _():
            dpg_ref[...] = jnp.zeros_like(dpg_ref)
            dqg_ref[...] = jnp.zeros_like(dqg_ref)

        dxo = dxo_ref[...]
        qg = qg_ref[...]
        _, ny, ry = _rms(y_ref[...].astype(F32), qg)
        dn = 0.5 * dxo
        dqg_ref[...] += _colsum(dn * ny)
        dyb = _rms_bwd(ny, ry, qg, dn).astype(BF16)
        dy_ref[...] = dyb
        pg = pg_ref[...]
        h, nx, rx = _rms(x_ref[...], pg)
        hb_ref[...] = h.astype(BF16)
        dh = jnp.zeros((tm, D), F32)
        for c0, cw in FF_CHUNKS:
            ds = _dot_nt(dyb, w2_ref[c0:c0 + cw, :])
            g = a_ref[:, c0:c0 + cw].astype(F32)
            u = a_ref[:, FF + c0:FF + c0 + cw].astype(F32)
            sg = _sigmoid(g)
            si = g * sg
            s_ref[:, c0:c0 + cw] = (si * u).astype(BF16)
            dg = (ds * u * (sg * (1.0 + g * (1.0 - sg)))).astype(BF16)
            du = (ds * si).astype(BF16)
            da_ref[:, c0:c0 + cw] = dg
            da_ref[:, FF + c0:FF + c0 + cw] = du
            dh = dh + _dot_nt(dg, w1_ref[:, c0:c0 + cw]) + _dot_nt(du, w1_ref[:, FF + c0:FF + c0 + cw])
        dpg_ref[...] += _colsum(dh * nx)
        dx_ref[...] = dxo + _rms_bwd(nx, rx, pg, dh)

    return _call(
        body, (T // tm,),
        [_row_spec(tm, D), _row_spec(tm, D), _row_spec(tm, D), _row_spec(tm, 2 * FF),
         _full_spec((1, D)), _full_spec((1, D)), _whole_spec(w1), _whole_spec(w2)],
        (dxo, x, y, a, pre_g, post_g, w1, w2),
        [_row_spec(tm, D), _row_spec(tm, 2 * FF), _row_spec(tm, FF), _row_spec(tm, D), _row_spec(tm, D),
         _full_spec((1, D)), _full_spec((1, D))],
        [jax.ShapeDtypeStruct((T, D), F32), jax.ShapeDtypeStruct((T, 2 * FF), BF16),
         jax.ShapeDtypeStruct((T, FF), BF16), jax.ShapeDtypeStruct((T, D), BF16),
         jax.ShapeDtypeStruct((T, D), BF16), jax.ShapeDtypeStruct((1, D), F32),
         jax.ShapeDtypeStruct((1, D), F32)],
        "ffn_bwd", copies=copies)


def mix_in_fwd(x, pre_g, win, copies=None):
    T = x.shape[0]
    tm = min(TM_FFN_FWD, T)

    def body(x_ref, pg_ref, w_ref, z_ref):
        h, _, _ = _rms(x_ref[...], pg_ref[...])
        hb = h.astype(BF16)
        for c0, cw in WIN_CHUNKS:
            z_ref[:, c0:c0 + cw] = _dot(hb, w_ref[:, c0:c0 + cw]).astype(BF16)

    return _call(
        body, (T // tm,), [_row_spec(tm, D), _full_spec((1, D)), _whole_spec(win)], (x, pre_g, win),
        [_row_spec(tm, IN_WIDTH)], [jax.ShapeDtypeStruct((T, IN_WIDTH), BF16)], "mix_in_fwd", copies=copies)


STACK = Q_PER_KV // 2
SROWS = STACK * BLK


def _band_mask(first):
    qi = lax.broadcasted_iota(jnp.int32, (SROWS, 2 * BLK), 0) & (BLK - 1)
    kj = lax.broadcasted_iota(jnp.int32, (SROWS, 2 * BLK), 1)
    band = (kj > qi) & (kj <= qi + BLK)
    if first is None:
        return band
    return band & (kj >= BLK * first.astype(jnp.int32))


def _lane_half(rows):
    return lax.broadcasted_iota(jnp.int32, (rows, BLK), 1) // HEAD_DIM


def _stack(ref, b, g):
    return jnp.concatenate([ref[b * BLK:(b + 1) * BLK, (STACK * g + j) * BLK:(STACK * g + j + 1) * BLK]
                            for j in range(STACK)], axis=0)


def _placed(pair, g, hp, fill):
    src = pair if hp == g else pltpu.roll(pair, HEAD_DIM, 1)
    return jnp.where(_lane_half(2 * BLK) == hp, src, fill).astype(BF16)


def _sink_column(sink_ref, g, hp):
    rb = lax.broadcasted_iota(jnp.int32, (SROWS, 1), 0) // BLK
    col = jnp.full((SROWS, 1), sink_ref[Q_PER_KV * g + hp], F32)
    for j in range(1, STACK):
        col = jnp.where(rb == j, sink_ref[Q_PER_KV * g + 2 * j + hp], col)
    return col


def _attn_scores(qs, kz, sink_col, mask):
    s = jnp.where(mask, _dot_nt(qs, kz) * ATTN_SCALE, MASK_VALUE)
    m = jnp.maximum(jnp.max(s, axis=-1, keepdims=True), sink_col)
    return jnp.exp(s - m), jnp.exp(sink_col - m)


def _kv_specs(tq, nb):
    kv_blk = ZKV // (2 * BLK)
    return [pl.BlockSpec((tq, 2 * BLK), lambda i: (i, kv_blk)),
            pl.BlockSpec((BLK, 2 * BLK), lambda i: (jnp.maximum(i * nb - 1, 0), kv_blk))]


def attn_fwd(z, sinks):
    T = z.shape[0]
    tq = min(TQ_ATTN, T)
    nb = tq // BLK

    def body(sink_ref, q_ref, kv_ref, kvh_ref, o_ref):
        i = pl.program_id(0)
        low = _lane_half(SROWS) == 0
        for b in range(nb):
            kvp = kvh_ref[...] if b == 0 else kv_ref[(b - 1) * BLK:b * BLK, :]
            kv2 = jnp.concatenate([kvp, kv_ref[b * BLK:(b + 1) * BLK, :]], axis=0).astype(F32)
            mask = _band_mask(i == 0) if b == 0 else _band_mask(None)
            for g in range(N_KV_HEADS):
                qs = _stack(q_ref, b, g)
                r, e = [], []
                for hp in range(2):
                    p, e_sink = _attn_scores(qs, _placed(kv2[:, 0:BLK], g, hp, 0.0), _sink_column(sink_ref, g, hp), mask)
                    r.append(_dot(p.astype(BF16), _placed(kv2[:, BLK:2 * BLK], g, hp, 1.0)))
                    e.append(e_sink)
                den = pltpu.roll(jnp.where(low, r[1], r[0]), HEAD_DIM, 1) + jnp.where(low, e[0], e[1])
                out = (jnp.where(low, r[0], r[1]) * (1.0 / den)).astype(BF16)
                for j in range(STACK):
                    o_ref[b * BLK:(b + 1) * BLK, (STACK * g + j) * BLK:(STACK * g + j + 1) * BLK] = \
                        out[j * BLK:(j + 1) * BLK, :]

    return pl.pallas_call(
        body, grid=(T // tq,),
        in_specs=[pl.BlockSpec(memory_space=pltpu.SMEM), _row_spec(tq, D)] + _kv_specs(tq, nb),
        out_specs=_row_spec(tq, D),
        out_shape=jax.ShapeDtypeStruct((T, D), BF16),
        name="attn_fwd", compiler_params=_params(1),
    )(sinks, z, z, z)


def attn_bwd(z, ya, dya, dzb, sinks):
    T = z.shape[0]
    tq = min(TQ_ATTN, T)
    nb = tq // BLK
    nt = T // tq
    mid = ZKV - ZU

    def body(sink_ref, q_ref, kv_ref, kvh_ref, y_ref, dy_ref, dzb_ref, dz_ref, dsink_ref, acc_ref, carry_ref):
        i = pl.program_id(0)
        first_tile = i == nt - 1

        @pl.when(i == 0)
        def _():
            carry_ref[...] = jnp.zeros_like(carry_ref)
            dsink_ref[...] = jnp.zeros_like(dsink_ref)

        acc_ref[...] = jnp.zeros_like(acc_ref)
        lane = lax.broadcasted_iota(jnp.int32, (1, BLK), 1)
        dsink = jnp.zeros((1, BLK), F32)
        half = _lane_half(SROWS)
        low_kv = _lane_half(2 * BLK) == 0
        for b in range(nb):
            kvp = kvh_ref[...] if b == 0 else kv_ref[(b - 1) * BLK:b * BLK, :]
            kv2 = jnp.concatenate([kvp, kv_ref[b * BLK:(b + 1) * BLK, :]], axis=0).astype(F32)
            mask = _band_mask(first_tile) if b == 0 else _band_mask(None)
            dk_groups, dv_groups = [], []
            for g in range(N_KV_HEADS):
                qs = _stack(q_ref, b, g)
                dys = _stack(dy_ref, b, g)
                dyy = dys.astype(F32) * _stack(y_ref, b, g).astype(F32)
                dq = jnp.zeros((SROWS, BLK), F32)
                dk_raw, dv_raw = [], []
                for hp in range(2):
                    kz = _placed(kv2[:, 0:BLK], g, hp, 0.0)
                    p, e_sink = _attn_scores(qs, kz, _sink_column(sink_ref, g, hp), mask)
                    inv = 1.0 / (jnp.sum(p, axis=-1, keepdims=True) + e_sink)
                    pn = p * inv
                    delta = jnp.sum(jnp.where(half == hp, dyy, 0.0), axis=-1, keepdims=True)
                    dp = _dot_nt(dys, _placed(kv2[:, BLK:2 * BLK], g, hp, 0.0))
                    ds = (pn * (dp - delta)).astype(BF16)
                    sink_term = e_sink * inv * delta
                    for j in range(STACK):
                        dsink = dsink + jnp.where(lane == Q_PER_KV * g + 2 * j + hp,
                                                  -_colsum(sink_term[j * BLK:(j + 1) * BLK, :]), 0.0)
                    dq = dq + _dot(ds, kz)
                    dk_raw.append(_dot_tn(ds, qs))
                    dv_raw.append(_dot_tn(pn.astype(BF16), dys))
                tk = jnp.where(low_kv, dk_raw[0], dk_raw[1])
                tv = jnp.where(low_kv, dv_raw[0], dv_raw[1])
                dk_groups.append(tk + pltpu.roll(tk, HEAD_DIM, 1))
                dv_groups.append(tv + pltpu.roll(tv, HEAD_DIM, 1))
                dqb = (dq * ATTN_SCALE).astype(BF16)
                for j in range(STACK):
                    dz_ref[b * BLK:(b + 1) * BLK, ZQ + (STACK * g + j) * BLK:ZQ + (STACK * g + j + 1) * BLK] = \
                        dqb[j * BLK:(j + 1) * BLK, :]
            acc_ref[b * BLK:(b + 2) * BLK, 0:BLK] += jnp.where(low_kv, dk_groups[0], dk_groups[1]) * ATTN_SCALE
            acc_ref[b * BLK:(b + 2) * BLK, BLK:2 * BLK] += jnp.where(low_kv, dv_groups[0], dv_groups[1])
        dsink_ref[...] += dsink
        dz_ref[:, ZU:ZKV] = dzb_ref[...]
        if nb > 1:
            dz_ref[0:tq - BLK, ZKV:IN_WIDTH] = acc_ref[BLK:tq, :].astype(BF16)
        dz_ref[tq - BLK:tq, ZKV:IN_WIDTH] = (acc_ref[tq:tq + BLK, :] + carry_ref[...]).astype(BF16)
        carry_ref[...] = acc_ref[0:BLK, :]

    kv_blk = ZKV // (2 * BLK)
    return pl.pallas_call(
        body, grid=(nt,),
        in_specs=[pl.BlockSpec(memory_space=pltpu.SMEM),
                  pl.BlockSpec((tq, D), lambda i: (nt - 1 - i, 0)),
                  pl.BlockSpec((tq, 2 * BLK), lambda i: (nt - 1 - i, kv_blk)),
                  pl.BlockSpec((BLK, 2 * BLK), lambda i: (jnp.maximum((nt - 1 - i) * nb - 1, 0), kv_blk)),
                  pl.BlockSpec((tq, D), lambda i: (nt - 1 - i, 0)),
                  pl.BlockSpec((tq, D), lambda i: (nt - 1 - i, 0)),
                  pl.BlockSpec((tq, mid), lambda i: (nt - 1 - i, 0))],
        out_specs=[pl.BlockSpec((tq, IN_WIDTH), lambda i: (nt - 1 - i, 0)), _full_spec((1, BLK))],
        out_shape=[jax.ShapeDtypeStruct((T, IN_WIDTH), BF16), jax.ShapeDtypeStruct((1, BLK), F32)],
        scratch_shapes=[pltpu.VMEM((tq + BLK, 2 * BLK), F32), pltpu.VMEM((BLK, 2 * BLK), F32)],
        name="attn_bwd", compiler_params=_params(1),
    )(sinks, z, z, z, ya, dya, dzb)


def _to_group_lanes(v, g, nch):
    return jnp.concatenate([v[n * BLK:(n + 1) * BLK, g * BLK:(g + 1) * BLK] for n in range(nch)], axis=1)


def _from_group_lanes(per_group, nch):
    rows = [jnp.concatenate([per_group[g][:, n * BLK:(n + 1) * BLK] for g in range(SGU_GROUPS)], axis=1)
            for n in range(nch)]
    return jnp.concatenate(rows, axis=0)


def _tril_bf16(w_ref, g):
    t = lax.broadcasted_iota(jnp.int32, (BLK, BLK), 0)
    s = lax.broadcasted_iota(jnp.int32, (BLK, BLK), 1)
    return jnp.where(t >= s, w_ref[g], 0.0).astype(BF16)


def _sgu_norm(v_s, ln_g, ln_b):
    cdf, pdf = _gelu_parts(v_s)
    gv = v_s * cdf
    xc = gv - jnp.mean(gv, axis=-1, keepdims=True)
    rstd = lax.rsqrt(jnp.mean(xc * xc, axis=-1, keepdims=True) + LN_EPS)
    nhat = xc * rstd
    return nhat * ln_g + ln_b, nhat, rstd, cdf + v_s * pdf


def _sgu_gate(vn, w_ref, bb_ref, nch):
    vnb = vn.astype(BF16)
    return _from_group_lanes(
        [_dot(_tril_bf16(w_ref, g), _to_group_lanes(vnb, g, nch)) + jnp.tile(bb_ref[g], (1, nch))
         for g in range(SGU_GROUPS)], nch)


def mix_fwd_out(x, z, ya, post_g, ln_g, ln_b, sgu_w, sgu_bb, wa, ws, wo, copies=None):
    T = x.shape[0]
    tm = min(TM_MIX, T)
    nch = tm // BLK

    def body(x_ref, us_ref, vs_ref, ga_ref, gb_ref, ya_ref, qg_ref, lg_ref, lb_ref, w_ref, bb_ref,
             wa_ref, ws_ref, wo_ref, xo_ref, ysg_ref, pa_ref, pb_ref, o_ref):
        vn, _, _, _ = _sgu_norm(vs_ref[...].astype(F32), lg_ref[...], lb_ref[...])
        gate = _sgu_gate(vn, w_ref, bb_ref, nch)
        us = us_ref[...].astype(F32)
        cdf, _ = _gelu_parts(us)
        ysg = (us * cdf * gate).astype(BF16)
        ysg_ref[...] = ysg
        pa = _dot(ya_ref[...], wa_ref[...])
        pb = _dot(ysg, ws_ref[...])
        pa_ref[...] = pa.astype(BF16)
        pb_ref[...] = pb.astype(BF16)
        merged = _sigmoid(ga_ref[...].astype(F32)) * pa + _sigmoid(gb_ref[...].astype(F32)) * pb
        o = _dot(merged.astype(BF16), wo_ref[...])
        o_ref[...] = o.astype(BF16)
        on, _, _ = _rms(o, qg_ref[...])
        xo_ref[...] = x_ref[...] + on

    zspec = lambda start: _row_spec(tm, D, start // D)
    act = jax.ShapeDtypeStruct((T, D), BF16)
    return _call(
        body, (T // tm,),
        [_row_spec(tm, D), zspec(ZU), zspec(ZV), zspec(ZGA), zspec(ZGB), _row_spec(tm, D),
         _full_spec((1, D)), _full_spec((1, D)), _full_spec((1, D)),
         _full_spec((SGU_GROUPS, BLK, BLK)), _full_spec((SGU_GROUPS, BLK, BLK)),
         _whole_spec(wa), _whole_spec(ws), _whole_spec(wo)],
        (x, z, z, z, z, ya, post_g, ln_g, ln_b, sgu_w, sgu_bb, wa, ws, wo),
        [_row_spec(tm, D)] * 5, [jax.ShapeDtypeStruct((T, D), F32), act, act, act, act],
        "mix_fwd_out", copies=copies)


def mix_bwd_out(dxo, z, o, pa, pb, post_g, ln_g, ln_b, sgu_w, sgu_bb, wa, ws, wo, copies=None):
    T = dxo.shape[0]
    tm = min(TM_MIX_BWD, T)
    nch = tm // BLK
    mid = ZKV - ZU

    def body(dxo_ref, us_ref, vs_ref, ga_ref, gb_ref, o_ref, pa_ref, pb_ref, qg_ref, lg_ref, lb_ref, w_ref, bb_ref,
             wa_ref, ws_ref, wo_ref,
             dzb_ref, dya_ref, mg_ref, do_ref, dpa_ref, dpb_ref, dqg_ref, dlg_ref, dlb_ref, dw_ref, dbb_ref):
        @pl.when(pl.program_id(0) == 0)
        def _():
            for r in (dqg_ref, dlg_ref, dlb_ref, dw_ref, dbb_ref):
                r[...] = jnp.zeros_like(r)

        qg = qg_ref[...]
        dxo = dxo_ref[...]
        _, no, ro = _rms(o_ref[...].astype(F32), qg)
        dqg_ref[...] += _colsum(dxo * no)
        dob = _rms_bwd(no, ro, qg, dxo).astype(BF16)
        do_ref[...] = dob
        dmerged = _dot_nt(dob, wo_ref[...])
        sa = _sigmoid(ga_ref[...].astype(F32))
        sb = _sigmoid(gb_ref[...].astype(F32))
        pa = pa_ref[...].astype(F32)
        pb = pb_ref[...].astype(F32)
        mg_ref[...] = (sa * pa + sb * pb).astype(BF16)
        dpa = (dmerged * sa).astype(BF16)
        dpb = (dmerged * sb).astype(BF16)
        dpa_ref[...] = dpa
        dpb_ref[...] = dpb
        dzb_ref[:, ZGA - ZU:ZGA - ZU + D] = (dmerged * pa * sa * (1.0 - sa)).astype(BF16)
        dzb_ref[:, ZGB - ZU:ZGB - ZU + D] = (dmerged * pb * sb * (1.0 - sb)).astype(BF16)
        dya_ref[...] = _dot_nt(dpa, wa_ref[...]).astype(BF16)
        dysg = _dot_nt(dpb, ws_ref[...])

        lg = lg_ref[...]
        vn, nhat, rstd, dgelu_v = _sgu_norm(vs_ref[...].astype(F32), lg, lb_ref[...])
        gate = _sgu_gate(vn, w_ref, bb_ref, nch)
        us = us_ref[...].astype(F32)
        cdf, pdf = _gelu_parts(us)
        dzb_ref[:, 0:D] = (dysg * gate * (cdf + us * pdf)).astype(BF16)
        dgate = (dysg * (us * cdf)).astype(BF16)
        vnb = vn.astype(BF16)
        t = lax.broadcasted_iota(jnp.int32, (BLK, BLK), 0)
        s = lax.broadcasted_iota(jnp.int32, (BLK, BLK), 1)
        dvn_groups = []
        for g in range(SGU_GROUPS):
            dgl = _to_group_lanes(dgate, g, nch)
            dbb_ref[g] += jnp.broadcast_to(jnp.sum(dgl.astype(F32), axis=-1, keepdims=True), (BLK, BLK))
            dw_ref[g] += jnp.where(t >= s, _dot_nt(dgl, _to_group_lanes(vnb, g, nch)), 0.0)
            dvn_groups.append(_dot_tn(_tril_bf16(w_ref, g), dgl))
        dvn = _from_group_lanes(dvn_groups, nch)
        dlg_ref[...] += _colsum(dvn * nhat)
        dlb_ref[...] += _colsum(dvn)
        dnh = dvn * lg
        dgv = rstd * (dnh - jnp.mean(dnh, axis=-1, keepdims=True) - nhat * jnp.mean(dnh * nhat, axis=-1, keepdims=True))
        dzb_ref[:, ZV - ZU:ZV - ZU + D] = (dgv * dgelu_v).astype(BF16)

    zspec = lambda start: _row_spec(tm, D, start // D)
    act = jax.ShapeDtypeStruct((T, D), BF16)
    grp = jax.ShapeDtypeStruct((SGU_GROUPS, BLK, BLK), F32)
    vec = jax.ShapeDtypeStruct((1, D), F32)
    return _call(
        body, (T // tm,),
        [_row_spec(tm, D), zspec(ZU), zspec(ZV), zspec(ZGA), zspec(ZGB),
         _row_spec(tm, D), _row_spec(tm, D), _row_spec(tm, D),
         _full_spec((1, D)), _full_spec((1, D)), _full_spec((1, D)),
         _full_spec((SGU_GROUPS, BLK, BLK)), _full_spec((SGU_GROUPS, BLK, BLK)),
         _whole_spec(wa), _whole_spec(ws), _whole_spec(wo)],
        (dxo, z, z, z, z, o, pa, pb, post_g, ln_g, ln_b, sgu_w, sgu_bb, wa, ws, wo),
        [_row_spec(tm, mid)] + [_row_spec(tm, D)] * 5 + [_full_spec((1, D))] * 3
        + [_full_spec((SGU_GROUPS, BLK, BLK))] * 2,
        [jax.ShapeDtypeStruct((T, mid), BF16), act, act, act, act, act, vec, vec, vec, grp, grp],
        "mix_bwd_out", copies=copies)


def mix_in_bwd(dxo, x, dz, pre_g, win):
    T = x.shape[0]
    tm = min(TM_MIX, T)

    def body(dxo_ref, x_ref, dz_ref, pg_ref, w_ref, dx_ref, hb_ref, dpg_ref):
        @pl.when(pl.program_id(0) == 0)
        def _():
            dpg_ref[...] = jnp.zeros_like(dpg_ref)

        pg = pg_ref[...]
        h, nx, rx = _rms(x_ref[...], pg)
        hb_ref[...] = h.astype(BF16)
        dh = jnp.zeros((tm, D), F32)
        for c0, cw in WIN_CHUNKS:
            dh = dh + _dot_nt(dz_ref[:, c0:c0 + cw], w_ref[:, c0:c0 + cw])
        dpg_ref[...] += _colsum(dh * nx)
        dx_ref[...] = dxo_ref[...] + _rms_bwd(nx, rx, pg, dh)

    return pl.pallas_call(
        body, grid=(T // tm,),
        in_specs=[_row_spec(tm, D), _row_spec(tm, D), _row_spec(tm, IN_WIDTH), _full_spec((1, D)), _whole_spec(win)],
        out_specs=[_row_spec(tm, D), _row_spec(tm, D), _full_spec((1, D))],
        out_shape=[jax.ShapeDtypeStruct((T, D), F32), jax.ShapeDtypeStruct((T, D), BF16),
                   jax.ShapeDtypeStruct((1, D), F32)],
        name="mix_in_bwd", compiler_params=_params(1),
    )(dxo, x, dz, pre_g, win)


def wgrad_cols(a, b, pieces, shard_width, name, tmo=512):
    T, M = a.shape
    N = b.shape[1]
    tk = min(TK_WGRAD, T)
    nk = T // tk

    def body(a_ref, b_ref, o_ref, acc_ref):
        k = pl.program_id(1)

        @pl.when(k == 0)
        def _():
            acc_ref[...] = jnp.zeros_like(acc_ref)

        acc_ref[...] += _dot_tn(a_ref[...], b_ref[...])

        @pl.when(k == nk - 1)
        def _():
            for j in range(N_DEV):
                for off, w, ps in pieces[j]:
                    o_ref[j, :, off:off + w] = acc_ref[:, ps:ps + w].astype(BF16)

    return pl.pallas_call(
        body, grid=(M // tmo, nk),
        in_specs=[pl.BlockSpec((tk, tmo), lambda m, k: (k, m)), pl.BlockSpec((tk, N), lambda m, k: (k, 0))],
        out_specs=pl.BlockSpec((N_DEV, tmo, shard_width), lambda m, k: (0, m, 0)),
        out_shape=jax.ShapeDtypeStruct((N_DEV, M, shard_width), BF16),
        scratch_shapes=[pltpu.VMEM((tmo, N), F32)],
        name=name, compiler_params=_params(2),
    )(a, b)


def wgrad_rows(a, b, name, tmo):
    T, M = a.shape
    N = b.shape[1]
    tk = min(TK_WGRAD, T)
    nk = T // tk

    def body(a_ref, b_ref, o_ref, acc_ref):
        k = pl.program_id(1)

        @pl.when(k == 0)
        def _():
            acc_ref[...] = jnp.zeros_like(acc_ref)

        acc_ref[...] += _dot_tn(a_ref[...], b_ref[...])

        @pl.when(k == nk - 1)
        def _():
            o_ref[...] = acc_ref[...].astype(BF16)

    return pl.pallas_call(
        body, grid=(M // tmo, nk),
        in_specs=[pl.BlockSpec((tk, tmo), lambda m, k: (k, m)), pl.BlockSpec((tk, N), lambda m, k: (k, 0))],
        out_specs=pl.BlockSpec((tmo, N), lambda m, k: (m, 0)),
        out_shape=jax.ShapeDtypeStruct((M, N), BF16),
        scratch_shapes=[pltpu.VMEM((tmo, N), F32)],
        name=name, compiler_params=_params(2),
    )(a, b)


def loss_head(y, target):
    T = y.shape[0]
    tm = min(TM_FFN_FWD, T)

    def body(y_ref, t_ref, sq_ref, dy_ref):
        @pl.when(pl.program_id(0) == 0)
        def _():
            sq_ref[...] = jnp.zeros_like(sq_ref)

        err = y_ref[...] - t_ref[...]
        sq_ref[...] += _colsum(err * err)
        dy_ref[...] = err * (1.0 / D)

    return pl.pallas_call(
        body, grid=(T // tm,), in_specs=[_row_spec(tm, D), _row_spec(tm, D)],
        out_specs=[_full_spec((1, D)), _row_spec(tm, D)],
        out_shape=[jax.ShapeDtypeStruct((1, D), F32), jax.ShapeDtypeStruct((T, D), F32)],
        name="loss_head", compiler_params=_params(1),
    )(y, target)


def adamw_sum(parts, w, m, v, name):
    layers, rows, cols = w.shape
    tr = _row_tile(rows, TR_ADAM)
    nr = rows // tr
    c1 = 1.0 - ADAM_B1 ** ADAM_STEP
    c2 = 1.0 - ADAM_B2 ** ADAM_STEP

    def body(*refs):
        p_refs = refs[:layers]
        w_ref, m_ref, v_ref, g_ref, d_ref, nm_ref, nv_ref = refs[layers:]
        for k in range(layers):
            @pl.when(pl.program_id(0) == k)
            def _(p_ref=p_refs[k]):
                g = p_ref[0].astype(F32)
                for j in range(1, N_DEV):
                    g = g + p_ref[j].astype(F32)
                nm = ADAM_B1 * m_ref[...] + (1.0 - ADAM_B1) * g
                nv = ADAM_B2 * v_ref[...] + (1.0 - ADAM_B2) * (g * g)
                g_ref[...] = g
                nm_ref[...] = nm
                nv_ref[...] = nv
                d_ref[...] = -ADAM_LR * ((nm / c1) / (jnp.sqrt(nv / c2) + ADAM_EPS) + ADAM_WD * w_ref[...])

    def part_spec(k):
        return pl.BlockSpec((N_DEV, tr, cols),
                            lambda l, i: (0, jnp.where(l < k, 0, jnp.where(l == k, i, nr - 1)), 0))

    spec = pl.BlockSpec((None, tr, cols), lambda l, i: (l, i, 0))
    out = jax.ShapeDtypeStruct((layers, rows, cols), F32)
    return pl.pallas_call(
        body, grid=(layers, nr),
        in_specs=[part_spec(k) for k in range(layers)] + [spec, spec, spec],
        out_specs=[spec] * 4, out_shape=[out] * 4, name=name, compiler_params=_params(2),
    )(*parts, w, m, v)


def _pack_small(p):
    layers = []
    for l in range(DEPTH):
        rows = [p[n][l].reshape(1, D) for n in SMALL_VEC]
        rows.append(p["sgu_b"][l].reshape(1, D))
        rows.append(jnp.pad(p["attn_sinks"][l].reshape(1, N_Q_HEADS), ((0, 0), (0, D - N_Q_HEADS))))
        rows.append(p["sgu_w"][l].reshape(BLK, D))
        used = len(SMALL_VEC) + 2 + BLK
        rows.append(jnp.zeros((SMALL_ROWS - used, D), F32))
        layers.append(jnp.concatenate(rows, axis=0))
    return jnp.concatenate(layers, axis=0)


def _unpack_small(packed):
    a = packed.reshape(DEPTH, SMALL_ROWS, D)
    out = {n: a[:, i, :] for i, n in enumerate(SMALL_VEC)}
    k = len(SMALL_VEC)
    out["sgu_b"] = a[:, k, :].reshape(DEPTH, SGU_GROUPS, BLK)
    out["attn_sinks"] = a[:, k + 1, :N_Q_HEADS]
    out["sgu_w"] = a[:, k + 2:k + 2 + BLK, :].reshape(DEPTH, SGU_GROUPS, BLK, BLK)
    return out


WEIGHT_NAMES = ("ffn1_pre_g", "ffn1_w1", "ffn1_w2", "ffn1_post_g", "mix_pre_g", "w_in", "attn_sinks", "sgu_ln_g",
                "sgu_ln_b", "sgu_w", "sgu_b", "w_attn_branch", "w_sgu_branch", "w_out", "mix_post_g", "ffn2_pre_g",
                "ffn2_w1", "ffn2_w2", "ffn2_post_g")
COL_SHARDED = ("ffn1_w1", "ffn2_w1", "w_in")
ROW_SHARDED = ("ffn1_w2", "ffn2_w2", "w_attn_branch", "w_sgu_branch", "w_out")
MATRICES = COL_SHARDED + ROW_SHARDED


def kernel(x, ffn1_pre_g, ffn1_w1, ffn1_w2, ffn1_post_g, mix_pre_g, w_in, attn_sinks, sgu_ln_g, sgu_ln_b, sgu_w, sgu_b, w_attn_branch, w_sgu_branch, w_out, mix_post_g, ffn2_pre_g, ffn2_w1, ffn2_w2, ffn2_post_g, loss_target, m_ffn1_pre_g, m_ffn1_w1, m_ffn1_w2, m_ffn1_post_g, m_mix_pre_g, m_w_in, m_attn_sinks, m_sgu_ln_g, m_sgu_ln_b, m_sgu_w, m_sgu_b, m_w_attn_branch, m_w_sgu_branch, m_w_out, m_mix_post_g, m_ffn2_pre_g, m_ffn2_w1, m_ffn2_w2, m_ffn2_post_g, v_ffn1_pre_g, v_ffn1_w1, v_ffn1_w2, v_ffn1_post_g, v_mix_pre_g, v_w_in, v_attn_sinks, v_sgu_ln_g, v_sgu_ln_b, v_sgu_w, v_sgu_b, v_w_attn_branch, v_w_sgu_branch, v_w_out, v_mix_post_g, v_ffn2_pre_g, v_ffn2_w1, v_ffn2_w2, v_ffn2_post_g):
    w = dict(zip(WEIGHT_NAMES, (ffn1_pre_g, ffn1_w1, ffn1_w2, ffn1_post_g, mix_pre_g, w_in, attn_sinks, sgu_ln_g,
                                sgu_ln_b, sgu_w, sgu_b, w_attn_branch, w_sgu_branch, w_out, mix_post_g, ffn2_pre_g,
                                ffn2_w1, ffn2_w2, ffn2_post_g)))
    mom = dict(zip(WEIGHT_NAMES, (m_ffn1_pre_g, m_ffn1_w1, m_ffn1_w2, m_ffn1_post_g, m_mix_pre_g, m_w_in,
                                  m_attn_sinks, m_sgu_ln_g, m_sgu_ln_b, m_sgu_w, m_sgu_b, m_w_attn_branch,
                                  m_w_sgu_branch, m_w_out, m_mix_post_g, m_ffn2_pre_g, m_ffn2_w1, m_ffn2_w2,
                                  m_ffn2_post_g)))
    var = dict(zip(WEIGHT_NAMES, (v_ffn1_pre_g, v_ffn1_w1, v_ffn1_w2, v_ffn1_post_g, v_mix_pre_g, v_w_in,
                                  v_attn_sinks, v_sgu_ln_g, v_sgu_ln_b, v_sgu_w, v_sgu_b, v_w_attn_branch,
                                  v_w_sgu_branch, v_w_out, v_mix_post_g, v_ffn2_pre_g, v_ffn2_w1, v_ffn2_w2,
                                  v_ffn2_post_g)))
    T = x.shape[1]
    xs = x.reshape(T, D)
    target = loss_target.reshape(T, D)

    shard = {n: [cast_layer(w[n], l) for l in range(DEPTH)] for n in MATRICES}

    def whole(n, gathered):
        if n in ("ffn1_w1", "ffn2_w1"):
            return relayout_cols(gathered, W1_PIECES)
        if n == "w_in":
            return relayout_cols(gathered, WIN_PIECES)
        return gathered.reshape(-1, D)

    weights = [{} for _ in range(DEPTH)]
    for n, g in zip(MATRICES, all_gather_weights([shard[n][0] for n in MATRICES])):
        weights[0][n] = whole(n, g)

    def fetch(l, names):
        return PeerCopies([("all", shard[n][l]) for n in names]) if l < DEPTH else None

    def landed(l, names, arrivals):
        for n, g in zip(names, arrivals):
            weights[l][n] = whole(n, g)

    vec = lambda n, l: w[n][l].reshape(1, D)
    sgu_bb = [jnp.broadcast_to(w["sgu_b"][l][:, :, None], (SGU_GROUPS, BLK, BLK)) for l in range(DEPTH)]
    ffn1, ffn2, squares = ("ffn1_w1", "ffn1_w2"), ("ffn2_w1", "ffn2_w2"), ("w_attn_branch", "w_sgu_branch", "w_out")

    saved = []
    h = xs
    for l in range(DEPTH):
        wl = weights[l]
        x0 = h
        (x1, a1, y1), got = ffn_fwd(x0, vec("ffn1_pre_g", l), vec("ffn1_post_g", l), wl["ffn1_w1"], wl["ffn1_w2"],
                                    fetch(l + 1, ffn1))
        landed(l + 1, ffn1, got)
        (z,), got = mix_in_fwd(x1, vec("mix_pre_g", l), wl["w_in"], fetch(l + 1, ("w_in",)))
        landed(l + 1, ("w_in",), got)
        ya = attn_fwd(z, w["attn_sinks"][l])
        (x2, ysg, pa, pb, o), got = mix_fwd_out(
            x1, z, ya, vec("mix_post_g", l), vec("sgu_ln_g", l), vec("sgu_ln_b", l), w["sgu_w"][l], sgu_bb[l],
            wl["w_attn_branch"], wl["w_sgu_branch"], wl["w_out"], fetch(l + 1, squares))
        landed(l + 1, squares, got)
        (h, a2, y2), got = ffn_fwd(x2, vec("ffn2_pre_g", l), vec("ffn2_post_g", l), wl["ffn2_w1"], wl["ffn2_w2"],
                                   fetch(l + 1, ffn2))
        landed(l + 1, ffn2, got)
        saved.append((x0, a1, y1, x1, z, ya, ysg, pa, pb, o, x2, a2, y2))

    sq, dx = loss_head(h, target)
    loss = lax.psum(0.5 / D * jnp.sum(sq), ("x", "y", "c"))

    def ffn_wgrads(hb, da, s, dy):
        return [("own", wgrad_cols(hb, da, W1_PIECES, W1_SHARD, "wgrad_ffn_w1")),
                ("own", wgrad_rows(s, dy, "wgrad_ffn_w2", FF // 2).reshape(N_DEV, W2_SHARD, D))]

    parts = {n: [None] * DEPTH for n in MATRICES}
    small = {n: [None] * DEPTH for n in WEIGHT_NAMES if n not in MATRICES}
    waiting = None
    for l in reversed(range(DEPTH)):
        wl = weights[l]
        x0, a1, y1, x1, z, ya, ysg, pa, pb, o, x2, a2, y2 = saved[l]
        (dx, da, s, dy, hb, dpg, dqg), got = ffn_bwd(
            dx, x2, y2, a2, vec("ffn2_pre_g", l), vec("ffn2_post_g", l), wl["ffn2_w1"], wl["ffn2_w2"],
            PeerCopies(waiting) if waiting else None)
        if waiting:
            parts["ffn1_w1"][l + 1], parts["ffn1_w2"][l + 1] = got
        small["ffn2_pre_g"][l], small["ffn2_post_g"][l] = dpg, dqg

        (dzb, dya, mg, dob, dpa, dpb, dqg, dlg, dlb, dsw, dsb), got = mix_bwd_out(
            dx, z, o, pa, pb, vec("mix_post_g", l), vec("sgu_ln_g", l), vec("sgu_ln_b", l), w["sgu_w"][l], sgu_bb[l],
            wl["w_attn_branch"], wl["w_sgu_branch"], wl["w_out"], PeerCopies(ffn_wgrads(hb, da, s, dy)))
        parts["ffn2_w1"][l], parts["ffn2_w2"][l] = got
        dz, dsink = attn_bwd(z, ya, dya, dzb, w["attn_sinks"][l])
        dx, hb, dpg = mix_in_bwd(dx, x1, dz, vec("mix_pre_g", l), wl["w_in"])
        small["mix_pre_g"][l], small["mix_post_g"][l] = dpg, dqg
        small["sgu_ln_g"][l], small["sgu_ln_b"][l] = dlg, dlb
        small["sgu_w"][l], small["sgu_b"][l] = dsw, dsb[:, :, 0]
        small["attn_sinks"][l] = dsink[0, :N_Q_HEADS]
        mixer = [("own", wgrad_cols(hb, dz, WIN_PIECES, WIN_SHARD, "wgrad_w_in"))]
        mixer += [("own", wgrad_rows(act, cot, "wgrad_square", D).reshape(N_DEV, SQ_SHARD, D))
                  for act, cot in ((ya, dpa), (ysg, dpb), (mg, dob))]

        (dx, da, s, dy, hb, dpg, dqg), got = ffn_bwd(
            dx, x0, y1, a1, vec("ffn1_pre_g", l), vec("ffn1_post_g", l), wl["ffn1_w1"], wl["ffn1_w2"],
            PeerCopies(mixer))
        parts["w_in"][l], parts["w_attn_branch"][l], parts["w_sgu_branch"][l], parts["w_out"][l] = got
        small["ffn1_pre_g"][l], small["ffn1_post_g"][l] = dpg, dqg
        waiting = ffn_wgrads(hb, da, s, dy)
    grad_x = dx.reshape(x.shape)

    small_send = _pack_small({n: jnp.stack(v, axis=0) for n, v in small.items()})
    parts["ffn1_w1"][0], parts["ffn1_w2"][0], small_parts = exchange_last(waiting + [("all", small_send)])

    grads, deltas, new_m, new_v = {}, {}, {}, {}
    for n in MATRICES:
        grads[n], deltas[n], new_m[n], new_v[n] = adamw_sum(parts[n], w[n], mom[n], var[n], "adamw_" + n)
    res = adamw_sum([small_parts], _pack_small(w)[None], _pack_small(mom)[None], _pack_small(var)[None],
                    "adamw_small")
    for out, packed in zip((grads, deltas, new_m, new_v), res):
        out.update(_unpack_small(packed[0]))

    return (loss, grad_x, *[grads[n] for n in WEIGHT_NAMES], *[deltas[n] for n in WEIGHT_NAMES],
            *[new_m[n] for n in WEIGHT_NAMES], *[new_v[n] for n in WEIGHT_NAMES])
```

```python
import math

import jax
import jax.numpy as jnp
from jax import lax
from jax.experimental import pallas as pl
from jax.experimental.pallas import tpu as pltpu

F32 = jnp.float32
BF16 = jnp.bfloat16

N_DEV = 8
D = 1024
FF = 2816
DEPTH = 4
HEAD_DIM = 64
N_Q_HEADS = 16
N_KV_HEADS = 2
Q_PER_KV = N_Q_HEADS // N_KV_HEADS
BLK = 128
SGU_GROUPS = 8
IN_WIDTH = 5376
W1_SHARD = 2 * FF // N_DEV
WIN_SHARD = IN_WIDTH // N_DEV
W2_SHARD = FF // N_DEV
SQ_SHARD = D // N_DEV

RMS_EPS = 1e-6
LN_EPS = 1e-5
MASK_VALUE = -1e30
ATTN_SCALE = 1.0 / math.sqrt(HEAD_DIM)

ADAM_LR = 0.001
ADAM_B1 = 0.9
ADAM_B2 = 0.999
ADAM_EPS = 1e-08
ADAM_WD = 0.01
ADAM_STEP = 10

VMEM_LIMIT_V7X = 56 * 1024 * 1024

WIN_SEGMENTS = ((0, 1024, 0), (1024, 256, 5120), (1280, 1024, 1024), (2304, 1024, 2048), (3328, 1024, 3072),
                (4352, 1024, 4096))
ZQ, ZU, ZV, ZGA, ZGB, ZKV = 0, 1024, 2048, 3072, 4096, 5120
DZ_Q, DZ_KV, DZ_MID = 0, 1024, 1280

FF_CHUNKS = ((0, 1024), (1024, 1024), (2048, 768))
WIN_CHUNKS = ((0, 1792), (1792, 1792), (3584, 1792))

SMALL_ROWS = 144
LATE_ROWS = 16
SMALL_VEC = ("ffn1_pre_g", "ffn1_post_g", "mix_pre_g", "mix_post_g", "ffn2_pre_g", "ffn2_post_g",
             "sgu_ln_g", "sgu_ln_b")

TM_FFN_FWD = 512
TM_FFN_BWD = 256
TM_MIX = 512
TM_MIX_BWD = 256
TQ_ATTN = 512
TK_WGRAD = 512
TR_ADAM = 256


def _params(n_grid, vmem=VMEM_LIMIT_V7X):
    return pltpu.CompilerParams(dimension_semantics=("arbitrary",) * n_grid, vmem_limit_bytes=vmem)


def _dot(a, b):
    return jnp.dot(a, b, preferred_element_type=F32)


def _dot_nt(a, b):
    return lax.dot_general(a, b, (((1,), (1,)), ((), ())), preferred_element_type=F32)


def _dot_tn(a, b):
    return lax.dot_general(a, b, (((0,), (0,)), ((), ())), preferred_element_type=F32)


def _rms(x, g):
    r = lax.rsqrt(jnp.mean(x * x, axis=-1, keepdims=True) + RMS_EPS)
    n = x * r
    return n * g, n, r


def _rms_bwd(n, r, g, dy):
    dn = dy * g
    return r * (dn - n * jnp.mean(dn * n, axis=-1, keepdims=True))


def _colsum(v):
    return jnp.sum(v, axis=0, keepdims=True)


def _sigmoid(v):
    return 1.0 / (1.0 + jnp.exp(-v))


def _gelu_parts(v):
    cdf = 0.5 * (1.0 + lax.erf(v * (1.0 / math.sqrt(2.0))))
    return cdf, jnp.exp(-0.5 * v * v) * (1.0 / math.sqrt(2.0 * math.pi))


def _row_tile(rows, cap):
    return max(t for t in range(16, min(rows, cap) + 1, 16) if rows % t == 0)


def _row_spec(tm, width, col_block=0):
    return pl.BlockSpec((tm, width), lambda i, cb=col_block: (i, cb))


def _full_spec(shape):
    nd = len(shape)
    return pl.BlockSpec(tuple(shape), lambda *_: (0,) * nd)


def _whole_spec(arr):
    nd = arr.ndim
    return pl.BlockSpec(tuple(arr.shape), lambda *_: (0,) * nd, pipeline_mode=pl.Buffered(1))


HBM_SPEC = pl.BlockSpec(memory_space=pltpu.HBM)
MESH_ID = pl.DeviceIdType.MESH
RELATIONS = tuple((rx, ry, rc) for rx in (0, 1) for ry in (0, 1) for rc in (0, 1))[1:]


class PeerCopies:
    def __init__(self, jobs):
        self.kinds = [k for k, _ in jobs]
        self.arrays = [a for _, a in jobs]
        self.n = len(jobs)
        self.out_shape = [jax.ShapeDtypeStruct((N_DEV,) + a.shape if k == "all" else a.shape, a.dtype)
                          for k, a in jobs]
        self.scratch = [pltpu.SemaphoreType.DMA((7 * self.n,)), pltpu.SemaphoreType.DMA((7 * self.n,)),
                        pltpu.SemaphoreType.DMA((self.n,))]

    def _copies(self, ins, outs, sems, arriving):
        send_sems, recv_sems, local_sems = sems
        x, y, c = lax.axis_index("x"), lax.axis_index("y"), lax.axis_index("c")
        me = 4 * x + 2 * y + c
        src = lambda a, d: ins[a] if self.kinds[a] == "all" else ins[a].at[d]
        if not arriving:
            local = [pltpu.make_async_copy(src(a, me), outs[a].at[me], local_sems.at[a]) for a in range(self.n)]
        remote = []
        for k, (rx, ry, rc) in enumerate(RELATIONS):
            tx, ty, tc = (1 - x if rx else x), (1 - y if ry else y), (1 - c if rc else c)
            peer = 4 * tx + 2 * ty + tc
            for a in range(self.n):
                from_slot, to_slot = (me, peer) if arriving else (peer, me)
                remote.append(pltpu.make_async_remote_copy(
                    src_ref=src(a, from_slot), dst_ref=outs[a].at[to_slot],
                    send_sem=send_sems.at[a * 7 + k], recv_sem=recv_sems.at[a * 7 + k],
                    device_id=(tx, ty, tc), device_id_type=MESH_ID))
        return remote if arriving else (local, remote)

    def start(self, ins, outs, sems):
        local, sends = self._copies(ins, outs, sems, False)
        for cp in local + sends:
            cp.start()

    def wait(self, ins, outs, sems):
        for cp in self._copies(ins, outs, sems, True):
            cp.wait_recv()
        local, sends = self._copies(ins, outs, sems, False)
        for cp in sends:
            cp.wait_send()
        for cp in local:
            cp.wait()


def _call(body, grid, in_specs, args, out_specs, out_shape, name, scratch=(), copies=None):
    if copies is None:
        outs = pl.pallas_call(body, grid=grid, in_specs=list(in_specs), out_specs=list(out_specs),
                              out_shape=list(out_shape), scratch_shapes=list(scratch), name=name,
                              compiler_params=_params(len(grid)))(*args)
        return outs, []
    n_in, n_out, n_scr, nc = len(in_specs), len(out_specs), len(scratch), copies.n
    last = grid[0] - 1

    def hosted(*refs):
        ins, refs = refs[:n_in], refs[n_in:]
        c_in, refs = refs[:nc], refs[nc:]
        outs, refs = refs[:n_out], refs[n_out:]
        c_out, refs = refs[:nc], refs[nc:]
        scr, sems = refs[:n_scr], refs[n_scr:]

        @pl.when(pl.program_id(0) == 0)
        def _():
            copies.start(c_in, c_out, sems)

        body(*ins, *outs, *scr)

        @pl.when(pl.program_id(0) == last)
        def _():
            copies.wait(c_in, c_out, sems)

    outs = pl.pallas_call(hosted, grid=grid, in_specs=list(in_specs) + [HBM_SPEC] * nc,
                          out_specs=list(out_specs) + [HBM_SPEC] * nc,
                          out_shape=list(out_shape) + copies.out_shape,
                          scratch_shapes=list(scratch) + copies.scratch, name=name,
                          compiler_params=_params(len(grid)))(*args, *copies.arrays)
    return outs[:n_out], outs[n_out:]


def cast_layer(w, l):
    _, rows, cols = w.shape
    tr = _row_tile(rows, 256)

    def body(w_ref, o_ref):
        o_ref[...] = w_ref[...].astype(BF16)

    return pl.pallas_call(
        body, grid=(rows // tr,), in_specs=[pl.BlockSpec((None, tr, cols), lambda i: (l, i, 0))],
        out_specs=_row_spec(tr, cols), out_shape=jax.ShapeDtypeStruct((rows, cols), BF16),
        name="cast_layer", compiler_params=_params(1),
    )(w)


def all_gather_weights(shards):
    n = len(shards)

    def body(*refs):
        ins, outs = refs[:n], refs[n:2 * n]
        send_sems, recv_sems, local_sems = refs[2 * n:]
        x, y, c = lax.axis_index("x"), lax.axis_index("y"), lax.axis_index("c")
        me, sibling = (x, y, c), (x, y, 1 - c)
        chips = [(1 - x, y), (x, 1 - y), (1 - x, 1 - y)]

        def slot(a, owner):
            return outs[a].at[4 * owner[0] + 2 * owner[1] + owner[2]]

        def copy(a, k, owner, to, src=None):
            return pltpu.make_async_remote_copy(
                src_ref=slot(a, owner) if src is None else src, dst_ref=slot(a, owner),
                send_sem=send_sems.at[a * 7 + k], recv_sem=recv_sems.at[a * 7 + k],
                device_id=to, device_id_type=MESH_ID)

        mine = [pltpu.make_async_copy(ins[a], slot(a, me), local_sems.at[a]) for a in range(n)]
        for cp in mine:
            cp.start()
        first = []
        for a in range(n):
            first.append(copy(a, 0, me, sibling, src=ins[a]))
            first += [copy(a, 1 + j, me, (*chip, c), src=ins[a]) for j, chip in enumerate(chips)]
        for cp in first:
            cp.start()
        passed = []
        for j, chip in enumerate(chips):
            for a in range(n):
                copy(a, 1 + j, (*chip, c), me).wait_recv()
                fwd = copy(a, 4 + j, (*chip, c), sibling)
                fwd.start()
                passed.append(fwd)
        for a in range(n):
            copy(a, 0, sibling, me).wait_recv()
            for j, chip in enumerate(chips):
                copy(a, 4 + j, (*chip, 1 - c), me).wait_recv()
        for cp in first + passed:
            cp.wait_send()
        for cp in mine:
            cp.wait()

    out_shape = [jax.ShapeDtypeStruct((N_DEV,) + s.shape, s.dtype) for s in shards]
    return pl.pallas_call(
        body, in_specs=[HBM_SPEC] * n, out_specs=[HBM_SPEC] * n, out_shape=out_shape,
        scratch_shapes=[pltpu.SemaphoreType.DMA((7 * n,)), pltpu.SemaphoreType.DMA((7 * n,)),
                        pltpu.SemaphoreType.DMA((n,))],
        name="all_gather_weights",
    )(*shards)


def exchange_last(jobs):
    copies = PeerCopies(jobs)
    n = copies.n

    def body(*refs):
        ins, outs, sems = refs[:n], refs[n:2 * n], refs[2 * n:]
        copies.start(ins, outs, sems)
        copies.wait(ins, outs, sems)

    return pl.pallas_call(
        body, in_specs=[HBM_SPEC] * n, out_specs=[HBM_SPEC] * n, out_shape=copies.out_shape,
        scratch_shapes=copies.scratch, name="exchange_last",
    )(*copies.arrays)


def ffn_fwd(x, pre_g, post_g, w1, w2, copies=None):
    T = x.shape[0]
    tm = min(TM_FFN_FWD, T)

    def body(x_ref, pg_ref, qg_ref, w1_ref, w2_ref, xo_ref, a_ref, y_ref):
        xv = x_ref[...]
        h, _, _ = _rms(xv, pg_ref[...])
        hb = h.astype(BF16)
        acc = jnp.zeros((tm, D), F32)
        for c0, cw in FF_CHUNKS:
            g = _dot_nt(hb, w1_ref[c0:c0 + cw, :])
            u = _dot_nt(hb, w1_ref[FF + c0:FF + c0 + cw, :])
            a_ref[:, c0:c0 + cw] = g.astype(BF16)
            a_ref[:, FF + c0:FF + c0 + cw] = u.astype(BF16)
            s = (g * _sigmoid(g) * u).astype(BF16)
            acc = acc + _dot(s, w2_ref[c0:c0 + cw, :])
        y_ref[...] = acc.astype(BF16)
        o, _, _ = _rms(acc, qg_ref[...])
        xo_ref[...] = xv + 0.5 * o

    return _call(
        body, (T // tm,),
        [_row_spec(tm, D), _full_spec((1, D)), _full_spec((1, D)), _whole_spec(w1), _whole_spec(w2)],
        (x, pre_g, post_g, w1, w2),
        [_row_spec(tm, D), _row_spec(tm, 2 * FF), _row_spec(tm, D)],
        [jax.ShapeDtypeStruct((T, D), F32), jax.ShapeDtypeStruct((T, 2 * FF), BF16),
         jax.ShapeDtypeStruct((T, D), BF16)],
        "ffn_fwd", copies=copies)


def ffn_bwd(dxo, x, y, a, pre_g, post_g, w1, w2, copies=None):
    T = x.shape[0]
    tm = min(TM_FFN_BWD, T)

    def body(dxo_ref, x_ref, y_ref, a_ref, pg_ref, qg_ref, w1_ref, w2_ref,
             dx_ref, da_ref, s_ref, dy_ref, hb_ref, dpg_ref, dqg_ref):
        @pl.when(pl.program_id(0) == 0)
        def _():
            dpg_ref[...] = jnp.zeros_like(dpg_ref)
            dqg_ref[...] = jnp.zeros_like(dqg_ref)

        dxo = dxo_ref[...]
        qg = qg_ref[...]
        _, ny, ry = _rms(y_ref[...].astype(F32), qg)
        dn = 0.5 * dxo
        dqg_ref[...] += _colsum(dn * ny)
        dyb = _rms_bwd(ny, ry, qg, dn).astype(BF16)
        dy_ref[...] = dyb
        pg = pg_ref[...]
        h, nx, rx = _rms(x_ref[...], pg)
        hb_ref[...] = h.astype(BF16)
        dh = jnp.zeros((tm, D), F32)
        for c0, cw in FF_CHUNKS:
            ds = _dot_nt(dyb, w2_ref[c0:c0 + cw, :])
            g = a_ref[:, c0:c0 + cw].astype(F32)
            u = a_ref[:, FF + c0:FF + c0 + cw].astype(F32)
            sg = _sigmoid(g)
            si = g * sg
            s_ref[:, c0:c0 + cw] = (si * u).astype(BF16)
            dg = (ds * u * (sg * (1.0 + g * (1.0 - sg)))).astype(BF16)
            du = (ds * si).astype(BF16)
            da_ref[:, c0:c0 + cw] = dg
            da_ref[:, FF + c0:FF + c0 + cw] = du
            dh = dh + _dot(dg, w1_ref[c0:c0 + cw, :]) + _dot(du, w1_ref[FF + c0:FF + c0 + cw, :])
        dpg_ref[...] += _colsum(dh * nx)
        dx_ref[...] = dxo + _rms_bwd(nx, rx, pg, dh)

    return _call(
        body, (T // tm,),
        [_row_spec(tm, D), _row_spec(tm, D), _row_spec(tm, D), _row_spec(tm, 2 * FF),
         _full_spec((1, D)), _full_spec((1, D)), _whole_spec(w1), _whole_spec(w2)],
        (dxo, x, y, a, pre_g, post_g, w1, w2),
        [_row_spec(tm, D), _row_spec(tm, 2 * FF), _row_spec(tm, FF), _row_spec(tm, D), _row_spec(tm, D),
         _full_spec((1, D)), _full_spec((1, D))],
        [jax.ShapeDtypeStruct((T, D), F32), jax.ShapeDtypeStruct((T, 2 * FF), BF16),
         jax.ShapeDtypeStruct((T, FF), BF16), jax.ShapeDtypeStruct((T, D), BF16),
         jax.ShapeDtypeStruct((T, D), BF16), jax.ShapeDtypeStruct((1, D), F32),
         jax.ShapeDtypeStruct((1, D), F32)],
        "ffn_bwd", copies=copies)


def mix_in_fwd(x, pre_g, win, copies=None):
    T = x.shape[0]
    tm = min(TM_FFN_FWD, T)

    def body(x_ref, pg_ref, w_ref, z_ref):
        h, _, _ = _rms(x_ref[...], pg_ref[...])
        hb = h.astype(BF16)
        for w0, n, z0 in WIN_SEGMENTS:
            z_ref[:, z0:z0 + n] = _dot_nt(hb, w_ref[w0:w0 + n, :]).astype(BF16)

    return _call(
        body, (T // tm,), [_row_spec(tm, D), _full_spec((1, D)), _whole_spec(win)], (x, pre_g, win),
        [_row_spec(tm, IN_WIDTH)], [jax.ShapeDtypeStruct((T, IN_WIDTH), BF16)], "mix_in_fwd", copies=copies)


STACK = Q_PER_KV // 2
SROWS = STACK * BLK


def _band_mask(first):
    qi = lax.broadcasted_iota(jnp.int32, (SROWS, 2 * BLK), 0) & (BLK - 1)
    kj = lax.broadcasted_iota(jnp.int32, (SROWS, 2 * BLK), 1)
    band = (kj > qi) & (kj <= qi + BLK)
    if first is None:
        return band
    return band & (kj >= BLK * first.astype(jnp.int32))


def _lane_half(rows):
    return lax.broadcasted_iota(jnp.int32, (rows, BLK), 1) // HEAD_DIM


def _stack(ref, b, g):
    return jnp.concatenate([ref[b * BLK:(b + 1) * BLK, (STACK * g + j) * BLK:(STACK * g + j + 1) * BLK]
                            for j in range(STACK)], axis=0)


def _placed(pair, g, hp, fill):
    src = pair if hp == g else pltpu.roll(pair, HEAD_DIM, 1)
    return jnp.where(_lane_half(2 * BLK) == hp, src, fill).astype(BF16)


def _sink_column(sink_ref, g, hp):
    rb = lax.broadcasted_iota(jnp.int32, (SROWS, 1), 0) // BLK
    col = jnp.full((SROWS, 1), sink_ref[Q_PER_KV * g + hp], F32)
    for j in range(1, STACK):
        col = jnp.where(rb == j, sink_ref[Q_PER_KV * g + 2 * j + hp], col)
    return col


def _attn_scores(qs, kz, sink_col, mask):
    s = jnp.where(mask, _dot_nt(qs, kz) * ATTN_SCALE, MASK_VALUE)
    m = jnp.maximum(jnp.max(s, axis=-1, keepdims=True), sink_col)
    return jnp.exp(s - m), jnp.exp(sink_col - m)


def _kv_specs(tq, nb):
    kv_blk = ZKV // (2 * BLK)
    return [pl.BlockSpec((tq, 2 * BLK), lambda i: (i, kv_blk)),
            pl.BlockSpec((BLK, 2 * BLK), lambda i: (jnp.maximum(i * nb - 1, 0), kv_blk))]


def attn_fwd(z, sinks):
    T = z.shape[0]
    tq = min(TQ_ATTN, T)
    nb = tq // BLK

    def body(sink_ref, q_ref, kv_ref, kvh_ref, o_ref):
        i = pl.program_id(0)
        low = _lane_half(SROWS) == 0
        for b in range(nb):
            kvp = kvh_ref[...] if b == 0 else kv_ref[(b - 1) * BLK:b * BLK, :]
            kv2 = jnp.concatenate([kvp, kv_ref[b * BLK:(b + 1) * BLK, :]], axis=0).astype(F32)
            mask = _band_mask(i == 0) if b == 0 else _band_mask(None)
            for g in range(N_KV_HEADS):
                qs = _stack(q_ref, b, g)
                r, e = [], []
                for hp in range(2):
                    p, e_sink = _attn_scores(qs, _placed(kv2[:, 0:BLK], g, hp, 0.0), _sink_column(sink_ref, g, hp), mask)
                    r.append(_dot(p.astype(BF16), _placed(kv2[:, BLK:2 * BLK], g, hp, 1.0)))
                    e.append(e_sink)
                den = pltpu.roll(jnp.where(low, r[1], r[0]), HEAD_DIM, 1) + jnp.where(low, e[0], e[1])
                out = (jnp.where(low, r[0], r[1]) * (1.0 / den)).astype(BF16)
                for j in range(STACK):
                    o_ref[b * BLK:(b + 1) * BLK, (STACK * g + j) * BLK:(STACK * g + j + 1) * BLK] = \
                        out[j * BLK:(j + 1) * BLK, :]

    return pl.pallas_call(
        body, grid=(T // tq,),
        in_specs=[pl.BlockSpec(memory_space=pltpu.SMEM), _row_spec(tq, D)] + _kv_specs(tq, nb),
        out_specs=_row_spec(tq, D),
        out_shape=jax.ShapeDtypeStruct((T, D), BF16),
        name="attn_fwd", compiler_params=_params(1),
    )(sinks, z, z, z)


def attn_bwd(z, ya, dya, dzb, sinks):
    T = z.shape[0]
    tq = min(TQ_ATTN, T)
    nb = tq // BLK
    nt = T // tq
    mid = ZKV - ZU

    def body(sink_ref, q_ref, kv_ref, kvh_ref, y_ref, dy_ref, dzb_ref, dz_ref, dsink_ref, acc_ref, carry_ref):
        i = pl.program_id(0)
        first_tile = i == nt - 1

        @pl.when(i == 0)
        def _():
            carry_ref[...] = jnp.zeros_like(carry_ref)
            dsink_ref[...] = jnp.zeros_like(dsink_ref)

        acc_ref[...] = jnp.zeros_like(acc_ref)
        lane = lax.broadcasted_iota(jnp.int32, (1, BLK), 1)
        dsink = jnp.zeros((1, BLK), F32)
        half = _lane_half(SROWS)
        low_kv = _lane_half(2 * BLK) == 0
        for b in range(nb):
            kvp = kvh_ref[...] if b == 0 else kv_ref[(b - 1) * BLK:b * BLK, :]
            kv2 = jnp.concatenate([kvp, kv_ref[b * BLK:(b + 1) * BLK, :]], axis=0).astype(F32)
            mask = _band_mask(first_tile) if b == 0 else _band_mask(None)
            dk_groups, dv_groups = [], []
            for g in range(N_KV_HEADS):
                qs = _stack(q_ref, b, g)
                dys = _stack(dy_ref, b, g)
                dyy = dys.astype(F32) * _stack(y_ref, b, g).astype(F32)
                dq = jnp.zeros((SROWS, BLK), F32)
                dk_raw, dv_raw = [], []
                for hp in range(2):
                    kz = _placed(kv2[:, 0:BLK], g, hp, 0.0)
                    p, e_sink = _attn_scores(qs, kz, _sink_column(sink_ref, g, hp), mask)
                    inv = 1.0 / (jnp.sum(p, axis=-1, keepdims=True) + e_sink)
                    pn = p * inv
                    delta = jnp.sum(jnp.where(half == hp, dyy, 0.0), axis=-1, keepdims=True)
                    dp = _dot_nt(dys, _placed(kv2[:, BLK:2 * BLK], g, hp, 0.0))
                    ds = (pn * (dp - delta)).astype(BF16)
                    sink_term = e_sink * inv * delta
                    for j in range(STACK):
                        dsink = dsink + jnp.where(lane == Q_PER_KV * g + 2 * j + hp,
                                                  -_colsum(sink_term[j * BLK:(j + 1) * BLK, :]), 0.0)
                    dq = dq + _dot(ds, kz)
                    dk_raw.append(_dot_tn(ds, qs))
                    dv_raw.append(_dot_tn(pn.astype(BF16), dys))
                tk = jnp.where(low_kv, dk_raw[0], dk_raw[1])
                tv = jnp.where(low_kv, dv_raw[0], dv_raw[1])
                dk_groups.append(tk + pltpu.roll(tk, HEAD_DIM, 1))
                dv_groups.append(tv + pltpu.roll(tv, HEAD_DIM, 1))
                dqb = (dq * ATTN_SCALE).astype(BF16)
                for j in range(STACK):
                    dz_ref[b * BLK:(b + 1) * BLK, DZ_Q + (STACK * g + j) * BLK:DZ_Q + (STACK * g + j + 1) * BLK] = \
                        dqb[j * BLK:(j + 1) * BLK, :]
            acc_ref[b * BLK:(b + 2) * BLK, 0:BLK] += jnp.where(low_kv, dk_groups[0], dk_groups[1]) * ATTN_SCALE
            acc_ref[b * BLK:(b + 2) * BLK, BLK:2 * BLK] += jnp.where(low_kv, dv_groups[0], dv_groups[1])
        dsink_ref[...] += dsink
        dz_ref[:, DZ_MID:IN_WIDTH] = dzb_ref[...]
        if nb > 1:
            dz_ref[0:tq - BLK, DZ_KV:DZ_MID] = acc_ref[BLK:tq, :].astype(BF16)
        dz_ref[tq - BLK:tq, DZ_KV:DZ_MID] = (acc_ref[tq:tq + BLK, :] + carry_ref[...]).astype(BF16)
        carry_ref[...] = acc_ref[0:BLK, :]

    kv_blk = ZKV // (2 * BLK)
    return pl.pallas_call(
        body, grid=(nt,),
        in_specs=[pl.BlockSpec(memory_space=pltpu.SMEM),
                  pl.BlockSpec((tq, D), lambda i: (nt - 1 - i, 0)),
                  pl.BlockSpec((tq, 2 * BLK), lambda i: (nt - 1 - i, kv_blk)),
                  pl.BlockSpec((BLK, 2 * BLK), lambda i: (jnp.maximum((nt - 1 - i) * nb - 1, 0), kv_blk)),
                  pl.BlockSpec((tq, D), lambda i: (nt - 1 - i, 0)),
                  pl.BlockSpec((tq, D), lambda i: (nt - 1 - i, 0)),
                  pl.BlockSpec((tq, mid), lambda i: (nt - 1 - i, 0))],
        out_specs=[pl.BlockSpec((tq, IN_WIDTH), lambda i: (nt - 1 - i, 0)), _full_spec((1, BLK))],
        out_shape=[jax.ShapeDtypeStruct((T, IN_WIDTH), BF16), jax.ShapeDtypeStruct((1, BLK), F32)],
        scratch_shapes=[pltpu.VMEM((tq + BLK, 2 * BLK), F32), pltpu.VMEM((BLK, 2 * BLK), F32)],
        name="attn_bwd", compiler_params=_params(1),
    )(sinks, z, z, z, ya, dya, dzb)


def _to_group_lanes(v, g, nch):
    return jnp.concatenate([v[n * BLK:(n + 1) * BLK, g * BLK:(g + 1) * BLK] for n in range(nch)], axis=1)


def _from_group_lanes(per_group, nch):
    rows = [jnp.concatenate([per_group[g][:, n * BLK:(n + 1) * BLK] for g in range(SGU_GROUPS)], axis=1)
            for n in range(nch)]
    return jnp.concatenate(rows, axis=0)


def _tril_bf16(w_ref, g):
    t = lax.broadcasted_iota(jnp.int32, (BLK, BLK), 0)
    s = lax.broadcasted_iota(jnp.int32, (BLK, BLK), 1)
    return jnp.where(t >= s, w_ref[g], 0.0).astype(BF16)


def _sgu_norm(v_s, ln_g, ln_b):
    cdf, pdf = _gelu_parts(v_s)
    gv = v_s * cdf
    xc = gv - jnp.mean(gv, axis=-1, keepdims=True)
    rstd = lax.rsqrt(jnp.mean(xc * xc, axis=-1, keepdims=True) + LN_EPS)
    nhat = xc * rstd
    return nhat * ln_g + ln_b, nhat, rstd, cdf + v_s * pdf


def _sgu_gate(vn, w_ref, bb_ref, nch):
    vnb = vn.astype(BF16)
    return _from_group_lanes(
        [_dot(_tril_bf16(w_ref, g), _to_group_lanes(vnb, g, nch)) + jnp.tile(bb_ref[g], (1, nch))
         for g in range(SGU_GROUPS)], nch)


def mix_fwd_out(x, z, ya, post_g, ln_g, ln_b, sgu_w, sgu_bb, wa, ws, wo, copies=None):
    T = x.shape[0]
    tm = min(TM_MIX, T)
    nch = tm // BLK

    def body(x_ref, us_ref, vs_ref, ga_ref, gb_ref, ya_ref, qg_ref, lg_ref, lb_ref, w_ref, bb_ref,
             wa_ref, ws_ref, wo_ref, xo_ref, ysg_ref, pa_ref, pb_ref, o_ref):
        vn, _, _, _ = _sgu_norm(vs_ref[...].astype(F32), lg_ref[...], lb_ref[...])
        gate = _sgu_gate(vn, w_ref, bb_ref, nch)
        us = us_ref[...].astype(F32)
        cdf, _ = _gelu_parts(us)
        ysg = (us * cdf * gate).astype(BF16)
        ysg_ref[...] = ysg
        pa = _dot(ya_ref[...], wa_ref[...])
        pb = _dot(ysg, ws_ref[...])
        pa_ref[...] = pa.astype(BF16)
        pb_ref[...] = pb.astype(BF16)
        merged = _sigmoid(ga_ref[...].astype(F32)) * pa + _sigmoid(gb_ref[...].astype(F32)) * pb
        o = _dot(merged.astype(BF16), wo_ref[...])
        o_ref[...] = o.astype(BF16)
        on, _, _ = _rms(o, qg_ref[...])
        xo_ref[...] = x_ref[...] + on

    zspec = lambda start: _row_spec(tm, D, start // D)
    act = jax.ShapeDtypeStruct((T, D), BF16)
    return _call(
        body, (T // tm,),
        [_row_spec(tm, D), zspec(ZU), zspec(ZV), zspec(ZGA), zspec(ZGB), _row_spec(tm, D),
         _full_spec((1, D)), _full_spec((1, D)), _full_spec((1, D)),
         _full_spec((SGU_GROUPS, BLK, BLK)), _full_spec((SGU_GROUPS, BLK, BLK)),
         _whole_spec(wa), _whole_spec(ws), _whole_spec(wo)],
        (x, z, z, z, z, ya, post_g, ln_g, ln_b, sgu_w, sgu_bb, wa, ws, wo),
        [_row_spec(tm, D)] * 5, [jax.ShapeDtypeStruct((T, D), F32), act, act, act, act],
        "mix_fwd_out", copies=copies)


def mix_bwd_out(dxo, z, o, pa, pb, post_g, ln_g, ln_b, sgu_w, sgu_bb, wa, ws, wo, copies=None):
    T = dxo.shape[0]
    tm = min(TM_MIX_BWD, T)
    nch = tm // BLK
    mid = ZKV - ZU

    def body(dxo_ref, us_ref, vs_ref, ga_ref, gb_ref, o_ref, pa_ref, pb_ref, qg_ref, lg_ref, lb_ref, w_ref, bb_ref,
             wa_ref, ws_ref, wo_ref,
             dzb_ref, dya_ref, mg_ref, do_ref, dpa_ref, dpb_ref, dqg_ref, dlg_ref, dlb_ref, dw_ref, dbb_ref):
        @pl.when(pl.program_id(0) == 0)
        def _():
            for r in (dqg_ref, dlg_ref, dlb_ref, dw_ref, dbb_ref):
                r[...] = jnp.zeros_like(r)

        qg = qg_ref[...]
        dxo = dxo_ref[...]
        _, no, ro = _rms(o_ref[...].astype(F32), qg)
        dqg_ref[...] += _colsum(dxo * no)
        dob = _rms_bwd(no, ro, qg, dxo).astype(BF16)
        do_ref[...] = dob
        dmerged = _dot_nt(dob, wo_ref[...])
        sa = _sigmoid(ga_ref[...].astype(F32))
        sb = _sigmoid(gb_ref[...].astype(F32))
        pa = pa_ref[...].astype(F32)
        pb = pb_ref[...].astype(F32)
        mg_ref[...] = (sa * pa + sb * pb).astype(BF16)
        dpa = (dmerged * sa).astype(BF16)
        dpb = (dmerged * sb).astype(BF16)
        dpa_ref[...] = dpa
        dpb_ref[...] = dpb
        dzb_ref[:, ZGA - ZU:ZGA - ZU + D] = (dmerged * pa * sa * (1.0 - sa)).astype(BF16)
        dzb_ref[:, ZGB - ZU:ZGB - ZU + D] = (dmerged * pb * sb * (1.0 - sb)).astype(BF16)
        dya_ref[...] = _dot_nt(dpa, wa_ref[...]).astype(BF16)
        dysg = _dot_nt(dpb, ws_ref[...])

        lg = lg_ref[...]
        vn, nhat, rstd, dgelu_v = _sgu_norm(vs_ref[...].astype(F32), lg, lb_ref[...])
        gate = _sgu_gate(vn, w_ref, bb_ref, nch)
        us = us_ref[...].astype(F32)
        cdf, pdf = _gelu_parts(us)
        dzb_ref[:, 0:D] = (dysg * gate * (cdf + us * pdf)).astype(BF16)
        dgate = (dysg * (us * cdf)).astype(BF16)
        vnb = vn.astype(BF16)
        t = lax.broadcasted_iota(jnp.int32, (BLK, BLK), 0)
        s = lax.broadcasted_iota(jnp.int32, (BLK, BLK), 1)
        dvn_groups = []
        for g in range(SGU_GROUPS):
            dgl = _to_group_lanes(dgate, g, nch)
            dbb_ref[g] += jnp.broadcast_to(jnp.sum(dgl.astype(F32), axis=-1, keepdims=True), (BLK, BLK))
            dw_ref[g] += jnp.where(t >= s, _dot_nt(dgl, _to_group_lanes(vnb, g, nch)), 0.0)
            dvn_groups.append(_dot_tn(_tril_bf16(w_ref, g), dgl))
        dvn = _from_group_lanes(dvn_groups, nch)
        dlg_ref[...] += _colsum(dvn * nhat)
        dlb_ref[...] += _colsum(dvn)
        dnh = dvn * lg
        dgv = rstd * (dnh - jnp.mean(dnh, axis=-1, keepdims=True) - nhat * jnp.mean(dnh * nhat, axis=-1, keepdims=True))
        dzb_ref[:, ZV - ZU:ZV - ZU + D] = (dgv * dgelu_v).astype(BF16)

    zspec = lambda start: _row_spec(tm, D, start // D)
    act = jax.ShapeDtypeStruct((T, D), BF16)
    grp = jax.ShapeDtypeStruct((SGU_GROUPS, BLK, BLK), F32)
    vec = jax.ShapeDtypeStruct((1, D), F32)
    return _call(
        body, (T // tm,),
        [_row_spec(tm, D), zspec(ZU), zspec(ZV), zspec(ZGA), zspec(ZGB),
         _row_spec(tm, D), _row_spec(tm, D), _row_spec(tm, D),
         _full_spec((1, D)), _full_spec((1, D)), _full_spec((1, D)),
         _full_spec((SGU_GROUPS, BLK, BLK)), _full_spec((SGU_GROUPS, BLK, BLK)),
         _whole_spec(wa), _whole_spec(ws), _whole_spec(wo)],
        (dxo, z, z, z, z, o, pa, pb, post_g, ln_g, ln_b, sgu_w, sgu_bb, wa, ws, wo),
        [_row_spec(tm, mid)] + [_row_spec(tm, D)] * 5 + [_full_spec((1, D))] * 3
        + [_full_spec((SGU_GROUPS, BLK, BLK))] * 2,
        [jax.ShapeDtypeStruct((T, mid), BF16), act, act, act, act, act, vec, vec, vec, grp, grp],
        "mix_bwd_out", copies=copies)


def mix_in_bwd(dxo, x, dz, pre_g, win):
    T = x.shape[0]
    tm = min(TM_MIX, T)

    def body(dxo_ref, x_ref, dz_ref, pg_ref, w_ref, dx_ref, hb_ref, dpg_ref):
        @pl.when(pl.program_id(0) == 0)
        def _():
            dpg_ref[...] = jnp.zeros_like(dpg_ref)

        pg = pg_ref[...]
        h, nx, rx = _rms(x_ref[...], pg)
        hb_ref[...] = h.astype(BF16)
        dh = jnp.zeros((tm, D), F32)
        for c0, cw in WIN_CHUNKS:
            dh = dh + _dot(dz_ref[:, c0:c0 + cw], w_ref[c0:c0 + cw, :])
        dpg_ref[...] += _colsum(dh * nx)
        dx_ref[...] = dxo_ref[...] + _rms_bwd(nx, rx, pg, dh)

    return pl.pallas_call(
        body, grid=(T // tm,),
        in_specs=[_row_spec(tm, D), _row_spec(tm, D), _row_spec(tm, IN_WIDTH), _full_spec((1, D)), _whole_spec(win)],
        out_specs=[_row_spec(tm, D), _row_spec(tm, D), _full_spec((1, D))],
        out_shape=[jax.ShapeDtypeStruct((T, D), F32), jax.ShapeDtypeStruct((T, D), BF16),
                   jax.ShapeDtypeStruct((1, D), F32)],
        name="mix_in_bwd", compiler_params=_params(1),
    )(dxo, x, dz, pre_g, win)


def wgrad(wide, narrow, name, tmo):
    T, N = wide.shape
    M = narrow.shape[1]
    tk = min(TK_WGRAD, T)
    nk = T // tk
    chunk = max(c for c in range(BLK, 1792 + 1, BLK) if N % c == 0)

    def body(a_ref, b_ref, o_ref, acc_ref):
        k = pl.program_id(1)

        @pl.when(k == 0)
        def _():
            acc_ref[...] = jnp.zeros_like(acc_ref)

        acc_ref[...] += _dot_tn(a_ref[...], b_ref[...])

        @pl.when(k == nk - 1)
        def _():
            for c0 in range(0, N, chunk):
                o_ref[c0:c0 + chunk, :] = acc_ref[:, c0:c0 + chunk].T.astype(BF16)

    return pl.pallas_call(
        body, grid=(M // tmo, nk),
        in_specs=[pl.BlockSpec((tk, tmo), lambda m, k: (k, m)), pl.BlockSpec((tk, N), lambda m, k: (k, 0))],
        out_specs=pl.BlockSpec((N, tmo), lambda m, k: (0, m)),
        out_shape=jax.ShapeDtypeStruct((N, M), BF16),
        scratch_shapes=[pltpu.VMEM((tmo, N), F32)],
        name=name, compiler_params=_params(2),
    )(narrow, wide)


def loss_head(y, target):
    T = y.shape[0]
    tm = min(TM_FFN_FWD, T)

    def body(y_ref, t_ref, sq_ref, dy_ref):
        @pl.when(pl.program_id(0) == 0)
        def _():
            sq_ref[...] = jnp.zeros_like(sq_ref)

        err = y_ref[...] - t_ref[...]
        sq_ref[...] += _colsum(err * err)
        dy_ref[...] = err * (1.0 / D)

    return pl.pallas_call(
        body, grid=(T // tm,), in_specs=[_row_spec(tm, D), _row_spec(tm, D)],
        out_specs=[_full_spec((1, D)), _row_spec(tm, D)],
        out_shape=[jax.ShapeDtypeStruct((1, D), F32), jax.ShapeDtypeStruct((T, D), F32)],
        name="loss_head", compiler_params=_params(1),
    )(y, target)


def adamw_sum(parts, w, m, v, name):
    layers, rows, cols = w.shape
    tr = _row_tile(rows, TR_ADAM)
    nr = rows // tr
    c1 = 1.0 - ADAM_B1 ** ADAM_STEP
    c2 = 1.0 - ADAM_B2 ** ADAM_STEP

    def body(*refs):
        p_refs = refs[:layers]
        w_ref, m_ref, v_ref, g_ref, d_ref, nm_ref, nv_ref = refs[layers:]
        for k in range(layers):
            @pl.when(pl.program_id(0) == k)
            def _(p_ref=p_refs[k]):
                g = p_ref[0].astype(F32)
                for j in range(1, N_DEV):
                    g = g + p_ref[j].astype(F32)
                nm = ADAM_B1 * m_ref[...] + (1.0 - ADAM_B1) * g
                nv = ADAM_B2 * v_ref[...] + (1.0 - ADAM_B2) * (g * g)
                g_ref[...] = g
                nm_ref[...] = nm
                nv_ref[...] = nv
                d_ref[...] = -ADAM_LR * ((nm / c1) / (jnp.sqrt(nv / c2) + ADAM_EPS) + ADAM_WD * w_ref[...])

    def part_spec(k):
        return pl.BlockSpec((N_DEV, tr, cols),
                            lambda l, i: (0, jnp.where(l < k, 0, jnp.where(l == k, i, nr - 1)), 0))

    spec = pl.BlockSpec((None, tr, cols), lambda l, i: (l, i, 0))
    out = jax.ShapeDtypeStruct((layers, rows, cols), F32)
    return pl.pallas_call(
        body, grid=(layers, nr),
        in_specs=[part_spec(k) for k in range(layers)] + [spec, spec, spec],
        out_specs=[spec] * 4, out_shape=[out] * 4, name=name, compiler_params=_params(2),
    )(*parts, w, m, v)


def _pack_small(p):
    layers = []
    for l in range(DEPTH):
        rows = [p[n][l].reshape(1, D) for n in SMALL_VEC]
        rows.append(p["sgu_b"][l].reshape(1, D))
        rows.append(jnp.pad(p["attn_sinks"][l].reshape(1, N_Q_HEADS), ((0, 0), (0, D - N_Q_HEADS))))
        rows.append(p["sgu_w"][l].reshape(BLK, D))
        used = len(SMALL_VEC) + 2 + BLK
        rows.append(jnp.zeros((SMALL_ROWS - used, D), F32))
        layers.append(jnp.concatenate(rows, axis=0))
    return jnp.concatenate(layers, axis=0)


def _unpack_small(packed):
    a = packed.reshape(DEPTH, SMALL_ROWS, D)
    out = {n: a[:, i, :] for i, n in enumerate(SMALL_VEC)}
    k = len(SMALL_VEC)
    out["sgu_b"] = a[:, k, :].reshape(DEPTH, SGU_GROUPS, BLK)
    out["attn_sinks"] = a[:, k + 1, :N_Q_HEADS]
    out["sgu_w"] = a[:, k + 2:k + 2 + BLK, :].reshape(DEPTH, SGU_GROUPS, BLK, BLK)
    return out


WEIGHT_NAMES = ("ffn1_pre_g", "ffn1_w1", "ffn1_w2", "ffn1_post_g", "mix_pre_g", "w_in", "attn_sinks", "sgu_ln_g",
                "sgu_ln_b", "sgu_w", "sgu_b", "w_attn_branch", "w_sgu_branch", "w_out", "mix_post_g", "ffn2_pre_g",
                "ffn2_w1", "ffn2_w2", "ffn2_post_g")
COL_SHARDED = ("ffn1_w1", "ffn2_w1", "w_in")
ROW_SHARDED = ("ffn1_w2", "ffn2_w2", "w_attn_branch", "w_sgu_branch", "w_out")
MATRICES = COL_SHARDED + ROW_SHARDED


def kernel(x, ffn1_pre_g, ffn1_w1, ffn1_w2, ffn1_post_g, mix_pre_g, w_in, attn_sinks, sgu_ln_g, sgu_ln_b, sgu_w, sgu_b, w_attn_branch, w_sgu_branch, w_out, mix_post_g, ffn2_pre_g, ffn2_w1, ffn2_w2, ffn2_post_g, loss_target, m_ffn1_pre_g, m_ffn1_w1, m_ffn1_w2, m_ffn1_post_g, m_mix_pre_g, m_w_in, m_attn_sinks, m_sgu_ln_g, m_sgu_ln_b, m_sgu_w, m_sgu_b, m_w_attn_branch, m_w_sgu_branch, m_w_out, m_mix_post_g, m_ffn2_pre_g, m_ffn2_w1, m_ffn2_w2, m_ffn2_post_g, v_ffn1_pre_g, v_ffn1_w1, v_ffn1_w2, v_ffn1_post_g, v_mix_pre_g, v_w_in, v_attn_sinks, v_sgu_ln_g, v_sgu_ln_b, v_sgu_w, v_sgu_b, v_w_attn_branch, v_w_sgu_branch, v_w_out, v_mix_post_g, v_ffn2_pre_g, v_ffn2_w1, v_ffn2_w2, v_ffn2_post_g):
    w = dict(zip(WEIGHT_NAMES, (ffn1_pre_g, ffn1_w1, ffn1_w2, ffn1_post_g, mix_pre_g, w_in, attn_sinks, sgu_ln_g,
                                sgu_ln_b, sgu_w, sgu_b, w_attn_branch, w_sgu_branch, w_out, mix_post_g, ffn2_pre_g,
                                ffn2_w1, ffn2_w2, ffn2_post_g)))
    mom = dict(zip(WEIGHT_NAMES, (m_ffn1_pre_g, m_ffn1_w1, m_ffn1_w2, m_ffn1_post_g, m_mix_pre_g, m_w_in,
                                  m_attn_sinks, m_sgu_ln_g, m_sgu_ln_b, m_sgu_w, m_sgu_b, m_w_attn_branch,
                                  m_w_sgu_branch, m_w_out, m_mix_post_g, m_ffn2_pre_g, m_ffn2_w1, m_ffn2_w2,
                                  m_ffn2_post_g)))
    var = dict(zip(WEIGHT_NAMES, (v_ffn1_pre_g, v_ffn1_w1, v_ffn1_w2, v_ffn1_post_g, v_mix_pre_g, v_w_in,
                                  v_attn_sinks, v_sgu_ln_g, v_sgu_ln_b, v_sgu_w, v_sgu_b, v_w_attn_branch,
                                  v_w_sgu_branch, v_w_out, v_mix_post_g, v_ffn2_pre_g, v_ffn2_w1, v_ffn2_w2,
                                  v_ffn2_post_g)))
    T = x.shape[1]
    xs = x.reshape(T, D)
    target = loss_target.reshape(T, D)

    for n in COL_SHARDED:
        w[n], mom[n], var[n] = (jnp.swapaxes(t[n], 1, 2) for t in (w, mom, var))

    shard = {n: [cast_layer(w[n], l) for l in range(DEPTH)] for n in MATRICES}
    weights = [{} for _ in range(DEPTH)]
    for n, g in zip(MATRICES, all_gather_weights([shard[n][0] for n in MATRICES])):
        weights[0][n] = g.reshape(-1, D)

    def fetch(l, names):
        return PeerCopies([("all", shard[n][l]) for n in names]) if l < DEPTH else None

    def landed(l, names, arrivals):
        for n, g in zip(names, arrivals):
            weights[l][n] = g.reshape(-1, D)

    vec = lambda n, l: w[n][l].reshape(1, D)
    sgu_bb = [jnp.broadcast_to(w["sgu_b"][l][:, :, None], (SGU_GROUPS, BLK, BLK)) for l in range(DEPTH)]
    ffn1, ffn2, squares = ("ffn1_w1", "ffn1_w2"), ("ffn2_w1", "ffn2_w2"), ("w_attn_branch", "w_sgu_branch", "w_out")

    saved = []
    h = xs
    for l in range(DEPTH):
        wl = weights[l]
        x0 = h
        (x1, a1, y1), got = ffn_fwd(x0, vec("ffn1_pre_g", l), vec("ffn1_post_g", l), wl["ffn1_w1"], wl["ffn1_w2"],
                                    fetch(l + 1, ffn1))
        landed(l + 1, ffn1, got)
        (z,), got = mix_in_fwd(x1, vec("mix_pre_g", l), wl["w_in"], fetch(l + 1, ("w_in",)))
        landed(l + 1, ("w_in",), got)
        ya = attn_fwd(z, w["attn_sinks"][l])
        (x2, ysg, pa, pb, o), got = mix_fwd_out(
            x1, z, ya, vec("mix_post_g", l), vec("sgu_ln_g", l), vec("sgu_ln_b", l), w["sgu_w"][l], sgu_bb[l],
            wl["w_attn_branch"], wl["w_sgu_branch"], wl["w_out"], fetch(l + 1, squares))
        landed(l + 1, squares, got)
        (h, a2, y2), got = ffn_fwd(x2, vec("ffn2_pre_g", l), vec("ffn2_post_g", l), wl["ffn2_w1"], wl["ffn2_w2"],
                                   fetch(l + 1, ffn2))
        landed(l + 1, ffn2, got)
        saved.append((x0, a1, y1, x1, z, ya, ysg, pa, pb, o, x2, a2, y2))

    sq, dx = loss_head(h, target)
    loss = lax.psum(0.5 / D * jnp.sum(sq), ("x", "y", "c"))

    def ffn_wgrads(hb, da, s, dy):
        return [("own", wgrad(da, hb, "wgrad_ffn_w1", D // 2).reshape(N_DEV, W1_SHARD, D)),
                ("own", wgrad(s, dy, "wgrad_ffn_w2", D).reshape(N_DEV, W2_SHARD, D))]

    parts = {n: [None] * DEPTH for n in MATRICES}
    small = {n: [None] * DEPTH for n in WEIGHT_NAMES if n not in MATRICES}
    waiting = None
    for l in reversed(range(DEPTH)):
        wl = weights[l]
        x0, a1, y1, x1, z, ya, ysg, pa, pb, o, x2, a2, y2 = saved[l]
        (dx, da, s, dy, hb, dpg, dqg), got = ffn_bwd(
            dx, x2, y2, a2, vec("ffn2_pre_g", l), vec("ffn2_post_g", l), wl["ffn2_w1"], wl["ffn2_w2"],
            PeerCopies(waiting) if waiting else None)
        if waiting:
            parts["ffn1_w1"][l + 1], parts["ffn1_w2"][l + 1] = got
        small["ffn2_pre_g"][l], small["ffn2_post_g"][l] = dpg, dqg

        (dzb, dya, mg, dob, dpa, dpb, dqg, dlg, dlb, dsw, dsb), got = mix_bwd_out(
            dx, z, o, pa, pb, vec("mix_post_g", l), vec("sgu_ln_g", l), vec("sgu_ln_b", l), w["sgu_w"][l], sgu_bb[l],
            wl["w_attn_branch"], wl["w_sgu_branch"], wl["w_out"], PeerCopies(ffn_wgrads(hb, da, s, dy)))
        parts["ffn2_w1"][l], parts["ffn2_w2"][l] = got
        dz, dsink = attn_bwd(z, ya, dya, dzb, w["attn_sinks"][l])
        dx, hb, dpg = mix_in_bwd(dx, x1, dz, vec("mix_pre_g", l), wl["w_in"])
        small["mix_pre_g"][l], small["mix_post_g"][l] = dpg, dqg
        small["sgu_ln_g"][l], small["sgu_ln_b"][l] = dlg, dlb
        small["sgu_w"][l], small["sgu_b"][l] = dsw, dsb[:, :, 0]
        small["attn_sinks"][l] = dsink[0, :N_Q_HEADS]
        mixer = [("own", wgrad(dz, hb, "wgrad_w_in", D // 2).reshape(N_DEV, WIN_SHARD, D))]
        mixer += [("own", wgrad(act, cot, "wgrad_square", D).reshape(N_DEV, SQ_SHARD, D))
                  for act, cot in ((ya, dpa), (ysg, dpb), (mg, dob))]
        if l == 0:
            small["ffn1_pre_g"][0] = small["ffn1_post_g"][0] = jnp.zeros((1, D), F32)
            mixer.append(("all", _pack_small({n: jnp.stack(v, axis=0) for n, v in small.items()})))

        (dx, da, s, dy, hb, dpg, dqg), got = ffn_bwd(
            dx, x0, y1, a1, vec("ffn1_pre_g", l), vec("ffn1_post_g", l), wl["ffn1_w1"], wl["ffn1_w2"],
            PeerCopies(mixer))
        parts["w_in"][l], parts["w_attn_branch"][l], parts["w_sgu_branch"][l], parts["w_out"][l] = got[:4]
        if l > 0:
            small["ffn1_pre_g"][l], small["ffn1_post_g"][l] = dpg, dqg
        waiting = ffn_wgrads(hb, da, s, dy)
    grad_x = dx.reshape(x.shape)
    small_parts = got[4]

    def late_rows(pre, post):
        return jnp.concatenate([pre.reshape(1, D), post.reshape(1, D), jnp.zeros((LATE_ROWS - 2, D), F32)], axis=0)

    parts["ffn1_w1"][0], parts["ffn1_w2"][0], late_parts = exchange_last(waiting + [("all", late_rows(dpg, dqg))])

    grads, deltas, new_m, new_v = {}, {}, {}, {}
    for n in MATRICES:
        grads[n], deltas[n], new_m[n], new_v[n] = adamw_sum(parts[n], w[n], mom[n], var[n], "adamw_" + n)
    for n in COL_SHARDED:
        for out in (grads, deltas, new_m, new_v):
            out[n] = jnp.swapaxes(out[n], 1, 2)
    res = adamw_sum([small_parts], _pack_small(w)[None], _pack_small(mom)[None], _pack_small(var)[None],
                    "adamw_small")
    late = adamw_sum([late_parts], *[late_rows(t["ffn1_pre_g"][0], t["ffn1_post_g"][0])[None] for t in (w, mom, var)],
                     "adamw_late")
    for out, packed, late_out in zip((grads, deltas, new_m, new_v), res, late):
        out.update(_unpack_small(packed[0]))
        for row, n in enumerate(("ffn1_pre_g", "ffn1_post_g")):
            out[n] = jnp.concatenate([late_out[0, row:row + 1], out[n][1:]], axis=0)

    return (loss, grad_x, *[grads[n] for n in WEIGHT_NAMES], *[deltas[n] for n in WEIGHT_NAMES],
            *[new_m[n] for n in WEIGHT_NAMES], *[new_v[n] for n in WEIGHT_NAMES])
```

```python
import math

import jax
import jax.numpy as jnp
from jax import lax
from jax.experimental import pallas as pl
from jax.experimental.pallas import tpu as pltpu

F32 = jnp.float32
BF16 = jnp.bfloat16

N_DEV = 8
D = 1024
FF = 2816
DEPTH = 4
HEAD_DIM = 64
N_Q_HEADS = 16
N_KV_HEADS = 2
Q_PER_KV = N_Q_HEADS // N_KV_HEADS
BLK = 128
SGU_GROUPS = 8
IN_WIDTH = 5376
W1_SHARD = 2 * FF // N_DEV
WIN_SHARD = IN_WIDTH // N_DEV
W2_SHARD = FF // N_DEV
SQ_SHARD = D // N_DEV

RMS_EPS = 1e-6
LN_EPS = 1e-5
MASK_VALUE = -1e30
ATTN_SCALE = 1.0 / math.sqrt(HEAD_DIM)

ADAM_LR = 0.001
ADAM_B1 = 0.9
ADAM_B2 = 0.999
ADAM_EPS = 1e-08
ADAM_WD = 0.01
ADAM_STEP = 10

VMEM_LIMIT_V7X = 56 * 1024 * 1024

WIN_SEGMENTS = ((0, 1024, 0), (1024, 256, 5120), (1280, 1024, 1024), (2304, 1024, 2048), (3328, 1024, 3072),
                (4352, 1024, 4096))
ZQ, ZU, ZV, ZGA, ZGB, ZKV = 0, 1024, 2048, 3072, 4096, 5120
DZ_Q, DZ_KV, DZ_MID = 0, 1024, 1280

FF_CHUNKS = ((0, 1024), (1024, 1024), (2048, 768))
WIN_CHUNKS = ((0, 1792), (1792, 1792), (3584, 1792))

SMALL_ROWS = 144
LATE_ROWS = 16
SMALL_VEC = ("ffn1_pre_g", "ffn1_post_g", "mix_pre_g", "mix_post_g", "ffn2_pre_g", "ffn2_post_g",
             "sgu_ln_g", "sgu_ln_b")

TM_FFN_FWD = 512
TM_FFN_BWD = 256
TM_MIX = 512
TM_MIX_BWD = 256
TQ_ATTN = 512
TK_WGRAD = 1024
TR_ADAM = 256


def _params(n_grid, vmem=VMEM_LIMIT_V7X):
    return pltpu.CompilerParams(dimension_semantics=("arbitrary",) * n_grid, vmem_limit_bytes=vmem)


def _dot(a, b):
    return jnp.dot(a, b, preferred_element_type=F32)


def _dot_nt(a, b):
    return lax.dot_general(a, b, (((1,), (1,)), ((), ())), preferred_element_type=F32)


def _dot_tn(a, b):
    return lax.dot_general(a, b, (((0,), (0,)), ((), ())), preferred_element_type=F32)


def _rms(x, g):
    r = lax.rsqrt(jnp.mean(x * x, axis=-1, keepdims=True) + RMS_EPS)
    n = x * r
    return n * g, n, r


def _rms_bwd(n, r, g, dy):
    dn = dy * g
    return r * (dn - n * jnp.mean(dn * n, axis=-1, keepdims=True))


def _colsum(v):
    return jnp.sum(v, axis=0, keepdims=True)


def _sigmoid(v):
    return 1.0 / (1.0 + jnp.exp(-v))


def _gelu_parts(v):
    cdf = 0.5 * (1.0 + lax.erf(v * (1.0 / math.sqrt(2.0))))
    return cdf, jnp.exp(-0.5 * v * v) * (1.0 / math.sqrt(2.0 * math.pi))


def _row_tile(rows, cap):
    return max(t for t in range(16, min(rows, cap) + 1, 16) if rows % t == 0)


def _row_spec(tm, width, col_block=0):
    return pl.BlockSpec((tm, width), lambda i, cb=col_block: (i, cb))


def _full_spec(shape):
    nd = len(shape)
    return pl.BlockSpec(tuple(shape), lambda *_: (0,) * nd)


def _whole_spec(arr):
    nd = arr.ndim
    return pl.BlockSpec(tuple(arr.shape), lambda *_: (0,) * nd, pipeline_mode=pl.Buffered(1))


HBM_SPEC = pl.BlockSpec(memory_space=pltpu.HBM)
MESH_ID = pl.DeviceIdType.MESH
RELATIONS = tuple((rx, ry, rc) for rx in (0, 1) for ry in (0, 1) for rc in (0, 1))[1:]


class PeerCopies:
    def __init__(self, jobs):
        self.kinds = [k for k, _ in jobs]
        self.arrays = [a for _, a in jobs]
        self.n = len(jobs)
        self.out_shape = [jax.ShapeDtypeStruct((N_DEV,) + a.shape if k == "all" else a.shape, a.dtype)
                          for k, a in jobs]
        self.scratch = [pltpu.SemaphoreType.DMA((7 * self.n,)), pltpu.SemaphoreType.DMA((7 * self.n,)),
                        pltpu.SemaphoreType.DMA((self.n,))]

    def _copies(self, ins, outs, sems, arriving):
        send_sems, recv_sems, local_sems = sems
        x, y, c = lax.axis_index("x"), lax.axis_index("y"), lax.axis_index("c")
        me = 4 * x + 2 * y + c
        src = lambda a, d: ins[a] if self.kinds[a] == "all" else ins[a].at[d]
        if not arriving:
            local = [pltpu.make_async_copy(src(a, me), outs[a].at[me], local_sems.at[a]) for a in range(self.n)]
        remote = []
        for k, (rx, ry, rc) in enumerate(RELATIONS):
            tx, ty, tc = (1 - x if rx else x), (1 - y if ry else y), (1 - c if rc else c)
            peer = 4 * tx + 2 * ty + tc
            for a in range(self.n):
                from_slot, to_slot = (me, peer) if arriving else (peer, me)
                remote.append(pltpu.make_async_remote_copy(
                    src_ref=src(a, from_slot), dst_ref=outs[a].at[to_slot],
                    send_sem=send_sems.at[a * 7 + k], recv_sem=recv_sems.at[a * 7 + k],
                    device_id=(tx, ty, tc), device_id_type=MESH_ID))
        return remote if arriving else (local, remote)

    def start(self, ins, outs, sems):
        local, sends = self._copies(ins, outs, sems, False)
        for cp in local + sends:
            cp.start()

    def wait(self, ins, outs, sems):
        for cp in self._copies(ins, outs, sems, True):
            cp.wait_recv()
        local, sends = self._copies(ins, outs, sems, False)
        for cp in sends:
            cp.wait_send()
        for cp in local:
            cp.wait()


def _call(body, grid, in_specs, args, out_specs, out_shape, name, scratch=(), copies=None):
    if copies is None:
        outs = pl.pallas_call(body, grid=grid, in_specs=list(in_specs), out_specs=list(out_specs),
                              out_shape=list(out_shape), scratch_shapes=list(scratch), name=name,
                              compiler_params=_params(len(grid)))(*args)
        return outs, []
    n_in, n_out, n_scr, nc = len(in_specs), len(out_specs), len(scratch), copies.n

    def at_step(steps):
        hit = pl.program_id(0) == steps[0]
        for axis in range(1, len(grid)):
            hit = jnp.logical_and(hit, pl.program_id(axis) == steps[axis])
        return hit

    def hosted(*refs):
        ins, refs = refs[:n_in], refs[n_in:]
        c_in, refs = refs[:nc], refs[nc:]
        outs, refs = refs[:n_out], refs[n_out:]
        c_out, refs = refs[:nc], refs[nc:]
        scr, sems = refs[:n_scr], refs[n_scr:]

        @pl.when(at_step([0] * len(grid)))
        def _():
            copies.start(c_in, c_out, sems)

        body(*ins, *outs, *scr)

        @pl.when(at_step([g - 1 for g in grid]))
        def _():
            copies.wait(c_in, c_out, sems)

    outs = pl.pallas_call(hosted, grid=grid, in_specs=list(in_specs) + [HBM_SPEC] * nc,
                          out_specs=list(out_specs) + [HBM_SPEC] * nc,
                          out_shape=list(out_shape) + copies.out_shape,
                          scratch_shapes=list(scratch) + copies.scratch, name=name,
                          compiler_params=_params(len(grid)))(*args, *copies.arrays)
    return outs[:n_out], outs[n_out:]


def cast_layer(w, l):
    _, rows, cols = w.shape
    tr = _row_tile(rows, 256)

    def body(w_ref, o_ref):
        o_ref[...] = w_ref[...].astype(BF16)

    return pl.pallas_call(
        body, grid=(rows // tr,), in_specs=[pl.BlockSpec((None, tr, cols), lambda i: (l, i, 0))],
        out_specs=_row_spec(tr, cols), out_shape=jax.ShapeDtypeStruct((rows, cols), BF16),
        name="cast_layer", compiler_params=_params(1),
    )(w)


def all_gather_weights(shards):
    n = len(shards)

    def body(*refs):
        ins, outs = refs[:n], refs[n:2 * n]
        send_sems, recv_sems, local_sems = refs[2 * n:]
        x, y, c = lax.axis_index("x"), lax.axis_index("y"), lax.axis_index("c")
        me, sibling = (x, y, c), (x, y, 1 - c)
        chips = [(1 - x, y), (x, 1 - y), (1 - x, 1 - y)]

        def slot(a, owner):
            return outs[a].at[4 * owner[0] + 2 * owner[1] + owner[2]]

        def copy(a, k, owner, to, src=None):
            return pltpu.make_async_remote_copy(
                src_ref=slot(a, owner) if src is None else src, dst_ref=slot(a, owner),
                send_sem=send_sems.at[a * 7 + k], recv_sem=recv_sems.at[a * 7 + k],
                device_id=to, device_id_type=MESH_ID)

        mine = [pltpu.make_async_copy(ins[a], slot(a, me), local_sems.at[a]) for a in range(n)]
        for cp in mine:
            cp.start()
        first = []
        for a in range(n):
            first.append(copy(a, 0, me, sibling, src=ins[a]))
            first += [copy(a, 1 + j, me, (*chip, c), src=ins[a]) for j, chip in enumerate(chips)]
        for cp in first:
            cp.start()
        passed = []
        for j, chip in enumerate(chips):
            for a in range(n):
                copy(a, 1 + j, (*chip, c), me).wait_recv()
                fwd = copy(a, 4 + j, (*chip, c), sibling)
                fwd.start()
                passed.append(fwd)
        for a in range(n):
            copy(a, 0, sibling, me).wait_recv()
            for j, chip in enumerate(chips):
                copy(a, 4 + j, (*chip, 1 - c), me).wait_recv()
        for cp in first + passed:
            cp.wait_send()
        for cp in mine:
            cp.wait()

    out_shape = [jax.ShapeDtypeStruct((N_DEV,) + s.shape, s.dtype) for s in shards]
    return pl.pallas_call(
        body, in_specs=[HBM_SPEC] * n, out_specs=[HBM_SPEC] * n, out_shape=out_shape,
        scratch_shapes=[pltpu.SemaphoreType.DMA((7 * n,)), pltpu.SemaphoreType.DMA((7 * n,)),
                        pltpu.SemaphoreType.DMA((n,))],
        name="all_gather_weights",
    )(*shards)


def exchange_last(jobs):
    copies = PeerCopies(jobs)
    n = copies.n

    def body(*refs):
        ins, outs, sems = refs[:n], refs[n:2 * n], refs[2 * n:]
        copies.start(ins, outs, sems)
        copies.wait(ins, outs, sems)

    return pl.pallas_call(
        body, in_specs=[HBM_SPEC] * n, out_specs=[HBM_SPEC] * n, out_shape=copies.out_shape,
        scratch_shapes=copies.scratch, name="exchange_last",
    )(*copies.arrays)


def ffn_fwd(x, pre_g, post_g, w1, w2, copies=None):
    T = x.shape[0]
    tm = min(TM_FFN_FWD, T)

    def body(x_ref, pg_ref, qg_ref, w1_ref, w2_ref, xo_ref, a_ref, s_ref, y_ref):
        xv = x_ref[...]
        h, _, _ = _rms(xv, pg_ref[...])
        hb = h.astype(BF16)
        acc = jnp.zeros((tm, D), F32)
        for c0, cw in FF_CHUNKS:
            g = _dot_nt(hb, w1_ref[c0:c0 + cw, :])
            u = _dot_nt(hb, w1_ref[FF + c0:FF + c0 + cw, :])
            sg = _sigmoid(g)
            si = g * sg
            a_ref[:, c0:c0 + cw] = (u * (sg * (1.0 + g * (1.0 - sg)))).astype(BF16)
            a_ref[:, FF + c0:FF + c0 + cw] = si.astype(BF16)
            s = (si * u).astype(BF16)
            s_ref[:, c0:c0 + cw] = s
            acc = acc + _dot(s, w2_ref[c0:c0 + cw, :])
        y_ref[...] = acc.astype(BF16)
        o, _, _ = _rms(acc, qg_ref[...])
        xo_ref[...] = xv + 0.5 * o

    return _call(
        body, (T // tm,),
        [_row_spec(tm, D), _full_spec((1, D)), _full_spec((1, D)), _whole_spec(w1), _whole_spec(w2)],
        (x, pre_g, post_g, w1, w2),
        [_row_spec(tm, D), _row_spec(tm, 2 * FF), _row_spec(tm, FF), _row_spec(tm, D)],
        [jax.ShapeDtypeStruct((T, D), F32), jax.ShapeDtypeStruct((T, 2 * FF), BF16),
         jax.ShapeDtypeStruct((T, FF), BF16), jax.ShapeDtypeStruct((T, D), BF16)],
        "ffn_fwd", copies=copies)


def ffn_bwd(dxo, x, y, a, pre_g, post_g, w1, w2, copies=None):
    T = x.shape[0]
    tm = min(TM_FFN_BWD, T)

    def body(dxo_ref, x_ref, y_ref, a_ref, pg_ref, qg_ref, w1_ref, w2_ref,
             dx_ref, da_ref, dy_ref, hb_ref, dpg_ref, dqg_ref):
        @pl.when(pl.program_id(0) == 0)
        def _():
            dpg_ref[...] = jnp.zeros_like(dpg_ref)
            dqg_ref[...] = jnp.zeros_like(dqg_ref)

        dxo = dxo_ref[...]
        qg = qg_ref[...]
        _, ny, ry = _rms(y_ref[...].astype(F32), qg)
        dn = 0.5 * dxo
        dqg_ref[...] += _colsum(dn * ny)
        dyb = _rms_bwd(ny, ry, qg, dn).astype(BF16)
        dy_ref[...] = dyb
        pg = pg_ref[...]
        h, nx, rx = _rms(x_ref[...], pg)
        hb_ref[...] = h.astype(BF16)
        dh = jnp.zeros((tm, D), F32)
        for c0, cw in FF_CHUNKS:
            ds = _dot_nt(dyb, w2_ref[c0:c0 + cw, :])
            dg = (ds * a_ref[:, c0:c0 + cw].astype(F32)).astype(BF16)
            du = (ds * a_ref[:, FF + c0:FF + c0 + cw].astype(F32)).astype(BF16)
            da_ref[:, c0:c0 + cw] = dg
            da_ref[:, FF + c0:FF + c0 + cw] = du
            dh = dh + _dot(dg, w1_ref[c0:c0 + cw, :]) + _dot(du, w1_ref[FF + c0:FF + c0 + cw, :])
        dpg_ref[...] += _colsum(dh * nx)
        dx_ref[...] = dxo + _rms_bwd(nx, rx, pg, dh)

    return _call(
        body, (T // tm,),
        [_row_spec(tm, D), _row_spec(tm, D), _row_spec(tm, D), _row_spec(tm, 2 * FF),
         _full_spec((1, D)), _full_spec((1, D)), _whole_spec(w1), _whole_spec(w2)],
        (dxo, x, y, a, pre_g, post_g, w1, w2),
        [_row_spec(tm, D), _row_spec(tm, 2 * FF), _row_spec(tm, D), _row_spec(tm, D),
         _full_spec((1, D)), _full_spec((1, D))],
        [jax.ShapeDtypeStruct((T, D), F32), jax.ShapeDtypeStruct((T, 2 * FF), BF16),
         jax.ShapeDtypeStruct((T, D), BF16), jax.ShapeDtypeStruct((T, D), BF16),
         jax.ShapeDtypeStruct((1, D), F32), jax.ShapeDtypeStruct((1, D), F32)],
        "ffn_bwd", copies=copies)


def mix_in_fwd(x, pre_g, win, copies=None):
    T = x.shape[0]
    tm = min(TM_FFN_FWD, T)

    def body(x_ref, pg_ref, w_ref, z_ref):
        h, _, _ = _rms(x_ref[...], pg_ref[...])
        hb = h.astype(BF16)
        for w0, n, z0 in WIN_SEGMENTS:
            z_ref[:, z0:z0 + n] = _dot_nt(hb, w_ref[w0:w0 + n, :]).astype(BF16)

    return _call(
        body, (T // tm,), [_row_spec(tm, D), _full_spec((1, D)), _whole_spec(win)], (x, pre_g, win),
        [_row_spec(tm, IN_WIDTH)], [jax.ShapeDtypeStruct((T, IN_WIDTH), BF16)], "mix_in_fwd", copies=copies)


STACK = Q_PER_KV // 2
SROWS = STACK * BLK


def _band_mask(first):
    qi = lax.broadcasted_iota(jnp.int32, (SROWS, 2 * BLK), 0) & (BLK - 1)
    kj = lax.broadcasted_iota(jnp.int32, (SROWS, 2 * BLK), 1)
    band = (kj > qi) & (kj <= qi + BLK)
    if first is None:
        return band
    return band & (kj >= BLK * first.astype(jnp.int32))


def _lane_half(rows):
    return lax.broadcasted_iota(jnp.int32, (rows, BLK), 1) // HEAD_DIM


def _stack(ref, b, g):
    return jnp.concatenate([ref[b * BLK:(b + 1) * BLK, (STACK * g + j) * BLK:(STACK * g + j + 1) * BLK]
                            for j in range(STACK)], axis=0)


def _placed(pair, g, hp, fill):
    src = pair if hp == g else pltpu.roll(pair, HEAD_DIM, 1)
    return jnp.where(_lane_half(2 * BLK) == hp, src, fill).astype(BF16)


def _sink_column(sink_ref, g, hp):
    rb = lax.broadcasted_iota(jnp.int32, (SROWS, 1), 0) // BLK
    col = jnp.full((SROWS, 1), sink_ref[Q_PER_KV * g + hp], F32)
    for j in range(1, STACK):
        col = jnp.where(rb == j, sink_ref[Q_PER_KV * g + 2 * j + hp], col)
    return col


def _attn_scores(qs, kz, sink_col, mask):
    s = jnp.where(mask, _dot_nt(qs, kz) * ATTN_SCALE, MASK_VALUE)
    m = jnp.maximum(jnp.max(s, axis=-1, keepdims=True), sink_col)
    return jnp.exp(s - m), jnp.exp(sink_col - m)


def _kv_specs(tq, nb):
    kv_blk = ZKV // (2 * BLK)
    return [pl.BlockSpec((tq, 2 * BLK), lambda i: (i, kv_blk)),
            pl.BlockSpec((BLK, 2 * BLK), lambda i: (jnp.maximum(i * nb - 1, 0), kv_blk))]


def attn_fwd(z, sinks, copies=None):
    T = z.shape[0]
    tq = min(TQ_ATTN, T)
    nb = tq // BLK

    def body(sink_ref, q_ref, kv_ref, kvh_ref, o_ref):
        i = pl.program_id(0)
        low = _lane_half(SROWS) == 0
        for b in range(nb):
            kvp = kvh_ref[...] if b == 0 else kv_ref[(b - 1) * BLK:b * BLK, :]
            kv2 = jnp.concatenate([kvp, kv_ref[b * BLK:(b + 1) * BLK, :]], axis=0).astype(F32)
            mask = _band_mask(i == 0) if b == 0 else _band_mask(None)
            for g in range(N_KV_HEADS):
                qs = _stack(q_ref, b, g)
                r, e = [], []
                for hp in range(2):
                    p, e_sink = _attn_scores(qs, _placed(kv2[:, 0:BLK], g, hp, 0.0), _sink_column(sink_ref, g, hp), mask)
                    r.append(_dot(p.astype(BF16), _placed(kv2[:, BLK:2 * BLK], g, hp, 1.0)))
                    e.append(e_sink)
                den = pltpu.roll(jnp.where(low, r[1], r[0]), HEAD_DIM, 1) + jnp.where(low, e[0], e[1])
                out = (jnp.where(low, r[0], r[1]) * (1.0 / den)).astype(BF16)
                for j in range(STACK):
                    o_ref[b * BLK:(b + 1) * BLK, (STACK * g + j) * BLK:(STACK * g + j + 1) * BLK] = \
                        out[j * BLK:(j + 1) * BLK, :]

    return _call(
        body, (T // tq,), [pl.BlockSpec(memory_space=pltpu.SMEM), _row_spec(tq, D)] + _kv_specs(tq, nb),
        (sinks, z, z, z), [_row_spec(tq, D)], [jax.ShapeDtypeStruct((T, D), BF16)], "attn_fwd", copies=copies)


def attn_bwd(z, ya, dya, dzb, sinks):
    T = z.shape[0]
    tq = min(TQ_ATTN, T)
    nb = tq // BLK
    nt = T // tq
    mid = ZKV - ZU

    def body(sink_ref, q_ref, kv_ref, kvh_ref, y_ref, dy_ref, dzb_ref, dz_ref, dsink_ref, acc_ref, carry_ref):
        i = pl.program_id(0)
        first_tile = i == nt - 1

        @pl.when(i == 0)
        def _():
            carry_ref[...] = jnp.zeros_like(carry_ref)
            dsink_ref[...] = jnp.zeros_like(dsink_ref)

        acc_ref[...] = jnp.zeros_like(acc_ref)
        lane = lax.broadcasted_iota(jnp.int32, (1, BLK), 1)
        dsink = jnp.zeros((1, BLK), F32)
        half = _lane_half(SROWS)
        low_kv = _lane_half(2 * BLK) == 0
        for b in range(nb):
            kvp = kvh_ref[...] if b == 0 else kv_ref[(b - 1) * BLK:b * BLK, :]
            kv2 = jnp.concatenate([kvp, kv_ref[b * BLK:(b + 1) * BLK, :]], axis=0).astype(F32)
            mask = _band_mask(first_tile) if b == 0 else _band_mask(None)
            dk_groups, dv_groups = [], []
            for g in range(N_KV_HEADS):
                qs = _stack(q_ref, b, g)
                dys = _stack(dy_ref, b, g)
                dyy = dys.astype(F32) * _stack(y_ref, b, g).astype(F32)
                dq = jnp.zeros((SROWS, BLK), F32)
                dk_raw, dv_raw = [], []
                for hp in range(2):
                    kz = _placed(kv2[:, 0:BLK], g, hp, 0.0)
                    p, e_sink = _attn_scores(qs, kz, _sink_column(sink_ref, g, hp), mask)
                    inv = 1.0 / (jnp.sum(p, axis=-1, keepdims=True) + e_sink)
                    pn = p * inv
                    delta = jnp.sum(jnp.where(half == hp, dyy, 0.0), axis=-1, keepdims=True)
                    dp = _dot_nt(dys, _placed(kv2[:, BLK:2 * BLK], g, hp, 0.0))
                    ds = (pn * (dp - delta)).astype(BF16)
                    sink_term = e_sink * inv * delta
                    for j in range(STACK):
                        dsink = dsink + jnp.where(lane == Q_PER_KV * g + 2 * j + hp,
                                                  -_colsum(sink_term[j * BLK:(j + 1) * BLK, :]), 0.0)
                    dq = dq + _dot(ds, kz)
                    dk_raw.append(_dot_tn(ds, qs))
                    dv_raw.append(_dot_tn(pn.astype(BF16), dys))
                tk = jnp.where(low_kv, dk_raw[0], dk_raw[1])
                tv = jnp.where(low_kv, dv_raw[0], dv_raw[1])
                dk_groups.append(tk + pltpu.roll(tk, HEAD_DIM, 1))
                dv_groups.append(tv + pltpu.roll(tv, HEAD_DIM, 1))
                dqb = (dq * ATTN_SCALE).astype(BF16)
                for j in range(STACK):
                    dz_ref[b * BLK:(b + 1) * BLK, DZ_Q + (STACK * g + j) * BLK:DZ_Q + (STACK * g + j + 1) * BLK] = \
                        dqb[j * BLK:(j + 1) * BLK, :]
            acc_ref[b * BLK:(b + 2) * BLK, 0:BLK] += jnp.where(low_kv, dk_groups[0], dk_groups[1]) * ATTN_SCALE
            acc_ref[b * BLK:(b + 2) * BLK, BLK:2 * BLK] += jnp.where(low_kv, dv_groups[0], dv_groups[1])
        dsink_ref[...] += dsink
        dz_ref[:, DZ_MID:IN_WIDTH] = dzb_ref[...]
        if nb > 1:
            dz_ref[0:tq - BLK, DZ_KV:DZ_MID] = acc_ref[BLK:tq, :].astype(BF16)
        dz_ref[tq - BLK:tq, DZ_KV:DZ_MID] = (acc_ref[tq:tq + BLK, :] + carry_ref[...]).astype(BF16)
        carry_ref[...] = acc_ref[0:BLK, :]

    kv_blk = ZKV // (2 * BLK)
    return pl.pallas_call(
        body, grid=(nt,),
        in_specs=[pl.BlockSpec(memory_space=pltpu.SMEM),
                  pl.BlockSpec((tq, D), lambda i: (nt - 1 - i, 0)),
                  pl.BlockSpec((tq, 2 * BLK), lambda i: (nt - 1 - i, kv_blk)),
                  pl.BlockSpec((BLK, 2 * BLK), lambda i: (jnp.maximum((nt - 1 - i) * nb - 1, 0), kv_blk)),
                  pl.BlockSpec((tq, D), lambda i: (nt - 1 - i, 0)),
                  pl.BlockSpec((tq, D), lambda i: (nt - 1 - i, 0)),
                  pl.BlockSpec((tq, mid), lambda i: (nt - 1 - i, 0))],
        out_specs=[pl.BlockSpec((tq, IN_WIDTH), lambda i: (nt - 1 - i, 0)), _full_spec((1, BLK))],
        out_shape=[jax.ShapeDtypeStruct((T, IN_WIDTH), BF16), jax.ShapeDtypeStruct((1, BLK), F32)],
        scratch_shapes=[pltpu.VMEM((tq + BLK, 2 * BLK), F32), pltpu.VMEM((BLK, 2 * BLK), F32)],
        name="attn_bwd", compiler_params=_params(1),
    )(sinks, z, z, z, ya, dya, dzb)


def _to_group_lanes(v, g, nch):
    return jnp.concatenate([v[n * BLK:(n + 1) * BLK, g * BLK:(g + 1) * BLK] for n in range(nch)], axis=1)


def _from_group_lanes(per_group, nch):
    rows = [jnp.concatenate([per_group[g][:, n * BLK:(n + 1) * BLK] for g in range(SGU_GROUPS)], axis=1)
            for n in range(nch)]
    return jnp.concatenate(rows, axis=0)


def _tril_bf16(w_ref, g):
    t = lax.broadcasted_iota(jnp.int32, (BLK, BLK), 0)
    s = lax.broadcasted_iota(jnp.int32, (BLK, BLK), 1)
    return jnp.where(t >= s, w_ref[g], 0.0).astype(BF16)


def _sgu_norm(v_s, ln_g, ln_b):
    cdf, pdf = _gelu_parts(v_s)
    gv = v_s * cdf
    xc = gv - jnp.mean(gv, axis=-1, keepdims=True)
    rstd = lax.rsqrt(jnp.mean(xc * xc, axis=-1, keepdims=True) + LN_EPS)
    nhat = xc * rstd
    return nhat * ln_g + ln_b, nhat, rstd, cdf + v_s * pdf


def _sgu_gate(vn, w_ref, bb_ref, nch):
    vnb = vn.astype(BF16)
    return _from_group_lanes(
        [_dot(_tril_bf16(w_ref, g), _to_group_lanes(vnb, g, nch)) + jnp.tile(bb_ref[g], (1, nch))
         for g in range(SGU_GROUPS)], nch)


def mix_fwd_out(x, z, ya, post_g, ln_g, ln_b, sgu_w, sgu_bb, wa, ws, wo, copies=None):
    T = x.shape[0]
    tm = min(TM_MIX, T)
    nch = tm // BLK

    def body(x_ref, us_ref, vs_ref, ga_ref, gb_ref, ya_ref, qg_ref, lg_ref, lb_ref, w_ref, bb_ref,
             wa_ref, ws_ref, wo_ref, xo_ref, ysg_ref, pa_ref, pb_ref, o_ref):
        vn, _, _, _ = _sgu_norm(vs_ref[...].astype(F32), lg_ref[...], lb_ref[...])
        gate = _sgu_gate(vn, w_ref, bb_ref, nch)
        us = us_ref[...].astype(F32)
        cdf, _ = _gelu_parts(us)
        ysg = (us * cdf * gate).astype(BF16)
        ysg_ref[...] = ysg
        pa = _dot(ya_ref[...], wa_ref[...])
        pb = _dot(ysg, ws_ref[...])
        pa_ref[...] = pa.astype(BF16)
        pb_ref[...] = pb.astype(BF16)
        merged = _sigmoid(ga_ref[...].astype(F32)) * pa + _sigmoid(gb_ref[...].astype(F32)) * pb
        o = _dot(merged.astype(BF16), wo_ref[...])
        o_ref[...] = o.astype(BF16)
        on, _, _ = _rms(o, qg_ref[...])
        xo_ref[...] = x_ref[...] + on

    zspec = lambda start: _row_spec(tm, D, start // D)
    act = jax.ShapeDtypeStruct((T, D), BF16)
    return _call(
        body, (T // tm,),
        [_row_spec(tm, D), zspec(ZU), zspec(ZV), zspec(ZGA), zspec(ZGB), _row_spec(tm, D),
         _full_spec((1, D)), _full_spec((1, D)), _full_spec((1, D)),
         _full_spec((SGU_GROUPS, BLK, BLK)), _full_spec((SGU_GROUPS, BLK, BLK)),
         _whole_spec(wa), _whole_spec(ws), _whole_spec(wo)],
        (x, z, z, z, z, ya, post_g, ln_g, ln_b, sgu_w, sgu_bb, wa, ws, wo),
        [_row_spec(tm, D)] * 5, [jax.ShapeDtypeStruct((T, D), F32), act, act, act, act],
        "mix_fwd_out", copies=copies)


def mix_bwd_out(dxo, z, o, pa, pb, post_g, ln_g, ln_b, sgu_w, sgu_bb, wa, ws, wo, copies=None):
    T = dxo.shape[0]
    tm = min(TM_MIX_BWD, T)
    nch = tm // BLK
    mid = ZKV - ZU

    def body(dxo_ref, us_ref, vs_ref, ga_ref, gb_ref, o_ref, pa_ref, pb_ref, qg_ref, lg_ref, lb_ref, w_ref, bb_ref,
             wa_ref, ws_ref, wo_ref,
             dzb_ref, dya_ref, mg_ref, do_ref, dpa_ref, dpb_ref, dqg_ref, dlg_ref, dlb_ref, dw_ref, dbb_ref):
        @pl.when(pl.program_id(0) == 0)
        def _():
            for r in (dqg_ref, dlg_ref, dlb_ref, dw_ref, dbb_ref):
                r[...] = jnp.zeros_like(r)

        qg = qg_ref[...]
        dxo = dxo_ref[...]
        _, no, ro = _rms(o_ref[...].astype(F32), qg)
        dqg_ref[...] += _colsum(dxo * no)
        dob = _rms_bwd(no, ro, qg, dxo).astype(BF16)
        do_ref[...] = dob
        dmerged = _dot_nt(dob, wo_ref[...])
        sa = _sigmoid(ga_ref[...].astype(F32))
        sb = _sigmoid(gb_ref[...].astype(F32))
        pa = pa_ref[...].astype(F32)
        pb = pb_ref[...].astype(F32)
        mg_ref[...] = (sa * pa + sb * pb).astype(BF16)
        dpa = (dmerged * sa).astype(BF16)
        dpb = (dmerged * sb).astype(BF16)
        dpa_ref[...] = dpa
        dpb_ref[...] = dpb
        dzb_ref[:, ZGA - ZU:ZGA - ZU + D] = (dmerged * pa * sa * (1.0 - sa)).astype(BF16)
        dzb_ref[:, ZGB - ZU:ZGB - ZU + D] = (dmerged * pb * sb * (1.0 - sb)).astype(BF16)
        dya_ref[...] = _dot_nt(dpa, wa_ref[...]).astype(BF16)
        dysg = _dot_nt(dpb, ws_ref[...])

        lg = lg_ref[...]
        vn, nhat, rstd, dgelu_v = _sgu_norm(vs_ref[...].astype(F32), lg, lb_ref[...])
        gate = _sgu_gate(vn, w_ref, bb_ref, nch)
        us = us_ref[...].astype(F32)
        cdf, pdf = _gelu_parts(us)
        dzb_ref[:, 0:D] = (dysg * gate * (cdf + us * pdf)).astype(BF16)
        dgate = (dysg * (us * cdf)).astype(BF16)
        vnb = vn.astype(BF16)
        t = lax.broadcasted_iota(jnp.int32, (BLK, BLK), 0)
        s = lax.broadcasted_iota(jnp.int32, (BLK, BLK), 1)
        dvn_groups = []
        for g in range(SGU_GROUPS):
            dgl = _to_group_lanes(dgate, g, nch)
            dbb_ref[g] += jnp.broadcast_to(jnp.sum(dgl.astype(F32), axis=-1, keepdims=True), (BLK, BLK))
            dw_ref[g] += jnp.where(t >= s, _dot_nt(dgl, _to_group_lanes(vnb, g, nch)), 0.0)
            dvn_groups.append(_dot_tn(_tril_bf16(w_ref, g), dgl))
        dvn = _from_group_lanes(dvn_groups, nch)
        dlg_ref[...] += _colsum(dvn * nhat)
        dlb_ref[...] += _colsum(dvn)
        dnh = dvn * lg
        dgv = rstd * (dnh - jnp.mean(dnh, axis=-1, keepdims=True) - nhat * jnp.mean(dnh * nhat, axis=-1, keepdims=True))
        dzb_ref[:, ZV - ZU:ZV - ZU + D] = (dgv * dgelu_v).astype(BF16)

    zspec = lambda start: _row_spec(tm, D, start // D)
    act = jax.ShapeDtypeStruct((T, D), BF16)
    grp = jax.ShapeDtypeStruct((SGU_GROUPS, BLK, BLK), F32)
    vec = jax.ShapeDtypeStruct((1, D), F32)
    return _call(
        body, (T // tm,),
        [_row_spec(tm, D), zspec(ZU), zspec(ZV), zspec(ZGA), zspec(ZGB),
         _row_spec(tm, D), _row_spec(tm, D), _row_spec(tm, D),
         _full_spec((1, D)), _full_spec((1, D)), _full_spec((1, D)),
         _full_spec((SGU_GROUPS, BLK, BLK)), _full_spec((SGU_GROUPS, BLK, BLK)),
         _whole_spec(wa), _whole_spec(ws), _whole_spec(wo)],
        (dxo, z, z, z, z, o, pa, pb, post_g, ln_g, ln_b, sgu_w, sgu_bb, wa, ws, wo),
        [_row_spec(tm, mid)] + [_row_spec(tm, D)] * 5 + [_full_spec((1, D))] * 3
        + [_full_spec((SGU_GROUPS, BLK, BLK))] * 2,
        [jax.ShapeDtypeStruct((T, mid), BF16), act, act, act, act, act, vec, vec, vec, grp, grp],
        "mix_bwd_out", copies=copies)


def mix_in_bwd(dxo, x, dz, pre_g, win):
    T = x.shape[0]
    tm = min(TM_MIX, T)

    def body(dxo_ref, x_ref, dz_ref, pg_ref, w_ref, dx_ref, hb_ref, dpg_ref):
        @pl.when(pl.program_id(0) == 0)
        def _():
            dpg_ref[...] = jnp.zeros_like(dpg_ref)

        pg = pg_ref[...]
        h, nx, rx = _rms(x_ref[...], pg)
        hb_ref[...] = h.astype(BF16)
        dh = jnp.zeros((tm, D), F32)
        for c0, cw in WIN_CHUNKS:
            dh = dh + _dot(dz_ref[:, c0:c0 + cw], w_ref[c0:c0 + cw, :])
        dpg_ref[...] += _colsum(dh * nx)
        dx_ref[...] = dxo_ref[...] + _rms_bwd(nx, rx, pg, dh)

    return pl.pallas_call(
        body, grid=(T // tm,),
        in_specs=[_row_spec(tm, D), _row_spec(tm, D), _row_spec(tm, IN_WIDTH), _full_spec((1, D)), _whole_spec(win)],
        out_specs=[_row_spec(tm, D), _row_spec(tm, D), _full_spec((1, D))],
        out_shape=[jax.ShapeDtypeStruct((T, D), F32), jax.ShapeDtypeStruct((T, D), BF16),
                   jax.ShapeDtypeStruct((1, D), F32)],
        name="mix_in_bwd", compiler_params=_params(1),
    )(dxo, x, dz, pre_g, win)


def wgrad(wide, narrow, name, tmo, copies=None):
    T, N = wide.shape
    M = narrow.shape[1]
    tk = min(TK_WGRAD, T)
    nk = T // tk
    chunk = max(c for c in range(BLK, 1792 + 1, BLK) if N % c == 0)

    def body(a_ref, b_ref, o_ref, acc_ref):
        k = pl.program_id(1)

        @pl.when(k == 0)
        def _():
            acc_ref[...] = jnp.zeros_like(acc_ref)

        acc_ref[...] += _dot_tn(a_ref[...], b_ref[...])

        @pl.when(k == nk - 1)
        def _():
            for c0 in range(0, N, chunk):
                o_ref[c0:c0 + chunk, :] = acc_ref[:, c0:c0 + chunk].T.astype(BF16)

    (out,), got = _call(
        body, (M // tmo, nk),
        [pl.BlockSpec((tk, tmo), lambda m, k: (k, m)), pl.BlockSpec((tk, N), lambda m, k: (k, 0))], (narrow, wide),
        [pl.BlockSpec((N, tmo), lambda m, k: (0, m))], [jax.ShapeDtypeStruct((N, M), BF16)], name,
        scratch=[pltpu.VMEM((tmo, N), F32)], copies=copies)
    return out if copies is None else (out, got)


def loss_head(y, target):
    T = y.shape[0]
    tm = min(TM_FFN_FWD, T)

    def body(y_ref, t_ref, sq_ref, dy_ref):
        @pl.when(pl.program_id(0) == 0)
        def _():
            sq_ref[...] = jnp.zeros_like(sq_ref)

        err = y_ref[...] - t_ref[...]
        sq_ref[...] += _colsum(err * err)
        dy_ref[...] = err * (1.0 / D)

    return pl.pallas_call(
        body, grid=(T // tm,), in_specs=[_row_spec(tm, D), _row_spec(tm, D)],
        out_specs=[_full_spec((1, D)), _row_spec(tm, D)],
        out_shape=[jax.ShapeDtypeStruct((1, D), F32), jax.ShapeDtypeStruct((T, D), F32)],
        name="loss_head", compiler_params=_params(1),
    )(y, target)


def adamw_sum(parts, w, m, v, name):
    layers, rows, cols = w.shape
    tr = _row_tile(rows, TR_ADAM)
    nr = rows // tr
    c1 = 1.0 - ADAM_B1 ** ADAM_STEP
    c2 = 1.0 - ADAM_B2 ** ADAM_STEP

    def body(*refs):
        p_refs = refs[:layers]
        w_ref, m_ref, v_ref, g_ref, d_ref, nm_ref, nv_ref = refs[layers:]
        for k in range(layers):
            @pl.when(pl.program_id(0) == k)
            def _(p_ref=p_refs[k]):
                g = p_ref[0].astype(F32)
                for j in range(1, N_DEV):
                    g = g + p_ref[j].astype(F32)
                nm = ADAM_B1 * m_ref[...] + (1.0 - ADAM_B1) * g
                nv = ADAM_B2 * v_ref[...] + (1.0 - ADAM_B2) * (g * g)
                g_ref[...] = g
                nm_ref[...] = nm
                nv_ref[...] = nv
                d_ref[...] = -ADAM_LR * ((nm / c1) / (jnp.sqrt(nv / c2) + ADAM_EPS) + ADAM_WD * w_ref[...])

    def part_spec(k):
        return pl.BlockSpec((N_DEV, tr, cols),
                            lambda l, i: (0, jnp.where(l < k, 0, jnp.where(l == k, i, nr - 1)), 0))

    spec = pl.BlockSpec((None, tr, cols), lambda l, i: (l, i, 0))
    out = jax.ShapeDtypeStruct((layers, rows, cols), F32)
    return pl.pallas_call(
        body, grid=(layers, nr),
        in_specs=[part_spec(k) for k in range(layers)] + [spec, spec, spec],
        out_specs=[spec] * 4, out_shape=[out] * 4, name=name, compiler_params=_params(2),
    )(*parts, w, m, v)


def _pack_small(p):
    layers = []
    for l in range(DEPTH):
        rows = [p[n][l].reshape(1, D) for n in SMALL_VEC]
        rows.append(p["sgu_b"][l].reshape(1, D))
        rows.append(jnp.pad(p["attn_sinks"][l].reshape(1, N_Q_HEADS), ((0, 0), (0, D - N_Q_HEADS))))
        rows.append(p["sgu_w"][l].reshape(BLK, D))
        used = len(SMALL_VEC) + 2 + BLK
        rows.append(jnp.zeros((SMALL_ROWS - used, D), F32))
        layers.append(jnp.concatenate(rows, axis=0))
    return jnp.concatenate(layers, axis=0)


def _unpack_small(packed):
    a = packed.reshape(DEPTH, SMALL_ROWS, D)
    out = {n: a[:, i, :] for i, n in enumerate(SMALL_VEC)}
    k = len(SMALL_VEC)
    out["sgu_b"] = a[:, k, :].reshape(DEPTH, SGU_GROUPS, BLK)
    out["attn_sinks"] = a[:, k + 1, :N_Q_HEADS]
    out["sgu_w"] = a[:, k + 2:k + 2 + BLK, :].reshape(DEPTH, SGU_GROUPS, BLK, BLK)
    return out


WEIGHT_NAMES = ("ffn1_pre_g", "ffn1_w1", "ffn1_w2", "ffn1_post_g", "mix_pre_g", "w_in", "attn_sinks", "sgu_ln_g",
                "sgu_ln_b", "sgu_w", "sgu_b", "w_attn_branch", "w_sgu_branch", "w_out", "mix_post_g", "ffn2_pre_g",
                "ffn2_w1", "ffn2_w2", "ffn2_post_g")
COL_SHARDED = ("ffn1_w1", "ffn2_w1", "w_in")
ROW_SHARDED = ("ffn1_w2", "ffn2_w2", "w_attn_branch", "w_sgu_branch", "w_out")
MATRICES = COL_SHARDED + ROW_SHARDED


def kernel(x, ffn1_pre_g, ffn1_w1, ffn1_w2, ffn1_post_g, mix_pre_g, w_in, attn_sinks, sgu_ln_g, sgu_ln_b, sgu_w, sgu_b, w_attn_branch, w_sgu_branch, w_out, mix_post_g, ffn2_pre_g, ffn2_w1, ffn2_w2, ffn2_post_g, loss_target, m_ffn1_pre_g, m_ffn1_w1, m_ffn1_w2, m_ffn1_post_g, m_mix_pre_g, m_w_in, m_attn_sinks, m_sgu_ln_g, m_sgu_ln_b, m_sgu_w, m_sgu_b, m_w_attn_branch, m_w_sgu_branch, m_w_out, m_mix_post_g, m_ffn2_pre_g, m_ffn2_w1, m_ffn2_w2, m_ffn2_post_g, v_ffn1_pre_g, v_ffn1_w1, v_ffn1_w2, v_ffn1_post_g, v_mix_pre_g, v_w_in, v_attn_sinks, v_sgu_ln_g, v_sgu_ln_b, v_sgu_w, v_sgu_b, v_w_attn_branch, v_w_sgu_branch, v_w_out, v_mix_post_g, v_ffn2_pre_g, v_ffn2_w1, v_ffn2_w2, v_ffn2_post_g):
    w = dict(zip(WEIGHT_NAMES, (ffn1_pre_g, ffn1_w1, ffn1_w2, ffn1_post_g, mix_pre_g, w_in, attn_sinks, sgu_ln_g,
                                sgu_ln_b, sgu_w, sgu_b, w_attn_branch, w_sgu_branch, w_out, mix_post_g, ffn2_pre_g,
                                ffn2_w1, ffn2_w2, ffn2_post_g)))
    mom = dict(zip(WEIGHT_NAMES, (m_ffn1_pre_g, m_ffn1_w1, m_ffn1_w2, m_ffn1_post_g, m_mix_pre_g, m_w_in,
                                  m_attn_sinks, m_sgu_ln_g, m_sgu_ln_b, m_sgu_w, m_sgu_b, m_w_attn_branch,
                                  m_w_sgu_branch, m_w_out, m_mix_post_g, m_ffn2_pre_g, m_ffn2_w1, m_ffn2_w2,
                                  m_ffn2_post_g)))
    var = dict(zip(WEIGHT_NAMES, (v_ffn1_pre_g, v_ffn1_w1, v_ffn1_w2, v_ffn1_post_g, v_mix_pre_g, v_w_in,
                                  v_attn_sinks, v_sgu_ln_g, v_sgu_ln_b, v_sgu_w, v_sgu_b, v_w_attn_branch,
                                  v_w_sgu_branch, v_w_out, v_mix_post_g, v_ffn2_pre_g, v_ffn2_w1, v_ffn2_w2,
                                  v_ffn2_post_g)))
    T = x.shape[1]
    xs = x.reshape(T, D)
    target = loss_target.reshape(T, D)

    for n in COL_SHARDED:
        w[n], mom[n], var[n] = (jnp.swapaxes(t[n], 1, 2) for t in (w, mom, var))

    shard = {n: [cast_layer(w[n], l) for l in range(DEPTH)] for n in MATRICES}
    weights = [{} for _ in range(DEPTH)]
    ffn1, ffn2, squares = ("ffn1_w1", "ffn1_w2"), ("ffn2_w1", "ffn2_w2"), ("w_attn_branch", "w_sgu_branch", "w_out")
    for n, g in zip(ffn1, all_gather_weights([shard[n][0] for n in ffn1])):
        weights[0][n] = g.reshape(-1, D)

    def fetch(groups):
        jobs = [("all", shard[n][l]) for l, names in groups if l < DEPTH for n in names]
        return PeerCopies(jobs) if jobs else None

    def landed(groups, arrivals):
        slots = [(l, n) for l, names in groups if l < DEPTH for n in names]
        for (l, n), g in zip(slots, arrivals):
            weights[l][n] = g.reshape(-1, D)

    vec = lambda n, l: w[n][l].reshape(1, D)
    sgu_bb = [jnp.broadcast_to(w["sgu_b"][l][:, :, None], (SGU_GROUPS, BLK, BLK)) for l in range(DEPTH)]

    saved = []
    h = xs
    for l in range(DEPTH):
        wl = weights[l]
        ahead = lambda names: [(0, names)] if l == 0 else []
        x0 = h
        groups = [(l + 1, ffn1)] + ahead(("w_in",))
        (x1, a1, s1, y1), got = ffn_fwd(x0, vec("ffn1_pre_g", l), vec("ffn1_post_g", l), wl["ffn1_w1"], wl["ffn1_w2"],
                                        fetch(groups))
        landed(groups, got)
        groups = [(l + 1, ("w_in",))] + ahead(squares)
        (z,), got = mix_in_fwd(x1, vec("mix_pre_g", l), wl["w_in"], fetch(groups))
        landed(groups, got)
        groups = ahead(("ffn2_w1",))
        (ya,), got = attn_fwd(z, w["attn_sinks"][l], fetch(groups))
        landed(groups, got)
        groups = [(l + 1, squares)] + ahead(("ffn2_w2",))
        (x2, ysg, pa, pb, o), got = mix_fwd_out(
            x1, z, ya, vec("mix_post_g", l), vec("sgu_ln_g", l), vec("sgu_ln_b", l), w["sgu_w"][l], sgu_bb[l],
            wl["w_attn_branch"], wl["w_sgu_branch"], wl["w_out"], fetch(groups))
        landed(groups, got)
        groups = [(l + 1, ffn2)]
        (h, a2, s2, y2), got = ffn_fwd(x2, vec("ffn2_pre_g", l), vec("ffn2_post_g", l), wl["ffn2_w1"], wl["ffn2_w2"],
                                       fetch(groups))
        landed(groups, got)
        saved.append((x0, a1, s1, y1, x1, z, ya, ysg, pa, pb, o, x2, a2, s2, y2))

    sq, dx = loss_head(h, target)
    loss = lax.psum(0.5 / D * jnp.sum(sq), ("x", "y", "c"))

    def ffn_wgrads(hb, da, s, dy):
        return [("own", wgrad(da, hb, "wgrad_ffn_w1", D // 2).reshape(N_DEV, W1_SHARD, D)),
                ("own", wgrad(s, dy, "wgrad_ffn_w2", D).reshape(N_DEV, W2_SHARD, D))]

    parts = {n: [None] * DEPTH for n in MATRICES}
    small = {n: [None] * DEPTH for n in WEIGHT_NAMES if n not in MATRICES}
    waiting = None
    for l in reversed(range(DEPTH)):
        wl = weights[l]
        x0, a1, s1, y1, x1, z, ya, ysg, pa, pb, o, x2, a2, s2, y2 = saved[l]
        (dx, da, dy, hb, dpg, dqg), got = ffn_bwd(
            dx, x2, y2, a2, vec("ffn2_pre_g", l), vec("ffn2_post_g", l), wl["ffn2_w1"], wl["ffn2_w2"],
            PeerCopies(waiting) if waiting else None)
        if waiting:
            parts["ffn1_w1"][l + 1], parts["ffn1_w2"][l + 1] = got
        small["ffn2_pre_g"][l], small["ffn2_post_g"][l] = dpg, dqg

        (dzb, dya, mg, dob, dpa, dpb, dqg, dlg, dlb, dsw, dsb), got = mix_bwd_out(
            dx, z, o, pa, pb, vec("mix_post_g", l), vec("sgu_ln_g", l), vec("sgu_ln_b", l), w["sgu_w"][l], sgu_bb[l],
            wl["w_attn_branch"], wl["w_sgu_branch"], wl["w_out"], PeerCopies(ffn_wgrads(hb, da, s2, dy)))
        parts["ffn2_w1"][l], parts["ffn2_w2"][l] = got
        dz, dsink = attn_bwd(z, ya, dya, dzb, w["attn_sinks"][l])
        dx, hb, dpg = mix_in_bwd(dx, x1, dz, vec("mix_pre_g", l), wl["w_in"])
        small["mix_pre_g"][l], small["mix_post_g"][l] = dpg, dqg
        small["sgu_ln_g"][l], small["sgu_ln_b"][l] = dlg, dlb
        small["sgu_w"][l], small["sgu_b"][l] = dsw, dsb[:, :, 0]
        small["attn_sinks"][l] = dsink[0, :N_Q_HEADS]
        mixer = [("own", wgrad(dz, hb, "wgrad_w_in", D // 2).reshape(N_DEV, WIN_SHARD, D))]
        mixer += [("own", wgrad(act, cot, "wgrad_square", D).reshape(N_DEV, SQ_SHARD, D))
                  for act, cot in ((ya, dpa), (ysg, dpb), (mg, dob))]
        if l == 0:
            small["ffn1_pre_g"][0] = small["ffn1_post_g"][0] = jnp.zeros((1, D), F32)
            mixer.append(("all", _pack_small({n: jnp.stack(v, axis=0) for n, v in small.items()})))

        (dx, da, dy, hb, dpg, dqg), got = ffn_bwd(
            dx, x0, y1, a1, vec("ffn1_pre_g", l), vec("ffn1_post_g", l), wl["ffn1_w1"], wl["ffn1_w2"],
            PeerCopies(mixer))
        parts["w_in"][l], parts["w_attn_branch"][l], parts["w_sgu_branch"][l], parts["w_out"][l] = got[:4]
        if l > 0:
            small["ffn1_pre_g"][l], small["ffn1_post_g"][l] = dpg, dqg
        if l > 0:
            waiting = ffn_wgrads(hb, da, s1, dy)
    grad_x = dx.reshape(x.shape)
    small_parts = got[4]

    def late_rows(pre, post):
        return jnp.concatenate([pre.reshape(1, D), post.reshape(1, D), jnp.zeros((LATE_ROWS - 2, D), F32)], axis=0)

    g_w1 = wgrad(da, hb, "wgrad_ffn_w1", D // 2).reshape(N_DEV, W1_SHARD, D)
    g_w2, got = wgrad(s1, dy, "wgrad_ffn_w2", D, PeerCopies([("own", g_w1)]))
    parts["ffn1_w1"][0] = got[0]
    parts["ffn1_w2"][0], late_parts = exchange_last([("own", g_w2.reshape(N_DEV, W2_SHARD, D)),
                                                     ("all", late_rows(dpg, dqg))])

    grads, deltas, new_m, new_v = {}, {}, {}, {}
    for n in MATRICES:
        grads[n], deltas[n], new_m[n], new_v[n] = adamw_sum(parts[n], w[n], mom[n], var[n], "adamw_" + n)
    for n in COL_SHARDED:
        for out in (grads, deltas, new_m, new_v):
            out[n] = jnp.swapaxes(out[n], 1, 2)
    res = adamw_sum([small_parts], _pack_small(w)[None], _pack_small(mom)[None], _pack_small(var)[None],
                    "adamw_small")
    late = adamw_sum([late_parts], *[late_rows(t["ffn1_pre_g"][0], t["ffn1_post_g"][0])[None] for t in (w, mom, var)],
                     "adamw_late")
    for out, packed, late_out in zip((grads, deltas, new_m, new_v), res, late):
        out.update(_unpack_small(packed[0]))
        for row, n in enumerate(("ffn1_pre_g", "ffn1_post_g")):
            out[n] = jnp.concatenate([late_out[0, row:row + 1], out[n][1:]], axis=0)

    return (loss, grad_x, *[grads[n] for n in WEIGHT_NAMES], *[deltas[n] for n in WEIGHT_NAMES],
            *[new_m[n] for n in WEIGHT_NAMES], *[new_v[n] for n in WEIGHT_NAMES])
```

```python
import math

import jax
import jax.numpy as jnp
from jax import lax
from jax.experimental import pallas as pl
from jax.experimental.pallas import tpu as pltpu

F32 = jnp.float32
BF16 = jnp.bfloat16

N_DEV = 8
D = 1024
FF = 2816
DEPTH = 4
HEAD_DIM = 64
N_Q_HEADS = 16
N_KV_HEADS = 2
Q_PER_KV = N_Q_HEADS // N_KV_HEADS
BLK = 128
SGU_GROUPS = 8
IN_WIDTH = 5376
W1_SHARD = 2 * FF // N_DEV
WIN_SHARD = IN_WIDTH // N_DEV
W2_SHARD = FF // N_DEV
SQ_SHARD = D // N_DEV

RMS_EPS = 1e-6
LN_EPS = 1e-5
MASK_VALUE = -1e30
ATTN_SCALE = 1.0 / math.sqrt(HEAD_DIM)

ADAM_LR = 0.001
ADAM_B1 = 0.9
ADAM_B2 = 0.999
ADAM_EPS = 1e-08
ADAM_WD = 0.01
ADAM_STEP = 10

VMEM_LIMIT_V7X = 56 * 1024 * 1024

WIN_SEGMENTS = ((0, 1024, 0), (1024, 256, 5120), (1280, 1024, 1024), (2304, 1024, 2048), (3328, 1024, 3072),
                (4352, 1024, 4096))
ZQ, ZU, ZV, ZGA, ZGB, ZKV = 0, 1024, 2048, 3072, 4096, 5120
DZ_Q, DZ_KV, DZ_MID = 0, 1024, 1280

FF_CHUNKS = ((0, 1024), (1024, 1024), (2048, 768))
WIN_CHUNKS = ((0, 1792), (1792, 1792), (3584, 1792))

SMALL_ROWS = 144
LATE_ROWS = 16
SMALL_VEC = ("ffn1_pre_g", "ffn1_post_g", "mix_pre_g", "mix_post_g", "ffn2_pre_g", "ffn2_post_g",
             "sgu_ln_g", "sgu_ln_b")

TM_FFN_FWD = 512
TM_FFN_BWD = 256
TM_MIX = 512
TM_MIX_BWD = 256
TQ_ATTN = 512
TK_WGRAD = 1024
TR_ADAM = 256


def _params(n_grid, vmem=VMEM_LIMIT_V7X):
    return pltpu.CompilerParams(dimension_semantics=("arbitrary",) * n_grid, vmem_limit_bytes=vmem)


def _dot(a, b):
    return jnp.dot(a, b, preferred_element_type=F32)


def _dot_nt(a, b):
    return lax.dot_general(a, b, (((1,), (1,)), ((), ())), preferred_element_type=F32)


def _dot_tn(a, b):
    return lax.dot_general(a, b, (((0,), (0,)), ((), ())), preferred_element_type=F32)


def _rms(x, g):
    r = lax.rsqrt(jnp.mean(x * x, axis=-1, keepdims=True) + RMS_EPS)
    n = x * r
    return n * g, n, r


def _rms_bwd(n, r, g, dy):
    dn = dy * g
    return r * (dn - n * jnp.mean(dn * n, axis=-1, keepdims=True))


def _colsum(v):
    return jnp.sum(v, axis=0, keepdims=True)


def _sigmoid(v):
    return 0.5 * jnp.tanh(0.5 * v) + 0.5


def _gelu_parts(v):
    cdf = 0.5 * (1.0 + lax.erf(v * (1.0 / math.sqrt(2.0))))
    return cdf, jnp.exp(-0.5 * v * v) * (1.0 / math.sqrt(2.0 * math.pi))


def _row_tile(rows, cap):
    return max(t for t in range(16, min(rows, cap) + 1, 16) if rows % t == 0)


def _row_spec(tm, width, col_block=0):
    return pl.BlockSpec((tm, width), lambda i, cb=col_block: (i, cb))


def _full_spec(shape):
    nd = len(shape)
    return pl.BlockSpec(tuple(shape), lambda *_: (0,) * nd)


def _whole_spec(arr):
    nd = arr.ndim
    return pl.BlockSpec(tuple(arr.shape), lambda *_: (0,) * nd, pipeline_mode=pl.Buffered(1))


HBM_SPEC = pl.BlockSpec(memory_space=pltpu.HBM)
MESH_ID = pl.DeviceIdType.MESH
RELATIONS = tuple((rx, ry, rc) for rx in (0, 1) for ry in (0, 1) for rc in (0, 1))[1:]


class PeerCopies:
    def __init__(self, jobs):
        self.kinds = [k for k, _ in jobs]
        self.arrays = [a for _, a in jobs]
        self.n = len(jobs)
        self.out_shape = [jax.ShapeDtypeStruct((N_DEV,) + a.shape if k == "all" else a.shape, a.dtype)
                          for k, a in jobs]
        self.scratch = [pltpu.SemaphoreType.DMA((7 * self.n,)), pltpu.SemaphoreType.DMA((7 * self.n,)),
                        pltpu.SemaphoreType.DMA((self.n,))]

    def _copies(self, ins, outs, sems, arriving):
        send_sems, recv_sems, local_sems = sems
        x, y, c = lax.axis_index("x"), lax.axis_index("y"), lax.axis_index("c")
        me = 4 * x + 2 * y + c
        src = lambda a, d: ins[a] if self.kinds[a] == "all" else ins[a].at[d]
        if not arriving:
            local = [pltpu.make_async_copy(src(a, me), outs[a].at[me], local_sems.at[a]) for a in range(self.n)]
        remote = []
        for k, (rx, ry, rc) in enumerate(RELATIONS):
            tx, ty, tc = (1 - x if rx else x), (1 - y if ry else y), (1 - c if rc else c)
            peer = 4 * tx + 2 * ty + tc
            for a in range(self.n):
                from_slot, to_slot = (me, peer) if arriving else (peer, me)
                remote.append(pltpu.make_async_remote_copy(
                    src_ref=src(a, from_slot), dst_ref=outs[a].at[to_slot],
                    send_sem=send_sems.at[a * 7 + k], recv_sem=recv_sems.at[a * 7 + k],
                    device_id=(tx, ty, tc), device_id_type=MESH_ID))
        return remote if arriving else (local, remote)

    def start(self, ins, outs, sems):
        local, sends = self._copies(ins, outs, sems, False)
        for cp in local + sends:
            cp.start()

    def wait(self, ins, outs, sems):
        for cp in self._copies(ins, outs, sems, True):
            cp.wait_recv()
        local, sends = self._copies(ins, outs, sems, False)
        for cp in sends:
            cp.wait_send()
        for cp in local:
            cp.wait()


def _call(body, grid, in_specs, args, out_specs, out_shape, name, scratch=(), copies=None):
    if copies is None:
        outs = pl.pallas_call(body, grid=grid, in_specs=list(in_specs), out_specs=list(out_specs),
                              out_shape=list(out_shape), scratch_shapes=list(scratch), name=name,
                              compiler_params=_params(len(grid)))(*args)
        return outs, []
    n_in, n_out, n_scr, nc = len(in_specs), len(out_specs), len(scratch), copies.n

    def at_step(steps):
        hit = pl.program_id(0) == steps[0]
        for axis in range(1, len(grid)):
            hit = jnp.logical_and(hit, pl.program_id(axis) == steps[axis])
        return hit

    def hosted(*refs):
        ins, refs = refs[:n_in], refs[n_in:]
        c_in, refs = refs[:nc], refs[nc:]
        outs, refs = refs[:n_out], refs[n_out:]
        c_out, refs = refs[:nc], refs[nc:]
        scr, sems = refs[:n_scr], refs[n_scr:]

        @pl.when(at_step([0] * len(grid)))
        def _():
            copies.start(c_in, c_out, sems)

        body(*ins, *outs, *scr)

        @pl.when(at_step([g - 1 for g in grid]))
        def _():
            copies.wait(c_in, c_out, sems)

    outs = pl.pallas_call(hosted, grid=grid, in_specs=list(in_specs) + [HBM_SPEC] * nc,
                          out_specs=list(out_specs) + [HBM_SPEC] * nc,
                          out_shape=list(out_shape) + copies.out_shape,
                          scratch_shapes=list(scratch) + copies.scratch, name=name,
                          compiler_params=_params(len(grid)))(*args, *copies.arrays)
    return outs[:n_out], outs[n_out:]


def cast_layer(w, l):
    _, rows, cols = w.shape
    tr = _row_tile(rows, 256)

    def body(w_ref, o_ref):
        o_ref[...] = w_ref[...].astype(BF16)

    return pl.pallas_call(
        body, grid=(rows // tr,), in_specs=[pl.BlockSpec((None, tr, cols), lambda i: (l, i, 0))],
        out_specs=_row_spec(tr, cols), out_shape=jax.ShapeDtypeStruct((rows, cols), BF16),
        name="cast_layer", compiler_params=_params(1),
    )(w)


def all_gather_weights(shards):
    n = len(shards)

    def body(*refs):
        ins, outs = refs[:n], refs[n:2 * n]
        send_sems, recv_sems, local_sems = refs[2 * n:]
        x, y, c = lax.axis_index("x"), lax.axis_index("y"), lax.axis_index("c")
        me, sibling = (x, y, c), (x, y, 1 - c)
        chips = [(1 - x, y), (x, 1 - y), (1 - x, 1 - y)]

        def slot(a, owner):
            return outs[a].at[4 * owner[0] + 2 * owner[1] + owner[2]]

        def copy(a, k, owner, to, src=None):
            return pltpu.make_async_remote_copy(
                src_ref=slot(a, owner) if src is None else src, dst_ref=slot(a, owner),
                send_sem=send_sems.at[a * 7 + k], recv_sem=recv_sems.at[a * 7 + k],
                device_id=to, device_id_type=MESH_ID)

        mine = [pltpu.make_async_copy(ins[a], slot(a, me), local_sems.at[a]) for a in range(n)]
        for cp in mine:
            cp.start()
        first = []
        for a in range(n):
            first.append(copy(a, 0, me, sibling, src=ins[a]))
            first += [copy(a, 1 + j, me, (*chip, c), src=ins[a]) for j, chip in enumerate(chips)]
        for cp in first:
            cp.start()
        passed = []
        for j, chip in enumerate(chips):
            for a in range(n):
                copy(a, 1 + j, (*chip, c), me).wait_recv()
                fwd = copy(a, 4 + j, (*chip, c), sibling)
                fwd.start()
                passed.append(fwd)
        for a in range(n):
            copy(a, 0, sibling, me).wait_recv()
            for j, chip in enumerate(chips):
                copy(a, 4 + j, (*chip, 1 - c), me).wait_recv()
        for cp in first + passed:
            cp.wait_send()
        for cp in mine:
            cp.wait()

    out_shape = [jax.ShapeDtypeStruct((N_DEV,) + s.shape, s.dtype) for s in shards]
    return pl.pallas_call(
        body, in_specs=[HBM_SPEC] * n, out_specs=[HBM_SPEC] * n, out_shape=out_shape,
        scratch_shapes=[pltpu.SemaphoreType.DMA((7 * n,)), pltpu.SemaphoreType.DMA((7 * n,)),
                        pltpu.SemaphoreType.DMA((n,))],
        name="all_gather_weights",
    )(*shards)


def exchange_last(jobs):
    copies = PeerCopies(jobs)
    n = copies.n

    def body(*refs):
        ins, outs, sems = refs[:n], refs[n:2 * n], refs[2 * n:]
        copies.start(ins, outs, sems)
        copies.wait(ins, outs, sems)

    return pl.pallas_call(
        body, in_specs=[HBM_SPEC] * n, out_specs=[HBM_SPEC] * n, out_shape=copies.out_shape,
        scratch_shapes=copies.scratch, name="exchange_last",
    )(*copies.arrays)


def ffn_fwd(x, pre_g, post_g, w1, w2, copies=None):
    T = x.shape[0]
    tm = min(TM_FFN_FWD, T)

    def body(x_ref, pg_ref, qg_ref, w1_ref, w2_ref, xo_ref, a_ref, s_ref, y_ref):
        xv = x_ref[...]
        h, _, _ = _rms(xv, pg_ref[...])
        hb = h.astype(BF16)
        acc = jnp.zeros((tm, D), F32)
        for c0, cw in FF_CHUNKS:
            g = _dot_nt(hb, w1_ref[c0:c0 + cw, :])
            u = _dot_nt(hb, w1_ref[FF + c0:FF + c0 + cw, :])
            sg = _sigmoid(g)
            si = g * sg
            a_ref[:, c0:c0 + cw] = (u * (sg + si - si * sg)).astype(BF16)
            a_ref[:, FF + c0:FF + c0 + cw] = si.astype(BF16)
            s = (si * u).astype(BF16)
            s_ref[:, c0:c0 + cw] = s
            acc = acc + _dot(s, w2_ref[c0:c0 + cw, :])
        y_ref[...] = acc.astype(BF16)
        o, _, _ = _rms(acc, qg_ref[...])
        xo_ref[...] = xv + 0.5 * o

    return _call(
        body, (T // tm,),
        [_row_spec(tm, D), _full_spec((1, D)), _full_spec((1, D)), _whole_spec(w1), _whole_spec(w2)],
        (x, pre_g, post_g, w1, w2),
        [_row_spec(tm, D), _row_spec(tm, 2 * FF), _row_spec(tm, FF), _row_spec(tm, D)],
        [jax.ShapeDtypeStruct((T, D), F32), jax.ShapeDtypeStruct((T, 2 * FF), BF16),
         jax.ShapeDtypeStruct((T, FF), BF16), jax.ShapeDtypeStruct((T, D), BF16)],
        "ffn_fwd", copies=copies)


def ffn_bwd(dxo, x, y, a, pre_g, post_g, w1, w2, copies=None):
    T = x.shape[0]
    tm = min(TM_FFN_BWD, T)

    def body(dxo_ref, x_ref, y_ref, a_ref, pg_ref, qg_ref, w1_ref, w2_ref,
             dx_ref, da_ref, dy_ref, hb_ref, dpg_ref, dqg_ref):
        @pl.when(pl.program_id(0) == 0)
        def _():
            dpg_ref[...] = jnp.zeros_like(dpg_ref)
            dqg_ref[...] = jnp.zeros_like(dqg_ref)

        dxo = dxo_ref[...]
        qg = qg_ref[...]
        _, ny, ry = _rms(y_ref[...].astype(F32), qg)
        dn = 0.5 * dxo
        dqg_ref[...] += _colsum(dn * ny)
        dyb = _rms_bwd(ny, ry, qg, dn).astype(BF16)
        dy_ref[...] = dyb
        pg = pg_ref[...]
        h, nx, rx = _rms(x_ref[...], pg)
        hb_ref[...] = h.astype(BF16)
        dh = jnp.zeros((tm, D), F32)
        for c0, cw in FF_CHUNKS:
            ds = _dot_nt(dyb, w2_ref[c0:c0 + cw, :])
            dg = (ds * a_ref[:, c0:c0 + cw].astype(F32)).astype(BF16)
            du = (ds * a_ref[:, FF + c0:FF + c0 + cw].astype(F32)).astype(BF16)
            da_ref[:, c0:c0 + cw] = dg
            da_ref[:, FF + c0:FF + c0 + cw] = du
            dh = dh + _dot(dg, w1_ref[c0:c0 + cw, :]) + _dot(du, w1_ref[FF + c0:FF + c0 + cw, :])
        dpg_ref[...] += _colsum(dh * nx)
        dx_ref[...] = dxo + _rms_bwd(nx, rx, pg, dh)

    return _call(
        body, (T // tm,),
        [_row_spec(tm, D), _row_spec(tm, D), _row_spec(tm, D), _row_spec(tm, 2 * FF),
         _full_spec((1, D)), _full_spec((1, D)), _whole_spec(w1), _whole_spec(w2)],
        (dxo, x, y, a, pre_g, post_g, w1, w2),
        [_row_spec(tm, D), _row_spec(tm, 2 * FF), _row_spec(tm, D), _row_spec(tm, D),
         _full_spec((1, D)), _full_spec((1, D))],
        [jax.ShapeDtypeStruct((T, D), F32), jax.ShapeDtypeStruct((T, 2 * FF), BF16),
         jax.ShapeDtypeStruct((T, D), BF16), jax.ShapeDtypeStruct((T, D), BF16),
         jax.ShapeDtypeStruct((1, D), F32), jax.ShapeDtypeStruct((1, D), F32)],
        "ffn_bwd", copies=copies)


def mix_in_fwd(x, pre_g, win, copies=None):
    T = x.shape[0]
    tm = min(TM_FFN_FWD, T)

    def body(x_ref, pg_ref, w_ref, z_ref):
        h, _, _ = _rms(x_ref[...], pg_ref[...])
        hb = h.astype(BF16)
        for w0, n, z0 in WIN_SEGMENTS:
            z_ref[:, z0:z0 + n] = _dot_nt(hb, w_ref[w0:w0 + n, :]).astype(BF16)

    return _call(
        body, (T // tm,), [_row_spec(tm, D), _full_spec((1, D)), _whole_spec(win)], (x, pre_g, win),
        [_row_spec(tm, IN_WIDTH)], [jax.ShapeDtypeStruct((T, IN_WIDTH), BF16)], "mix_in_fwd", copies=copies)


STACK = Q_PER_KV // 2
SROWS = STACK * BLK


def _sees_own():
    qi = lax.broadcasted_iota(jnp.int32, (SROWS, BLK), 0) & (BLK - 1)
    return lax.broadcasted_iota(jnp.int32, (SROWS, BLK), 1) <= qi


def _both_blocks(picked, own):
    return jnp.concatenate([jnp.where(own, 0.0, picked), jnp.where(own, picked, 0.0)], axis=1)


def _lane_half(rows):
    return lax.broadcasted_iota(jnp.int32, (rows, BLK), 1) // HEAD_DIM


def _stack(ref, b, g):
    return jnp.concatenate([ref[b * BLK:(b + 1) * BLK, (STACK * g + j) * BLK:(STACK * g + j + 1) * BLK]
                            for j in range(STACK)], axis=0)


def _placed(pair, g, hp, fill):
    src = pair if hp == g else pltpu.roll(pair, HEAD_DIM, 1)
    return jnp.where(_lane_half(2 * BLK) == hp, src, fill).astype(BF16)


def _sink_column(sink_ref, g, hp):
    rb = lax.broadcasted_iota(jnp.int32, (SROWS, 1), 0) // BLK
    col = jnp.full((SROWS, 1), sink_ref[Q_PER_KV * g + hp], F32)
    for j in range(1, STACK):
        col = jnp.where(rb == j, sink_ref[Q_PER_KV * g + 2 * j + hp], col)
    return col


def _attn_scores(qs, kz, sink_col, own, no_previous):
    raw = _dot_nt(qs, kz)
    prev = raw[:, 0:BLK]
    if no_previous is not None:
        prev = prev + no_previous.astype(F32) * MASK_VALUE
    picked = jnp.where(own, raw[:, BLK:2 * BLK], prev)
    sink_raw = sink_col * (1.0 / ATTN_SCALE)
    m = jnp.maximum(jnp.max(picked, axis=-1, keepdims=True), sink_raw)
    factor = ATTN_SCALE / math.log(2.0)
    return jnp.exp2((picked - m) * factor), jnp.exp2((sink_raw - m) * factor)


def _kv_specs(tq, nb):
    kv_blk = ZKV // (2 * BLK)
    return [pl.BlockSpec((tq, 2 * BLK), lambda i: (i, kv_blk)),
            pl.BlockSpec((BLK, 2 * BLK), lambda i: (jnp.maximum(i * nb - 1, 0), kv_blk))]


def attn_fwd(z, sinks, copies=None):
    T = z.shape[0]
    tq = min(TQ_ATTN, T)
    nb = tq // BLK

    def body(sink_ref, q_ref, kv_ref, kvh_ref, o_ref):
        i = pl.program_id(0)
        low = _lane_half(SROWS) == 0
        own = _sees_own()
        for b in range(nb):
            kvp = kvh_ref[...] if b == 0 else kv_ref[(b - 1) * BLK:b * BLK, :]
            kv2 = jnp.concatenate([kvp, kv_ref[b * BLK:(b + 1) * BLK, :]], axis=0).astype(F32)
            no_previous = (i == 0) if b == 0 else None
            for g in range(N_KV_HEADS):
                qs = _stack(q_ref, b, g)
                r, e = [], []
                for hp in range(2):
                    p, e_sink = _attn_scores(qs, _placed(kv2[:, 0:BLK], g, hp, 0.0), _sink_column(sink_ref, g, hp),
                                             own, no_previous)
                    r.append(_dot(_both_blocks(p, own).astype(BF16), _placed(kv2[:, BLK:2 * BLK], g, hp, 1.0)))
                    e.append(e_sink)
                den = pltpu.roll(jnp.where(low, r[1], r[0]), HEAD_DIM, 1) + jnp.where(low, e[0], e[1])
                out = (jnp.where(low, r[0], r[1]) * (1.0 / den)).astype(BF16)
                for j in range(STACK):
                    o_ref[b * BLK:(b + 1) * BLK, (STACK * g + j) * BLK:(STACK * g + j + 1) * BLK] = \
                        out[j * BLK:(j + 1) * BLK, :]

    return _call(
        body, (T // tq,), [pl.BlockSpec(memory_space=pltpu.SMEM), _row_spec(tq, D)] + _kv_specs(tq, nb),
        (sinks, z, z, z), [_row_spec(tq, D)], [jax.ShapeDtypeStruct((T, D), BF16)], "attn_fwd", copies=copies)


def attn_bwd(z, ya, dya, dzb, sinks):
    T = z.shape[0]
    tq = min(TQ_ATTN, T)
    nb = tq // BLK
    nt = T // tq
    mid = ZKV - ZU

    def body(sink_ref, q_ref, kv_ref, kvh_ref, y_ref, dy_ref, dzb_ref, dz_ref, dsink_ref, acc_ref, carry_ref):
        i = pl.program_id(0)
        first_tile = i == nt - 1

        @pl.when(i == 0)
        def _():
            carry_ref[...] = jnp.zeros_like(carry_ref)
            dsink_ref[...] = jnp.zeros_like(dsink_ref)

        acc_ref[...] = jnp.zeros_like(acc_ref)
        lane = lax.broadcasted_iota(jnp.int32, (1, BLK), 1)
        dsink = jnp.zeros((1, BLK), F32)
        half = _lane_half(SROWS)
        own = _sees_own()
        for b in range(nb):
            kvp = kvh_ref[...] if b == 0 else kv_ref[(b - 1) * BLK:b * BLK, :]
            kv2 = jnp.concatenate([kvp, kv_ref[b * BLK:(b + 1) * BLK, :]], axis=0).astype(F32)
            no_previous = first_tile if b == 0 else None
            dk_groups, dv_groups = [], []
            for g in range(N_KV_HEADS):
                qs = _stack(q_ref, b, g)
                dys = _stack(dy_ref, b, g)
                dyy = dys.astype(F32) * _stack(y_ref, b, g).astype(F32)
                dq = jnp.zeros((SROWS, BLK), F32)
                ds_both, pn_both = [], []
                for hp in range(2):
                    kz = _placed(kv2[:, 0:BLK], g, hp, 0.0)
                    p, e_sink = _attn_scores(qs, kz, _sink_column(sink_ref, g, hp), own, no_previous)
                    inv = 1.0 / (jnp.sum(p, axis=-1, keepdims=True) + e_sink)
                    p = p * inv
                    delta = jnp.sum(jnp.where(half == hp, dyy, 0.0), axis=-1, keepdims=True)
                    dp = _dot_nt(dys, _placed(kv2[:, BLK:2 * BLK], g, hp, 0.0))
                    ds = p * (jnp.where(own, dp[:, BLK:2 * BLK], dp[:, 0:BLK]) - delta)
                    ds = _both_blocks(ds, own).astype(BF16)
                    pn = _both_blocks(p, own)
                    sink_term = e_sink * inv * delta
                    for j in range(STACK):
                        dsink = dsink + jnp.where(lane == Q_PER_KV * g + 2 * j + hp,
                                                  -_colsum(sink_term[j * BLK:(j + 1) * BLK, :]), 0.0)
                    dq = dq + _dot(ds, kz)
                    ds_both.append(ds)
                    pn_both.append(pn.astype(BF16))
                dk_t = _dot_tn(qs, jnp.concatenate(ds_both, axis=1))
                dv_t = _dot_tn(dys, jnp.concatenate(pn_both, axis=1))
                for t, groups in ((dk_t, dk_groups), (dv_t, dv_groups)):
                    groups.append(t[0:HEAD_DIM, 0:2 * BLK] + t[HEAD_DIM:BLK, 2 * BLK:4 * BLK])
                dqb = (dq * ATTN_SCALE).astype(BF16)
                for j in range(STACK):
                    dz_ref[b * BLK:(b + 1) * BLK, DZ_Q + (STACK * g + j) * BLK:DZ_Q + (STACK * g + j + 1) * BLK] = \
                        dqb[j * BLK:(j + 1) * BLK, :]
            acc_ref[b * BLK:(b + 2) * BLK, 0:BLK] += jnp.concatenate(dk_groups, axis=0).T * ATTN_SCALE
            acc_ref[b * BLK:(b + 2) * BLK, BLK:2 * BLK] += jnp.concatenate(dv_groups, axis=0).T
        dsink_ref[...] += dsink
        dz_ref[:, DZ_MID:IN_WIDTH] = dzb_ref[...]
        if nb > 1:
            dz_ref[0:tq - BLK, DZ_KV:DZ_MID] = acc_ref[BLK:tq, :].astype(BF16)
        dz_ref[tq - BLK:tq, DZ_KV:DZ_MID] = (acc_ref[tq:tq + BLK, :] + carry_ref[...]).astype(BF16)
        carry_ref[...] = acc_ref[0:BLK, :]

    kv_blk = ZKV // (2 * BLK)
    return pl.pallas_call(
        body, grid=(nt,),
        in_specs=[pl.BlockSpec(memory_space=pltpu.SMEM),
                  pl.BlockSpec((tq, D), lambda i: (nt - 1 - i, 0)),
                  pl.BlockSpec((tq, 2 * BLK), lambda i: (nt - 1 - i, kv_blk)),
                  pl.BlockSpec((BLK, 2 * BLK), lambda i: (jnp.maximum((nt - 1 - i) * nb - 1, 0), kv_blk)),
                  pl.BlockSpec((tq, D), lambda i: (nt - 1 - i, 0)),
                  pl.BlockSpec((tq, D), lambda i: (nt - 1 - i, 0)),
                  pl.BlockSpec((tq, mid), lambda i: (nt - 1 - i, 0))],
        out_specs=[pl.BlockSpec((tq, IN_WIDTH), lambda i: (nt - 1 - i, 0)), _full_spec((1, BLK))],
        out_shape=[jax.ShapeDtypeStruct((T, IN_WIDTH), BF16), jax.ShapeDtypeStruct((1, BLK), F32)],
        scratch_shapes=[pltpu.VMEM((tq + BLK, 2 * BLK), F32), pltpu.VMEM((BLK, 2 * BLK), F32)],
        name="attn_bwd", compiler_params=_params(1),
    )(sinks, z, z, z, ya, dya, dzb)


def _to_group_lanes(v, g, nch):
    return jnp.concatenate([v[n * BLK:(n + 1) * BLK, g * BLK:(g + 1) * BLK] for n in range(nch)], axis=1)


def _from_group_lanes(per_group, nch):
    rows = [jnp.concatenate([per_group[g][:, n * BLK:(n + 1) * BLK] for g in range(SGU_GROUPS)], axis=1)
            for n in range(nch)]
    return jnp.concatenate(rows, axis=0)


def _tril_bf16(w_ref, g):
    t = lax.broadcasted_iota(jnp.int32, (BLK, BLK), 0)
    s = lax.broadcasted_iota(jnp.int32, (BLK, BLK), 1)
    return jnp.where(t >= s, w_ref[g], 0.0).astype(BF16)


def _sgu_norm(v_s, ln_g, ln_b):
    cdf, pdf = _gelu_parts(v_s)
    gv = v_s * cdf
    xc = gv - jnp.mean(gv, axis=-1, keepdims=True)
    rstd = lax.rsqrt(jnp.mean(xc * xc, axis=-1, keepdims=True) + LN_EPS)
    nhat = xc * rstd
    return nhat * ln_g + ln_b, nhat, rstd, cdf + v_s * pdf


def _sgu_gate(vn, w_ref, bb_ref, nch):
    vnb = vn.astype(BF16)
    return _from_group_lanes(
        [_dot(_tril_bf16(w_ref, g), _to_group_lanes(vnb, g, nch)) + jnp.tile(bb_ref[g], (1, nch))
         for g in range(SGU_GROUPS)], nch)


def mix_fwd_out(x, z, ya, post_g, ln_g, ln_b, sgu_w, sgu_bb, wa, ws, wo, copies=None):
    T = x.shape[0]
    tm = min(TM_MIX, T)
    nch = tm // BLK

    def body(x_ref, us_ref, vs_ref, ga_ref, gb_ref, ya_ref, qg_ref, lg_ref, lb_ref, w_ref, bb_ref,
             wa_ref, ws_ref, wo_ref, xo_ref, ysg_ref, pa_ref, pb_ref, o_ref):
        vn, _, _, _ = _sgu_norm(vs_ref[...].astype(F32), lg_ref[...], lb_ref[...])
        gate = _sgu_gate(vn, w_ref, bb_ref, nch)
        us = us_ref[...].astype(F32)
        cdf, _ = _gelu_parts(us)
        ysg = (us * cdf * gate).astype(BF16)
        ysg_ref[...] = ysg
        pa = _dot(ya_ref[...], wa_ref[...])
        pb = _dot(ysg, ws_ref[...])
        pa_ref[...] = pa.astype(BF16)
        pb_ref[...] = pb.astype(BF16)
        merged = _sigmoid(ga_ref[...].astype(F32)) * pa + _sigmoid(gb_ref[...].astype(F32)) * pb
        o = _dot(merged.astype(BF16), wo_ref[...])
        o_ref[...] = o.astype(BF16)
        on, _, _ = _rms(o, qg_ref[...])
        xo_ref[...] = x_ref[...] + on

    zspec = lambda start: _row_spec(tm, D, start // D)
    act = jax.ShapeDtypeStruct((T, D), BF16)
    return _call(
        body, (T // tm,),
        [_row_spec(tm, D), zspec(ZU), zspec(ZV), zspec(ZGA), zspec(ZGB), _row_spec(tm, D),
         _full_spec((1, D)), _full_spec((1, D)), _full_spec((1, D)),
         _full_spec((SGU_GROUPS, BLK, BLK)), _full_spec((SGU_GROUPS, BLK, BLK)),
         _whole_spec(wa), _whole_spec(ws), _whole_spec(wo)],
        (x, z, z, z, z, ya, post_g, ln_g, ln_b, sgu_w, sgu_bb, wa, ws, wo),
        [_row_spec(tm, D)] * 5, [jax.ShapeDtypeStruct((T, D), F32), act, act, act, act],
        "mix_fwd_out", copies=copies)


def mix_bwd_out(dxo, z, o, pa, pb, post_g, ln_g, ln_b, sgu_w, sgu_bb, wa, ws, wo, copies=None):
    T = dxo.shape[0]
    tm = min(TM_MIX_BWD, T)
    nch = tm // BLK
    mid = ZKV - ZU

    def body(dxo_ref, us_ref, vs_ref, ga_ref, gb_ref, o_ref, pa_ref, pb_ref, qg_ref, lg_ref, lb_ref, w_ref, bb_ref,
             wa_ref, ws_ref, wo_ref,
             dzb_ref, dya_ref, mg_ref, do_ref, dpa_ref, dpb_ref, dqg_ref, dlg_ref, dlb_ref, dw_ref, dbb_ref):
        @pl.when(pl.program_id(0) == 0)
        def _():
            for r in (dqg_ref, dlg_ref, dlb_ref, dw_ref, dbb_ref):
                r[...] = jnp.zeros_like(r)

        qg = qg_ref[...]
        dxo = dxo_ref[...]
        _, no, ro = _rms(o_ref[...].astype(F32), qg)
        dqg_ref[...] += _colsum(dxo * no)
        dob = _rms_bwd(no, ro, qg, dxo).astype(BF16)
        do_ref[...] = dob
        dmerged = _dot_nt(dob, wo_ref[...])
        sa = _sigmoid(ga_ref[...].astype(F32))
        sb = _sigmoid(gb_ref[...].astype(F32))
        pa = pa_ref[...].astype(F32)
        pb = pb_ref[...].astype(F32)
        mg_ref[...] = (sa * pa + sb * pb).astype(BF16)
        dpa = (dmerged * sa).astype(BF16)
        dpb = (dmerged * sb).astype(BF16)
        dpa_ref[...] = dpa
        dpb_ref[...] = dpb
        dzb_ref[:, ZGA - ZU:ZGA - ZU + D] = (dmerged * pa * sa * (1.0 - sa)).astype(BF16)
        dzb_ref[:, ZGB - ZU:ZGB - ZU + D] = (dmerged * pb * sb * (1.0 - sb)).astype(BF16)
        dya_ref[...] = _dot_nt(dpa, wa_ref[...]).astype(BF16)
        dysg = _dot_nt(dpb, ws_ref[...])

        lg = lg_ref[...]
        vn, nhat, rstd, dgelu_v = _sgu_norm(vs_ref[...].astype(F32), lg, lb_ref[...])
        gate = _sgu_gate(vn, w_ref, bb_ref, nch)
        us = us_ref[...].astype(F32)
        cdf, pdf = _gelu_parts(us)
        dzb_ref[:, 0:D] = (dysg * gate * (cdf + us * pdf)).astype(BF16)
        dgate = (dysg * (us * cdf)).astype(BF16)
        vnb = vn.astype(BF16)
        t = lax.broadcasted_iota(jnp.int32, (BLK, BLK), 0)
        s = lax.broadcasted_iota(jnp.int32, (BLK, BLK), 1)
        dvn_groups = []
        for g in range(SGU_GROUPS):
            dgl = _to_group_lanes(dgate, g, nch)
            dbb_ref[g] += jnp.broadcast_to(jnp.sum(dgl.astype(F32), axis=-1, keepdims=True), (BLK, BLK))
            dw_ref[g] += jnp.where(t >= s, _dot_nt(dgl, _to_group_lanes(vnb, g, nch)), 0.0)
            dvn_groups.append(_dot_tn(_tril_bf16(w_ref, g), dgl))
        dvn = _from_group_lanes(dvn_groups, nch)
        dlg_ref[...] += _colsum(dvn * nhat)
        dlb_ref[...] += _colsum(dvn)
        dnh = dvn * lg
        dgv = rstd * (dnh - jnp.mean(dnh, axis=-1, keepdims=True) - nhat * jnp.mean(dnh * nhat, axis=-1, keepdims=True))
        dzb_ref[:, ZV - ZU:ZV - ZU + D] = (dgv * dgelu_v).astype(BF16)

    zspec = lambda start: _row_spec(tm, D, start // D)
    act = jax.ShapeDtypeStruct((T, D), BF16)
    grp = jax.ShapeDtypeStruct((SGU_GROUPS, BLK, BLK), F32)
    vec = jax.ShapeDtypeStruct((1, D), F32)
    return _call(
        body, (T // tm,),
        [_row_spec(tm, D), zspec(ZU), zspec(ZV), zspec(ZGA), zspec(ZGB),
         _row_spec(tm, D), _row_spec(tm, D), _row_spec(tm, D),
         _full_spec((1, D)), _full_spec((1, D)), _full_spec((1, D)),
         _full_spec((SGU_GROUPS, BLK, BLK)), _full_spec((SGU_GROUPS, BLK, BLK)),
         _whole_spec(wa), _whole_spec(ws), _whole_spec(wo)],
        (dxo, z, z, z, z, o, pa, pb, post_g, ln_g, ln_b, sgu_w, sgu_bb, wa, ws, wo),
        [_row_spec(tm, mid)] + [_row_spec(tm, D)] * 5 + [_full_spec((1, D))] * 3
        + [_full_spec((SGU_GROUPS, BLK, BLK))] * 2,
        [jax.ShapeDtypeStruct((T, mid), BF16), act, act, act, act, act, vec, vec, vec, grp, grp],
        "mix_bwd_out", copies=copies)


def mix_in_bwd(dxo, x, dz, pre_g, win):
    T = x.shape[0]
    tm = min(TM_MIX, T)

    def body(dxo_ref, x_ref, dz_ref, pg_ref, w_ref, dx_ref, hb_ref, dpg_ref):
        @pl.when(pl.program_id(0) == 0)
        def _():
            dpg_ref[...] = jnp.zeros_like(dpg_ref)

        pg = pg_ref[...]
        h, nx, rx = _rms(x_ref[...], pg)
        hb_ref[...] = h.astype(BF16)
        dh = jnp.zeros((tm, D), F32)
        for c0, cw in WIN_CHUNKS:
            dh = dh + _dot(dz_ref[:, c0:c0 + cw], w_ref[c0:c0 + cw, :])
        dpg_ref[...] += _colsum(dh * nx)
        dx_ref[...] = dxo_ref[...] + _rms_bwd(nx, rx, pg, dh)

    return pl.pallas_call(
        body, grid=(T // tm,),
        in_specs=[_row_spec(tm, D), _row_spec(tm, D), _row_spec(tm, IN_WIDTH), _full_spec((1, D)), _whole_spec(win)],
        out_specs=[_row_spec(tm, D), _row_spec(tm, D), _full_spec((1, D))],
        out_shape=[jax.ShapeDtypeStruct((T, D), F32), jax.ShapeDtypeStruct((T, D), BF16),
                   jax.ShapeDtypeStruct((1, D), F32)],
        name="mix_in_bwd", compiler_params=_params(1),
    )(dxo, x, dz, pre_g, win)


def wgrad(wide, narrow, name, tmo, copies=None):
    T, N = wide.shape
    M = narrow.shape[1]
    tk = min(TK_WGRAD, T)
    nk = T // tk
    chunk = max(c for c in range(BLK, 1792 + 1, BLK) if N % c == 0)

    def body(a_ref, b_ref, o_ref, acc_ref):
        k = pl.program_id(1)

        @pl.when(k == 0)
        def _():
            acc_ref[...] = jnp.zeros_like(acc_ref)

        acc_ref[...] += _dot_tn(a_ref[...], b_ref[...])

        @pl.when(k == nk - 1)
        def _():
            for c0 in range(0, N, chunk):
                o_ref[c0:c0 + chunk, :] = acc_ref[:, c0:c0 + chunk].T.astype(BF16)

    (out,), got = _call(
        body, (M // tmo, nk),
        [pl.BlockSpec((tk, tmo), lambda m, k: (k, m)), pl.BlockSpec((tk, N), lambda m, k: (k, 0))], (narrow, wide),
        [pl.BlockSpec((N, tmo), lambda m, k: (0, m))], [jax.ShapeDtypeStruct((N, M), BF16)], name,
        scratch=[pltpu.VMEM((tmo, N), F32)], copies=copies)
    return out if copies is None else (out, got)


def loss_head(y, target):
    T = y.shape[0]
    tm = min(TM_FFN_FWD, T)

    def body(y_ref, t_ref, sq_ref, dy_ref):
        @pl.when(pl.program_id(0) == 0)
        def _():
            sq_ref[...] = jnp.zeros_like(sq_ref)

        err = y_ref[...] - t_ref[...]
        sq_ref[...] += _colsum(err * err)
        dy_ref[...] = err * (1.0 / D)

    return pl.pallas_call(
        body, grid=(T // tm,), in_specs=[_row_spec(tm, D), _row_spec(tm, D)],
        out_specs=[_full_spec((1, D)), _row_spec(tm, D)],
        out_shape=[jax.ShapeDtypeStruct((1, D), F32), jax.ShapeDtypeStruct((T, D), F32)],
        name="loss_head", compiler_params=_params(1),
    )(y, target)


def adamw_sum(parts, w, m, v, name):
    layers, rows, cols = w.shape
    tr = _row_tile(rows, TR_ADAM)
    nr = rows // tr
    c1 = 1.0 - ADAM_B1 ** ADAM_STEP
    c2 = 1.0 - ADAM_B2 ** ADAM_STEP

    def body(*refs):
        p_refs = refs[:layers]
        w_ref, m_ref, v_ref, g_ref, d_ref, nm_ref, nv_ref = refs[layers:]
        for k in range(layers):
            @pl.when(pl.program_id(0) == k)
            def _(p_ref=p_refs[k]):
                g = p_ref[0].astype(F32)
                for j in range(1, N_DEV):
                    g = g + p_ref[j].astype(F32)
                nm = ADAM_B1 * m_ref[...] + (1.0 - ADAM_B1) * g
                nv = ADAM_B2 * v_ref[...] + (1.0 - ADAM_B2) * (g * g)
                g_ref[...] = g
                nm_ref[...] = nm
                nv_ref[...] = nv
                d_ref[...] = -ADAM_LR * ((nm / c1) / (jnp.sqrt(nv / c2) + ADAM_EPS) + ADAM_WD * w_ref[...])

    def part_spec(k):
        return pl.BlockSpec((N_DEV, tr, cols),
                            lambda l, i: (0, jnp.where(l < k, 0, jnp.where(l == k, i, nr - 1)), 0))

    spec = pl.BlockSpec((None, tr, cols), lambda l, i: (l, i, 0))
    out = jax.ShapeDtypeStruct((layers, rows, cols), F32)
    return pl.pallas_call(
        body, grid=(layers, nr),
        in_specs=[part_spec(k) for k in range(layers)] + [spec, spec, spec],
        out_specs=[spec] * 4, out_shape=[out] * 4, name=name, compiler_params=_params(2),
    )(*parts, w, m, v)


def _pack_small(p):
    layers = []
    for l in range(DEPTH):
        rows = [p[n][l].reshape(1, D) for n in SMALL_VEC]
        rows.append(p["sgu_b"][l].reshape(1, D))
        rows.append(jnp.pad(p["attn_sinks"][l].reshape(1, N_Q_HEADS), ((0, 0), (0, D - N_Q_HEADS))))
        rows.append(p["sgu_w"][l].reshape(BLK, D))
        used = len(SMALL_VEC) + 2 + BLK
        rows.append(jnp.zeros((SMALL_ROWS - used, D), F32))
        layers.append(jnp.concatenate(rows, axis=0))
    return jnp.concatenate(layers, axis=0)


def _unpack_small(packed):
    a = packed.reshape(DEPTH, SMALL_ROWS, D)
    out = {n: a[:, i, :] for i, n in enumerate(SMALL_VEC)}
    k = len(SMALL_VEC)
    out["sgu_b"] = a[:, k, :].reshape(DEPTH, SGU_GROUPS, BLK)
    out["attn_sinks"] = a[:, k + 1, :N_Q_HEADS]
    out["sgu_w"] = a[:, k + 2:k + 2 + BLK, :].reshape(DEPTH, SGU_GROUPS, BLK, BLK)
    return out


WEIGHT_NAMES = ("ffn1_pre_g", "ffn1_w1", "ffn1_w2", "ffn1_post_g", "mix_pre_g", "w_in", "attn_sinks", "sgu_ln_g",
                "sgu_ln_b", "sgu_w", "sgu_b", "w_attn_branch", "w_sgu_branch", "w_out", "mix_post_g", "ffn2_pre_g",
                "ffn2_w1", "ffn2_w2", "ffn2_post_g")
COL_SHARDED = ("ffn1_w1", "ffn2_w1", "w_in")
ROW_SHARDED = ("ffn1_w2", "ffn2_w2", "w_attn_branch", "w_sgu_branch", "w_out")
MATRICES = COL_SHARDED + ROW_SHARDED


def kernel(x, ffn1_pre_g, ffn1_w1, ffn1_w2, ffn1_post_g, mix_pre_g, w_in, attn_sinks, sgu_ln_g, sgu_ln_b, sgu_w, sgu_b, w_attn_branch, w_sgu_branch, w_out, mix_post_g, ffn2_pre_g, ffn2_w1, ffn2_w2, ffn2_post_g, loss_target, m_ffn1_pre_g, m_ffn1_w1, m_ffn1_w2, m_ffn1_post_g, m_mix_pre_g, m_w_in, m_attn_sinks, m_sgu_ln_g, m_sgu_ln_b, m_sgu_w, m_sgu_b, m_w_attn_branch, m_w_sgu_branch, m_w_out, m_mix_post_g, m_ffn2_pre_g, m_ffn2_w1, m_ffn2_w2, m_ffn2_post_g, v_ffn1_pre_g, v_ffn1_w1, v_ffn1_w2, v_ffn1_post_g, v_mix_pre_g, v_w_in, v_attn_sinks, v_sgu_ln_g, v_sgu_ln_b, v_sgu_w, v_sgu_b, v_w_attn_branch, v_w_sgu_branch, v_w_out, v_mix_post_g, v_ffn2_pre_g, v_ffn2_w1, v_ffn2_w2, v_ffn2_post_g):
    w = dict(zip(WEIGHT_NAMES, (ffn1_pre_g, ffn1_w1, ffn1_w2, ffn1_post_g, mix_pre_g, w_in, attn_sinks, sgu_ln_g,
                                sgu_ln_b, sgu_w, sgu_b, w_attn_branch, w_sgu_branch, w_out, mix_post_g, ffn2_pre_g,
                                ffn2_w1, ffn2_w2, ffn2_post_g)))
    mom = dict(zip(WEIGHT_NAMES, (m_ffn1_pre_g, m_ffn1_w1, m_ffn1_w2, m_ffn1_post_g, m_mix_pre_g, m_w_in,
                                  m_attn_sinks, m_sgu_ln_g, m_sgu_ln_b, m_sgu_w, m_sgu_b, m_w_attn_branch,
                                  m_w_sgu_branch, m_w_out, m_mix_post_g, m_ffn2_pre_g, m_ffn2_w1, m_ffn2_w2,
                                  m_ffn2_post_g)))
    var = dict(zip(WEIGHT_NAMES, (v_ffn1_pre_g, v_ffn1_w1, v_ffn1_w2, v_ffn1_post_g, v_mix_pre_g, v_w_in,
                                  v_attn_sinks, v_sgu_ln_g, v_sgu_ln_b, v_sgu_w, v_sgu_b, v_w_attn_branch,
                                  v_w_sgu_branch, v_w_out, v_mix_post_g, v_ffn2_pre_g, v_ffn2_w1, v_ffn2_w2,
                                  v_ffn2_post_g)))
    T = x.shape[1]
    xs = x.reshape(T, D)
    target = loss_target.reshape(T, D)

    for n in COL_SHARDED:
        w[n], mom[n], var[n] = (jnp.swapaxes(t[n], 1, 2) for t in (w, mom, var))

    shard = {n: [cast_layer(w[n], l) for l in range(DEPTH)] for n in MATRICES}
    weights = [{} for _ in range(DEPTH)]
    ffn1, ffn2, squares = ("ffn1_w1", "ffn1_w2"), ("ffn2_w1", "ffn2_w2"), ("w_attn_branch", "w_sgu_branch", "w_out")
    for n, g in zip(ffn1, all_gather_weights([shard[n][0] for n in ffn1])):
        weights[0][n] = g.reshape(-1, D)

    def fetch(groups):
        jobs = [("all", shard[n][l]) for l, names in groups if l < DEPTH for n in names]
        return PeerCopies(jobs) if jobs else None

    def landed(groups, arrivals):
        slots = [(l, n) for l, names in groups if l < DEPTH for n in names]
        for (l, n), g in zip(slots, arrivals):
            weights[l][n] = g.reshape(-1, D)

    vec = lambda n, l: w[n][l].reshape(1, D)
    sgu_bb = [jnp.broadcast_to(w["sgu_b"][l][:, :, None], (SGU_GROUPS, BLK, BLK)) for l in range(DEPTH)]

    saved = []
    h = xs
    for l in range(DEPTH):
        wl = weights[l]
        ahead = lambda names: [(0, names)] if l == 0 else []
        x0 = h
        groups = [(l + 1, ffn1)] + ahead(("w_in",))
        (x1, a1, s1, y1), got = ffn_fwd(x0, vec("ffn1_pre_g", l), vec("ffn1_post_g", l), wl["ffn1_w1"], wl["ffn1_w2"],
                                        fetch(groups))
        landed(groups, got)
        groups = [(l + 1, ("w_in",))] + ahead(squares)
        (z,), got = mix_in_fwd(x1, vec("mix_pre_g", l), wl["w_in"], fetch(groups))
        landed(groups, got)
        groups = ahead(("ffn2_w1",))
        (ya,), got = attn_fwd(z, w["attn_sinks"][l], fetch(groups))
        landed(groups, got)
        groups = [(l + 1, squares)] + ahead(("ffn2_w2",))
        (x2, ysg, pa, pb, o), got = mix_fwd_out(
            x1, z, ya, vec("mix_post_g", l), vec("sgu_ln_g", l), vec("sgu_ln_b", l), w["sgu_w"][l], sgu_bb[l],
            wl["w_attn_branch"], wl["w_sgu_branch"], wl["w_out"], fetch(groups))
        landed(groups, got)
        groups = [(l + 1, ffn2)]
        (h, a2, s2, y2), got = ffn_fwd(x2, vec("ffn2_pre_g", l), vec("ffn2_post_g", l), wl["ffn2_w1"], wl["ffn2_w2"],
                                       fetch(groups))
        landed(groups, got)
        saved.append((x0, a1, s1, y1, x1, z, ya, ysg, pa, pb, o, x2, a2, s2, y2))

    sq, dx = loss_head(h, target)
    loss = lax.psum(0.5 / D * jnp.sum(sq), ("x", "y", "c"))

    def ffn_wgrads(hb, da, s, dy):
        return [("own", wgrad(da, hb, "wgrad_ffn_w1", D // 2).reshape(N_DEV, W1_SHARD, D)),
                ("own", wgrad(s, dy, "wgrad_ffn_w2", D).reshape(N_DEV, W2_SHARD, D))]

    parts = {n: [None] * DEPTH for n in MATRICES}
    small = {n: [None] * DEPTH for n in WEIGHT_NAMES if n not in MATRICES}
    waiting = None
    for l in reversed(range(DEPTH)):
        wl = weights[l]
        x0, a1, s1, y1, x1, z, ya, ysg, pa, pb, o, x2, a2, s2, y2 = saved[l]
        (dx, da, dy, hb, dpg, dqg), got = ffn_bwd(
            dx, x2, y2, a2, vec("ffn2_pre_g", l), vec("ffn2_post_g", l), wl["ffn2_w1"], wl["ffn2_w2"],
            PeerCopies(waiting) if waiting else None)
        if waiting:
            parts["ffn1_w1"][l + 1], parts["ffn1_w2"][l + 1] = got
        small["ffn2_pre_g"][l], small["ffn2_post_g"][l] = dpg, dqg

        (dzb, dya, mg, dob, dpa, dpb, dqg, dlg, dlb, dsw, dsb), got = mix_bwd_out(
            dx, z, o, pa, pb, vec("mix_post_g", l), vec("sgu_ln_g", l), vec("sgu_ln_b", l), w["sgu_w"][l], sgu_bb[l],
            wl["w_attn_branch"], wl["w_sgu_branch"], wl["w_out"], PeerCopies(ffn_wgrads(hb, da, s2, dy)))
        parts["ffn2_w1"][l], parts["ffn2_w2"][l] = got
        dz, dsink = attn_bwd(z, ya, dya, dzb, w["attn_sinks"][l])
        dx, hb, dpg = mix_in_bwd(dx, x1, dz, vec("mix_pre_g", l), wl["w_in"])
        small["mix_pre_g"][l], small["mix_post_g"][l] = dpg, dqg
        small["sgu_ln_g"][l], small["sgu_ln_b"][l] = dlg, dlb
        small["sgu_w"][l], small["sgu_b"][l] = dsw, dsb[:, :, 0]
        small["attn_sinks"][l] = dsink[0, :N_Q_HEADS]
        mixer = [("own", wgrad(dz, hb, "wgrad_w_in", D // 2).reshape(N_DEV, WIN_SHARD, D))]
        mixer += [("own", wgrad(act, cot, "wgrad_square", D).reshape(N_DEV, SQ_SHARD, D))
                  for act, cot in ((ya, dpa), (ysg, dpb), (mg, dob))]
        if l == 0:
            small["ffn1_pre_g"][0] = small["ffn1_post_g"][0] = jnp.zeros((1, D), F32)
            mixer.append(("all", _pack_small({n: jnp.stack(v, axis=0) for n, v in small.items()})))

        (dx, da, dy, hb, dpg, dqg), got = ffn_bwd(
            dx, x0, y1, a1, vec("ffn1_pre_g", l), vec("ffn1_post_g", l), wl["ffn1_w1"], wl["ffn1_w2"],
            PeerCopies(mixer))
        parts["w_in"][l], parts["w_attn_branch"][l], parts["w_sgu_branch"][l], parts["w_out"][l] = got[:4]
        if l > 0:
            small["ffn1_pre_g"][l], small["ffn1_post_g"][l] = dpg, dqg
        if l > 0:
            waiting = ffn_wgrads(hb, da, s1, dy)
    grad_x = dx.reshape(x.shape)
    small_parts = got[4]

    def late_rows(pre, post):
        return jnp.concatenate([pre.reshape(1, D), post.reshape(1, D), jnp.zeros((LATE_ROWS - 2, D), F32)], axis=0)

    g_w1 = wgrad(da, hb, "wgrad_ffn_w1", D // 2).reshape(N_DEV, W1_SHARD, D)
    g_w2, got = wgrad(s1, dy, "wgrad_ffn_w2", D, PeerCopies([("own", g_w1)]))
    parts["ffn1_w1"][0] = got[0]
    parts["ffn1_w2"][0], late_parts = exchange_last([("own", g_w2.reshape(N_DEV, W2_SHARD, D)),
                                                     ("all", late_rows(dpg, dqg))])

    grads, deltas, new_m, new_v = {}, {}, {}, {}
    for n in MATRICES:
        grads[n], deltas[n], new_m[n], new_v[n] = adamw_sum(parts[n], w[n], mom[n], var[n], "adamw_" + n)
    for n in COL_SHARDED:
        for out in (grads, deltas, new_m, new_v):
            out[n] = jnp.swapaxes(out[n], 1, 2)
    res = adamw_sum([small_parts], _pack_small(w)[None], _pack_small(mom)[None], _pack_small(var)[None],
                    "adamw_small")
    late = adamw_sum([late_parts], *[late_rows(t["ffn1_pre_g"][0], t["ffn1_post_g"][0])[None] for t in (w, mom, var)],
                     "adamw_late")
    for out, packed, late_out in zip((grads, deltas, new_m, new_v), res, late):
        out.update(_unpack_small(packed[0]))
        for row, n in enumerate(("ffn1_pre_g", "ffn1_post_g")):
            out[n] = jnp.concatenate([late_out[0, row:row + 1], out[n][1:]], axis=0)

    return (loss, grad_x, *[grads[n] for n in WEIGHT_NAMES], *[deltas[n] for n in WEIGHT_NAMES],
            *[new_m[n] for n in WEIGHT_NAMES], *[new_v[n] for n in WEIGHT_NAMES])
```

```python
import math

import jax
import jax.numpy as jnp
from jax import lax
from jax.experimental import pallas as pl
from jax.experimental.pallas import tpu as pltpu

F32 = jnp.float32
BF16 = jnp.bfloat16

N_DEV = 8
D = 1024
FF = 2816
DEPTH = 4
HEAD_DIM = 64
N_Q_HEADS = 16
N_KV_HEADS = 2
Q_PER_KV = N_Q_HEADS // N_KV_HEADS
BLK = 128
SGU_GROUPS = 8
IN_WIDTH = 5376
W1_SHARD = 2 * FF // N_DEV
WIN_SHARD = IN_WIDTH // N_DEV
W2_SHARD = FF // N_DEV
SQ_SHARD = D // N_DEV

RMS_EPS = 1e-6
LN_EPS = 1e-5
MASK_VALUE = -1e30
ATTN_SCALE = 1.0 / math.sqrt(HEAD_DIM)

ADAM_LR = 0.001
ADAM_B1 = 0.9
ADAM_B2 = 0.999
ADAM_EPS = 1e-08
ADAM_WD = 0.01
ADAM_STEP = 10

VMEM_LIMIT_V7X = 56 * 1024 * 1024

WIN_SEGMENTS = ((0, 1024, 0), (1024, 256, 5120), (1280, 1024, 1024), (2304, 1024, 2048), (3328, 1024, 3072),
                (4352, 1024, 4096))
ZQ, ZU, ZV, ZGA, ZGB, ZKV = 0, 1024, 2048, 3072, 4096, 5120
DZ_Q, DZ_KV, DZ_MID = 0, 1024, 1280

FF_CHUNKS = ((0, 1024), (1024, 1024), (2048, 768))
WIN_CHUNKS = ((0, 1792), (1792, 1792), (3584, 1792))

SMALL_ROWS = 144
LATE_ROWS = 16
SMALL_VEC = ("ffn1_pre_g", "ffn1_post_g", "mix_pre_g", "mix_post_g", "ffn2_pre_g", "ffn2_post_g",
             "sgu_ln_g", "sgu_ln_b")

TM_FFN_FWD = 512
TM_FFN_BWD = 256
TM_MIX = 512
TM_MIX_BWD = 256
TQ_ATTN = 512
TK_WGRAD = 1024
TR_ADAM = 256


def _params(n_grid, vmem=VMEM_LIMIT_V7X):
    return pltpu.CompilerParams(dimension_semantics=("arbitrary",) * n_grid, vmem_limit_bytes=vmem)


def _dot(a, b):
    return jnp.dot(a, b, preferred_element_type=F32)


def _dot_nt(a, b):
    return lax.dot_general(a, b, (((1,), (1,)), ((), ())), preferred_element_type=F32)


def _dot_tn(a, b):
    return lax.dot_general(a, b, (((0,), (0,)), ((), ())), preferred_element_type=F32)


def _rms(x, g):
    r = lax.rsqrt(jnp.mean(x * x, axis=-1, keepdims=True) + RMS_EPS)
    n = x * r
    return n * g, n, r


def _rms_bwd(n, r, g, dy):
    dn = dy * g
    return r * (dn - n * jnp.mean(dn * n, axis=-1, keepdims=True))


def _colsum(v):
    return jnp.sum(v, axis=0, keepdims=True)


def _sigmoid(v):
    return 0.5 * jnp.tanh(0.5 * v) + 0.5


def _gelu_parts(v):
    cdf = 0.5 * lax.erf(v * (1.0 / math.sqrt(2.0))) + 0.5
    return cdf, jnp.exp2(v * v * (-0.5 / math.log(2.0))) * (1.0 / math.sqrt(2.0 * math.pi))


def _row_tile(rows, cap):
    return max(t for t in range(16, min(rows, cap) + 1, 16) if rows % t == 0)


def _row_spec(tm, width, col_block=0):
    return pl.BlockSpec((tm, width), lambda i, cb=col_block: (i, cb))


def _full_spec(shape):
    nd = len(shape)
    return pl.BlockSpec(tuple(shape), lambda *_: (0,) * nd)


def _whole_spec(arr):
    nd = arr.ndim
    return pl.BlockSpec(tuple(arr.shape), lambda *_: (0,) * nd, pipeline_mode=pl.Buffered(1))


HBM_SPEC = pl.BlockSpec(memory_space=pltpu.HBM)
MESH_ID = pl.DeviceIdType.MESH
RELATIONS = tuple((rx, ry, rc) for rx in (0, 1) for ry in (0, 1) for rc in (0, 1))[1:]


class PeerCopies:
    def __init__(self, jobs):
        self.kinds = [k for k, _ in jobs]
        self.arrays = [a for _, a in jobs]
        self.n = len(jobs)
        self.out_shape = [jax.ShapeDtypeStruct((N_DEV,) + a.shape if k == "all" else a.shape, a.dtype)
                          for k, a in jobs]
        self.scratch = [pltpu.SemaphoreType.DMA((7 * self.n,)), pltpu.SemaphoreType.DMA((7 * self.n,)),
                        pltpu.SemaphoreType.DMA((self.n,))]

    def _copies(self, ins, outs, sems, arriving):
        send_sems, recv_sems, local_sems = sems
        x, y, c = lax.axis_index("x"), lax.axis_index("y"), lax.axis_index("c")
        me = 4 * x + 2 * y + c
        src = lambda a, d: ins[a] if self.kinds[a] == "all" else ins[a].at[d]
        if not arriving:
            local = [pltpu.make_async_copy(src(a, me), outs[a].at[me], local_sems.at[a]) for a in range(self.n)]
        remote = []
        for k, (rx, ry, rc) in enumerate(RELATIONS):
            tx, ty, tc = (1 - x if rx else x), (1 - y if ry else y), (1 - c if rc else c)
            peer = 4 * tx + 2 * ty + tc
            for a in range(self.n):
                from_slot, to_slot = (me, peer) if arriving else (peer, me)
                remote.append(pltpu.make_async_remote_copy(
                    src_ref=src(a, from_slot), dst_ref=outs[a].at[to_slot],
                    send_sem=send_sems.at[a * 7 + k], recv_sem=recv_sems.at[a * 7 + k],
                    device_id=(tx, ty, tc), device_id_type=MESH_ID))
        return remote if arriving else (local, remote)

    def start(self, ins, outs, sems):
        local, sends = self._copies(ins, outs, sems, False)
        for cp in local + sends:
            cp.start()

    def wait(self, ins, outs, sems):
        for cp in self._copies(ins, outs, sems, True):
            cp.wait_recv()
        local, sends = self._copies(ins, outs, sems, False)
        for cp in sends:
            cp.wait_send()
        for cp in local:
            cp.wait()


def _call(body, grid, in_specs, args, out_specs, out_shape, name, scratch=(), copies=None):
    if copies is None:
        outs = pl.pallas_call(body, grid=grid, in_specs=list(in_specs), out_specs=list(out_specs),
                              out_shape=list(out_shape), scratch_shapes=list(scratch), name=name,
                              compiler_params=_params(len(grid)))(*args)
        return outs, []
    n_in, n_out, n_scr, nc = len(in_specs), len(out_specs), len(scratch), copies.n

    def at_step(steps):
        hit = pl.program_id(0) == steps[0]
        for axis in range(1, len(grid)):
            hit = jnp.logical_and(hit, pl.program_id(axis) == steps[axis])
        return hit

    def hosted(*refs):
        ins, refs = refs[:n_in], refs[n_in:]
        c_in, refs = refs[:nc], refs[nc:]
        outs, refs = refs[:n_out], refs[n_out:]
        c_out, refs = refs[:nc], refs[nc:]
        scr, sems = refs[:n_scr], refs[n_scr:]

        @pl.when(at_step([0] * len(grid)))
        def _():
            copies.start(c_in, c_out, sems)

        body(*ins, *outs, *scr)

        @pl.when(at_step([g - 1 for g in grid]))
        def _():
            copies.wait(c_in, c_out, sems)

    outs = pl.pallas_call(hosted, grid=grid, in_specs=list(in_specs) + [HBM_SPEC] * nc,
                          out_specs=list(out_specs) + [HBM_SPEC] * nc,
                          out_shape=list(out_shape) + copies.out_shape,
                          scratch_shapes=list(scratch) + copies.scratch, name=name,
                          compiler_params=_params(len(grid)))(*args, *copies.arrays)
    return outs[:n_out], outs[n_out:]


def cast_layer(w, l):
    _, rows, cols = w.shape
    tr = _row_tile(rows, 256)

    def body(w_ref, o_ref):
        o_ref[...] = w_ref[...].astype(BF16)

    return pl.pallas_call(
        body, grid=(rows // tr,), in_specs=[pl.BlockSpec((None, tr, cols), lambda i: (l, i, 0))],
        out_specs=_row_spec(tr, cols), out_shape=jax.ShapeDtypeStruct((rows, cols), BF16),
        name="cast_layer", compiler_params=_params(1),
    )(w)


def all_gather_weights(shards):
    n = len(shards)

    def body(*refs):
        ins, outs = refs[:n], refs[n:2 * n]
        send_sems, recv_sems, local_sems = refs[2 * n:]
        x, y, c = lax.axis_index("x"), lax.axis_index("y"), lax.axis_index("c")
        me, sibling = (x, y, c), (x, y, 1 - c)
        chips = [(1 - x, y), (x, 1 - y), (1 - x, 1 - y)]

        def slot(a, owner):
            return outs[a].at[4 * owner[0] + 2 * owner[1] + owner[2]]

        def copy(a, k, owner, to, src=None):
            return pltpu.make_async_remote_copy(
                src_ref=slot(a, owner) if src is None else src, dst_ref=slot(a, owner),
                send_sem=send_sems.at[a * 7 + k], recv_sem=recv_sems.at[a * 7 + k],
                device_id=to, device_id_type=MESH_ID)

        mine = [pltpu.make_async_copy(ins[a], slot(a, me), local_sems.at[a]) for a in range(n)]
        for cp in mine:
            cp.start()
        first = []
        for a in range(n):
            first.append(copy(a, 0, me, sibling, src=ins[a]))
            first += [copy(a, 1 + j, me, (*chip, c), src=ins[a]) for j, chip in enumerate(chips)]
        for cp in first:
            cp.start()
        passed = []
        for j, chip in enumerate(chips):
            for a in range(n):
                copy(a, 1 + j, (*chip, c), me).wait_recv()
                fwd = copy(a, 4 + j, (*chip, c), sibling)
                fwd.start()
                passed.append(fwd)
        for a in range(n):
            copy(a, 0, sibling, me).wait_recv()
            for j, chip in enumerate(chips):
                copy(a, 4 + j, (*chip, 1 - c), me).wait_recv()
        for cp in first + passed:
            cp.wait_send()
        for cp in mine:
            cp.wait()

    out_shape = [jax.ShapeDtypeStruct((N_DEV,) + s.shape, s.dtype) for s in shards]
    return pl.pallas_call(
        body, in_specs=[HBM_SPEC] * n, out_specs=[HBM_SPEC] * n, out_shape=out_shape,
        scratch_shapes=[pltpu.SemaphoreType.DMA((7 * n,)), pltpu.SemaphoreType.DMA((7 * n,)),
                        pltpu.SemaphoreType.DMA((n,))],
        name="all_gather_weights",
    )(*shards)


def exchange_last(jobs):
    copies = PeerCopies(jobs)
    n = copies.n

    def body(*refs):
        ins, outs, sems = refs[:n], refs[n:2 * n], refs[2 * n:]
        copies.start(ins, outs, sems)
        copies.wait(ins, outs, sems)

    return pl.pallas_call(
        body, in_specs=[HBM_SPEC] * n, out_specs=[HBM_SPEC] * n, out_shape=copies.out_shape,
        scratch_shapes=copies.scratch, name="exchange_last",
    )(*copies.arrays)


def ffn_fwd(x, pre_g, post_g, w1, w2, copies=None, target=None):
    T = x.shape[0]
    tm = min(TM_FFN_FWD, T)

    def body(*refs):
        if target is None:
            x_ref, pg_ref, qg_ref, w1_ref, w2_ref, xo_ref, a_ref, s_ref, y_ref = refs
        else:
            x_ref, pg_ref, qg_ref, w1_ref, w2_ref, t_ref, a_ref, s_ref, y_ref, sq_ref, dy_ref = refs
        xv = x_ref[...]
        h, _, _ = _rms(xv, pg_ref[...])
        hb = h.astype(BF16)
        acc = jnp.zeros((tm, D), F32)
        for c0, cw in FF_CHUNKS:
            g = _dot_nt(hb, w1_ref[c0:c0 + cw, :])
            u = _dot_nt(hb, w1_ref[FF + c0:FF + c0 + cw, :])
            sg = _sigmoid(g)
            si = g * sg
            a_ref[:, c0:c0 + cw] = (u * (sg + si - si * sg)).astype(BF16)
            a_ref[:, FF + c0:FF + c0 + cw] = si.astype(BF16)
            s = (si * u).astype(BF16)
            s_ref[:, c0:c0 + cw] = s
            acc = acc + _dot(s, w2_ref[c0:c0 + cw, :])
        y_ref[...] = acc.astype(BF16)
        o, _, _ = _rms(acc, qg_ref[...])
        xo = xv + 0.5 * o
        if target is None:
            xo_ref[...] = xo
        else:
            @pl.when(pl.program_id(0) == 0)
            def _():
                sq_ref[...] = jnp.zeros_like(sq_ref)

            err = xo - t_ref[...]
            sq_ref[...] += _colsum(err * err)
            dy_ref[...] = err * (1.0 / D)

    in_specs = [_row_spec(tm, D), _full_spec((1, D)), _full_spec((1, D)), _whole_spec(w1), _whole_spec(w2)]
    saved_specs = [_row_spec(tm, 2 * FF), _row_spec(tm, FF), _row_spec(tm, D)]
    saved_shapes = [jax.ShapeDtypeStruct((T, 2 * FF), BF16), jax.ShapeDtypeStruct((T, FF), BF16),
                    jax.ShapeDtypeStruct((T, D), BF16)]
    row_f32 = jax.ShapeDtypeStruct((T, D), F32)
    if target is None:
        return _call(body, (T // tm,), in_specs, (x, pre_g, post_g, w1, w2), [_row_spec(tm, D)] + saved_specs,
                     [row_f32] + saved_shapes, "ffn_fwd", copies=copies)
    return _call(body, (T // tm,), in_specs + [_row_spec(tm, D)], (x, pre_g, post_g, w1, w2, target),
                 saved_specs + [_full_spec((1, D)), _row_spec(tm, D)],
                 saved_shapes + [jax.ShapeDtypeStruct((1, D), F32), row_f32], "ffn_fwd_loss", copies=copies)


def ffn_bwd(dxo, x, y, a, pre_g, post_g, w1, w2, copies=None):
    T = x.shape[0]
    tm = min(TM_FFN_BWD, T)

    def body(dxo_ref, x_ref, y_ref, a_ref, pg_ref, qg_ref, w1_ref, w2_ref,
             dx_ref, da_ref, dy_ref, hb_ref, dpg_ref, dqg_ref):
        @pl.when(pl.program_id(0) == 0)
        def _():
            dpg_ref[...] = jnp.zeros_like(dpg_ref)
            dqg_ref[...] = jnp.zeros_like(dqg_ref)

        dxo = dxo_ref[...]
        qg = qg_ref[...]
        _, ny, ry = _rms(y_ref[...].astype(F32), qg)
        dn = 0.5 * dxo
        dqg_ref[...] += _colsum(dn * ny)
        dyb = _rms_bwd(ny, ry, qg, dn).astype(BF16)
        dy_ref[...] = dyb
        pg = pg_ref[...]
        h, nx, rx = _rms(x_ref[...], pg)
        hb_ref[...] = h.astype(BF16)
        dh = jnp.zeros((tm, D), F32)
        for c0, cw in FF_CHUNKS:
            ds = _dot_nt(dyb, w2_ref[c0:c0 + cw, :])
            dg = (ds * a_ref[:, c0:c0 + cw].astype(F32)).astype(BF16)
            du = (ds * a_ref[:, FF + c0:FF + c0 + cw].astype(F32)).astype(BF16)
            da_ref[:, c0:c0 + cw] = dg
            da_ref[:, FF + c0:FF + c0 + cw] = du
            dh = dh + _dot(dg, w1_ref[c0:c0 + cw, :]) + _dot(du, w1_ref[FF + c0:FF + c0 + cw, :])
        dpg_ref[...] += _colsum(dh * nx)
        dx_ref[...] = dxo + _rms_bwd(nx, rx, pg, dh)

    return _call(
        body, (T // tm,),
        [_row_spec(tm, D), _row_spec(tm, D), _row_spec(tm, D), _row_spec(tm, 2 * FF),
         _full_spec((1, D)), _full_spec((1, D)), _whole_spec(w1), _whole_spec(w2)],
        (dxo, x, y, a, pre_g, post_g, w1, w2),
        [_row_spec(tm, D), _row_spec(tm, 2 * FF), _row_spec(tm, D), _row_spec(tm, D),
         _full_spec((1, D)), _full_spec((1, D))],
        [jax.ShapeDtypeStruct((T, D), F32), jax.ShapeDtypeStruct((T, 2 * FF), BF16),
         jax.ShapeDtypeStruct((T, D), BF16), jax.ShapeDtypeStruct((T, D), BF16),
         jax.ShapeDtypeStruct((1, D), F32), jax.ShapeDtypeStruct((1, D), F32)],
        "ffn_bwd", copies=copies)


def mix_in_fwd(x, pre_g, win, copies=None):
    T = x.shape[0]
    tm = min(TM_FFN_FWD, T)

    def body(x_ref, pg_ref, w_ref, z_ref):
        h, _, _ = _rms(x_ref[...], pg_ref[...])
        hb = h.astype(BF16)
        for w0, n, z0 in WIN_SEGMENTS:
            z_ref[:, z0:z0 + n] = _dot_nt(hb, w_ref[w0:w0 + n, :]).astype(BF16)

    return _call(
        body, (T // tm,), [_row_spec(tm, D), _full_spec((1, D)), _whole_spec(win)], (x, pre_g, win),
        [_row_spec(tm, IN_WIDTH)], [jax.ShapeDtypeStruct((T, IN_WIDTH), BF16)], "mix_in_fwd", copies=copies)


STACK = Q_PER_KV // 2
SROWS = STACK * BLK


def _sees_own():
    qi = lax.broadcasted_iota(jnp.int32, (SROWS, BLK), 0) & (BLK - 1)
    return lax.broadcasted_iota(jnp.int32, (SROWS, BLK), 1) <= qi


def _both_blocks(picked, own):
    return jnp.concatenate([jnp.where(own, 0.0, picked), jnp.where(own, picked, 0.0)], axis=1)


def _lane_half(rows):
    return lax.broadcasted_iota(jnp.int32, (rows, BLK), 1) // HEAD_DIM


def _stack(ref, b, g):
    return jnp.concatenate([ref[b * BLK:(b + 1) * BLK, (STACK * g + j) * BLK:(STACK * g + j + 1) * BLK]
                            for j in range(STACK)], axis=0)


def _placed(pair, g, hp, fill):
    src = pair if hp == g else pltpu.roll(pair, HEAD_DIM, 1)
    return jnp.where(_lane_half(2 * BLK) == hp, src, fill).astype(BF16)


def _sink_column(sink_ref, g, hp):
    rb = lax.broadcasted_iota(jnp.int32, (SROWS, 1), 0) // BLK
    col = jnp.full((SROWS, 1), sink_ref[Q_PER_KV * g + hp], F32)
    for j in range(1, STACK):
        col = jnp.where(rb == j, sink_ref[Q_PER_KV * g + 2 * j + hp], col)
    return col


def _attn_scores(qs, kz, sink_col, own, no_previous):
    raw = _dot_nt(qs, kz)
    prev = raw[:, 0:BLK]
    if no_previous is not None:
        prev = prev + no_previous.astype(F32) * MASK_VALUE
    picked = jnp.where(own, raw[:, BLK:2 * BLK], prev)
    sink_raw = sink_col * (1.0 / ATTN_SCALE)
    m = jnp.maximum(jnp.max(picked, axis=-1, keepdims=True), sink_raw)
    factor = ATTN_SCALE / math.log(2.0)
    return jnp.exp2((picked - m) * factor), jnp.exp2((sink_raw - m) * factor)


def _kv_specs(tq, nb):
    kv_blk = ZKV // (2 * BLK)
    return [pl.BlockSpec((tq, 2 * BLK), lambda i: (i, kv_blk)),
            pl.BlockSpec((BLK, 2 * BLK), lambda i: (jnp.maximum(i * nb - 1, 0), kv_blk))]


def attn_fwd(z, sinks, copies=None):
    T = z.shape[0]
    tq = min(TQ_ATTN, T)
    nb = tq // BLK

    def body(sink_ref, q_ref, kv_ref, kvh_ref, o_ref):
        i = pl.program_id(0)
        low = _lane_half(SROWS) == 0
        own = _sees_own()
        for b in range(nb):
            kvp = kvh_ref[...] if b == 0 else kv_ref[(b - 1) * BLK:b * BLK, :]
            kv2 = jnp.concatenate([kvp, kv_ref[b * BLK:(b + 1) * BLK, :]], axis=0).astype(F32)
            no_previous = (i == 0) if b == 0 else None
            for g in range(N_KV_HEADS):
                qs = _stack(q_ref, b, g)
                r, e = [], []
                for hp in range(2):
                    p, e_sink = _attn_scores(qs, _placed(kv2[:, 0:BLK], g, hp, 0.0), _sink_column(sink_ref, g, hp),
                                             own, no_previous)
                    r.append(_dot(_both_blocks(p, own).astype(BF16), _placed(kv2[:, BLK:2 * BLK], g, hp, 1.0)))
                    e.append(e_sink)
                den = pltpu.roll(jnp.where(low, r[1], r[0]), HEAD_DIM, 1) + jnp.where(low, e[0], e[1])
                out = (jnp.where(low, r[0], r[1]) * (1.0 / den)).astype(BF16)
                for j in range(STACK):
                    o_ref[b * BLK:(b + 1) * BLK, (STACK * g + j) * BLK:(STACK * g + j + 1) * BLK] = \
                        out[j * BLK:(j + 1) * BLK, :]

    return _call(
        body, (T // tq,), [pl.BlockSpec(memory_space=pltpu.SMEM), _row_spec(tq, D)] + _kv_specs(tq, nb),
        (sinks, z, z, z), [_row_spec(tq, D)], [jax.ShapeDtypeStruct((T, D), BF16)], "attn_fwd", copies=copies)


def attn_bwd(z, ya, dya, dzb, sinks):
    T = z.shape[0]
    tq = min(TQ_ATTN, T)
    nb = tq // BLK
    nt = T // tq
    mid = ZKV - ZU

    def body(sink_ref, q_ref, kv_ref, kvh_ref, y_ref, dy_ref, dzb_ref, dz_ref, dsink_ref, acc_ref, carry_ref):
        i = pl.program_id(0)
        first_tile = i == nt - 1

        @pl.when(i == 0)
        def _():
            carry_ref[...] = jnp.zeros_like(carry_ref)
            dsink_ref[...] = jnp.zeros_like(dsink_ref)

        acc_ref[...] = jnp.zeros_like(acc_ref)
        lane = lax.broadcasted_iota(jnp.int32, (1, BLK), 1)
        dsink = jnp.zeros((1, BLK), F32)
        half = _lane_half(SROWS)
        own = _sees_own()
        for b in range(nb):
            kvp = kvh_ref[...] if b == 0 else kv_ref[(b - 1) * BLK:b * BLK, :]
            kv2 = jnp.concatenate([kvp, kv_ref[b * BLK:(b + 1) * BLK, :]], axis=0).astype(F32)
            no_previous = first_tile if b == 0 else None
            dk_groups, dv_groups = [], []
            for g in range(N_KV_HEADS):
                qs = _stack(q_ref, b, g)
                dys = _stack(dy_ref, b, g)
                dyy = dys.astype(F32) * _stack(y_ref, b, g).astype(F32)
                dq = jnp.zeros((SROWS, BLK), F32)
                ds_both, pn_both = [], []
                for hp in range(2):
                    kz = _placed(kv2[:, 0:BLK], g, hp, 0.0)
                    p, e_sink = _attn_scores(qs, kz, _sink_column(sink_ref, g, hp), own, no_previous)
                    inv = 1.0 / (jnp.sum(p, axis=-1, keepdims=True) + e_sink)
                    p = p * inv
                    delta = jnp.sum(jnp.where(half == hp, dyy, 0.0), axis=-1, keepdims=True)
                    dp = _dot_nt(dys, _placed(kv2[:, BLK:2 * BLK], g, hp, 0.0))
                    ds = p * (jnp.where(own, dp[:, BLK:2 * BLK], dp[:, 0:BLK]) - delta)
                    ds = _both_blocks(ds, own).astype(BF16)
                    pn = _both_blocks(p, own)
                    sink_term = e_sink * inv * delta
                    for j in range(STACK):
                        dsink = dsink + jnp.where(lane == Q_PER_KV * g + 2 * j + hp,
                                                  -_colsum(sink_term[j * BLK:(j + 1) * BLK, :]), 0.0)
                    dq = dq + _dot(ds, kz)
                    ds_both.append(ds)
                    pn_both.append(pn.astype(BF16))
                dk_t = _dot_tn(qs, jnp.concatenate(ds_both, axis=1))
                dv_t = _dot_tn(dys, jnp.concatenate(pn_both, axis=1))
                for t, groups in ((dk_t, dk_groups), (dv_t, dv_groups)):
                    groups.append(t[0:HEAD_DIM, 0:2 * BLK] + t[HEAD_DIM:BLK, 2 * BLK:4 * BLK])
                dqb = (dq * ATTN_SCALE).astype(BF16)
                for j in range(STACK):
                    dz_ref[b * BLK:(b + 1) * BLK, DZ_Q + (STACK * g + j) * BLK:DZ_Q + (STACK * g + j + 1) * BLK] = \
                        dqb[j * BLK:(j + 1) * BLK, :]
            acc_ref[b * BLK:(b + 2) * BLK, 0:BLK] += jnp.concatenate(dk_groups, axis=0).T * ATTN_SCALE
            acc_ref[b * BLK:(b + 2) * BLK, BLK:2 * BLK] += jnp.concatenate(dv_groups, axis=0).T
        dsink_ref[...] += dsink
        dz_ref[:, DZ_MID:IN_WIDTH] = dzb_ref[...]
        if nb > 1:
            dz_ref[0:tq - BLK, DZ_KV:DZ_MID] = acc_ref[BLK:tq, :].astype(BF16)
        dz_ref[tq - BLK:tq, DZ_KV:DZ_MID] = (acc_ref[tq:tq + BLK, :] + carry_ref[...]).astype(BF16)
        carry_ref[...] = acc_ref[0:BLK, :]

    kv_blk = ZKV // (2 * BLK)
    return pl.pallas_call(
        body, grid=(nt,),
        in_specs=[pl.BlockSpec(memory_space=pltpu.SMEM),
                  pl.BlockSpec((tq, D), lambda i: (nt - 1 - i, 0)),
                  pl.BlockSpec((tq, 2 * BLK), lambda i: (nt - 1 - i, kv_blk)),
                  pl.BlockSpec((BLK, 2 * BLK), lambda i: (jnp.maximum((nt - 1 - i) * nb - 1, 0), kv_blk)),
                  pl.BlockSpec((tq, D), lambda i: (nt - 1 - i, 0)),
                  pl.BlockSpec((tq, D), lambda i: (nt - 1 - i, 0)),
                  pl.BlockSpec((tq, mid), lambda i: (nt - 1 - i, 0))],
        out_specs=[pl.BlockSpec((tq, IN_WIDTH), lambda i: (nt - 1 - i, 0)), _full_spec((1, BLK))],
        out_shape=[jax.ShapeDtypeStruct((T, IN_WIDTH), BF16), jax.ShapeDtypeStruct((1, BLK), F32)],
        scratch_shapes=[pltpu.VMEM((tq + BLK, 2 * BLK), F32), pltpu.VMEM((BLK, 2 * BLK), F32)],
        name="attn_bwd", compiler_params=_params(1),
    )(sinks, z, z, z, ya, dya, dzb)


def _to_group_lanes(v, g, nch):
    return jnp.concatenate([v[n * BLK:(n + 1) * BLK, g * BLK:(g + 1) * BLK] for n in range(nch)], axis=1)


def _from_group_lanes(per_group, nch):
    rows = [jnp.concatenate([per_group[g][:, n * BLK:(n + 1) * BLK] for g in range(SGU_GROUPS)], axis=1)
            for n in range(nch)]
    return jnp.concatenate(rows, axis=0)


def _tril_bf16(w_ref, g):
    t = lax.broadcasted_iota(jnp.int32, (BLK, BLK), 0)
    s = lax.broadcasted_iota(jnp.int32, (BLK, BLK), 1)
    return jnp.where(t >= s, w_ref[g], 0.0).astype(BF16)


def _sgu_norm(v_s, ln_g, ln_b):
    cdf, pdf = _gelu_parts(v_s)
    gv = v_s * cdf
    xc = gv - jnp.mean(gv, axis=-1, keepdims=True)
    rstd = lax.rsqrt(jnp.mean(xc * xc, axis=-1, keepdims=True) + LN_EPS)
    nhat = xc * rstd
    return nhat * ln_g + ln_b, nhat, rstd, cdf + v_s * pdf


def _sgu_gate(vn, w_ref, bb_ref, nch):
    vnb = vn.astype(BF16)
    return _from_group_lanes(
        [_dot(_tril_bf16(w_ref, g), _to_group_lanes(vnb, g, nch)) + jnp.tile(bb_ref[g], (1, nch))
         for g in range(SGU_GROUPS)], nch)


def mix_fwd_out(x, z, ya, post_g, ln_g, ln_b, sgu_w, sgu_bb, wa, ws, wo, copies=None):
    T = x.shape[0]
    tm = min(TM_MIX, T)
    nch = tm // BLK

    def body(x_ref, us_ref, vs_ref, ga_ref, gb_ref, ya_ref, qg_ref, lg_ref, lb_ref, w_ref, bb_ref,
             wa_ref, ws_ref, wo_ref, xo_ref, ysg_ref, pa_ref, pb_ref, o_ref):
        vn, _, _, _ = _sgu_norm(vs_ref[...].astype(F32), lg_ref[...], lb_ref[...])
        gate = _sgu_gate(vn, w_ref, bb_ref, nch)
        us = us_ref[...].astype(F32)
        cdf, _ = _gelu_parts(us)
        ysg = (us * cdf * gate).astype(BF16)
        ysg_ref[...] = ysg
        pa = _dot(ya_ref[...], wa_ref[...])
        pb = _dot(ysg, ws_ref[...])
        pa_ref[...] = pa.astype(BF16)
        pb_ref[...] = pb.astype(BF16)
        merged = _sigmoid(ga_ref[...].astype(F32)) * pa + _sigmoid(gb_ref[...].astype(F32)) * pb
        o = _dot(merged.astype(BF16), wo_ref[...])
        o_ref[...] = o.astype(BF16)
        on, _, _ = _rms(o, qg_ref[...])
        xo_ref[...] = x_ref[...] + on

    zspec = lambda start: _row_spec(tm, D, start // D)
    act = jax.ShapeDtypeStruct((T, D), BF16)
    return _call(
        body, (T // tm,),
        [_row_spec(tm, D), zspec(ZU), zspec(ZV), zspec(ZGA), zspec(ZGB), _row_spec(tm, D),
         _full_spec((1, D)), _full_spec((1, D)), _full_spec((1, D)),
         _full_spec((SGU_GROUPS, BLK, BLK)), _full_spec((SGU_GROUPS, BLK, BLK)),
         _whole_spec(wa), _whole_spec(ws), _whole_spec(wo)],
        (x, z, z, z, z, ya, post_g, ln_g, ln_b, sgu_w, sgu_bb, wa, ws, wo),
        [_row_spec(tm, D)] * 5, [jax.ShapeDtypeStruct((T, D), F32), act, act, act, act],
        "mix_fwd_out", copies=copies)


def mix_bwd_out(dxo, z, o, pa, pb, post_g, ln_g, ln_b, sgu_w, sgu_bb, wa, ws, wo, copies=None):
    T = dxo.shape[0]
    tm = min(TM_MIX_BWD, T)
    nch = tm // BLK
    mid = ZKV - ZU

    def body(dxo_ref, us_ref, vs_ref, ga_ref, gb_ref, o_ref, pa_ref, pb_ref, qg_ref, lg_ref, lb_ref, w_ref, bb_ref,
             wa_ref, ws_ref, wo_ref,
             dzb_ref, dya_ref, mg_ref, do_ref, dpa_ref, dpb_ref, dqg_ref, dlg_ref, dlb_ref, dw_ref, dbb_ref):
        @pl.when(pl.program_id(0) == 0)
        def _():
            for r in (dqg_ref, dlg_ref, dlb_ref, dw_ref, dbb_ref):
                r[...] = jnp.zeros_like(r)

        qg = qg_ref[...]
        dxo = dxo_ref[...]
        _, no, ro = _rms(o_ref[...].astype(F32), qg)
        dqg_ref[...] += _colsum(dxo * no)
        dob = _rms_bwd(no, ro, qg, dxo).astype(BF16)
        do_ref[...] = dob
        dmerged = _dot_nt(dob, wo_ref[...])
        sa = _sigmoid(ga_ref[...].astype(F32))
        sb = _sigmoid(gb_ref[...].astype(F32))
        pa = pa_ref[...].astype(F32)
        pb = pb_ref[...].astype(F32)
        ta = sa * pa
        tb = sb * pb
        mg_ref[...] = (ta + tb).astype(BF16)
        dpa = (dmerged * sa).astype(BF16)
        dpb = (dmerged * sb).astype(BF16)
        dpa_ref[...] = dpa
        dpb_ref[...] = dpb
        dzb_ref[:, ZGA - ZU:ZGA - ZU + D] = (dmerged * (ta - ta * sa)).astype(BF16)
        dzb_ref[:, ZGB - ZU:ZGB - ZU + D] = (dmerged * (tb - tb * sb)).astype(BF16)
        dya_ref[...] = _dot_nt(dpa, wa_ref[...]).astype(BF16)
        dysg = _dot_nt(dpb, ws_ref[...])

        lg = lg_ref[...]
        vn, nhat, rstd, dgelu_v = _sgu_norm(vs_ref[...].astype(F32), lg, lb_ref[...])
        gate = _sgu_gate(vn, w_ref, bb_ref, nch)
        us = us_ref[...].astype(F32)
        cdf, pdf = _gelu_parts(us)
        dzb_ref[:, 0:D] = (dysg * gate * (cdf + us * pdf)).astype(BF16)
        dgate = (dysg * (us * cdf)).astype(BF16)
        vnb = vn.astype(BF16)
        t = lax.broadcasted_iota(jnp.int32, (BLK, BLK), 0)
        s = lax.broadcasted_iota(jnp.int32, (BLK, BLK), 1)
        dvn_groups = []
        for g in range(SGU_GROUPS):
            dgl = _to_group_lanes(dgate, g, nch)
            dbb_ref[g] += jnp.broadcast_to(jnp.sum(dgl.astype(F32), axis=-1, keepdims=True), (BLK, BLK))
            dw_ref[g] += jnp.where(t >= s, _dot_nt(dgl, _to_group_lanes(vnb, g, nch)), 0.0)
            dvn_groups.append(_dot_tn(_tril_bf16(w_ref, g), dgl))
        dvn = _from_group_lanes(dvn_groups, nch)
        dlg_ref[...] += _colsum(dvn * nhat)
        dlb_ref[...] += _colsum(dvn)
        dnh = dvn * lg
        dgv = rstd * (dnh - jnp.mean(dnh, axis=-1, keepdims=True) - nhat * jnp.mean(dnh * nhat, axis=-1, keepdims=True))
        dzb_ref[:, ZV - ZU:ZV - ZU + D] = (dgv * dgelu_v).astype(BF16)

    zspec = lambda start: _row_spec(tm, D, start // D)
    act = jax.ShapeDtypeStruct((T, D), BF16)
    grp = jax.ShapeDtypeStruct((SGU_GROUPS, BLK, BLK), F32)
    vec = jax.ShapeDtypeStruct((1, D), F32)
    return _call(
        body, (T // tm,),
        [_row_spec(tm, D), zspec(ZU), zspec(ZV), zspec(ZGA), zspec(ZGB),
         _row_spec(tm, D), _row_spec(tm, D), _row_spec(tm, D),
         _full_spec((1, D)), _full_spec((1, D)), _full_spec((1, D)),
         _full_spec((SGU_GROUPS, BLK, BLK)), _full_spec((SGU_GROUPS, BLK, BLK)),
         _whole_spec(wa), _whole_spec(ws), _whole_spec(wo)],
        (dxo, z, z, z, z, o, pa, pb, post_g, ln_g, ln_b, sgu_w, sgu_bb, wa, ws, wo),
        [_row_spec(tm, mid)] + [_row_spec(tm, D)] * 5 + [_full_spec((1, D))] * 3
        + [_full_spec((SGU_GROUPS, BLK, BLK))] * 2,
        [jax.ShapeDtypeStruct((T, mid), BF16), act, act, act, act, act, vec, vec, vec, grp, grp],
        "mix_bwd_out", copies=copies)


def mix_in_bwd(dxo, x, dz, pre_g, win):
    T = x.shape[0]
    tm = min(TM_MIX, T)

    def body(dxo_ref, x_ref, dz_ref, pg_ref, w_ref, dx_ref, hb_ref, dpg_ref):
        @pl.when(pl.program_id(0) == 0)
        def _():
            dpg_ref[...] = jnp.zeros_like(dpg_ref)

        pg = pg_ref[...]
        h, nx, rx = _rms(x_ref[...], pg)
        hb_ref[...] = h.astype(BF16)
        dh = jnp.zeros((tm, D), F32)
        for c0, cw in WIN_CHUNKS:
            dh = dh + _dot(dz_ref[:, c0:c0 + cw], w_ref[c0:c0 + cw, :])
        dpg_ref[...] += _colsum(dh * nx)
        dx_ref[...] = dxo_ref[...] + _rms_bwd(nx, rx, pg, dh)

    return pl.pallas_call(
        body, grid=(T // tm,),
        in_specs=[_row_spec(tm, D), _row_spec(tm, D), _row_spec(tm, IN_WIDTH), _full_spec((1, D)), _whole_spec(win)],
        out_specs=[_row_spec(tm, D), _row_spec(tm, D), _full_spec((1, D))],
        out_shape=[jax.ShapeDtypeStruct((T, D), F32), jax.ShapeDtypeStruct((T, D), BF16),
                   jax.ShapeDtypeStruct((1, D), F32)],
        name="mix_in_bwd", compiler_params=_params(1),
    )(dxo, x, dz, pre_g, win)


def wgrad(wide, narrow, name, tmo, copies=None):
    T, N = wide.shape
    M = narrow.shape[1]
    tk = min(TK_WGRAD, T)
    nk = T // tk
    chunk = max(c for c in range(BLK, 1792 + 1, BLK) if N % c == 0)

    def body(a_ref, b_ref, o_ref, acc_ref):
        k = pl.program_id(1)

        @pl.when(k == 0)
        def _():
            acc_ref[...] = jnp.zeros_like(acc_ref)

        acc_ref[...] += _dot_tn(a_ref[...], b_ref[...])

        @pl.when(k == nk - 1)
        def _():
            for c0 in range(0, N, chunk):
                o_ref[c0:c0 + chunk, :] = acc_ref[:, c0:c0 + chunk].T.astype(BF16)

    (out,), got = _call(
        body, (M // tmo, nk),
        [pl.BlockSpec((tk, tmo), lambda m, k: (k, m)), pl.BlockSpec((tk, N), lambda m, k: (k, 0))], (narrow, wide),
        [pl.BlockSpec((N, tmo), lambda m, k: (0, m))], [jax.ShapeDtypeStruct((N, M), BF16)], name,
        scratch=[pltpu.VMEM((tmo, N), F32)], copies=copies)
    return out if copies is None else (out, got)


def adamw_sum(parts, w, m, v, name):
    layers, rows, cols = w.shape
    tr = _row_tile(rows, TR_ADAM)
    nr = rows // tr
    c1 = 1.0 - ADAM_B1 ** ADAM_STEP
    c2 = 1.0 - ADAM_B2 ** ADAM_STEP

    def body(*refs):
        p_refs = refs[:layers]
        w_ref, m_ref, v_ref, g_ref, d_ref, nm_ref, nv_ref = refs[layers:]
        for k in range(layers):
            @pl.when(pl.program_id(0) == k)
            def _(p_ref=p_refs[k]):
                g = p_ref[0].astype(F32)
                for j in range(1, N_DEV):
                    g = g + p_ref[j].astype(F32)
                nm = ADAM_B1 * m_ref[...] + (1.0 - ADAM_B1) * g
                nv = ADAM_B2 * v_ref[...] + (1.0 - ADAM_B2) * (g * g)
                g_ref[...] = g
                nm_ref[...] = nm
                nv_ref[...] = nv
                d_ref[...] = -ADAM_LR * ((nm * (1.0 / c1)) / (jnp.sqrt(nv * (1.0 / c2)) + ADAM_EPS)
                                         + ADAM_WD * w_ref[...])

    def part_spec(k):
        return pl.BlockSpec((N_DEV, tr, cols),
                            lambda l, i: (0, jnp.where(l < k, 0, jnp.where(l == k, i, nr - 1)), 0))

    spec = pl.BlockSpec((None, tr, cols), lambda l, i: (l, i, 0))
    out = jax.ShapeDtypeStruct((layers, rows, cols), F32)
    return pl.pallas_call(
        body, grid=(layers, nr),
        in_specs=[part_spec(k) for k in range(layers)] + [spec, spec, spec],
        out_specs=[spec] * 4, out_shape=[out] * 4, name=name, compiler_params=_params(2),
    )(*parts, w, m, v)


def _pack_small(p):
    layers = []
    for l in range(DEPTH):
        rows = [p[n][l].reshape(1, D) for n in SMALL_VEC]
        rows.append(p["sgu_b"][l].reshape(1, D))
        rows.append(jnp.pad(p["attn_sinks"][l].reshape(1, N_Q_HEADS), ((0, 0), (0, D - N_Q_HEADS))))
        rows.append(p["sgu_w"][l].reshape(BLK, D))
        used = len(SMALL_VEC) + 2 + BLK
        rows.append(jnp.zeros((SMALL_ROWS - used, D), F32))
        layers.append(jnp.concatenate(rows, axis=0))
    return jnp.concatenate(layers, axis=0)


def _unpack_small(packed):
    a = packed.reshape(DEPTH, SMALL_ROWS, D)
    out = {n: a[:, i, :] for i, n in enumerate(SMALL_VEC)}
    k = len(SMALL_VEC)
    out["sgu_b"] = a[:, k, :].reshape(DEPTH, SGU_GROUPS, BLK)
    out["attn_sinks"] = a[:, k + 1, :N_Q_HEADS]
    out["sgu_w"] = a[:, k + 2:k + 2 + BLK, :].reshape(DEPTH, SGU_GROUPS, BLK, BLK)
    return out


WEIGHT_NAMES = ("ffn1_pre_g", "ffn1_w1", "ffn1_w2", "ffn1_post_g", "mix_pre_g", "w_in", "attn_sinks", "sgu_ln_g",
                "sgu_ln_b", "sgu_w", "sgu_b", "w_attn_branch", "w_sgu_branch", "w_out", "mix_post_g", "ffn2_pre_g",
                "ffn2_w1", "ffn2_w2", "ffn2_post_g")
COL_SHARDED = ("ffn1_w1", "ffn2_w1", "w_in")
ROW_SHARDED = ("ffn1_w2", "ffn2_w2", "w_attn_branch", "w_sgu_branch", "w_out")
MATRICES = COL_SHARDED + ROW_SHARDED


def kernel(x, ffn1_pre_g, ffn1_w1, ffn1_w2, ffn1_post_g, mix_pre_g, w_in, attn_sinks, sgu_ln_g, sgu_ln_b, sgu_w, sgu_b, w_attn_branch, w_sgu_branch, w_out, mix_post_g, ffn2_pre_g, ffn2_w1, ffn2_w2, ffn2_post_g, loss_target, m_ffn1_pre_g, m_ffn1_w1, m_ffn1_w2, m_ffn1_post_g, m_mix_pre_g, m_w_in, m_attn_sinks, m_sgu_ln_g, m_sgu_ln_b, m_sgu_w, m_sgu_b, m_w_attn_branch, m_w_sgu_branch, m_w_out, m_mix_post_g, m_ffn2_pre_g, m_ffn2_w1, m_ffn2_w2, m_ffn2_post_g, v_ffn1_pre_g, v_ffn1_w1, v_ffn1_w2, v_ffn1_post_g, v_mix_pre_g, v_w_in, v_attn_sinks, v_sgu_ln_g, v_sgu_ln_b, v_sgu_w, v_sgu_b, v_w_attn_branch, v_w_sgu_branch, v_w_out, v_mix_post_g, v_ffn2_pre_g, v_ffn2_w1, v_ffn2_w2, v_ffn2_post_g):
    w = dict(zip(WEIGHT_NAMES, (ffn1_pre_g, ffn1_w1, ffn1_w2, ffn1_post_g, mix_pre_g, w_in, attn_sinks, sgu_ln_g,
                                sgu_ln_b, sgu_w, sgu_b, w_attn_branch, w_sgu_branch, w_out, mix_post_g, ffn2_pre_g,
                                ffn2_w1, ffn2_w2, ffn2_post_g)))
    mom = dict(zip(WEIGHT_NAMES, (m_ffn1_pre_g, m_ffn1_w1, m_ffn1_w2, m_ffn1_post_g, m_mix_pre_g, m_w_in,
                                  m_attn_sinks, m_sgu_ln_g, m_sgu_ln_b, m_sgu_w, m_sgu_b, m_w_attn_branch,
                                  m_w_sgu_branch, m_w_out, m_mix_post_g, m_ffn2_pre_g, m_ffn2_w1, m_ffn2_w2,
                                  m_ffn2_post_g)))
    var = dict(zip(WEIGHT_NAMES, (v_ffn1_pre_g, v_ffn1_w1, v_ffn1_w2, v_ffn1_post_g, v_mix_pre_g, v_w_in,
                                  v_attn_sinks, v_sgu_ln_g, v_sgu_ln_b, v_sgu_w, v_sgu_b, v_w_attn_branch,
                                  v_w_sgu_branch, v_w_out, v_mix_post_g, v_ffn2_pre_g, v_ffn2_w1, v_ffn2_w2,
                                  v_ffn2_post_g)))
    T = x.shape[1]
    xs = x.reshape(T, D)
    target = loss_target.reshape(T, D)

    for n in COL_SHARDED:
        w[n], mom[n], var[n] = (jnp.swapaxes(t[n], 1, 2) for t in (w, mom, var))

    shard = {n: [cast_layer(w[n], l) for l in range(DEPTH)] for n in MATRICES}
    weights = [{} for _ in range(DEPTH)]
    ffn1, ffn2, squares = ("ffn1_w1", "ffn1_w2"), ("ffn2_w1", "ffn2_w2"), ("w_attn_branch", "w_sgu_branch", "w_out")
    for n, g in zip(ffn1, all_gather_weights([shard[n][0] for n in ffn1])):
        weights[0][n] = g.reshape(-1, D)

    def fetch(groups):
        jobs = [("all", shard[n][l]) for l, names in groups if l < DEPTH for n in names]
        return PeerCopies(jobs) if jobs else None

    def landed(groups, arrivals):
        slots = [(l, n) for l, names in groups if l < DEPTH for n in names]
        for (l, n), g in zip(slots, arrivals):
            weights[l][n] = g.reshape(-1, D)

    vec = lambda n, l: w[n][l].reshape(1, D)
    sgu_bb = [jnp.broadcast_to(w["sgu_b"][l][:, :, None], (SGU_GROUPS, BLK, BLK)) for l in range(DEPTH)]

    saved = []
    h = xs
    for l in range(DEPTH):
        wl = weights[l]
        ahead = lambda names: [(0, names)] if l == 0 else []
        x0 = h
        groups = [(l + 1, ffn1)] + ahead(("w_in",))
        (x1, a1, s1, y1), got = ffn_fwd(x0, vec("ffn1_pre_g", l), vec("ffn1_post_g", l), wl["ffn1_w1"], wl["ffn1_w2"],
                                        fetch(groups))
        landed(groups, got)
        groups = [(l + 1, ("w_in",))] + ahead(squares)
        (z,), got = mix_in_fwd(x1, vec("mix_pre_g", l), wl["w_in"], fetch(groups))
        landed(groups, got)
        groups = ahead(("ffn2_w1",))
        (ya,), got = attn_fwd(z, w["attn_sinks"][l], fetch(groups))
        landed(groups, got)
        groups = [(l + 1, squares)] + ahead(("ffn2_w2",))
        (x2, ysg, pa, pb, o), got = mix_fwd_out(
            x1, z, ya, vec("mix_post_g", l), vec("sgu_ln_g", l), vec("sgu_ln_b", l), w["sgu_w"][l], sgu_bb[l],
            wl["w_attn_branch"], wl["w_sgu_branch"], wl["w_out"], fetch(groups))
        landed(groups, got)
        groups = [(l + 1, ffn2)]
        last = (vec("ffn2_pre_g", l), vec("ffn2_post_g", l), wl["ffn2_w1"], wl["ffn2_w2"])
        if l < DEPTH - 1:
            (h, a2, s2, y2), got = ffn_fwd(x2, *last, fetch(groups))
            landed(groups, got)
        else:
            (a2, s2, y2, sq, dx), _ = ffn_fwd(x2, *last, target=target)
        saved.append((x0, a1, s1, y1, x1, z, ya, ysg, pa, pb, o, x2, a2, s2, y2))

    loss = lax.psum(0.5 / D * jnp.sum(sq), ("x", "y", "c"))

    def ffn_wgrads(hb, da, s, dy):
        return [("own", wgrad(da, hb, "wgrad_ffn_w1", D // 2).reshape(N_DEV, W1_SHARD, D)),
                ("own", wgrad(s, dy, "wgrad_ffn_w2", D).reshape(N_DEV, W2_SHARD, D))]

    parts = {n: [None] * DEPTH for n in MATRICES}
    small = {n: [None] * DEPTH for n in WEIGHT_NAMES if n not in MATRICES}
    waiting = None
    for l in reversed(range(DEPTH)):
        wl = weights[l]
        x0, a1, s1, y1, x1, z, ya, ysg, pa, pb, o, x2, a2, s2, y2 = saved[l]
        (dx, da, dy, hb, dpg, dqg), got = ffn_bwd(
            dx, x2, y2, a2, vec("ffn2_pre_g", l), vec("ffn2_post_g", l), wl["ffn2_w1"], wl["ffn2_w2"],
            PeerCopies(waiting) if waiting else None)
        if waiting:
            parts["ffn1_w1"][l + 1], parts["ffn1_w2"][l + 1] = got
        small["ffn2_pre_g"][l], small["ffn2_post_g"][l] = dpg, dqg

        (dzb, dya, mg, dob, dpa, dpb, dqg, dlg, dlb, dsw, dsb), got = mix_bwd_out(
            dx, z, o, pa, pb, vec("mix_post_g", l), vec("sgu_ln_g", l), vec("sgu_ln_b", l), w["sgu_w"][l], sgu_bb[l],
            wl["w_attn_branch"], wl["w_sgu_branch"], wl["w_out"], PeerCopies(ffn_wgrads(hb, da, s2, dy)))
        parts["ffn2_w1"][l], parts["ffn2_w2"][l] = got
        dz, dsink = attn_bwd(z, ya, dya, dzb, w["attn_sinks"][l])
        dx, hb, dpg = mix_in_bwd(dx, x1, dz, vec("mix_pre_g", l), wl["w_in"])
        small["mix_pre_g"][l], small["mix_post_g"][l] = dpg, dqg
        small["sgu_ln_g"][l], small["sgu_ln_b"][l] = dlg, dlb
        small["sgu_w"][l], small["sgu_b"][l] = dsw, dsb[:, :, 0]
        small["attn_sinks"][l] = dsink[0, :N_Q_HEADS]
        mixer = [("own", wgrad(dz, hb, "wgrad_w_in", D // 2).reshape(N_DEV, WIN_SHARD, D))]
        mixer += [("own", wgrad(act, cot, "wgrad_square", D).reshape(N_DEV, SQ_SHARD, D))
                  for act, cot in ((ya, dpa), (ysg, dpb), (mg, dob))]
        if l == 0:
            small["ffn1_pre_g"][0] = small["ffn1_post_g"][0] = jnp.zeros((1, D), F32)
            mixer.append(("all", _pack_small({n: jnp.stack(v, axis=0) for n, v in small.items()})))

        (dx, da, dy, hb, dpg, dqg), got = ffn_bwd(
            dx, x0, y1, a1, vec("ffn1_pre_g", l), vec("ffn1_post_g", l), wl["ffn1_w1"], wl["ffn1_w2"],
            PeerCopies(mixer))
        parts["w_in"][l], parts["w_attn_branch"][l], parts["w_sgu_branch"][l], parts["w_out"][l] = got[:4]
        if l > 0:
            small["ffn1_pre_g"][l], small["ffn1_post_g"][l] = dpg, dqg
        if l > 0:
            waiting = ffn_wgrads(hb, da, s1, dy)
    grad_x = dx.reshape(x.shape)
    small_parts = got[4]

    def late_rows(pre, post):
        return jnp.concatenate([pre.reshape(1, D), post.reshape(1, D), jnp.zeros((LATE_ROWS - 2, D), F32)], axis=0)

    g_w1 = wgrad(da, hb, "wgrad_ffn_w1", D // 2).reshape(N_DEV, W1_SHARD, D)
    g_w2, got = wgrad(s1, dy, "wgrad_ffn_w2", D, PeerCopies([("own", g_w1)]))
    parts["ffn1_w1"][0] = got[0]
    parts["ffn1_w2"][0], late_parts = exchange_last([("own", g_w2.reshape(N_DEV, W2_SHARD, D)),
                                                     ("all", late_rows(dpg, dqg))])

    grads, deltas, new_m, new_v = {}, {}, {}, {}
    for n in MATRICES:
        grads[n], deltas[n], new_m[n], new_v[n] = adamw_sum(parts[n], w[n], mom[n], var[n], "adamw_" + n)
    for n in COL_SHARDED:
        for out in (grads, deltas, new_m, new_v):
            out[n] = jnp.swapaxes(out[n], 1, 2)
    res = adamw_sum([small_parts], _pack_small(w)[None], _pack_small(mom)[None], _pack_small(var)[None],
                    "adamw_small")
    late = adamw_sum([late_parts], *[late_rows(t["ffn1_pre_g"][0], t["ffn1_post_g"][0])[None] for t in (w, mom, var)],
                     "adamw_late")
    for out, packed, late_out in zip((grads, deltas, new_m, new_v), res, late):
        out.update(_unpack_small(packed[0]))
        for row, n in enumerate(("ffn1_pre_g", "ffn1_post_g")):
            out[n] = jnp.concatenate([late_out[0, row:row + 1], out[n][1:]], axis=0)

    return (loss, grad_x, *[grads[n] for n in WEIGHT_NAMES], *[deltas[n] for n in WEIGHT_NAMES],
            *[new_m[n] for n in WEIGHT_NAMES], *[new_v[n] for n in WEIGHT_NAMES])
```

```python
import math

import jax
import jax.numpy as jnp
from jax import lax
from jax.experimental import pallas as pl
from jax.experimental.pallas import tpu as pltpu

F32 = jnp.float32
BF16 = jnp.bfloat16

N_DEV = 8
D = 1024
FF = 2816
DEPTH = 4
HEAD_DIM = 64
N_Q_HEADS = 16
N_KV_HEADS = 2
Q_PER_KV = N_Q_HEADS // N_KV_HEADS
BLK = 128
SGU_GROUPS = 8
IN_WIDTH = 5376
W1_SHARD = 2 * FF // N_DEV
WIN_SHARD = IN_WIDTH // N_DEV
W2_SHARD = FF // N_DEV
SQ_SHARD = D // N_DEV

RMS_EPS = 1e-6
LN_EPS = 1e-5
MASK_VALUE = -1e30
ATTN_SCALE = 1.0 / math.sqrt(HEAD_DIM)

ADAM_LR = 0.001
ADAM_B1 = 0.9
ADAM_B2 = 0.999
ADAM_EPS = 1e-08
ADAM_WD = 0.01
ADAM_STEP = 10

VMEM_LIMIT_V7X = 56 * 1024 * 1024

WIN_SEGMENTS = ((0, 1024, 0), (1024, 256, 5120), (1280, 1024, 1024), (2304, 1024, 2048), (3328, 1024, 3072),
                (4352, 1024, 4096))
ZQ, ZU, ZV, ZGA, ZGB, ZKV = 0, 1024, 2048, 3072, 4096, 5120
DZ_Q, DZ_KV, DZ_MID = 0, 1024, 1280

FF_CHUNKS = ((0, 1024), (1024, 1024), (2048, 768))
WIN_CHUNKS = ((0, 1792), (1792, 1792), (3584, 1792))

SMALL_ROWS = 16
LATE_ROWS = 16
SMALL_VEC = ("ffn1_pre_g", "ffn1_post_g", "mix_pre_g", "mix_post_g", "ffn2_pre_g", "ffn2_post_g",
             "sgu_ln_g", "sgu_ln_b")

TM_FFN_FWD = 512
TM_FFN_BWD = 256
TM_MIX = 512
TM_MIX_BWD = 256
TQ_ATTN = 512
TK_WGRAD = 1024
TR_ADAM = 256


def _params(n_grid, vmem=VMEM_LIMIT_V7X):
    return pltpu.CompilerParams(dimension_semantics=("arbitrary",) * n_grid, vmem_limit_bytes=vmem)


def _dot(a, b):
    return jnp.dot(a, b, preferred_element_type=F32)


def _dot_nt(a, b):
    return lax.dot_general(a, b, (((1,), (1,)), ((), ())), preferred_element_type=F32)


def _dot_tn(a, b):
    return lax.dot_general(a, b, (((0,), (0,)), ((), ())), preferred_element_type=F32)


def _rms(x, g):
    r = lax.rsqrt(jnp.mean(x * x, axis=-1, keepdims=True) + RMS_EPS)
    n = x * r
    return n * g, n, r


def _rms_bwd(n, r, g, dy):
    dn = dy * g
    return r * (dn - n * jnp.mean(dn * n, axis=-1, keepdims=True))


def _colsum(v):
    return jnp.sum(v, axis=0, keepdims=True)


def _sigmoid(v):
    return 0.5 * jnp.tanh(0.5 * v) + 0.5


def _gelu_parts(v):
    cdf = 0.5 * lax.erf(v * (1.0 / math.sqrt(2.0))) + 0.5
    return cdf, jnp.exp2(v * v * (-0.5 / math.log(2.0))) * (1.0 / math.sqrt(2.0 * math.pi))


def _row_tile(rows, cap):
    return max(t for t in range(16, min(rows, cap) + 1, 16) if rows % t == 0)


def _row_spec(tm, width, col_block=0):
    return pl.BlockSpec((tm, width), lambda i, cb=col_block: (i, cb))


def _full_spec(shape):
    nd = len(shape)
    return pl.BlockSpec(tuple(shape), lambda *_: (0,) * nd)


def _whole_spec(arr):
    nd = arr.ndim
    return pl.BlockSpec(tuple(arr.shape), lambda *_: (0,) * nd, pipeline_mode=pl.Buffered(1))


HBM_SPEC = pl.BlockSpec(memory_space=pltpu.HBM)
MESH_ID = pl.DeviceIdType.MESH
RELATIONS = tuple((rx, ry, rc) for rx in (0, 1) for ry in (0, 1) for rc in (0, 1))[1:]


class PeerCopies:
    def __init__(self, jobs):
        self.kinds = [k for k, _ in jobs]
        self.arrays = [a for _, a in jobs]
        self.n = len(jobs)
        self.out_shape = [jax.ShapeDtypeStruct((N_DEV,) + a.shape if k == "all" else a.shape, a.dtype)
                          for k, a in jobs]
        self.scratch = [pltpu.SemaphoreType.DMA((7 * self.n,)), pltpu.SemaphoreType.DMA((7 * self.n,)),
                        pltpu.SemaphoreType.DMA((self.n,))]

    def _copies(self, ins, outs, sems, arriving):
        send_sems, recv_sems, local_sems = sems
        x, y, c = lax.axis_index("x"), lax.axis_index("y"), lax.axis_index("c")
        me = 4 * x + 2 * y + c
        src = lambda a, d: ins[a] if self.kinds[a] == "all" else ins[a].at[d]
        if not arriving:
            local = [pltpu.make_async_copy(src(a, me), outs[a].at[me], local_sems.at[a]) for a in range(self.n)]
        remote = []
        for k, (rx, ry, rc) in enumerate(RELATIONS):
            tx, ty, tc = (1 - x if rx else x), (1 - y if ry else y), (1 - c if rc else c)
            peer = 4 * tx + 2 * ty + tc
            for a in range(self.n):
                from_slot, to_slot = (me, peer) if arriving else (peer, me)
                remote.append(pltpu.make_async_remote_copy(
                    src_ref=src(a, from_slot), dst_ref=outs[a].at[to_slot],
                    send_sem=send_sems.at[a * 7 + k], recv_sem=recv_sems.at[a * 7 + k],
                    device_id=(tx, ty, tc), device_id_type=MESH_ID))
        return remote if arriving else (local, remote)

    def start(self, ins, outs, sems):
        local, sends = self._copies(ins, outs, sems, False)
        for cp in local + sends:
            cp.start()

    def wait(self, ins, outs, sems):
        for cp in self._copies(ins, outs, sems, True):
            cp.wait_recv()
        local, sends = self._copies(ins, outs, sems, False)
        for cp in sends:
            cp.wait_send()
        for cp in local:
            cp.wait()


def _call(body, grid, in_specs, args, out_specs, out_shape, name, scratch=(), copies=None):
    if copies is None:
        outs = pl.pallas_call(body, grid=grid, in_specs=list(in_specs), out_specs=list(out_specs),
                              out_shape=list(out_shape), scratch_shapes=list(scratch), name=name,
                              compiler_params=_params(len(grid)))(*args)
        return outs, []
    n_in, n_out, n_scr, nc = len(in_specs), len(out_specs), len(scratch), copies.n

    def at_step(steps):
        hit = pl.program_id(0) == steps[0]
        for axis in range(1, len(grid)):
            hit = jnp.logical_and(hit, pl.program_id(axis) == steps[axis])
        return hit

    def hosted(*refs):
        ins, refs = refs[:n_in], refs[n_in:]
        c_in, refs = refs[:nc], refs[nc:]
        outs, refs = refs[:n_out], refs[n_out:]
        c_out, refs = refs[:nc], refs[nc:]
        scr, sems = refs[:n_scr], refs[n_scr:]

        @pl.when(at_step([0] * len(grid)))
        def _():
            copies.start(c_in, c_out, sems)

        body(*ins, *outs, *scr)

        @pl.when(at_step([g - 1 for g in grid]))
        def _():
            copies.wait(c_in, c_out, sems)

    outs = pl.pallas_call(hosted, grid=grid, in_specs=list(in_specs) + [HBM_SPEC] * nc,
                          out_specs=list(out_specs) + [HBM_SPEC] * nc,
                          out_shape=list(out_shape) + copies.out_shape,
                          scratch_shapes=list(scratch) + copies.scratch, name=name,
                          compiler_params=_params(len(grid)))(*args, *copies.arrays)
    return outs[:n_out], outs[n_out:]


def cast_layer(w, l):
    _, rows, cols = w.shape
    tr = _row_tile(rows, 256)

    def body(w_ref, o_ref):
        o_ref[...] = w_ref[...].astype(BF16)

    return pl.pallas_call(
        body, grid=(rows // tr,), in_specs=[pl.BlockSpec((None, tr, cols), lambda i: (l, i, 0))],
        out_specs=_row_spec(tr, cols), out_shape=jax.ShapeDtypeStruct((rows, cols), BF16),
        name="cast_layer", compiler_params=_params(1),
    )(w)


def all_gather_weights(shards):
    n = len(shards)

    def body(*refs):
        ins, outs = refs[:n], refs[n:2 * n]
        send_sems, recv_sems, local_sems = refs[2 * n:]
        x, y, c = lax.axis_index("x"), lax.axis_index("y"), lax.axis_index("c")
        me, sibling = (x, y, c), (x, y, 1 - c)
        chips = [(1 - x, y), (x, 1 - y), (1 - x, 1 - y)]

        def slot(a, owner):
            return outs[a].at[4 * owner[0] + 2 * owner[1] + owner[2]]

        def copy(a, k, owner, to, src=None):
            return pltpu.make_async_remote_copy(
                src_ref=slot(a, owner) if src is None else src, dst_ref=slot(a, owner),
                send_sem=send_sems.at[a * 7 + k], recv_sem=recv_sems.at[a * 7 + k],
                device_id=to, device_id_type=MESH_ID)

        mine = [pltpu.make_async_copy(ins[a], slot(a, me), local_sems.at[a]) for a in range(n)]
        for cp in mine:
            cp.start()
        first = []
        for a in range(n):
            first.append(copy(a, 0, me, sibling, src=ins[a]))
            first += [copy(a, 1 + j, me, (*chip, c), src=ins[a]) for j, chip in enumerate(chips)]
        for cp in first:
            cp.start()
        passed = []
        for j, chip in enumerate(chips):
            for a in range(n):
                copy(a, 1 + j, (*chip, c), me).wait_recv()
                fwd = copy(a, 4 + j, (*chip, c), sibling)
                fwd.start()
                passed.append(fwd)
        for a in range(n):
            copy(a, 0, sibling, me).wait_recv()
            for j, chip in enumerate(chips):
                copy(a, 4 + j, (*chip, 1 - c), me).wait_recv()
        for cp in first + passed:
            cp.wait_send()
        for cp in mine:
            cp.wait()

    out_shape = [jax.ShapeDtypeStruct((N_DEV,) + s.shape, s.dtype) for s in shards]
    return pl.pallas_call(
        body, in_specs=[HBM_SPEC] * n, out_specs=[HBM_SPEC] * n, out_shape=out_shape,
        scratch_shapes=[pltpu.SemaphoreType.DMA((7 * n,)), pltpu.SemaphoreType.DMA((7 * n,)),
                        pltpu.SemaphoreType.DMA((n,))],
        name="all_gather_weights",
    )(*shards)


def exchange_last(jobs):
    copies = PeerCopies(jobs)
    n = copies.n

    def body(*refs):
        ins, outs, sems = refs[:n], refs[n:2 * n], refs[2 * n:]
        copies.start(ins, outs, sems)
        copies.wait(ins, outs, sems)

    return pl.pallas_call(
        body, in_specs=[HBM_SPEC] * n, out_specs=[HBM_SPEC] * n, out_shape=copies.out_shape,
        scratch_shapes=copies.scratch, name="exchange_last",
    )(*copies.arrays)


def ffn_fwd(x, pre_g, post_g, w1, w2, copies=None, target=None):
    T = x.shape[0]
    tm = min(TM_FFN_FWD, T)

    def body(*refs):
        if target is None:
            x_ref, pg_ref, qg_ref, w1_ref, w2_ref, xo_ref, a_ref, s_ref, y_ref = refs
        else:
            x_ref, pg_ref, qg_ref, w1_ref, w2_ref, t_ref, a_ref, s_ref, y_ref, sq_ref, dy_ref = refs
        xv = x_ref[...]
        h, _, _ = _rms(xv, pg_ref[...])
        hb = h.astype(BF16)
        acc = jnp.zeros((tm, D), F32)
        for c0, cw in FF_CHUNKS:
            g = _dot_nt(hb, w1_ref[c0:c0 + cw, :])
            u = _dot_nt(hb, w1_ref[FF + c0:FF + c0 + cw, :])
            sg = _sigmoid(g)
            si = g * sg
            a_ref[:, c0:c0 + cw] = (u * (sg + si - si * sg)).astype(BF16)
            a_ref[:, FF + c0:FF + c0 + cw] = si.astype(BF16)
            s = (si * u).astype(BF16)
            s_ref[:, c0:c0 + cw] = s
            acc = acc + _dot(s, w2_ref[c0:c0 + cw, :])
        y_ref[...] = acc.astype(BF16)
        o, _, _ = _rms(acc, qg_ref[...])
        xo = xv + 0.5 * o
        if target is None:
            xo_ref[...] = xo
        else:
            @pl.when(pl.program_id(0) == 0)
            def _():
                sq_ref[...] = jnp.zeros_like(sq_ref)

            err = xo - t_ref[...]
            sq_ref[...] += _colsum(err * err)
            dy_ref[...] = err * (1.0 / D)

    in_specs = [_row_spec(tm, D), _full_spec((1, D)), _full_spec((1, D)), _whole_spec(w1), _whole_spec(w2)]
    saved_specs = [_row_spec(tm, 2 * FF), _row_spec(tm, FF), _row_spec(tm, D)]
    saved_shapes = [jax.ShapeDtypeStruct((T, 2 * FF), BF16), jax.ShapeDtypeStruct((T, FF), BF16),
                    jax.ShapeDtypeStruct((T, D), BF16)]
    row_f32 = jax.ShapeDtypeStruct((T, D), F32)
    if target is None:
        return _call(body, (T // tm,), in_specs, (x, pre_g, post_g, w1, w2), [_row_spec(tm, D)] + saved_specs,
                     [row_f32] + saved_shapes, "ffn_fwd", copies=copies)
    return _call(body, (T // tm,), in_specs + [_row_spec(tm, D)], (x, pre_g, post_g, w1, w2, target),
                 saved_specs + [_full_spec((1, D)), _row_spec(tm, D)],
                 saved_shapes + [jax.ShapeDtypeStruct((1, D), F32), row_f32], "ffn_fwd_loss", copies=copies)


def ffn_bwd(dxo, x, y, a, pre_g, post_g, w1, w2, copies=None):
    T = x.shape[0]
    tm = min(TM_FFN_BWD, T)

    def body(dxo_ref, x_ref, y_ref, a_ref, pg_ref, qg_ref, w1_ref, w2_ref,
             dx_ref, da_ref, dy_ref, hb_ref, dpg_ref, dqg_ref):
        @pl.when(pl.program_id(0) == 0)
        def _():
            dpg_ref[...] = jnp.zeros_like(dpg_ref)
            dqg_ref[...] = jnp.zeros_like(dqg_ref)

        dxo = dxo_ref[...]
        qg = qg_ref[...]
        _, ny, ry = _rms(y_ref[...].astype(F32), qg)
        dn = 0.5 * dxo
        dqg_ref[...] += _colsum(dn * ny)
        dyb = _rms_bwd(ny, ry, qg, dn).astype(BF16)
        dy_ref[...] = dyb
        pg = pg_ref[...]
        h, nx, rx = _rms(x_ref[...], pg)
        hb_ref[...] = h.astype(BF16)
        dh = jnp.zeros((tm, D), F32)
        for c0, cw in FF_CHUNKS:
            ds = _dot_nt(dyb, w2_ref[c0:c0 + cw, :])
            dg = (ds * a_ref[:, c0:c0 + cw].astype(F32)).astype(BF16)
            du = (ds * a_ref[:, FF + c0:FF + c0 + cw].astype(F32)).astype(BF16)
            da_ref[:, c0:c0 + cw] = dg
            da_ref[:, FF + c0:FF + c0 + cw] = du
            dh = dh + _dot(dg, w1_ref[c0:c0 + cw, :]) + _dot(du, w1_ref[FF + c0:FF + c0 + cw, :])
        dpg_ref[...] += _colsum(dh * nx)
        dx_ref[...] = dxo + _rms_bwd(nx, rx, pg, dh)

    return _call(
        body, (T // tm,),
        [_row_spec(tm, D), _row_spec(tm, D), _row_spec(tm, D), _row_spec(tm, 2 * FF),
         _full_spec((1, D)), _full_spec((1, D)), _whole_spec(w1), _whole_spec(w2)],
        (dxo, x, y, a, pre_g, post_g, w1, w2),
        [_row_spec(tm, D), _row_spec(tm, 2 * FF), _row_spec(tm, D), _row_spec(tm, D),
         _full_spec((1, D)), _full_spec((1, D))],
        [jax.ShapeDtypeStruct((T, D), F32), jax.ShapeDtypeStruct((T, 2 * FF), BF16),
         jax.ShapeDtypeStruct((T, D), BF16), jax.ShapeDtypeStruct((T, D), BF16),
         jax.ShapeDtypeStruct((1, D), F32), jax.ShapeDtypeStruct((1, D), F32)],
        "ffn_bwd", copies=copies)


def mix_in_fwd(x, pre_g, win, copies=None):
    T = x.shape[0]
    tm = min(TM_FFN_FWD, T)

    def body(x_ref, pg_ref, w_ref, z_ref):
        h, _, _ = _rms(x_ref[...], pg_ref[...])
        hb = h.astype(BF16)
        for w0, n, z0 in WIN_SEGMENTS:
            z_ref[:, z0:z0 + n] = _dot_nt(hb, w_ref[w0:w0 + n, :]).astype(BF16)

    return _call(
        body, (T // tm,), [_row_spec(tm, D), _full_spec((1, D)), _whole_spec(win)], (x, pre_g, win),
        [_row_spec(tm, IN_WIDTH)], [jax.ShapeDtypeStruct((T, IN_WIDTH), BF16)], "mix_in_fwd", copies=copies)


STACK = Q_PER_KV // 2
SROWS = STACK * BLK


def _sees_own():
    qi = lax.broadcasted_iota(jnp.int32, (SROWS, BLK), 0) & (BLK - 1)
    return lax.broadcasted_iota(jnp.int32, (SROWS, BLK), 1) <= qi


def _both_blocks(picked, own):
    return jnp.concatenate([jnp.where(own, 0.0, picked), jnp.where(own, picked, 0.0)], axis=1)


def _lane_half(rows):
    return lax.broadcasted_iota(jnp.int32, (rows, BLK), 1) // HEAD_DIM


def _stack(ref, b, g):
    return jnp.concatenate([ref[b * BLK:(b + 1) * BLK, (STACK * g + j) * BLK:(STACK * g + j + 1) * BLK]
                            for j in range(STACK)], axis=0)


def _placed(pair, g, hp, fill):
    src = pair if hp == g else pltpu.roll(pair, HEAD_DIM, 1)
    return jnp.where(_lane_half(2 * BLK) == hp, src, fill).astype(BF16)


def _sink_column(sink_ref, g, hp):
    rb = lax.broadcasted_iota(jnp.int32, (SROWS, 1), 0) // BLK
    col = jnp.full((SROWS, 1), sink_ref[Q_PER_KV * g + hp], F32)
    for j in range(1, STACK):
        col = jnp.where(rb == j, sink_ref[Q_PER_KV * g + 2 * j + hp], col)
    return col


def _attn_scores(qs, kz, sink_col, own, no_previous):
    raw = _dot_nt(qs, kz)
    prev = raw[:, 0:BLK]
    if no_previous is not None:
        prev = prev + no_previous.astype(F32) * MASK_VALUE
    picked = jnp.where(own, raw[:, BLK:2 * BLK], prev)
    sink_raw = sink_col * (1.0 / ATTN_SCALE)
    m = jnp.maximum(jnp.max(picked, axis=-1, keepdims=True), sink_raw)
    factor = ATTN_SCALE / math.log(2.0)
    return jnp.exp2((picked - m) * factor), jnp.exp2((sink_raw - m) * factor)


def _kv_specs(tq, nb):
    kv_blk = ZKV // (2 * BLK)
    return [pl.BlockSpec((tq, 2 * BLK), lambda i: (i, kv_blk)),
            pl.BlockSpec((BLK, 2 * BLK), lambda i: (jnp.maximum(i * nb - 1, 0), kv_blk))]


def attn_fwd(z, sinks, copies=None):
    T = z.shape[0]
    tq = min(TQ_ATTN, T)
    nb = tq // BLK

    def body(sink_ref, q_ref, kv_ref, kvh_ref, o_ref):
        i = pl.program_id(0)
        low = _lane_half(SROWS) == 0
        own = _sees_own()
        for b in range(nb):
            kvp = kvh_ref[...] if b == 0 else kv_ref[(b - 1) * BLK:b * BLK, :]
            kv2 = jnp.concatenate([kvp, kv_ref[b * BLK:(b + 1) * BLK, :]], axis=0).astype(F32)
            no_previous = (i == 0) if b == 0 else None
            for g in range(N_KV_HEADS):
                qs = _stack(q_ref, b, g)
                r, e = [], []
                for hp in range(2):
                    p, e_sink = _attn_scores(qs, _placed(kv2[:, 0:BLK], g, hp, 0.0), _sink_column(sink_ref, g, hp),
                                             own, no_previous)
                    r.append(_dot(_both_blocks(p, own).astype(BF16), _placed(kv2[:, BLK:2 * BLK], g, hp, 1.0)))
                    e.append(e_sink)
                den = pltpu.roll(jnp.where(low, r[1], r[0]), HEAD_DIM, 1) + jnp.where(low, e[0], e[1])
                out = (jnp.where(low, r[0], r[1]) * (1.0 / den)).astype(BF16)
                for j in range(STACK):
                    o_ref[b * BLK:(b + 1) * BLK, (STACK * g + j) * BLK:(STACK * g + j + 1) * BLK] = \
                        out[j * BLK:(j + 1) * BLK, :]

    return _call(
        body, (T // tq,), [pl.BlockSpec(memory_space=pltpu.SMEM), _row_spec(tq, D)] + _kv_specs(tq, nb),
        (sinks, z, z, z), [_row_spec(tq, D)], [jax.ShapeDtypeStruct((T, D), BF16)], "attn_fwd", copies=copies)


def attn_bwd(z, ya, dya, dzb, sinks):
    T = z.shape[0]
    tq = min(TQ_ATTN, T)
    nb = tq // BLK
    nt = T // tq
    mid = ZKV - ZU

    def body(sink_ref, q_ref, kv_ref, kvh_ref, y_ref, dy_ref, dzb_ref, dz_ref, dsink_ref, acc_ref, carry_ref):
        i = pl.program_id(0)
        first_tile = i == nt - 1

        @pl.when(i == 0)
        def _():
            carry_ref[...] = jnp.zeros_like(carry_ref)
            dsink_ref[...] = jnp.zeros_like(dsink_ref)

        acc_ref[...] = jnp.zeros_like(acc_ref)
        lane = lax.broadcasted_iota(jnp.int32, (1, BLK), 1)
        dsink = jnp.zeros((1, BLK), F32)
        half = _lane_half(SROWS)
        own = _sees_own()
        for b in range(nb):
            kvp = kvh_ref[...] if b == 0 else kv_ref[(b - 1) * BLK:b * BLK, :]
            kv2 = jnp.concatenate([kvp, kv_ref[b * BLK:(b + 1) * BLK, :]], axis=0).astype(F32)
            no_previous = first_tile if b == 0 else None
            dk_groups, dv_groups = [], []
            for g in range(N_KV_HEADS):
                qs = _stack(q_ref, b, g)
                dys = _stack(dy_ref, b, g)
                dyy = dys.astype(F32) * _stack(y_ref, b, g).astype(F32)
                dq = jnp.zeros((SROWS, BLK), F32)
                ds_both, pn_both = [], []
                for hp in range(2):
                    kz = _placed(kv2[:, 0:BLK], g, hp, 0.0)
                    p, e_sink = _attn_scores(qs, kz, _sink_column(sink_ref, g, hp), own, no_previous)
                    inv = 1.0 / (jnp.sum(p, axis=-1, keepdims=True) + e_sink)
                    p = p * inv
                    delta = jnp.sum(jnp.where(half == hp, dyy, 0.0), axis=-1, keepdims=True)
                    dp = _dot_nt(dys, _placed(kv2[:, BLK:2 * BLK], g, hp, 0.0))
                    ds = p * (jnp.where(own, dp[:, BLK:2 * BLK], dp[:, 0:BLK]) - delta)
                    ds = _both_blocks(ds, own).astype(BF16)
                    pn = _both_blocks(p, own)
                    sink_term = e_sink * inv * delta
                    for j in range(STACK):
                        dsink = dsink + jnp.where(lane == Q_PER_KV * g + 2 * j + hp,
                                                  -_colsum(sink_term[j * BLK:(j + 1) * BLK, :]), 0.0)
                    dq = dq + _dot(ds, kz)
                    ds_both.append(ds)
                    pn_both.append(pn.astype(BF16))
                dk_t = _dot_tn(qs, jnp.concatenate(ds_both, axis=1))
                dv_t = _dot_tn(dys, jnp.concatenate(pn_both, axis=1))
                for t, groups in ((dk_t, dk_groups), (dv_t, dv_groups)):
                    groups.append(t[0:HEAD_DIM, 0:2 * BLK] + t[HEAD_DIM:BLK, 2 * BLK:4 * BLK])
                dqb = (dq * ATTN_SCALE).astype(BF16)
                for j in range(STACK):
                    dz_ref[b * BLK:(b + 1) * BLK, DZ_Q + (STACK * g + j) * BLK:DZ_Q + (STACK * g + j + 1) * BLK] = \
                        dqb[j * BLK:(j + 1) * BLK, :]
            acc_ref[b * BLK:(b + 2) * BLK, 0:BLK] += jnp.concatenate(dk_groups, axis=0).T * ATTN_SCALE
            acc_ref[b * BLK:(b + 2) * BLK, BLK:2 * BLK] += jnp.concatenate(dv_groups, axis=0).T
        dsink_ref[...] += dsink
        dz_ref[:, DZ_MID:IN_WIDTH] = dzb_ref[...]
        if nb > 1:
            dz_ref[0:tq - BLK, DZ_KV:DZ_MID] = acc_ref[BLK:tq, :].astype(BF16)
        dz_ref[tq - BLK:tq, DZ_KV:DZ_MID] = (acc_ref[tq:tq + BLK, :] + carry_ref[...]).astype(BF16)
        carry_ref[...] = acc_ref[0:BLK, :]

    kv_blk = ZKV // (2 * BLK)
    return pl.pallas_call(
        body, grid=(nt,),
        in_specs=[pl.BlockSpec(memory_space=pltpu.SMEM),
                  pl.BlockSpec((tq, D), lambda i: (nt - 1 - i, 0)),
                  pl.BlockSpec((tq, 2 * BLK), lambda i: (nt - 1 - i, kv_blk)),
                  pl.BlockSpec((BLK, 2 * BLK), lambda i: (jnp.maximum((nt - 1 - i) * nb - 1, 0), kv_blk)),
                  pl.BlockSpec((tq, D), lambda i: (nt - 1 - i, 0)),
                  pl.BlockSpec((tq, D), lambda i: (nt - 1 - i, 0)),
                  pl.BlockSpec((tq, mid), lambda i: (nt - 1 - i, 0))],
        out_specs=[pl.BlockSpec((tq, IN_WIDTH), lambda i: (nt - 1 - i, 0)), _full_spec((1, BLK))],
        out_shape=[jax.ShapeDtypeStruct((T, IN_WIDTH), BF16), jax.ShapeDtypeStruct((1, BLK), F32)],
        scratch_shapes=[pltpu.VMEM((tq + BLK, 2 * BLK), F32), pltpu.VMEM((BLK, 2 * BLK), F32)],
        name="attn_bwd", compiler_params=_params(1),
    )(sinks, z, z, z, ya, dya, dzb)


def _to_group_lanes(v, g, nch):
    return jnp.concatenate([v[n * BLK:(n + 1) * BLK, g * BLK:(g + 1) * BLK] for n in range(nch)], axis=1)


def _from_group_lanes(per_group, nch):
    rows = [jnp.concatenate([per_group[g][:, n * BLK:(n + 1) * BLK] for g in range(SGU_GROUPS)], axis=1)
            for n in range(nch)]
    return jnp.concatenate(rows, axis=0)


def _tril_bf16(w_ref, g):
    t = lax.broadcasted_iota(jnp.int32, (BLK, BLK), 0)
    s = lax.broadcasted_iota(jnp.int32, (BLK, BLK), 1)
    return jnp.where(t >= s, w_ref[g], 0.0).astype(BF16)


def _sgu_norm(v_s, ln_g, ln_b):
    cdf, pdf = _gelu_parts(v_s)
    gv = v_s * cdf
    xc = gv - jnp.mean(gv, axis=-1, keepdims=True)
    rstd = lax.rsqrt(jnp.mean(xc * xc, axis=-1, keepdims=True) + LN_EPS)
    nhat = xc * rstd
    return nhat * ln_g + ln_b, nhat, rstd, cdf + v_s * pdf


def _sgu_gate(vn, w_ref, bb_ref, nch):
    vnb = vn.astype(BF16)
    return _from_group_lanes(
        [_dot(_tril_bf16(w_ref, g), _to_group_lanes(vnb, g, nch)) + jnp.tile(bb_ref[g], (1, nch))
         for g in range(SGU_GROUPS)], nch)


def mix_fwd_out(x, z, ya, post_g, ln_g, ln_b, sgu_w, sgu_bb, wa, ws, wo, copies=None):
    T = x.shape[0]
    tm = min(TM_MIX, T)
    nch = tm // BLK

    def body(x_ref, us_ref, vs_ref, ga_ref, gb_ref, ya_ref, qg_ref, lg_ref, lb_ref, w_ref, bb_ref,
             wa_ref, ws_ref, wo_ref, xo_ref, ysg_ref, pa_ref, pb_ref, o_ref):
        vn, _, _, _ = _sgu_norm(vs_ref[...].astype(F32), lg_ref[...], lb_ref[...])
        gate = _sgu_gate(vn, w_ref, bb_ref, nch)
        us = us_ref[...].astype(F32)
        cdf, _ = _gelu_parts(us)
        ysg = (us * cdf * gate).astype(BF16)
        ysg_ref[...] = ysg
        pa = _dot(ya_ref[...], wa_ref[...])
        pb = _dot(ysg, ws_ref[...])
        pa_ref[...] = pa.astype(BF16)
        pb_ref[...] = pb.astype(BF16)
        merged = _sigmoid(ga_ref[...].astype(F32)) * pa + _sigmoid(gb_ref[...].astype(F32)) * pb
        o = _dot(merged.astype(BF16), wo_ref[...])
        o_ref[...] = o.astype(BF16)
        on, _, _ = _rms(o, qg_ref[...])
        xo_ref[...] = x_ref[...] + on

    zspec = lambda start: _row_spec(tm, D, start // D)
    act = jax.ShapeDtypeStruct((T, D), BF16)
    return _call(
        body, (T // tm,),
        [_row_spec(tm, D), zspec(ZU), zspec(ZV), zspec(ZGA), zspec(ZGB), _row_spec(tm, D),
         _full_spec((1, D)), _full_spec((1, D)), _full_spec((1, D)),
         _full_spec((SGU_GROUPS, BLK, BLK)), _full_spec((SGU_GROUPS, BLK, BLK)),
         _whole_spec(wa), _whole_spec(ws), _whole_spec(wo)],
        (x, z, z, z, z, ya, post_g, ln_g, ln_b, sgu_w, sgu_bb, wa, ws, wo),
        [_row_spec(tm, D)] * 5, [jax.ShapeDtypeStruct((T, D), F32), act, act, act, act],
        "mix_fwd_out", copies=copies)


def mix_bwd_out(dxo, z, o, pa, pb, post_g, ln_g, ln_b, sgu_w, sgu_bb, wa, ws, wo, copies=None):
    T = dxo.shape[0]
    tm = min(TM_MIX_BWD, T)
    nch = tm // BLK
    mid = ZKV - ZU

    def body(dxo_ref, us_ref, vs_ref, ga_ref, gb_ref, o_ref, pa_ref, pb_ref, qg_ref, lg_ref, lb_ref, w_ref, bb_ref,
             wa_ref, ws_ref, wo_ref,
             dzb_ref, dya_ref, mg_ref, do_ref, dpa_ref, dpb_ref, dqg_ref, dlg_ref, dlb_ref, dw_ref, dbb_ref):
        @pl.when(pl.program_id(0) == 0)
        def _():
            for r in (dqg_ref, dlg_ref, dlb_ref, dw_ref, dbb_ref):
                r[...] = jnp.zeros_like(r)

        qg = qg_ref[...]
        dxo = dxo_ref[...]
        _, no, ro = _rms(o_ref[...].astype(F32), qg)
        dqg_ref[...] += _colsum(dxo * no)
        dob = _rms_bwd(no, ro, qg, dxo).astype(BF16)
        do_ref[...] = dob
        dmerged = _dot_nt(dob, wo_ref[...])
        sa = _sigmoid(ga_ref[...].astype(F32))
        sb = _sigmoid(gb_ref[...].astype(F32))
        pa = pa_ref[...].astype(F32)
        pb = pb_ref[...].astype(F32)
        ta = sa * pa
        tb = sb * pb
        mg_ref[...] = (ta + tb).astype(BF16)
        dpa = (dmerged * sa).astype(BF16)
        dpb = (dmerged * sb).astype(BF16)
        dpa_ref[...] = dpa
        dpb_ref[...] = dpb
        dzb_ref[:, ZGA - ZU:ZGA - ZU + D] = (dmerged * (ta - ta * sa)).astype(BF16)
        dzb_ref[:, ZGB - ZU:ZGB - ZU + D] = (dmerged * (tb - tb * sb)).astype(BF16)
        dya_ref[...] = _dot_nt(dpa, wa_ref[...]).astype(BF16)
        dysg = _dot_nt(dpb, ws_ref[...])

        lg = lg_ref[...]
        vn, nhat, rstd, dgelu_v = _sgu_norm(vs_ref[...].astype(F32), lg, lb_ref[...])
        gate = _sgu_gate(vn, w_ref, bb_ref, nch)
        us = us_ref[...].astype(F32)
        cdf, pdf = _gelu_parts(us)
        dzb_ref[:, 0:D] = (dysg * gate * (cdf + us * pdf)).astype(BF16)
        dgate = (dysg * (us * cdf)).astype(BF16)
        vnb = vn.astype(BF16)
        t = lax.broadcasted_iota(jnp.int32, (BLK, BLK), 0)
        s = lax.broadcasted_iota(jnp.int32, (BLK, BLK), 1)
        dvn_groups = []
        for g in range(SGU_GROUPS):
            dgl = _to_group_lanes(dgate, g, nch)
            dbb_ref[g] += jnp.broadcast_to(jnp.sum(dgl.astype(F32), axis=-1, keepdims=True), (BLK, BLK))
            dw_ref[g] += jnp.where(t >= s, _dot_nt(dgl, _to_group_lanes(vnb, g, nch)), 0.0)
            dvn_groups.append(_dot_tn(_tril_bf16(w_ref, g), dgl))
        dvn = _from_group_lanes(dvn_groups, nch)
        dlg_ref[...] += _colsum(dvn * nhat)
        dlb_ref[...] += _colsum(dvn)
        dnh = dvn * lg
        dgv = rstd * (dnh - jnp.mean(dnh, axis=-1, keepdims=True) - nhat * jnp.mean(dnh * nhat, axis=-1, keepdims=True))
        dzb_ref[:, ZV - ZU:ZV - ZU + D] = (dgv * dgelu_v).astype(BF16)

    zspec = lambda start: _row_spec(tm, D, start // D)
    act = jax.ShapeDtypeStruct((T, D), BF16)
    grp = jax.ShapeDtypeStruct((SGU_GROUPS, BLK, BLK), F32)
    vec = jax.ShapeDtypeStruct((1, D), F32)
    return _call(
        body, (T // tm,),
        [_row_spec(tm, D), zspec(ZU), zspec(ZV), zspec(ZGA), zspec(ZGB),
         _row_spec(tm, D), _row_spec(tm, D), _row_spec(tm, D),
         _full_spec((1, D)), _full_spec((1, D)), _full_spec((1, D)),
         _full_spec((SGU_GROUPS, BLK, BLK)), _full_spec((SGU_GROUPS, BLK, BLK)),
         _whole_spec(wa), _whole_spec(ws), _whole_spec(wo)],
        (dxo, z, z, z, z, o, pa, pb, post_g, ln_g, ln_b, sgu_w, sgu_bb, wa, ws, wo),
        [_row_spec(tm, mid)] + [_row_spec(tm, D)] * 5 + [_full_spec((1, D))] * 3
        + [_full_spec((SGU_GROUPS, BLK, BLK))] * 2,
        [jax.ShapeDtypeStruct((T, mid), BF16), act, act, act, act, act, vec, vec, vec, grp, grp],
        "mix_bwd_out", copies=copies)


def mix_in_bwd(dxo, x, dz, pre_g, win):
    T = x.shape[0]
    tm = min(TM_MIX, T)

    def body(dxo_ref, x_ref, dz_ref, pg_ref, w_ref, dx_ref, hb_ref, dpg_ref):
        @pl.when(pl.program_id(0) == 0)
        def _():
            dpg_ref[...] = jnp.zeros_like(dpg_ref)

        pg = pg_ref[...]
        h, nx, rx = _rms(x_ref[...], pg)
        hb_ref[...] = h.astype(BF16)
        dh = jnp.zeros((tm, D), F32)
        for c0, cw in WIN_CHUNKS:
            dh = dh + _dot(dz_ref[:, c0:c0 + cw], w_ref[c0:c0 + cw, :])
        dpg_ref[...] += _colsum(dh * nx)
        dx_ref[...] = dxo_ref[...] + _rms_bwd(nx, rx, pg, dh)

    return pl.pallas_call(
        body, grid=(T // tm,),
        in_specs=[_row_spec(tm, D), _row_spec(tm, D), _row_spec(tm, IN_WIDTH), _full_spec((1, D)), _whole_spec(win)],
        out_specs=[_row_spec(tm, D), _row_spec(tm, D), _full_spec((1, D))],
        out_shape=[jax.ShapeDtypeStruct((T, D), F32), jax.ShapeDtypeStruct((T, D), BF16),
                   jax.ShapeDtypeStruct((1, D), F32)],
        name="mix_in_bwd", compiler_params=_params(1),
    )(dxo, x, dz, pre_g, win)


def wgrad(wide, narrow, name, tmo, copies=None):
    T, N = wide.shape
    M = narrow.shape[1]
    tk = min(TK_WGRAD, T)
    nk = T // tk
    chunk = max(c for c in range(BLK, 1792 + 1, BLK) if N % c == 0)

    def body(a_ref, b_ref, o_ref, acc_ref):
        k = pl.program_id(1)

        @pl.when(k == 0)
        def _():
            acc_ref[...] = jnp.zeros_like(acc_ref)

        acc_ref[...] += _dot_tn(a_ref[...], b_ref[...])

        @pl.when(k == nk - 1)
        def _():
            for c0 in range(0, N, chunk):
                o_ref[c0:c0 + chunk, :] = acc_ref[:, c0:c0 + chunk].T.astype(BF16)

    (out,), got = _call(
        body, (M // tmo, nk),
        [pl.BlockSpec((tk, tmo), lambda m, k: (k, m)), pl.BlockSpec((tk, N), lambda m, k: (k, 0))], (narrow, wide),
        [pl.BlockSpec((N, tmo), lambda m, k: (0, m))], [jax.ShapeDtypeStruct((N, M), BF16)], name,
        scratch=[pltpu.VMEM((tmo, N), F32)], copies=copies)
    return out if copies is None else (out, got)


def adamw_sum(parts, w, m, v, name):
    layers, rows, cols = w.shape
    tr = _row_tile(rows, TR_ADAM)
    nr = rows // tr
    c1 = 1.0 - ADAM_B1 ** ADAM_STEP
    c2 = 1.0 - ADAM_B2 ** ADAM_STEP

    def body(*refs):
        p_refs = refs[:layers]
        w_ref, m_ref, v_ref, g_ref, d_ref, nm_ref, nv_ref = refs[layers:]
        for k in range(layers):
            @pl.when(pl.program_id(0) == k)
            def _(p_ref=p_refs[k]):
                g = p_ref[0].astype(F32)
                for j in range(1, N_DEV):
                    g = g + p_ref[j].astype(F32)
                nm = ADAM_B1 * m_ref[...] + (1.0 - ADAM_B1) * g
                nv = ADAM_B2 * v_ref[...] + (1.0 - ADAM_B2) * (g * g)
                g_ref[...] = g
                nm_ref[...] = nm
                nv_ref[...] = nv
                d_ref[...] = -ADAM_LR * ((nm * (1.0 / c1)) / (jnp.sqrt(nv * (1.0 / c2)) + ADAM_EPS)
                                         + ADAM_WD * w_ref[...])

    def part_spec(k):
        return pl.BlockSpec((N_DEV, tr, cols),
                            lambda l, i: (0, jnp.where(l < k, 0, jnp.where(l == k, i, nr - 1)), 0))

    spec = pl.BlockSpec((None, tr, cols), lambda l, i: (l, i, 0))
    out = jax.ShapeDtypeStruct((layers, rows, cols), F32)
    return pl.pallas_call(
        body, grid=(layers, nr),
        in_specs=[part_spec(k) for k in range(layers)] + [spec, spec, spec],
        out_specs=[spec] * 4, out_shape=[out] * 4, name=name, compiler_params=_params(2),
    )(*parts, w, m, v)


def _pack_small(p):
    layers = []
    for l in range(DEPTH):
        rows = [p[n][l].reshape(1, D) for n in SMALL_VEC]
        rows.append(p["sgu_b"][l].reshape(1, D))
        rows.append(jnp.pad(p["attn_sinks"][l].reshape(1, N_Q_HEADS), ((0, 0), (0, D - N_Q_HEADS))))
        rows.append(jnp.zeros((SMALL_ROWS - len(SMALL_VEC) - 2, D), F32))
        layers.append(jnp.concatenate(rows, axis=0))
    return jnp.concatenate(layers, axis=0)


def _unpack_small(packed):
    a = packed.reshape(DEPTH, SMALL_ROWS, D)
    out = {n: a[:, i, :] for i, n in enumerate(SMALL_VEC)}
    k = len(SMALL_VEC)
    out["sgu_b"] = a[:, k, :].reshape(DEPTH, SGU_GROUPS, BLK)
    out["attn_sinks"] = a[:, k + 1, :N_Q_HEADS]
    return out


WEIGHT_NAMES = ("ffn1_pre_g", "ffn1_w1", "ffn1_w2", "ffn1_post_g", "mix_pre_g", "w_in", "attn_sinks", "sgu_ln_g",
                "sgu_ln_b", "sgu_w", "sgu_b", "w_attn_branch", "w_sgu_branch", "w_out", "mix_post_g", "ffn2_pre_g",
                "ffn2_w1", "ffn2_w2", "ffn2_post_g")
COL_SHARDED = ("ffn1_w1", "ffn2_w1", "w_in")
ROW_SHARDED = ("ffn1_w2", "ffn2_w2", "w_attn_branch", "w_sgu_branch", "w_out")
MATRICES = COL_SHARDED + ROW_SHARDED


def kernel(x, ffn1_pre_g, ffn1_w1, ffn1_w2, ffn1_post_g, mix_pre_g, w_in, attn_sinks, sgu_ln_g, sgu_ln_b, sgu_w, sgu_b, w_attn_branch, w_sgu_branch, w_out, mix_post_g, ffn2_pre_g, ffn2_w1, ffn2_w2, ffn2_post_g, loss_target, m_ffn1_pre_g, m_ffn1_w1, m_ffn1_w2, m_ffn1_post_g, m_mix_pre_g, m_w_in, m_attn_sinks, m_sgu_ln_g, m_sgu_ln_b, m_sgu_w, m_sgu_b, m_w_attn_branch, m_w_sgu_branch, m_w_out, m_mix_post_g, m_ffn2_pre_g, m_ffn2_w1, m_ffn2_w2, m_ffn2_post_g, v_ffn1_pre_g, v_ffn1_w1, v_ffn1_w2, v_ffn1_post_g, v_mix_pre_g, v_w_in, v_attn_sinks, v_sgu_ln_g, v_sgu_ln_b, v_sgu_w, v_sgu_b, v_w_attn_branch, v_w_sgu_branch, v_w_out, v_mix_post_g, v_ffn2_pre_g, v_ffn2_w1, v_ffn2_w2, v_ffn2_post_g):
    w = dict(zip(WEIGHT_NAMES, (ffn1_pre_g, ffn1_w1, ffn1_w2, ffn1_post_g, mix_pre_g, w_in, attn_sinks, sgu_ln_g,
                                sgu_ln_b, sgu_w, sgu_b, w_attn_branch, w_sgu_branch, w_out, mix_post_g, ffn2_pre_g,
                                ffn2_w1, ffn2_w2, ffn2_post_g)))
    mom = dict(zip(WEIGHT_NAMES, (m_ffn1_pre_g, m_ffn1_w1, m_ffn1_w2, m_ffn1_post_g, m_mix_pre_g, m_w_in,
                                  m_attn_sinks, m_sgu_ln_g, m_sgu_ln_b, m_sgu_w, m_sgu_b, m_w_attn_branch,
                                  m_w_sgu_branch, m_w_out, m_mix_post_g, m_ffn2_pre_g, m_ffn2_w1, m_ffn2_w2,
                                  m_ffn2_post_g)))
    var = dict(zip(WEIGHT_NAMES, (v_ffn1_pre_g, v_ffn1_w1, v_ffn1_w2, v_ffn1_post_g, v_mix_pre_g, v_w_in,
                                  v_attn_sinks, v_sgu_ln_g, v_sgu_ln_b, v_sgu_w, v_sgu_b, v_w_attn_branch,
                                  v_w_sgu_branch, v_w_out, v_mix_post_g, v_ffn2_pre_g, v_ffn2_w1, v_ffn2_w2,
                                  v_ffn2_post_g)))
    T = x.shape[1]
    xs = x.reshape(T, D)
    target = loss_target.reshape(T, D)

    for n in COL_SHARDED:
        w[n], mom[n], var[n] = (jnp.swapaxes(t[n], 1, 2) for t in (w, mom, var))

    shard = {n: [cast_layer(w[n], l) for l in range(DEPTH)] for n in MATRICES}
    weights = [{} for _ in range(DEPTH)]
    ffn1, ffn2, squares = ("ffn1_w1", "ffn1_w2"), ("ffn2_w1", "ffn2_w2"), ("w_attn_branch", "w_sgu_branch", "w_out")
    for n, g in zip(ffn1, all_gather_weights([shard[n][0] for n in ffn1])):
        weights[0][n] = g.reshape(-1, D)

    def fetch(groups):
        jobs = [("all", shard[n][l]) for l, names in groups if l < DEPTH for n in names]
        return PeerCopies(jobs) if jobs else None

    def landed(groups, arrivals):
        slots = [(l, n) for l, names in groups if l < DEPTH for n in names]
        for (l, n), g in zip(slots, arrivals):
            weights[l][n] = g.reshape(-1, D)

    vec = lambda n, l: w[n][l].reshape(1, D)
    sgu_bb = [jnp.broadcast_to(w["sgu_b"][l][:, :, None], (SGU_GROUPS, BLK, BLK)) for l in range(DEPTH)]

    saved = []
    h = xs
    for l in range(DEPTH):
        wl = weights[l]
        ahead = lambda names: [(0, names)] if l == 0 else []
        x0 = h
        groups = [(l + 1, ffn1)] + ahead(("w_in",))
        (x1, a1, s1, y1), got = ffn_fwd(x0, vec("ffn1_pre_g", l), vec("ffn1_post_g", l), wl["ffn1_w1"], wl["ffn1_w2"],
                                        fetch(groups))
        landed(groups, got)
        groups = [(l + 1, ("w_in",))] + ahead(squares)
        (z,), got = mix_in_fwd(x1, vec("mix_pre_g", l), wl["w_in"], fetch(groups))
        landed(groups, got)
        groups = ahead(("ffn2_w1",))
        (ya,), got = attn_fwd(z, w["attn_sinks"][l], fetch(groups))
        landed(groups, got)
        groups = [(l + 1, squares)] + ahead(("ffn2_w2",))
        (x2, ysg, pa, pb, o), got = mix_fwd_out(
            x1, z, ya, vec("mix_post_g", l), vec("sgu_ln_g", l), vec("sgu_ln_b", l), w["sgu_w"][l], sgu_bb[l],
            wl["w_attn_branch"], wl["w_sgu_branch"], wl["w_out"], fetch(groups))
        landed(groups, got)
        groups = [(l + 1, ffn2)]
        last = (vec("ffn2_pre_g", l), vec("ffn2_post_g", l), wl["ffn2_w1"], wl["ffn2_w2"])
        if l < DEPTH - 1:
            (h, a2, s2, y2), got = ffn_fwd(x2, *last, fetch(groups))
            landed(groups, got)
        else:
            (a2, s2, y2, sq, dx), _ = ffn_fwd(x2, *last, target=target)
        saved.append((x0, a1, s1, y1, x1, z, ya, ysg, pa, pb, o, x2, a2, s2, y2))

    loss = lax.psum(0.5 / D * jnp.sum(sq), ("x", "y", "c"))

    def ffn_wgrads(hb, da, s, dy):
        return [("own", wgrad(da, hb, "wgrad_ffn_w1", D // 2).reshape(N_DEV, W1_SHARD, D)),
                ("own", wgrad(s, dy, "wgrad_ffn_w2", D).reshape(N_DEV, W2_SHARD, D))]

    parts = {n: [None] * DEPTH for n in MATRICES}
    small = {n: [None] * DEPTH for n in WEIGHT_NAMES if n not in MATRICES + ("sgu_w",)}
    sgu_w_parts = [None] * DEPTH
    waiting = None
    for l in reversed(range(DEPTH)):
        wl = weights[l]
        x0, a1, s1, y1, x1, z, ya, ysg, pa, pb, o, x2, a2, s2, y2 = saved[l]
        (dx, da, dy, hb, dpg, dqg), got = ffn_bwd(
            dx, x2, y2, a2, vec("ffn2_pre_g", l), vec("ffn2_post_g", l), wl["ffn2_w1"], wl["ffn2_w2"],
            PeerCopies(waiting) if waiting else None)
        if waiting:
            parts["ffn1_w1"][l + 1], parts["ffn1_w2"][l + 1] = got
        small["ffn2_pre_g"][l], small["ffn2_post_g"][l] = dpg, dqg

        (dzb, dya, mg, dob, dpa, dpb, dqg, dlg, dlb, dsw, dsb), got = mix_bwd_out(
            dx, z, o, pa, pb, vec("mix_post_g", l), vec("sgu_ln_g", l), vec("sgu_ln_b", l), w["sgu_w"][l], sgu_bb[l],
            wl["w_attn_branch"], wl["w_sgu_branch"], wl["w_out"], PeerCopies(ffn_wgrads(hb, da, s2, dy)))
        parts["ffn2_w1"][l], parts["ffn2_w2"][l] = got
        dz, dsink = attn_bwd(z, ya, dya, dzb, w["attn_sinks"][l])
        dx, hb, dpg = mix_in_bwd(dx, x1, dz, vec("mix_pre_g", l), wl["w_in"])
        small["mix_pre_g"][l], small["mix_post_g"][l] = dpg, dqg
        small["sgu_ln_g"][l], small["sgu_ln_b"][l] = dlg, dlb
        small["sgu_b"][l] = dsb[:, :, 0]
        small["attn_sinks"][l] = dsink[0, :N_Q_HEADS]
        mixer = [("own", wgrad(dz, hb, "wgrad_w_in", D // 2).reshape(N_DEV, WIN_SHARD, D))]
        mixer += [("own", wgrad(act, cot, "wgrad_square", D).reshape(N_DEV, SQ_SHARD, D))
                  for act, cot in ((ya, dpa), (ysg, dpb), (mg, dob))]
        mixer.append(("all", dsw.reshape(SGU_GROUPS * BLK, BLK)))
        if l == 0:
            small["ffn1_pre_g"][0] = small["ffn1_post_g"][0] = jnp.zeros((1, D), F32)
            mixer.append(("all", _pack_small({n: jnp.stack(v, axis=0) for n, v in small.items()})))

        (dx, da, dy, hb, dpg, dqg), got = ffn_bwd(
            dx, x0, y1, a1, vec("ffn1_pre_g", l), vec("ffn1_post_g", l), wl["ffn1_w1"], wl["ffn1_w2"],
            PeerCopies(mixer))
        parts["w_in"][l], parts["w_attn_branch"][l], parts["w_sgu_branch"][l], parts["w_out"][l] = got[:4]
        sgu_w_parts[l] = got[4]
        if l > 0:
            small["ffn1_pre_g"][l], small["ffn1_post_g"][l] = dpg, dqg
            waiting = ffn_wgrads(hb, da, s1, dy)
    grad_x = dx.reshape(x.shape)
    small_parts = got[5]

    def late_rows(pre, post):
        return jnp.concatenate([pre.reshape(1, D), post.reshape(1, D), jnp.zeros((LATE_ROWS - 2, D), F32)], axis=0)

    g_w1 = wgrad(da, hb, "wgrad_ffn_w1", D // 2).reshape(N_DEV, W1_SHARD, D)
    g_w2, got = wgrad(s1, dy, "wgrad_ffn_w2", D, PeerCopies([("own", g_w1)]))
    parts["ffn1_w1"][0] = got[0]
    parts["ffn1_w2"][0], late_parts = exchange_last([("own", g_w2.reshape(N_DEV, W2_SHARD, D)),
                                                     ("all", late_rows(dpg, dqg))])

    grads, deltas, new_m, new_v = {}, {}, {}, {}
    for n in MATRICES:
        grads[n], deltas[n], new_m[n], new_v[n] = adamw_sum(parts[n], w[n], mom[n], var[n], "adamw_" + n)
    for n in COL_SHARDED:
        for out in (grads, deltas, new_m, new_v):
            out[n] = jnp.swapaxes(out[n], 1, 2)
    flat = lambda t: t["sgu_w"].reshape(DEPTH, SGU_GROUPS * BLK, BLK)
    for out, r in zip((grads, deltas, new_m, new_v),
                      adamw_sum(sgu_w_parts, flat(w), flat(mom), flat(var), "adamw_sgu_w")):
        out["sgu_w"] = r.reshape(w["sgu_w"].shape)
    res = adamw_sum([small_parts], _pack_small(w)[None], _pack_small(mom)[None], _pack_small(var)[None],
                    "adamw_small")
    late = adamw_sum([late_parts], *[late_rows(t["ffn1_pre_g"][0], t["ffn1_post_g"][0])[None] for t in (w, mom, var)],
                     "adamw_late")
    for out, packed, late_out in zip((grads, deltas, new_m, new_v), res, late):
        out.update(_unpack_small(packed[0]))
        for row, n in enumerate(("ffn1_pre_g", "ffn1_post_g")):
            out[n] = jnp.concatenate([late_out[0, row:row + 1], out[n][1:]], axis=0)

    return (loss, grad_x, *[grads[n] for n in WEIGHT_NAMES], *[deltas[n] for n in WEIGHT_NAMES],
            *[new_m[n] for n in WEIGHT_NAMES], *[new_v[n] for n in WEIGHT_NAMES])
```

```python
import math

import jax
import jax.numpy as jnp
from jax import lax
from jax.experimental import pallas as pl
from jax.experimental.pallas import tpu as pltpu

F32 = jnp.float32
BF16 = jnp.bfloat16

N_DEV = 8
D = 1024
FF = 2816
DEPTH = 4
HEAD_DIM = 64
N_Q_HEADS = 16
N_KV_HEADS = 2
Q_PER_KV = N_Q_HEADS // N_KV_HEADS
BLK = 128
SGU_GROUPS = 8
IN_WIDTH = 5376
W1_SHARD = 2 * FF // N_DEV
WIN_SHARD = IN_WIDTH // N_DEV
W2_SHARD = FF // N_DEV
SQ_SHARD = D // N_DEV

RMS_EPS = 1e-6
LN_EPS = 1e-5
MASK_VALUE = -1e30
ATTN_SCALE = 1.0 / math.sqrt(HEAD_DIM)

ADAM_LR = 0.001
ADAM_B1 = 0.9
ADAM_B2 = 0.999
ADAM_EPS = 1e-08
ADAM_WD = 0.01
ADAM_STEP = 10

VMEM_LIMIT_V7X = 56 * 1024 * 1024

WIN_SEGMENTS = ((0, 1024, 0), (1024, 256, 5120), (1280, 1024, 1024), (2304, 1024, 2048), (3328, 1024, 3072),
                (4352, 1024, 4096))
ZQ, ZU, ZV, ZGA, ZGB, ZKV = 0, 1024, 2048, 3072, 4096, 5120
DZ_Q, DZ_KV, DZ_MID = 0, 1024, 1280

FF_CHUNKS = ((0, 1024), (1024, 1024), (2048, 768))
WIN_CHUNKS = ((0, 1792), (1792, 1792), (3584, 1792))

SMALL_ROWS = 16
LATE_ROWS = 16
SMALL_VEC = ("ffn1_pre_g", "ffn1_post_g", "mix_pre_g", "mix_post_g", "ffn2_pre_g", "ffn2_post_g",
             "sgu_ln_g", "sgu_ln_b")

TM_FFN_FWD = 512
TM_FFN_BWD = 256
TM_MIX_IN = 1024
TM_MIX = 512
TM_MIX_BWD = 256
TQ_ATTN = 512
TK_WGRAD = 1024
TR_ADAM = 256


def _params(n_grid, vmem=VMEM_LIMIT_V7X):
    return pltpu.CompilerParams(dimension_semantics=("arbitrary",) * n_grid, vmem_limit_bytes=vmem)


def _dot(a, b):
    return jnp.dot(a, b, preferred_element_type=F32)


def _dot_nt(a, b):
    return lax.dot_general(a, b, (((1,), (1,)), ((), ())), preferred_element_type=F32)


def _dot_tn(a, b):
    return lax.dot_general(a, b, (((0,), (0,)), ((), ())), preferred_element_type=F32)


def _rms(x, g):
    r = lax.rsqrt(jnp.mean(x * x, axis=-1, keepdims=True) + RMS_EPS)
    n = x * r
    return n * g, n, r


def _rms_bwd(n, r, g, dy):
    dn = dy * g
    return r * (dn - n * jnp.mean(dn * n, axis=-1, keepdims=True))


def _colsum(v):
    return jnp.sum(v, axis=0, keepdims=True)


def _sigmoid(v):
    return 0.5 * jnp.tanh(0.5 * v) + 0.5


def _gelu_parts(v):
    cdf = 0.5 * lax.erf(v * (1.0 / math.sqrt(2.0))) + 0.5
    return cdf, jnp.exp2(v * v * (-0.5 / math.log(2.0))) * (1.0 / math.sqrt(2.0 * math.pi))


def _row_tile(rows, cap):
    return max(t for t in range(16, min(rows, cap) + 1, 16) if rows % t == 0)


def _row_spec(tm, width, col_block=0):
    return pl.BlockSpec((tm, width), lambda i, cb=col_block: (i, cb))


def _full_spec(shape):
    nd = len(shape)
    return pl.BlockSpec(tuple(shape), lambda *_: (0,) * nd)


def _whole_spec(arr):
    nd = arr.ndim
    return pl.BlockSpec(tuple(arr.shape), lambda *_: (0,) * nd, pipeline_mode=pl.Buffered(1))


HBM_SPEC = pl.BlockSpec(memory_space=pltpu.HBM)
MESH_ID = pl.DeviceIdType.MESH
RELATIONS = tuple((rx, ry, rc) for rx in (0, 1) for ry in (0, 1) for rc in (0, 1))[1:]


class PeerCopies:
    def __init__(self, jobs):
        self.kinds = [k for k, _ in jobs]
        self.arrays = [a for _, a in jobs]
        self.n = len(jobs)
        self.out_shape = [jax.ShapeDtypeStruct((N_DEV,) + a.shape if k == "all" else a.shape, a.dtype)
                          for k, a in jobs]
        self.scratch = [pltpu.SemaphoreType.DMA((7 * self.n,)), pltpu.SemaphoreType.DMA((7 * self.n,)),
                        pltpu.SemaphoreType.DMA((self.n,))]

    def _copies(self, ins, outs, sems, arriving):
        send_sems, recv_sems, local_sems = sems
        x, y, c = lax.axis_index("x"), lax.axis_index("y"), lax.axis_index("c")
        me = 4 * x + 2 * y + c
        src = lambda a, d: ins[a] if self.kinds[a] == "all" else ins[a].at[d]
        if not arriving:
            local = [pltpu.make_async_copy(src(a, me), outs[a].at[me], local_sems.at[a]) for a in range(self.n)]
        remote = []
        for k, (rx, ry, rc) in enumerate(RELATIONS):
            tx, ty, tc = (1 - x if rx else x), (1 - y if ry else y), (1 - c if rc else c)
            peer = 4 * tx + 2 * ty + tc
            for a in range(self.n):
                from_slot, to_slot = (me, peer) if arriving else (peer, me)
                remote.append(pltpu.make_async_remote_copy(
                    src_ref=src(a, from_slot), dst_ref=outs[a].at[to_slot],
                    send_sem=send_sems.at[a * 7 + k], recv_sem=recv_sems.at[a * 7 + k],
                    device_id=(tx, ty, tc), device_id_type=MESH_ID))
        return remote if arriving else (local, remote)

    def start(self, ins, outs, sems):
        local, sends = self._copies(ins, outs, sems, False)
        for cp in local + sends:
            cp.start()

    def wait(self, ins, outs, sems):
        for cp in self._copies(ins, outs, sems, True):
            cp.wait_recv()
        local, sends = self._copies(ins, outs, sems, False)
        for cp in sends:
            cp.wait_send()
        for cp in local:
            cp.wait()


def _call(body, grid, in_specs, args, out_specs, out_shape, name, scratch=(), copies=None):
    if copies is None:
        outs = pl.pallas_call(body, grid=grid, in_specs=list(in_specs), out_specs=list(out_specs),
                              out_shape=list(out_shape), scratch_shapes=list(scratch), name=name,
                              compiler_params=_params(len(grid)))(*args)
        return outs, []
    n_in, n_out, n_scr, nc = len(in_specs), len(out_specs), len(scratch), copies.n

    def at_step(steps):
        hit = pl.program_id(0) == steps[0]
        for axis in range(1, len(grid)):
            hit = jnp.logical_and(hit, pl.program_id(axis) == steps[axis])
        return hit

    def hosted(*refs):
        ins, refs = refs[:n_in], refs[n_in:]
        c_in, refs = refs[:nc], refs[nc:]
        outs, refs = refs[:n_out], refs[n_out:]
        c_out, refs = refs[:nc], refs[nc:]
        scr, sems = refs[:n_scr], refs[n_scr:]

        @pl.when(at_step([0] * len(grid)))
        def _():
            copies.start(c_in, c_out, sems)

        body(*ins, *outs, *scr)

        @pl.when(at_step([g - 1 for g in grid]))
        def _():
            copies.wait(c_in, c_out, sems)

    outs = pl.pallas_call(hosted, grid=grid, in_specs=list(in_specs) + [HBM_SPEC] * nc,
                          out_specs=list(out_specs) + [HBM_SPEC] * nc,
                          out_shape=list(out_shape) + copies.out_shape,
                          scratch_shapes=list(scratch) + copies.scratch, name=name,
                          compiler_params=_params(len(grid)))(*args, *copies.arrays)
    return outs[:n_out], outs[n_out:]


def cast_layer(w, l):
    _, rows, cols = w.shape
    tr = _row_tile(rows, 256)

    def body(w_ref, o_ref):
        o_ref[...] = w_ref[...].astype(BF16)

    return pl.pallas_call(
        body, grid=(rows // tr,), in_specs=[pl.BlockSpec((None, tr, cols), lambda i: (l, i, 0))],
        out_specs=_row_spec(tr, cols), out_shape=jax.ShapeDtypeStruct((rows, cols), BF16),
        name="cast_layer", compiler_params=_params(1),
    )(w)


def all_gather_weights(shards):
    n = len(shards)

    def body(*refs):
        ins, outs = refs[:n], refs[n:2 * n]
        send_sems, recv_sems, local_sems = refs[2 * n:]
        x, y, c = lax.axis_index("x"), lax.axis_index("y"), lax.axis_index("c")
        me, sibling = (x, y, c), (x, y, 1 - c)
        chips = [(1 - x, y), (x, 1 - y), (1 - x, 1 - y)]

        def slot(a, owner):
            return outs[a].at[4 * owner[0] + 2 * owner[1] + owner[2]]

        def copy(a, k, owner, to, src=None):
            return pltpu.make_async_remote_copy(
                src_ref=slot(a, owner) if src is None else src, dst_ref=slot(a, owner),
                send_sem=send_sems.at[a * 7 + k], recv_sem=recv_sems.at[a * 7 + k],
                device_id=to, device_id_type=MESH_ID)

        mine = [pltpu.make_async_copy(ins[a], slot(a, me), local_sems.at[a]) for a in range(n)]
        for cp in mine:
            cp.start()
        first = []
        for a in range(n):
            first.append(copy(a, 0, me, sibling, src=ins[a]))
            first += [copy(a, 1 + j, me, (*chip, c), src=ins[a]) for j, chip in enumerate(chips)]
        for cp in first:
            cp.start()
        passed = []
        for j, chip in enumerate(chips):
            for a in range(n):
                copy(a, 1 + j, (*chip, c), me).wait_recv()
                fwd = copy(a, 4 + j, (*chip, c), sibling)
                fwd.start()
                passed.append(fwd)
        for a in range(n):
            copy(a, 0, sibling, me).wait_recv()
            for j, chip in enumerate(chips):
                copy(a, 4 + j, (*chip, 1 - c), me).wait_recv()
        for cp in first + passed:
            cp.wait_send()
        for cp in mine:
            cp.wait()

    out_shape = [jax.ShapeDtypeStruct((N_DEV,) + s.shape, s.dtype) for s in shards]
    return pl.pallas_call(
        body, in_specs=[HBM_SPEC] * n, out_specs=[HBM_SPEC] * n, out_shape=out_shape,
        scratch_shapes=[pltpu.SemaphoreType.DMA((7 * n,)), pltpu.SemaphoreType.DMA((7 * n,)),
                        pltpu.SemaphoreType.DMA((n,))],
        name="all_gather_weights",
    )(*shards)


def exchange_last(jobs):
    copies = PeerCopies(jobs)
    n = copies.n

    def body(*refs):
        ins, outs, sems = refs[:n], refs[n:2 * n], refs[2 * n:]
        copies.start(ins, outs, sems)
        copies.wait(ins, outs, sems)

    return pl.pallas_call(
        body, in_specs=[HBM_SPEC] * n, out_specs=[HBM_SPEC] * n, out_shape=copies.out_shape,
        scratch_shapes=copies.scratch, name="exchange_last",
    )(*copies.arrays)


def ffn_fwd(x, pre_g, post_g, w1, w2, copies=None, target=None):
    T = x.shape[0]
    tm = min(TM_FFN_FWD, T)

    def body(*refs):
        if target is None:
            x_ref, pg_ref, qg_ref, w1_ref, w2_ref, xo_ref, a_ref, s_ref, y_ref = refs
        else:
            x_ref, pg_ref, qg_ref, w1_ref, w2_ref, t_ref, a_ref, s_ref, y_ref, sq_ref, dy_ref = refs
        xv = x_ref[...]
        h, _, _ = _rms(xv, pg_ref[...])
        hb = h.astype(BF16)
        acc = jnp.zeros((tm, D), F32)
        for c0, cw in FF_CHUNKS:
            g = _dot_nt(hb, w1_ref[c0:c0 + cw, :])
            u = _dot_nt(hb, w1_ref[FF + c0:FF + c0 + cw, :])
            sg = _sigmoid(g)
            si = g * sg
            a_ref[:, c0:c0 + cw] = (u * (sg + si - si * sg)).astype(BF16)
            a_ref[:, FF + c0:FF + c0 + cw] = si.astype(BF16)
            s = (si * u).astype(BF16)
            s_ref[:, c0:c0 + cw] = s
            acc = acc + _dot(s, w2_ref[c0:c0 + cw, :])
        y_ref[...] = acc.astype(BF16)
        o, _, _ = _rms(acc, qg_ref[...])
        xo = xv + 0.5 * o
        if target is None:
            xo_ref[...] = xo
        else:
            @pl.when(pl.program_id(0) == 0)
            def _():
                sq_ref[...] = jnp.zeros_like(sq_ref)

            err = xo - t_ref[...]
            sq_ref[...] += _colsum(err * err)
            dy_ref[...] = err * (1.0 / D)

    in_specs = [_row_spec(tm, D), _full_spec((1, D)), _full_spec((1, D)), _whole_spec(w1), _whole_spec(w2)]
    saved_specs = [_row_spec(tm, 2 * FF), _row_spec(tm, FF), _row_spec(tm, D)]
    saved_shapes = [jax.ShapeDtypeStruct((T, 2 * FF), BF16), jax.ShapeDtypeStruct((T, FF), BF16),
                    jax.ShapeDtypeStruct((T, D), BF16)]
    row_f32 = jax.ShapeDtypeStruct((T, D), F32)
    if target is None:
        return _call(body, (T // tm,), in_specs, (x, pre_g, post_g, w1, w2), [_row_spec(tm, D)] + saved_specs,
                     [row_f32] + saved_shapes, "ffn_fwd", copies=copies)
    return _call(body, (T // tm,), in_specs + [_row_spec(tm, D)], (x, pre_g, post_g, w1, w2, target),
                 saved_specs + [_full_spec((1, D)), _row_spec(tm, D)],
                 saved_shapes + [jax.ShapeDtypeStruct((1, D), F32), row_f32], "ffn_fwd_loss", copies=copies)


def ffn_bwd(dxo, x, y, a, pre_g, post_g, w1, w2, copies=None):
    T = x.shape[0]
    tm = min(TM_FFN_BWD, T)

    def body(dxo_ref, x_ref, y_ref, a_ref, pg_ref, qg_ref, w1_ref, w2_ref,
             dx_ref, da_ref, dy_ref, hb_ref, dpg_ref, dqg_ref):
        @pl.when(pl.program_id(0) == 0)
        def _():
            dpg_ref[...] = jnp.zeros_like(dpg_ref)
            dqg_ref[...] = jnp.zeros_like(dqg_ref)

        dxo = dxo_ref[...]
        qg = qg_ref[...]
        _, ny, ry = _rms(y_ref[...].astype(F32), qg)
        dn = 0.5 * dxo
        dqg_ref[...] += _colsum(dn * ny)
        dyb = _rms_bwd(ny, ry, qg, dn).astype(BF16)
        dy_ref[...] = dyb
        pg = pg_ref[...]
        h, nx, rx = _rms(x_ref[...], pg)
        hb_ref[...] = h.astype(BF16)
        dh = jnp.zeros((tm, D), F32)
        for c0, cw in FF_CHUNKS:
            ds = _dot_nt(dyb, w2_ref[c0:c0 + cw, :])
            dg = (ds * a_ref[:, c0:c0 + cw].astype(F32)).astype(BF16)
            du = (ds * a_ref[:, FF + c0:FF + c0 + cw].astype(F32)).astype(BF16)
            da_ref[:, c0:c0 + cw] = dg
            da_ref[:, FF + c0:FF + c0 + cw] = du
            dh = dh + _dot(dg, w1_ref[c0:c0 + cw, :]) + _dot(du, w1_ref[FF + c0:FF + c0 + cw, :])
        dpg_ref[...] += _colsum(dh * nx)
        dx_ref[...] = dxo + _rms_bwd(nx, rx, pg, dh)

    return _call(
        body, (T // tm,),
        [_row_spec(tm, D), _row_spec(tm, D), _row_spec(tm, D), _row_spec(tm, 2 * FF),
         _full_spec((1, D)), _full_spec((1, D)), _whole_spec(w1), _whole_spec(w2)],
        (dxo, x, y, a, pre_g, post_g, w1, w2),
        [_row_spec(tm, D), _row_spec(tm, 2 * FF), _row_spec(tm, D), _row_spec(tm, D),
         _full_spec((1, D)), _full_spec((1, D))],
        [jax.ShapeDtypeStruct((T, D), F32), jax.ShapeDtypeStruct((T, 2 * FF), BF16),
         jax.ShapeDtypeStruct((T, D), BF16), jax.ShapeDtypeStruct((T, D), BF16),
         jax.ShapeDtypeStruct((1, D), F32), jax.ShapeDtypeStruct((1, D), F32)],
        "ffn_bwd", copies=copies)


def mix_in_fwd(x, pre_g, win, copies=None):
    T = x.shape[0]
    tm = min(TM_MIX_IN, T)

    def body(x_ref, pg_ref, w_ref, z_ref):
        h, _, _ = _rms(x_ref[...], pg_ref[...])
        hb = h.astype(BF16)
        for w0, n, z0 in WIN_SEGMENTS:
            z_ref[:, z0:z0 + n] = _dot_nt(hb, w_ref[w0:w0 + n, :]).astype(BF16)

    return _call(
        body, (T // tm,), [_row_spec(tm, D), _full_spec((1, D)), _whole_spec(win)], (x, pre_g, win),
        [_row_spec(tm, IN_WIDTH)], [jax.ShapeDtypeStruct((T, IN_WIDTH), BF16)], "mix_in_fwd", copies=copies)


STACK = Q_PER_KV // 2
SROWS = STACK * BLK


def _sees_own():
    qi = lax.broadcasted_iota(jnp.int32, (SROWS, BLK), 0) & (BLK - 1)
    return lax.broadcasted_iota(jnp.int32, (SROWS, BLK), 1) <= qi


def _both_blocks(picked, own):
    return jnp.concatenate([jnp.where(own, 0.0, picked), jnp.where(own, picked, 0.0)], axis=1)


def _lane_half(rows):
    return lax.broadcasted_iota(jnp.int32, (rows, BLK), 1) // HEAD_DIM


def _stack(ref, b, g):
    return jnp.concatenate([ref[b * BLK:(b + 1) * BLK, (STACK * g + j) * BLK:(STACK * g + j + 1) * BLK]
                            for j in range(STACK)], axis=0)


def _placed(pair, g, hp, fill):
    src = pair if hp == g else pltpu.roll(pair, HEAD_DIM, 1)
    return jnp.where(_lane_half(2 * BLK) == hp, src, fill).astype(BF16)


def _sink_column(sink_ref, g, hp):
    rb = lax.broadcasted_iota(jnp.int32, (SROWS, 1), 0) // BLK
    col = jnp.full((SROWS, 1), sink_ref[Q_PER_KV * g + hp], F32)
    for j in range(1, STACK):
        col = jnp.where(rb == j, sink_ref[Q_PER_KV * g + 2 * j + hp], col)
    return col


def _attn_scores(qs, kz, sink_col, own, no_previous):
    raw = _dot_nt(qs, kz)
    prev = raw[:, 0:BLK]
    if no_previous is not None:
        prev = prev + no_previous.astype(F32) * MASK_VALUE
    picked = jnp.where(own, raw[:, BLK:2 * BLK], prev)
    sink_raw = sink_col * (1.0 / ATTN_SCALE)
    m = jnp.maximum(jnp.max(picked, axis=-1, keepdims=True), sink_raw)
    factor = ATTN_SCALE / math.log(2.0)
    return jnp.exp2((picked - m) * factor), jnp.exp2((sink_raw - m) * factor)


def _kv_specs(tq, nb):
    kv_blk = ZKV // (2 * BLK)
    return [pl.BlockSpec((tq, 2 * BLK), lambda i: (i, kv_blk)),
            pl.BlockSpec((BLK, 2 * BLK), lambda i: (jnp.maximum(i * nb - 1, 0), kv_blk))]


def attn_fwd(z, sinks, copies=None):
    T = z.shape[0]
    tq = min(TQ_ATTN, T)
    nb = tq // BLK

    def body(sink_ref, q_ref, kv_ref, kvh_ref, o_ref):
        i = pl.program_id(0)
        low = _lane_half(SROWS) == 0
        own = _sees_own()
        for b in range(nb):
            kvp = kvh_ref[...] if b == 0 else kv_ref[(b - 1) * BLK:b * BLK, :]
            kv2 = jnp.concatenate([kvp, kv_ref[b * BLK:(b + 1) * BLK, :]], axis=0).astype(F32)
            no_previous = (i == 0) if b == 0 else None
            for g in range(N_KV_HEADS):
                qs = _stack(q_ref, b, g)
                r, e = [], []
                for hp in range(2):
                    p, e_sink = _attn_scores(qs, _placed(kv2[:, 0:BLK], g, hp, 0.0), _sink_column(sink_ref, g, hp),
                                             own, no_previous)
                    r.append(_dot(_both_blocks(p, own).astype(BF16), _placed(kv2[:, BLK:2 * BLK], g, hp, 1.0)))
                    e.append(e_sink)
                den = pltpu.roll(jnp.where(low, r[1], r[0]), HEAD_DIM, 1) + jnp.where(low, e[0], e[1])
                out = (jnp.where(low, r[0], r[1]) * (1.0 / den)).astype(BF16)
                for j in range(STACK):
                    o_ref[b * BLK:(b + 1) * BLK, (STACK * g + j) * BLK:(STACK * g + j + 1) * BLK] = \
                        out[j * BLK:(j + 1) * BLK, :]

    return _call(
        body, (T // tq,), [pl.BlockSpec(memory_space=pltpu.SMEM), _row_spec(tq, D)] + _kv_specs(tq, nb),
        (sinks, z, z, z), [_row_spec(tq, D)], [jax.ShapeDtypeStruct((T, D), BF16)], "attn_fwd", copies=copies)


def attn_bwd(z, ya, dya, dzb, sinks):
    T = z.shape[0]
    tq = min(TQ_ATTN, T)
    nb = tq // BLK
    nt = T // tq
    mid = ZKV - ZU

    def body(sink_ref, q_ref, kv_ref, kvh_ref, y_ref, dy_ref, dzb_ref, dz_ref, dsink_ref, acc_ref, carry_ref):
        i = pl.program_id(0)
        first_tile = i == nt - 1

        @pl.when(i == 0)
        def _():
            carry_ref[...] = jnp.zeros_like(carry_ref)
            dsink_ref[...] = jnp.zeros_like(dsink_ref)

        acc_ref[...] = jnp.zeros_like(acc_ref)
        lane = lax.broadcasted_iota(jnp.int32, (1, BLK), 1)
        dsink = jnp.zeros((1, BLK), F32)
        half = _lane_half(SROWS)
        own = _sees_own()
        for b in range(nb):
            kvp = kvh_ref[...] if b == 0 else kv_ref[(b - 1) * BLK:b * BLK, :]
            kv2 = jnp.concatenate([kvp, kv_ref[b * BLK:(b + 1) * BLK, :]], axis=0).astype(F32)
            no_previous = first_tile if b == 0 else None
            dk_groups, dv_groups = [], []
            for g in range(N_KV_HEADS):
                qs = _stack(q_ref, b, g)
                dys = _stack(dy_ref, b, g)
                dyy = dys.astype(F32) * _stack(y_ref, b, g).astype(F32)
                dq = jnp.zeros((SROWS, BLK), F32)
                ds_both, pn_both = [], []
                for hp in range(2):
                    kz = _placed(kv2[:, 0:BLK], g, hp, 0.0)
                    p, e_sink = _attn_scores(qs, kz, _sink_column(sink_ref, g, hp), own, no_previous)
                    inv = 1.0 / (jnp.sum(p, axis=-1, keepdims=True) + e_sink)
                    p = p * inv
                    delta = jnp.sum(jnp.where(half == hp, dyy, 0.0), axis=-1, keepdims=True)
                    dp = _dot_nt(dys, _placed(kv2[:, BLK:2 * BLK], g, hp, 0.0))
                    ds = p * (jnp.where(own, dp[:, BLK:2 * BLK], dp[:, 0:BLK]) - delta)
                    ds = _both_blocks(ds, own).astype(BF16)
                    pn = _both_blocks(p, own)
                    sink_term = e_sink * inv * delta
                    for j in range(STACK):
                        dsink = dsink + jnp.where(lane == Q_PER_KV * g + 2 * j + hp,
                                                  -_colsum(sink_term[j * BLK:(j + 1) * BLK, :]), 0.0)
                    dq = dq + _dot(ds, kz)
                    ds_both.append(ds)
                    pn_both.append(pn.astype(BF16))
                dk_t = _dot_tn(qs, jnp.concatenate(ds_both, axis=1))
                dv_t = _dot_tn(dys, jnp.concatenate(pn_both, axis=1))
                for t, groups in ((dk_t, dk_groups), (dv_t, dv_groups)):
                    groups.append(t[0:HEAD_DIM, 0:2 * BLK] + t[HEAD_DIM:BLK, 2 * BLK:4 * BLK])
                dqb = (dq * ATTN_SCALE).astype(BF16)
                for j in range(STACK):
                    dz_ref[b * BLK:(b + 1) * BLK, DZ_Q + (STACK * g + j) * BLK:DZ_Q + (STACK * g + j + 1) * BLK] = \
                        dqb[j * BLK:(j + 1) * BLK, :]
            acc_ref[b * BLK:(b + 2) * BLK, 0:BLK] += jnp.concatenate(dk_groups, axis=0).T * ATTN_SCALE
            acc_ref[b * BLK:(b + 2) * BLK, BLK:2 * BLK] += jnp.concatenate(dv_groups, axis=0).T
        dsink_ref[...] += dsink
        dz_ref[:, DZ_MID:IN_WIDTH] = dzb_ref[...]
        if nb > 1:
            dz_ref[0:tq - BLK, DZ_KV:DZ_MID] = acc_ref[BLK:tq, :].astype(BF16)
        dz_ref[tq - BLK:tq, DZ_KV:DZ_MID] = (acc_ref[tq:tq + BLK, :] + carry_ref[...]).astype(BF16)
        carry_ref[...] = acc_ref[0:BLK, :]

    kv_blk = ZKV // (2 * BLK)
    return pl.pallas_call(
        body, grid=(nt,),
        in_specs=[pl.BlockSpec(memory_space=pltpu.SMEM),
                  pl.BlockSpec((tq, D), lambda i: (nt - 1 - i, 0)),
                  pl.BlockSpec((tq, 2 * BLK), lambda i: (nt - 1 - i, kv_blk)),
                  pl.BlockSpec((BLK, 2 * BLK), lambda i: (jnp.maximum((nt - 1 - i) * nb - 1, 0), kv_blk)),
                  pl.BlockSpec((tq, D), lambda i: (nt - 1 - i, 0)),
                  pl.BlockSpec((tq, D), lambda i: (nt - 1 - i, 0)),
                  pl.BlockSpec((tq, mid), lambda i: (nt - 1 - i, 0))],
        out_specs=[pl.BlockSpec((tq, IN_WIDTH), lambda i: (nt - 1 - i, 0)), _full_spec((1, BLK))],
        out_shape=[jax.ShapeDtypeStruct((T, IN_WIDTH), BF16), jax.ShapeDtypeStruct((1, BLK), F32)],
        scratch_shapes=[pltpu.VMEM((tq + BLK, 2 * BLK), F32), pltpu.VMEM((BLK, 2 * BLK), F32)],
        name="attn_bwd", compiler_params=_params(1),
    )(sinks, z, z, z, ya, dya, dzb)


def _to_group_lanes(v, g, nch):
    return jnp.concatenate([v[n * BLK:(n + 1) * BLK, g * BLK:(g + 1) * BLK] for n in range(nch)], axis=1)


def _from_group_lanes(per_group, nch):
    rows = [jnp.concatenate([per_group[g][:, n * BLK:(n + 1) * BLK] for g in range(SGU_GROUPS)], axis=1)
            for n in range(nch)]
    return jnp.concatenate(rows, axis=0)


def _tril_bf16(w_ref, g):
    t = lax.broadcasted_iota(jnp.int32, (BLK, BLK), 0)
    s = lax.broadcasted_iota(jnp.int32, (BLK, BLK), 1)
    return jnp.where(t >= s, w_ref[g], 0.0).astype(BF16)


def _sgu_norm(v_s, ln_g, ln_b):
    cdf, pdf = _gelu_parts(v_s)
    gv = v_s * cdf
    xc = gv - jnp.mean(gv, axis=-1, keepdims=True)
    rstd = lax.rsqrt(jnp.mean(xc * xc, axis=-1, keepdims=True) + LN_EPS)
    nhat = xc * rstd
    return nhat * ln_g + ln_b, nhat, rstd, cdf + v_s * pdf


def _sgu_gate(vn, w_ref, bb_ref, nch):
    vnb = vn.astype(BF16)
    return _from_group_lanes(
        [_dot(_tril_bf16(w_ref, g), _to_group_lanes(vnb, g, nch)) + jnp.tile(bb_ref[g], (1, nch))
         for g in range(SGU_GROUPS)], nch)


def mix_fwd_out(x, z, ya, post_g, ln_g, ln_b, sgu_w, sgu_bb, wa, ws, wo, copies=None):
    T = x.shape[0]
    tm = min(TM_MIX, T)
    nch = tm // BLK

    def body(x_ref, us_ref, vs_ref, ga_ref, gb_ref, ya_ref, qg_ref, lg_ref, lb_ref, w_ref, bb_ref,
             wa_ref, ws_ref, wo_ref, xo_ref, ysg_ref, pa_ref, pb_ref, o_ref):
        vn, _, _, _ = _sgu_norm(vs_ref[...].astype(F32), lg_ref[...], lb_ref[...])
        gate = _sgu_gate(vn, w_ref, bb_ref, nch)
        us = us_ref[...].astype(F32)
        cdf, _ = _gelu_parts(us)
        ysg = (us * cdf * gate).astype(BF16)
        ysg_ref[...] = ysg
        pa = _dot(ya_ref[...], wa_ref[...])
        pb = _dot(ysg, ws_ref[...])
        pa_ref[...] = pa.astype(BF16)
        pb_ref[...] = pb.astype(BF16)
        merged = _sigmoid(ga_ref[...].astype(F32)) * pa + _sigmoid(gb_ref[...].astype(F32)) * pb
        o = _dot(merged.astype(BF16), wo_ref[...])
        o_ref[...] = o.astype(BF16)
        on, _, _ = _rms(o, qg_ref[...])
        xo_ref[...] = x_ref[...] + on

    zspec = lambda start: _row_spec(tm, D, start // D)
    act = jax.ShapeDtypeStruct((T, D), BF16)
    return _call(
        body, (T // tm,),
        [_row_spec(tm, D), zspec(ZU), zspec(ZV), zspec(ZGA), zspec(ZGB), _row_spec(tm, D),
         _full_spec((1, D)), _full_spec((1, D)), _full_spec((1, D)),
         _full_spec((SGU_GROUPS, BLK, BLK)), _full_spec((SGU_GROUPS, BLK, BLK)),
         _whole_spec(wa), _whole_spec(ws), _whole_spec(wo)],
        (x, z, z, z, z, ya, post_g, ln_g, ln_b, sgu_w, sgu_bb, wa, ws, wo),
        [_row_spec(tm, D)] * 5, [jax.ShapeDtypeStruct((T, D), F32), act, act, act, act],
        "mix_fwd_out", copies=copies)


def mix_bwd_out(dxo, z, o, pa, pb, post_g, ln_g, ln_b, sgu_w, sgu_bb, wa, ws, wo, copies=None):
    T = dxo.shape[0]
    tm = min(TM_MIX_BWD, T)
    nch = tm // BLK
    mid = ZKV - ZU

    def body(dxo_ref, us_ref, vs_ref, ga_ref, gb_ref, o_ref, pa_ref, pb_ref, qg_ref, lg_ref, lb_ref, w_ref, bb_ref,
             wa_ref, ws_ref, wo_ref,
             dzb_ref, dya_ref, mg_ref, do_ref, dpa_ref, dpb_ref, dqg_ref, dlg_ref, dlb_ref, dw_ref, dbb_ref):
        @pl.when(pl.program_id(0) == 0)
        def _():
            for r in (dqg_ref, dlg_ref, dlb_ref, dw_ref, dbb_ref):
                r[...] = jnp.zeros_like(r)

        qg = qg_ref[...]
        dxo = dxo_ref[...]
        _, no, ro = _rms(o_ref[...].astype(F32), qg)
        dqg_ref[...] += _colsum(dxo * no)
        dob = _rms_bwd(no, ro, qg, dxo).astype(BF16)
        do_ref[...] = dob
        dmerged = _dot_nt(dob, wo_ref[...])
        sa = _sigmoid(ga_ref[...].astype(F32))
        sb = _sigmoid(gb_ref[...].astype(F32))
        pa = pa_ref[...].astype(F32)
        pb = pb_ref[...].astype(F32)
        ta = sa * pa
        tb = sb * pb
        mg_ref[...] = (ta + tb).astype(BF16)
        dpa = (dmerged * sa).astype(BF16)
        dpb = (dmerged * sb).astype(BF16)
        dpa_ref[...] = dpa
        dpb_ref[...] = dpb
        dzb_ref[:, ZGA - ZU:ZGA - ZU + D] = (dmerged * (ta - ta * sa)).astype(BF16)
        dzb_ref[:, ZGB - ZU:ZGB - ZU + D] = (dmerged * (tb - tb * sb)).astype(BF16)
        dya_ref[...] = _dot_nt(dpa, wa_ref[...]).astype(BF16)
        dysg = _dot_nt(dpb, ws_ref[...])

        lg = lg_ref[...]
        vn, nhat, rstd, dgelu_v = _sgu_norm(vs_ref[...].astype(F32), lg, lb_ref[...])
        gate = _sgu_gate(vn, w_ref, bb_ref, nch)
        us = us_ref[...].astype(F32)
        cdf, pdf = _gelu_parts(us)
        dzb_ref[:, 0:D] = (dysg * gate * (cdf + us * pdf)).astype(BF16)
        dgate = (dysg * (us * cdf)).astype(BF16)
        vnb = vn.astype(BF16)
        t = lax.broadcasted_iota(jnp.int32, (BLK, BLK), 0)
        s = lax.broadcasted_iota(jnp.int32, (BLK, BLK), 1)
        dvn_groups = []
        for g in range(SGU_GROUPS):
            dgl = _to_group_lanes(dgate, g, nch)
            dbb_ref[g] += jnp.broadcast_to(jnp.sum(dgl.astype(F32), axis=-1, keepdims=True), (BLK, BLK))
            dw_ref[g] += jnp.where(t >= s, _dot_nt(dgl, _to_group_lanes(vnb, g, nch)), 0.0)
            dvn_groups.append(_dot_tn(_tril_bf16(w_ref, g), dgl))
        dvn = _from_group_lanes(dvn_groups, nch)
        dlg_ref[...] += _colsum(dvn * nhat)
        dlb_ref[...] += _colsum(dvn)
        dnh = dvn * lg
        dgv = rstd * (dnh - jnp.mean(dnh, axis=-1, keepdims=True) - nhat * jnp.mean(dnh * nhat, axis=-1, keepdims=True))
        dzb_ref[:, ZV - ZU:ZV - ZU + D] = (dgv * dgelu_v).astype(BF16)

    zspec = lambda start: _row_spec(tm, D, start // D)
    act = jax.ShapeDtypeStruct((T, D), BF16)
    grp = jax.ShapeDtypeStruct((SGU_GROUPS, BLK, BLK), F32)
    vec = jax.ShapeDtypeStruct((1, D), F32)
    return _call(
        body, (T // tm,),
        [_row_spec(tm, D), zspec(ZU), zspec(ZV), zspec(ZGA), zspec(ZGB),
         _row_spec(tm, D), _row_spec(tm, D), _row_spec(tm, D),
         _full_spec((1, D)), _full_spec((1, D)), _full_spec((1, D)),
         _full_spec((SGU_GROUPS, BLK, BLK)), _full_spec((SGU_GROUPS, BLK, BLK)),
         _whole_spec(wa), _whole_spec(ws), _whole_spec(wo)],
        (dxo, z, z, z, z, o, pa, pb, post_g, ln_g, ln_b, sgu_w, sgu_bb, wa, ws, wo),
        [_row_spec(tm, mid)] + [_row_spec(tm, D)] * 5 + [_full_spec((1, D))] * 3
        + [_full_spec((SGU_GROUPS, BLK, BLK))] * 2,
        [jax.ShapeDtypeStruct((T, mid), BF16), act, act, act, act, act, vec, vec, vec, grp, grp],
        "mix_bwd_out", copies=copies)


def mix_in_bwd(dxo, x, dz, pre_g, win):
    T = x.shape[0]
    tm = min(TM_MIX, T)

    def body(dxo_ref, x_ref, dz_ref, pg_ref, w_ref, dx_ref, hb_ref, dpg_ref):
        @pl.when(pl.program_id(0) == 0)
        def _():
            dpg_ref[...] = jnp.zeros_like(dpg_ref)

        pg = pg_ref[...]
        h, nx, rx = _rms(x_ref[...], pg)
        hb_ref[...] = h.astype(BF16)
        dh = jnp.zeros((tm, D), F32)
        for c0, cw in WIN_CHUNKS:
            dh = dh + _dot(dz_ref[:, c0:c0 + cw], w_ref[c0:c0 + cw, :])
        dpg_ref[...] += _colsum(dh * nx)
        dx_ref[...] = dxo_ref[...] + _rms_bwd(nx, rx, pg, dh)

    return pl.pallas_call(
        body, grid=(T // tm,),
        in_specs=[_row_spec(tm, D), _row_spec(tm, D), _row_spec(tm, IN_WIDTH), _full_spec((1, D)), _whole_spec(win)],
        out_specs=[_row_spec(tm, D), _row_spec(tm, D), _full_spec((1, D))],
        out_shape=[jax.ShapeDtypeStruct((T, D), F32), jax.ShapeDtypeStruct((T, D), BF16),
                   jax.ShapeDtypeStruct((1, D), F32)],
        name="mix_in_bwd", compiler_params=_params(1),
    )(dxo, x, dz, pre_g, win)


def wgrad(wide, narrow, name, tmo, copies=None, tk=TK_WGRAD):
    T, N = wide.shape
    M = narrow.shape[1]
    tk = min(tk, T)
    nk = T // tk
    chunk = max(c for c in range(BLK, 1792 + 1, BLK) if N % c == 0)

    def body(a_ref, b_ref, o_ref, acc_ref):
        k = pl.program_id(1)

        @pl.when(k == 0)
        def _():
            acc_ref[...] = jnp.zeros_like(acc_ref)

        acc_ref[...] += _dot_tn(a_ref[...], b_ref[...])

        @pl.when(k == nk - 1)
        def _():
            for c0 in range(0, N, chunk):
                o_ref[c0:c0 + chunk, :] = acc_ref[:, c0:c0 + chunk].T.astype(BF16)

    (out,), got = _call(
        body, (M // tmo, nk),
        [pl.BlockSpec((tk, tmo), lambda m, k: (k, m)), pl.BlockSpec((tk, N), lambda m, k: (k, 0))], (narrow, wide),
        [pl.BlockSpec((N, tmo), lambda m, k: (0, m))], [jax.ShapeDtypeStruct((N, M), BF16)], name,
        scratch=[pltpu.VMEM((tmo, N), F32)], copies=copies)
    return out if copies is None else (out, got)


def adamw_sum(parts, w, m, v, name):
    layers, rows, cols = w.shape
    tr = _row_tile(rows, TR_ADAM)
    nr = rows // tr
    c1 = 1.0 - ADAM_B1 ** ADAM_STEP
    c2 = 1.0 - ADAM_B2 ** ADAM_STEP

    def body(*refs):
        p_refs = refs[:layers]
        w_ref, m_ref, v_ref, g_ref, d_ref, nm_ref, nv_ref = refs[layers:]
        for k in range(layers):
            @pl.when(pl.program_id(0) == k)
            def _(p_ref=p_refs[k]):
                g = p_ref[0].astype(F32)
                for j in range(1, N_DEV):
                    g = g + p_ref[j].astype(F32)
                nm = ADAM_B1 * m_ref[...] + (1.0 - ADAM_B1) * g
                nv = ADAM_B2 * v_ref[...] + (1.0 - ADAM_B2) * (g * g)
                g_ref[...] = g
                nm_ref[...] = nm
                nv_ref[...] = nv
                d_ref[...] = -ADAM_LR * ((nm * (1.0 / c1)) / (jnp.sqrt(nv * (1.0 / c2)) + ADAM_EPS)
                                         + ADAM_WD * w_ref[...])

    def part_spec(k):
        return pl.BlockSpec((N_DEV, tr, cols),
                            lambda l, i: (0, jnp.where(l < k, 0, jnp.where(l == k, i, nr - 1)), 0))

    spec = pl.BlockSpec((None, tr, cols), lambda l, i: (l, i, 0))
    out = jax.ShapeDtypeStruct((layers, rows, cols), F32)
    return pl.pallas_call(
        body, grid=(layers, nr),
        in_specs=[part_spec(k) for k in range(layers)] + [spec, spec, spec],
        out_specs=[spec] * 4, out_shape=[out] * 4, name=name, compiler_params=_params(2),
    )(*parts, w, m, v)


def _pack_small(p):
    layers = []
    for l in range(DEPTH):
        rows = [p[n][l].reshape(1, D) for n in SMALL_VEC]
        rows.append(p["sgu_b"][l].reshape(1, D))
        rows.append(jnp.pad(p["attn_sinks"][l].reshape(1, N_Q_HEADS), ((0, 0), (0, D - N_Q_HEADS))))
        rows.append(jnp.zeros((SMALL_ROWS - len(SMALL_VEC) - 2, D), F32))
        layers.append(jnp.concatenate(rows, axis=0))
    return jnp.concatenate(layers, axis=0)


def _unpack_small(packed):
    a = packed.reshape(DEPTH, SMALL_ROWS, D)
    out = {n: a[:, i, :] for i, n in enumerate(SMALL_VEC)}
    k = len(SMALL_VEC)
    out["sgu_b"] = a[:, k, :].reshape(DEPTH, SGU_GROUPS, BLK)
    out["attn_sinks"] = a[:, k + 1, :N_Q_HEADS]
    return out


WEIGHT_NAMES = ("ffn1_pre_g", "ffn1_w1", "ffn1_w2", "ffn1_post_g", "mix_pre_g", "w_in", "attn_sinks", "sgu_ln_g",
                "sgu_ln_b", "sgu_w", "sgu_b", "w_attn_branch", "w_sgu_branch", "w_out", "mix_post_g", "ffn2_pre_g",
                "ffn2_w1", "ffn2_w2", "ffn2_post_g")
COL_SHARDED = ("ffn1_w1", "ffn2_w1", "w_in")
ROW_SHARDED = ("ffn1_w2", "ffn2_w2", "w_attn_branch", "w_sgu_branch", "w_out")
MATRICES = COL_SHARDED + ROW_SHARDED


def kernel(x, ffn1_pre_g, ffn1_w1, ffn1_w2, ffn1_post_g, mix_pre_g, w_in, attn_sinks, sgu_ln_g, sgu_ln_b, sgu_w, sgu_b, w_attn_branch, w_sgu_branch, w_out, mix_post_g, ffn2_pre_g, ffn2_w1, ffn2_w2, ffn2_post_g, loss_target, m_ffn1_pre_g, m_ffn1_w1, m_ffn1_w2, m_ffn1_post_g, m_mix_pre_g, m_w_in, m_attn_sinks, m_sgu_ln_g, m_sgu_ln_b, m_sgu_w, m_sgu_b, m_w_attn_branch, m_w_sgu_branch, m_w_out, m_mix_post_g, m_ffn2_pre_g, m_ffn2_w1, m_ffn2_w2, m_ffn2_post_g, v_ffn1_pre_g, v_ffn1_w1, v_ffn1_w2, v_ffn1_post_g, v_mix_pre_g, v_w_in, v_attn_sinks, v_sgu_ln_g, v_sgu_ln_b, v_sgu_w, v_sgu_b, v_w_attn_branch, v_w_sgu_branch, v_w_out, v_mix_post_g, v_ffn2_pre_g, v_ffn2_w1, v_ffn2_w2, v_ffn2_post_g):
    w = dict(zip(WEIGHT_NAMES, (ffn1_pre_g, ffn1_w1, ffn1_w2, ffn1_post_g, mix_pre_g, w_in, attn_sinks, sgu_ln_g,
                                sgu_ln_b, sgu_w, sgu_b, w_attn_branch, w_sgu_branch, w_out, mix_post_g, ffn2_pre_g,
                                ffn2_w1, ffn2_w2, ffn2_post_g)))
    mom = dict(zip(WEIGHT_NAMES, (m_ffn1_pre_g, m_ffn1_w1, m_ffn1_w2, m_ffn1_post_g, m_mix_pre_g, m_w_in,
                                  m_attn_sinks, m_sgu_ln_g, m_sgu_ln_b, m_sgu_w, m_sgu_b, m_w_attn_branch,
                                  m_w_sgu_branch, m_w_out, m_mix_post_g, m_ffn2_pre_g, m_ffn2_w1, m_ffn2_w2,
                                  m_ffn2_post_g)))
    var = dict(zip(WEIGHT_NAMES, (v_ffn1_pre_g, v_ffn1_w1, v_ffn1_w2, v_ffn1_post_g, v_mix_pre_g, v_w_in,
                                  v_attn_sinks, v_sgu_ln_g, v_sgu_ln_b, v_sgu_w, v_sgu_b, v_w_attn_branch,
                                  v_w_sgu_branch, v_w_out, v_mix_post_g, v_ffn2_pre_g, v_ffn2_w1, v_ffn2_w2,
                                  v_ffn2_post_g)))
    T = x.shape[1]
    xs = x.reshape(T, D)
    target = loss_target.reshape(T, D)

    for n in COL_SHARDED:
        w[n], mom[n], var[n] = (jnp.swapaxes(t[n], 1, 2) for t in (w, mom, var))

    shard = {n: [cast_layer(w[n], l) for l in range(DEPTH)] for n in MATRICES}
    weights = [{} for _ in range(DEPTH)]
    ffn1, ffn2, squares = ("ffn1_w1", "ffn1_w2"), ("ffn2_w1", "ffn2_w2"), ("w_attn_branch", "w_sgu_branch", "w_out")
    for n, g in zip(ffn1, all_gather_weights([shard[n][0] for n in ffn1])):
        weights[0][n] = g.reshape(-1, D)

    def fetch(groups):
        jobs = [("all", shard[n][l]) for l, names in groups if l < DEPTH for n in names]
        return PeerCopies(jobs) if jobs else None

    def landed(groups, arrivals):
        slots = [(l, n) for l, names in groups if l < DEPTH for n in names]
        for (l, n), g in zip(slots, arrivals):
            weights[l][n] = g.reshape(-1, D)

    vec = lambda n, l: w[n][l].reshape(1, D)
    sgu_bb = [jnp.broadcast_to(w["sgu_b"][l][:, :, None], (SGU_GROUPS, BLK, BLK)) for l in range(DEPTH)]

    saved = []
    h = xs
    for l in range(DEPTH):
        wl = weights[l]
        ahead = lambda names: [(0, names)] if l == 0 else []
        x0 = h
        groups = [(l + 1, ffn1)] + ahead(("w_in",))
        (x1, a1, s1, y1), got = ffn_fwd(x0, vec("ffn1_pre_g", l), vec("ffn1_post_g", l), wl["ffn1_w1"], wl["ffn1_w2"],
                                        fetch(groups))
        landed(groups, got)
        groups = [(l + 1, ("w_in",))] + ahead(squares)
        (z,), got = mix_in_fwd(x1, vec("mix_pre_g", l), wl["w_in"], fetch(groups))
        landed(groups, got)
        groups = ahead(("ffn2_w1",))
        (ya,), got = attn_fwd(z, w["attn_sinks"][l], fetch(groups))
        landed(groups, got)
        groups = [(l + 1, squares)] + ahead(("ffn2_w2",))
        (x2, ysg, pa, pb, o), got = mix_fwd_out(
            x1, z, ya, vec("mix_post_g", l), vec("sgu_ln_g", l), vec("sgu_ln_b", l), w["sgu_w"][l], sgu_bb[l],
            wl["w_attn_branch"], wl["w_sgu_branch"], wl["w_out"], fetch(groups))
        landed(groups, got)
        groups = [(l + 1, ffn2)]
        last = (vec("ffn2_pre_g", l), vec("ffn2_post_g", l), wl["ffn2_w1"], wl["ffn2_w2"])
        if l < DEPTH - 1:
            (h, a2, s2, y2), got = ffn_fwd(x2, *last, fetch(groups))
            landed(groups, got)
        else:
            (a2, s2, y2, sq, dx), _ = ffn_fwd(x2, *last, target=target)
        saved.append((x0, a1, s1, y1, x1, z, ya, ysg, pa, pb, o, x2, a2, s2, y2))

    loss = lax.psum(0.5 / D * jnp.sum(sq), ("x", "y", "c"))

    def ffn_wgrads(hb, da, s, dy):
        return [("own", wgrad(da, hb, "wgrad_ffn_w1", D // 2).reshape(N_DEV, W1_SHARD, D)),
                ("own", wgrad(s, dy, "wgrad_ffn_w2", D).reshape(N_DEV, W2_SHARD, D))]

    parts = {n: [None] * DEPTH for n in MATRICES}
    small = {n: [None] * DEPTH for n in WEIGHT_NAMES if n not in MATRICES + ("sgu_w",)}
    sgu_w_parts = [None] * DEPTH
    waiting = None
    for l in reversed(range(DEPTH)):
        wl = weights[l]
        x0, a1, s1, y1, x1, z, ya, ysg, pa, pb, o, x2, a2, s2, y2 = saved[l]
        (dx, da, dy, hb, dpg, dqg), got = ffn_bwd(
            dx, x2, y2, a2, vec("ffn2_pre_g", l), vec("ffn2_post_g", l), wl["ffn2_w1"], wl["ffn2_w2"],
            PeerCopies(waiting) if waiting else None)
        if waiting:
            parts["ffn1_w1"][l + 1], parts["ffn1_w2"][l + 1] = got
        small["ffn2_pre_g"][l], small["ffn2_post_g"][l] = dpg, dqg

        (dzb, dya, mg, dob, dpa, dpb, dqg, dlg, dlb, dsw, dsb), got = mix_bwd_out(
            dx, z, o, pa, pb, vec("mix_post_g", l), vec("sgu_ln_g", l), vec("sgu_ln_b", l), w["sgu_w"][l], sgu_bb[l],
            wl["w_attn_branch"], wl["w_sgu_branch"], wl["w_out"], PeerCopies(ffn_wgrads(hb, da, s2, dy)))
        parts["ffn2_w1"][l], parts["ffn2_w2"][l] = got
        dz, dsink = attn_bwd(z, ya, dya, dzb, w["attn_sinks"][l])
        dx, hb, dpg = mix_in_bwd(dx, x1, dz, vec("mix_pre_g", l), wl["w_in"])
        small["mix_pre_g"][l], small["mix_post_g"][l] = dpg, dqg
        small["sgu_ln_g"][l], small["sgu_ln_b"][l] = dlg, dlb
        small["sgu_b"][l] = dsb[:, :, 0]
        small["attn_sinks"][l] = dsink[0, :N_Q_HEADS]
        mixer = [("own", wgrad(dz, hb, "wgrad_w_in", D // 2).reshape(N_DEV, WIN_SHARD, D))]
        mixer += [("own", wgrad(act, cot, "wgrad_square", D, tk=2 * TK_WGRAD).reshape(N_DEV, SQ_SHARD, D))
                  for act, cot in ((ya, dpa), (ysg, dpb), (mg, dob))]
        mixer.append(("all", dsw.reshape(SGU_GROUPS * BLK, BLK)))
        if l == 0:
            small["ffn1_pre_g"][0] = small["ffn1_post_g"][0] = jnp.zeros((1, D), F32)
            mixer.append(("all", _pack_small({n: jnp.stack(v, axis=0) for n, v in small.items()})))

        (dx, da, dy, hb, dpg, dqg), got = ffn_bwd(
            dx, x0, y1, a1, vec("ffn1_pre_g", l), vec("ffn1_post_g", l), wl["ffn1_w1"], wl["ffn1_w2"],
            PeerCopies(mixer))
        parts["w_in"][l], parts["w_attn_branch"][l], parts["w_sgu_branch"][l], parts["w_out"][l] = got[:4]
        sgu_w_parts[l] = got[4]
        if l > 0:
            small["ffn1_pre_g"][l], small["ffn1_post_g"][l] = dpg, dqg
            waiting = ffn_wgrads(hb, da, s1, dy)
    grad_x = dx.reshape(x.shape)
    small_parts = got[5]

    def late_rows(pre, post):
        return jnp.concatenate([pre.reshape(1, D), post.reshape(1, D), jnp.zeros((LATE_ROWS - 2, D), F32)], axis=0)

    g_w1 = wgrad(da, hb, "wgrad_ffn_w1", D // 2).reshape(N_DEV, W1_SHARD, D)
    g_w2, got = wgrad(s1, dy, "wgrad_ffn_w2", D, PeerCopies([("own", g_w1)]))
    parts["ffn1_w1"][0] = got[0]
    parts["ffn1_w2"][0], late_parts = exchange_last([("own", g_w2.reshape(N_DEV, W2_SHARD, D)),
                                                     ("all", late_rows(dpg, dqg))])

    grads, deltas, new_m, new_v = {}, {}, {}, {}
    for n in MATRICES:
        grads[n], deltas[n], new_m[n], new_v[n] = adamw_sum(parts[n], w[n], mom[n], var[n], "adamw_" + n)
    for n in COL_SHARDED:
        for out in (grads, deltas, new_m, new_v):
            out[n] = jnp.swapaxes(out[n], 1, 2)
    flat = lambda t: t["sgu_w"].reshape(DEPTH, SGU_GROUPS * BLK, BLK)
    for out, r in zip((grads, deltas, new_m, new_v),
                      adamw_sum(sgu_w_parts, flat(w), flat(mom), flat(var), "adamw_sgu_w")):
        out["sgu_w"] = r.reshape(w["sgu_w"].shape)
    res = adamw_sum([small_parts], _pack_small(w)[None], _pack_small(mom)[None], _pack_small(var)[None],
                    "adamw_small")
    late = adamw_sum([late_parts], *[late_rows(t["ffn1_pre_g"][0], t["ffn1_post_g"][0])[None] for t in (w, mom, var)],
                     "adamw_late")
    for out, packed, late_out in zip((grads, deltas, new_m, new_v), res, late):
        out.update(_unpack_small(packed[0]))
        for row, n in enumerate(("ffn1_pre_g", "ffn1_post_g")):
            out[n] = jnp.concatenate([late_out[0, row:row + 1], out[n][1:]], axis=0)

    return (loss, grad_x, *[grads[n] for n in WEIGHT_NAMES], *[deltas[n] for n in WEIGHT_NAMES],
            *[new_m[n] for n in WEIGHT_NAMES], *[new_v[n] for n in WEIGHT_NAMES])
```

```python
import math

import jax
import jax.numpy as jnp
from jax import lax
from jax.experimental import pallas as pl
from jax.experimental.pallas import tpu as pltpu

F32 = jnp.float32
BF16 = jnp.bfloat16

N_DEV = 8
D = 1024
FF = 2816
DEPTH = 4
HEAD_DIM = 64
N_Q_HEADS = 16
N_KV_HEADS = 2
Q_PER_KV = N_Q_HEADS // N_KV_HEADS
BLK = 128
SGU_GROUPS = 8
IN_WIDTH = 5376
W1_SHARD = 2 * FF // N_DEV
WIN_SHARD = IN_WIDTH // N_DEV
W2_SHARD = FF // N_DEV
SQ_SHARD = D // N_DEV

RMS_EPS = 1e-6
LN_EPS = 1e-5
MASK_VALUE = -1e30
ATTN_SCALE = 1.0 / math.sqrt(HEAD_DIM)

ADAM_LR = 0.001
ADAM_B1 = 0.9
ADAM_B2 = 0.999
ADAM_EPS = 1e-08
ADAM_WD = 0.01
ADAM_STEP = 10

VMEM_LIMIT_V7X = 56 * 1024 * 1024

WIN_SEGMENTS = ((0, 1024, 0), (1024, 256, 5120), (1280, 1024, 1024), (2304, 1024, 2048), (3328, 1024, 3072),
                (4352, 1024, 4096))
ZQ, ZU, ZV, ZGA, ZGB, ZKV = 0, 1024, 2048, 3072, 4096, 5120
DZ_Q, DZ_KV, DZ_MID = 0, 1024, 1280

FF_CHUNKS = ((0, 1024), (1024, 1024), (2048, 768))
WIN_CHUNKS = ((0, 1792), (1792, 1792), (3584, 1792))

SMALL_ROWS = 16
LATE_ROWS = 16
SMALL_VEC = ("ffn1_pre_g", "ffn1_post_g", "mix_pre_g", "mix_post_g", "ffn2_pre_g", "ffn2_post_g",
             "sgu_ln_g", "sgu_ln_b")

TM_FFN_FWD = 512
TM_FFN_BWD = 256
TM_MIX_IN = 1024
TM_MIX = 512
TM_MIX_BWD = 256
TQ_ATTN = 512
TK_WGRAD = 1024
TR_ADAM = 256


def _params(n_grid, vmem=VMEM_LIMIT_V7X):
    return pltpu.CompilerParams(dimension_semantics=("arbitrary",) * n_grid, vmem_limit_bytes=vmem)


def _dot(a, b):
    return jnp.dot(a, b, preferred_element_type=F32)


def _dot_nt(a, b):
    return lax.dot_general(a, b, (((1,), (1,)), ((), ())), preferred_element_type=F32)


def _dot_tn(a, b):
    return lax.dot_general(a, b, (((0,), (0,)), ((), ())), preferred_element_type=F32)


def _rms(x, g):
    r = lax.rsqrt(jnp.mean(x * x, axis=-1, keepdims=True) + RMS_EPS)
    n = x * r
    return n * g, n, r


def _rms_bwd(n, r, g, dy):
    dn = dy * g
    return r * (dn - n * jnp.mean(dn * n, axis=-1, keepdims=True))


def _colsum(v):
    return jnp.sum(v, axis=0, keepdims=True)


def _sigmoid(v):
    return 0.5 * jnp.tanh(0.5 * v) + 0.5


def _gelu_parts(v):
    cdf = 0.5 * lax.erf(v * (1.0 / math.sqrt(2.0))) + 0.5
    return cdf, jnp.exp2(v * v * (-0.5 / math.log(2.0))) * (1.0 / math.sqrt(2.0 * math.pi))


def _row_tile(rows, cap):
    return max(t for t in range(16, min(rows, cap) + 1, 16) if rows % t == 0)


def _row_spec(tm, width, col_block=0):
    return pl.BlockSpec((tm, width), lambda i, cb=col_block: (i, cb))


def _full_spec(shape):
    nd = len(shape)
    return pl.BlockSpec(tuple(shape), lambda *_: (0,) * nd)


def _whole_spec(arr):
    nd = arr.ndim
    return pl.BlockSpec(tuple(arr.shape), lambda *_: (0,) * nd, pipeline_mode=pl.Buffered(1))


HBM_SPEC = pl.BlockSpec(memory_space=pltpu.HBM)
MESH_ID = pl.DeviceIdType.MESH
RELATIONS = tuple((rx, ry, rc) for rx in (0, 1) for ry in (0, 1) for rc in (0, 1))[1:]


class PeerCopies:
    def __init__(self, jobs):
        self.kinds = [k for k, _ in jobs]
        self.arrays = [a for _, a in jobs]
        self.n = len(jobs)
        self.out_shape = [jax.ShapeDtypeStruct((N_DEV,) + a.shape if k == "all" else a.shape, a.dtype)
                          for k, a in jobs]
        self.scratch = [pltpu.SemaphoreType.DMA((7 * self.n,)), pltpu.SemaphoreType.DMA((7 * self.n,)),
                        pltpu.SemaphoreType.DMA((self.n,))]

    def _copies(self, ins, outs, sems, arriving):
        send_sems, recv_sems, local_sems = sems
        x, y, c = lax.axis_index("x"), lax.axis_index("y"), lax.axis_index("c")
        me = 4 * x + 2 * y + c
        src = lambda a, d: ins[a] if self.kinds[a] == "all" else ins[a].at[d]
        if not arriving:
            local = [pltpu.make_async_copy(src(a, me), outs[a].at[me], local_sems.at[a]) for a in range(self.n)]
        remote = []
        for k, (rx, ry, rc) in enumerate(RELATIONS):
            tx, ty, tc = (1 - x if rx else x), (1 - y if ry else y), (1 - c if rc else c)
            peer = 4 * tx + 2 * ty + tc
            for a in range(self.n):
                from_slot, to_slot = (me, peer) if arriving else (peer, me)
                remote.append(pltpu.make_async_remote_copy(
                    src_ref=src(a, from_slot), dst_ref=outs[a].at[to_slot],
                    send_sem=send_sems.at[a * 7 + k], recv_sem=recv_sems.at[a * 7 + k],
                    device_id=(tx, ty, tc), device_id_type=MESH_ID))
        return remote if arriving else (local, remote)

    def start(self, ins, outs, sems):
        local, sends = self._copies(ins, outs, sems, False)
        for cp in local + sends:
            cp.start()

    def wait(self, ins, outs, sems):
        for cp in self._copies(ins, outs, sems, True):
            cp.wait_recv()
        local, sends = self._copies(ins, outs, sems, False)
        for cp in sends:
            cp.wait_send()
        for cp in local:
            cp.wait()


def _call(body, grid, in_specs, args, out_specs, out_shape, name, scratch=(), copies=None):
    if copies is None:
        outs = pl.pallas_call(body, grid=grid, in_specs=list(in_specs), out_specs=list(out_specs),
                              out_shape=list(out_shape), scratch_shapes=list(scratch), name=name,
                              compiler_params=_params(len(grid)))(*args)
        return outs, []
    n_in, n_out, n_scr, nc = len(in_specs), len(out_specs), len(scratch), copies.n

    def at_step(steps):
        hit = pl.program_id(0) == steps[0]
        for axis in range(1, len(grid)):
            hit = jnp.logical_and(hit, pl.program_id(axis) == steps[axis])
        return hit

    def hosted(*refs):
        ins, refs = refs[:n_in], refs[n_in:]
        c_in, refs = refs[:nc], refs[nc:]
        outs, refs = refs[:n_out], refs[n_out:]
        c_out, refs = refs[:nc], refs[nc:]
        scr, sems = refs[:n_scr], refs[n_scr:]

        @pl.when(at_step([0] * len(grid)))
        def _():
            copies.start(c_in, c_out, sems)

        body(*ins, *outs, *scr)

        @pl.when(at_step([g - 1 for g in grid]))
        def _():
            copies.wait(c_in, c_out, sems)

    outs = pl.pallas_call(hosted, grid=grid, in_specs=list(in_specs) + [HBM_SPEC] * nc,
                          out_specs=list(out_specs) + [HBM_SPEC] * nc,
                          out_shape=list(out_shape) + copies.out_shape,
                          scratch_shapes=list(scratch) + copies.scratch, name=name,
                          compiler_params=_params(len(grid)))(*args, *copies.arrays)
    return outs[:n_out], outs[n_out:]


def cast_layer(ws, l):
    n = len(ws)

    def body(*refs):
        for w_ref, o_ref in zip(refs[:n], refs[n:]):
            o_ref[...] = w_ref[...].astype(BF16)

    halves = [w.shape[1] // 2 for w in ws]
    return pl.pallas_call(
        body, grid=(2,),
        in_specs=[pl.BlockSpec((None, h, D), lambda i: (l, i, 0)) for h in halves],
        out_specs=[_row_spec(h, D) for h in halves],
        out_shape=[jax.ShapeDtypeStruct(w.shape[1:], BF16) for w in ws],
        name="cast_layer", compiler_params=_params(1),
    )(*ws)


def all_gather_weights(shards):
    n = len(shards)

    def body(*refs):
        ins, outs = refs[:n], refs[n:2 * n]
        send_sems, recv_sems, local_sems = refs[2 * n:]
        x, y, c = lax.axis_index("x"), lax.axis_index("y"), lax.axis_index("c")
        me, sibling = (x, y, c), (x, y, 1 - c)
        chips = [(1 - x, y), (x, 1 - y), (1 - x, 1 - y)]

        def slot(a, owner):
            return outs[a].at[4 * owner[0] + 2 * owner[1] + owner[2]]

        def copy(a, k, owner, to, src=None):
            return pltpu.make_async_remote_copy(
                src_ref=slot(a, owner) if src is None else src, dst_ref=slot(a, owner),
                send_sem=send_sems.at[a * 7 + k], recv_sem=recv_sems.at[a * 7 + k],
                device_id=to, device_id_type=MESH_ID)

        mine = [pltpu.make_async_copy(ins[a], slot(a, me), local_sems.at[a]) for a in range(n)]
        for cp in mine:
            cp.start()
        first = []
        for a in range(n):
            first.append(copy(a, 0, me, sibling, src=ins[a]))
            first += [copy(a, 1 + j, me, (*chip, c), src=ins[a]) for j, chip in enumerate(chips)]
        for cp in first:
            cp.start()
        passed = []
        for j, chip in enumerate(chips):
            for a in range(n):
                copy(a, 1 + j, (*chip, c), me).wait_recv()
                fwd = copy(a, 4 + j, (*chip, c), sibling)
                fwd.start()
                passed.append(fwd)
        for a in range(n):
            copy(a, 0, sibling, me).wait_recv()
            for j, chip in enumerate(chips):
                copy(a, 4 + j, (*chip, 1 - c), me).wait_recv()
        for cp in first + passed:
            cp.wait_send()
        for cp in mine:
            cp.wait()

    out_shape = [jax.ShapeDtypeStruct((N_DEV,) + s.shape, s.dtype) for s in shards]
    return pl.pallas_call(
        body, in_specs=[HBM_SPEC] * n, out_specs=[HBM_SPEC] * n, out_shape=out_shape,
        scratch_shapes=[pltpu.SemaphoreType.DMA((7 * n,)), pltpu.SemaphoreType.DMA((7 * n,)),
                        pltpu.SemaphoreType.DMA((n,))],
        name="all_gather_weights",
    )(*shards)


def exchange_last(jobs):
    copies = PeerCopies(jobs)
    n = copies.n

    def body(*refs):
        ins, outs, sems = refs[:n], refs[n:2 * n], refs[2 * n:]
        copies.start(ins, outs, sems)
        copies.wait(ins, outs, sems)

    return pl.pallas_call(
        body, in_specs=[HBM_SPEC] * n, out_specs=[HBM_SPEC] * n, out_shape=copies.out_shape,
        scratch_shapes=copies.scratch, name="exchange_last",
    )(*copies.arrays)


def ffn_fwd(x, pre_g, post_g, w1, w2, copies=None, target=None):
    T = x.shape[0]
    tm = min(TM_FFN_FWD, T)

    def body(*refs):
        if target is None:
            x_ref, pg_ref, qg_ref, w1_ref, w2_ref, xo_ref, a_ref, s_ref, y_ref = refs
        else:
            x_ref, pg_ref, qg_ref, w1_ref, w2_ref, t_ref, a_ref, s_ref, y_ref, sq_ref, dy_ref = refs
        xv = x_ref[...]
        h, _, _ = _rms(xv, pg_ref[...])
        hb = h.astype(BF16)
        acc = jnp.zeros((tm, D), F32)
        for c0, cw in FF_CHUNKS:
            g = _dot_nt(hb, w1_ref[c0:c0 + cw, :])
            u = _dot_nt(hb, w1_ref[FF + c0:FF + c0 + cw, :])
            sg = _sigmoid(g)
            si = g * sg
            a_ref[:, c0:c0 + cw] = (u * (sg + si - si * sg)).astype(BF16)
            a_ref[:, FF + c0:FF + c0 + cw] = si.astype(BF16)
            s = (si * u).astype(BF16)
            s_ref[:, c0:c0 + cw] = s
            acc = acc + _dot(s, w2_ref[c0:c0 + cw, :])
        y_ref[...] = acc.astype(BF16)
        o, _, _ = _rms(acc, qg_ref[...])
        xo = xv + 0.5 * o
        if target is None:
            xo_ref[...] = xo
        else:
            @pl.when(pl.program_id(0) == 0)
            def _():
                sq_ref[...] = jnp.zeros_like(sq_ref)

            err = xo - t_ref[...]
            sq_ref[...] += _colsum(err * err)
            dy_ref[...] = err * (1.0 / D)

    in_specs = [_row_spec(tm, D), _full_spec((1, D)), _full_spec((1, D)), _whole_spec(w1), _whole_spec(w2)]
    saved_specs = [_row_spec(tm, 2 * FF), _row_spec(tm, FF), _row_spec(tm, D)]
    saved_shapes = [jax.ShapeDtypeStruct((T, 2 * FF), BF16), jax.ShapeDtypeStruct((T, FF), BF16),
                    jax.ShapeDtypeStruct((T, D), BF16)]
    row_f32 = jax.ShapeDtypeStruct((T, D), F32)
    if target is None:
        return _call(body, (T // tm,), in_specs, (x, pre_g, post_g, w1, w2), [_row_spec(tm, D)] + saved_specs,
                     [row_f32] + saved_shapes, "ffn_fwd", copies=copies)
    return _call(body, (T // tm,), in_specs + [_row_spec(tm, D)], (x, pre_g, post_g, w1, w2, target),
                 saved_specs + [_full_spec((1, D)), _row_spec(tm, D)],
                 saved_shapes + [jax.ShapeDtypeStruct((1, D), F32), row_f32], "ffn_fwd_loss", copies=copies)


def ffn_bwd(dxo, x, y, a, pre_g, post_g, w1, w2, copies=None):
    T = x.shape[0]
    tm = min(TM_FFN_BWD, T)

    def body(dxo_ref, x_ref, y_ref, a_ref, pg_ref, qg_ref, w1_ref, w2_ref,
             dx_ref, da_ref, dy_ref, hb_ref, dpg_ref, dqg_ref):
        @pl.when(pl.program_id(0) == 0)
        def _():
            dpg_ref[...] = jnp.zeros_like(dpg_ref)
            dqg_ref[...] = jnp.zeros_like(dqg_ref)

        dxo = dxo_ref[...]
        qg = qg_ref[...]
        _, ny, ry = _rms(y_ref[...].astype(F32), qg)
        dn = 0.5 * dxo
        dqg_ref[...] += _colsum(dn * ny)
        dyb = _rms_bwd(ny, ry, qg, dn).astype(BF16)
        dy_ref[...] = dyb
        pg = pg_ref[...]
        h, nx, rx = _rms(x_ref[...], pg)
        hb_ref[...] = h.astype(BF16)
        dh = jnp.zeros((tm, D), F32)
        for c0, cw in FF_CHUNKS:
            ds = _dot_nt(dyb, w2_ref[c0:c0 + cw, :])
            dg = (ds * a_ref[:, c0:c0 + cw].astype(F32)).astype(BF16)
            du = (ds * a_ref[:, FF + c0:FF + c0 + cw].astype(F32)).astype(BF16)
            da_ref[:, c0:c0 + cw] = dg
            da_ref[:, FF + c0:FF + c0 + cw] = du
            dh = dh + _dot(dg, w1_ref[c0:c0 + cw, :]) + _dot(du, w1_ref[FF + c0:FF + c0 + cw, :])
        dpg_ref[...] += _colsum(dh * nx)
        dx_ref[...] = dxo + _rms_bwd(nx, rx, pg, dh)

    return _call(
        body, (T // tm,),
        [_row_spec(tm, D), _row_spec(tm, D), _row_spec(tm, D), _row_spec(tm, 2 * FF),
         _full_spec((1, D)), _full_spec((1, D)), _whole_spec(w1), _whole_spec(w2)],
        (dxo, x, y, a, pre_g, post_g, w1, w2),
        [_row_spec(tm, D), _row_spec(tm, 2 * FF), _row_spec(tm, D), _row_spec(tm, D),
         _full_spec((1, D)), _full_spec((1, D))],
        [jax.ShapeDtypeStruct((T, D), F32), jax.ShapeDtypeStruct((T, 2 * FF), BF16),
         jax.ShapeDtypeStruct((T, D), BF16), jax.ShapeDtypeStruct((T, D), BF16),
         jax.ShapeDtypeStruct((1, D), F32), jax.ShapeDtypeStruct((1, D), F32)],
        "ffn_bwd", copies=copies)


def mix_in_fwd(x, pre_g, win, copies=None):
    T = x.shape[0]
    tm = min(TM_MIX_IN, T)

    def body(x_ref, pg_ref, w_ref, z_ref):
        h, _, _ = _rms(x_ref[...], pg_ref[...])
        hb = h.astype(BF16)
        for w0, n, z0 in WIN_SEGMENTS:
            z_ref[:, z0:z0 + n] = _dot_nt(hb, w_ref[w0:w0 + n, :]).astype(BF16)

    return _call(
        body, (T // tm,), [_row_spec(tm, D), _full_spec((1, D)), _whole_spec(win)], (x, pre_g, win),
        [_row_spec(tm, IN_WIDTH)], [jax.ShapeDtypeStruct((T, IN_WIDTH), BF16)], "mix_in_fwd", copies=copies)


STACK = Q_PER_KV // 2
SROWS = STACK * BLK


def _sees_own():
    qi = lax.broadcasted_iota(jnp.int32, (SROWS, BLK), 0) & (BLK - 1)
    return lax.broadcasted_iota(jnp.int32, (SROWS, BLK), 1) <= qi


def _both_blocks(picked, own):
    return jnp.concatenate([jnp.where(own, 0.0, picked), jnp.where(own, picked, 0.0)], axis=1)


def _lane_half(rows):
    return lax.broadcasted_iota(jnp.int32, (rows, BLK), 1) // HEAD_DIM


def _stack(ref, b, g):
    return jnp.concatenate([ref[b * BLK:(b + 1) * BLK, (STACK * g + j) * BLK:(STACK * g + j + 1) * BLK]
                            for j in range(STACK)], axis=0)


def _placed(pair, g, hp, fill):
    src = pair if hp == g else pltpu.roll(pair, HEAD_DIM, 1)
    return jnp.where(_lane_half(2 * BLK) == hp, src, fill).astype(BF16)


def _sink_column(sink_ref, g, hp):
    rb = lax.broadcasted_iota(jnp.int32, (SROWS, 1), 0) // BLK
    col = jnp.full((SROWS, 1), sink_ref[Q_PER_KV * g + hp], F32)
    for j in range(1, STACK):
        col = jnp.where(rb == j, sink_ref[Q_PER_KV * g + 2 * j + hp], col)
    return col


def _attn_scores(qs, kz, sink_col, own, no_previous):
    raw = _dot_nt(qs, kz)
    prev = raw[:, 0:BLK]
    if no_previous is not None:
        prev = prev + no_previous.astype(F32) * MASK_VALUE
    picked = jnp.where(own, raw[:, BLK:2 * BLK], prev)
    sink_raw = sink_col * (1.0 / ATTN_SCALE)
    m = jnp.maximum(jnp.max(picked, axis=-1, keepdims=True), sink_raw)
    factor = ATTN_SCALE / math.log(2.0)
    return jnp.exp2((picked - m) * factor), jnp.exp2((sink_raw - m) * factor)


def _kv_specs(tq, nb):
    kv_blk = ZKV // (2 * BLK)
    return [pl.BlockSpec((tq, 2 * BLK), lambda i: (i, kv_blk)),
            pl.BlockSpec((BLK, 2 * BLK), lambda i: (jnp.maximum(i * nb - 1, 0), kv_blk))]


def attn_fwd(z, sinks, copies=None):
    T = z.shape[0]
    tq = min(TQ_ATTN, T)
    nb = tq // BLK

    def body(sink_ref, q_ref, kv_ref, kvh_ref, o_ref):
        i = pl.program_id(0)
        low = _lane_half(SROWS) == 0
        own = _sees_own()
        for b in range(nb):
            kvp = kvh_ref[...] if b == 0 else kv_ref[(b - 1) * BLK:b * BLK, :]
            kv2 = jnp.concatenate([kvp, kv_ref[b * BLK:(b + 1) * BLK, :]], axis=0).astype(F32)
            no_previous = (i == 0) if b == 0 else None
            for g in range(N_KV_HEADS):
                qs = _stack(q_ref, b, g)
                r, e = [], []
                for hp in range(2):
                    p, e_sink = _attn_scores(qs, _placed(kv2[:, 0:BLK], g, hp, 0.0), _sink_column(sink_ref, g, hp),
                                             own, no_previous)
                    r.append(_dot(_both_blocks(p, own).astype(BF16), _placed(kv2[:, BLK:2 * BLK], g, hp, 1.0)))
                    e.append(e_sink)
                den = pltpu.roll(jnp.where(low, r[1], r[0]), HEAD_DIM, 1) + jnp.where(low, e[0], e[1])
                out = (jnp.where(low, r[0], r[1]) * (1.0 / den)).astype(BF16)
                for j in range(STACK):
                    o_ref[b * BLK:(b + 1) * BLK, (STACK * g + j) * BLK:(STACK * g + j + 1) * BLK] = \
                        out[j * BLK:(j + 1) * BLK, :]

    return _call(
        body, (T // tq,), [pl.BlockSpec(memory_space=pltpu.SMEM), _row_spec(tq, D)] + _kv_specs(tq, nb),
        (sinks, z, z, z), [_row_spec(tq, D)], [jax.ShapeDtypeStruct((T, D), BF16)], "attn_fwd", copies=copies)


def attn_bwd(z, ya, dya, dzb, sinks):
    T = z.shape[0]
    tq = min(TQ_ATTN, T)
    nb = tq // BLK
    nt = T // tq
    mid = ZKV - ZU

    def body(sink_ref, q_ref, kv_ref, kvh_ref, y_ref, dy_ref, dzb_ref, dz_ref, dsink_ref, acc_ref, carry_ref):
        i = pl.program_id(0)
        first_tile = i == nt - 1

        @pl.when(i == 0)
        def _():
            carry_ref[...] = jnp.zeros_like(carry_ref)
            dsink_ref[...] = jnp.zeros_like(dsink_ref)

        acc_ref[...] = jnp.zeros_like(acc_ref)
        lane = lax.broadcasted_iota(jnp.int32, (1, BLK), 1)
        dsink = jnp.zeros((1, BLK), F32)
        half = _lane_half(SROWS)
        own = _sees_own()
        for b in range(nb):
            kvp = kvh_ref[...] if b == 0 else kv_ref[(b - 1) * BLK:b * BLK, :]
            kv2 = jnp.concatenate([kvp, kv_ref[b * BLK:(b + 1) * BLK, :]], axis=0).astype(F32)
            no_previous = first_tile if b == 0 else None
            dk_groups, dv_groups = [], []
            for g in range(N_KV_HEADS):
                qs = _stack(q_ref, b, g)
                dys = _stack(dy_ref, b, g)
                dyy = dys.astype(F32) * _stack(y_ref, b, g).astype(F32)
                dq = jnp.zeros((SROWS, BLK), F32)
                ds_both, pn_both = [], []
                for hp in range(2):
                    kz = _placed(kv2[:, 0:BLK], g, hp, 0.0)
                    p, e_sink = _attn_scores(qs, kz, _sink_column(sink_ref, g, hp), own, no_previous)
                    inv = 1.0 / (jnp.sum(p, axis=-1, keepdims=True) + e_sink)
                    p = p * inv
                    delta = jnp.sum(jnp.where(half == hp, dyy, 0.0), axis=-1, keepdims=True)
                    dp = _dot_nt(dys, _placed(kv2[:, BLK:2 * BLK], g, hp, 0.0))
                    ds = p * (jnp.where(own, dp[:, BLK:2 * BLK], dp[:, 0:BLK]) - delta)
                    ds = _both_blocks(ds, own).astype(BF16)
                    pn = _both_blocks(p, own)
                    sink_term = e_sink * inv * delta
                    for j in range(STACK):
                        dsink = dsink + jnp.where(lane == Q_PER_KV * g + 2 * j + hp,
                                                  -_colsum(sink_term[j * BLK:(j + 1) * BLK, :]), 0.0)
                    dq = dq + _dot(ds, kz)
                    ds_both.append(ds)
                    pn_both.append(pn.astype(BF16))
                dk_t = _dot_tn(qs, jnp.concatenate(ds_both, axis=1))
                dv_t = _dot_tn(dys, jnp.concatenate(pn_both, axis=1))
                for t, groups in ((dk_t, dk_groups), (dv_t, dv_groups)):
                    groups.append(t[0:HEAD_DIM, 0:2 * BLK] + t[HEAD_DIM:BLK, 2 * BLK:4 * BLK])
                dqb = (dq * ATTN_SCALE).astype(BF16)
                for j in range(STACK):
                    dz_ref[b * BLK:(b + 1) * BLK, DZ_Q + (STACK * g + j) * BLK:DZ_Q + (STACK * g + j + 1) * BLK] = \
                        dqb[j * BLK:(j + 1) * BLK, :]
            acc_ref[b * BLK:(b + 2) * BLK, 0:BLK] += jnp.concatenate(dk_groups, axis=0).T * ATTN_SCALE
            acc_ref[b * BLK:(b + 2) * BLK, BLK:2 * BLK] += jnp.concatenate(dv_groups, axis=0).T
        dsink_ref[...] += dsink
        dz_ref[:, DZ_MID:IN_WIDTH] = dzb_ref[...]
        if nb > 1:
            dz_ref[0:tq - BLK, DZ_KV:DZ_MID] = acc_ref[BLK:tq, :].astype(BF16)
        dz_ref[tq - BLK:tq, DZ_KV:DZ_MID] = (acc_ref[tq:tq + BLK, :] + carry_ref[...]).astype(BF16)
        carry_ref[...] = acc_ref[0:BLK, :]

    kv_blk = ZKV // (2 * BLK)
    return pl.pallas_call(
        body, grid=(nt,),
        in_specs=[pl.BlockSpec(memory_space=pltpu.SMEM),
                  pl.BlockSpec((tq, D), lambda i: (nt - 1 - i, 0)),
                  pl.BlockSpec((tq, 2 * BLK), lambda i: (nt - 1 - i, kv_blk)),
                  pl.BlockSpec((BLK, 2 * BLK), lambda i: (jnp.maximum((nt - 1 - i) * nb - 1, 0), kv_blk)),
                  pl.BlockSpec((tq, D), lambda i: (nt - 1 - i, 0)),
                  pl.BlockSpec((tq, D), lambda i: (nt - 1 - i, 0)),
                  pl.BlockSpec((tq, mid), lambda i: (nt - 1 - i, 0))],
        out_specs=[pl.BlockSpec((tq, IN_WIDTH), lambda i: (nt - 1 - i, 0)), _full_spec((1, BLK))],
        out_shape=[jax.ShapeDtypeStruct((T, IN_WIDTH), BF16), jax.ShapeDtypeStruct((1, BLK), F32)],
        scratch_shapes=[pltpu.VMEM((tq + BLK, 2 * BLK), F32), pltpu.VMEM((BLK, 2 * BLK), F32)],
        name="attn_bwd", compiler_params=_params(1),
    )(sinks, z, z, z, ya, dya, dzb)


def _to_group_lanes(v, g, nch):
    return jnp.concatenate([v[n * BLK:(n + 1) * BLK, g * BLK:(g + 1) * BLK] for n in range(nch)], axis=1)


def _from_group_lanes(per_group, nch):
    rows = [jnp.concatenate([per_group[g][:, n * BLK:(n + 1) * BLK] for g in range(SGU_GROUPS)], axis=1)
            for n in range(nch)]
    return jnp.concatenate(rows, axis=0)


def _tril_bf16(w_ref, g):
    t = lax.broadcasted_iota(jnp.int32, (BLK, BLK), 0)
    s = lax.broadcasted_iota(jnp.int32, (BLK, BLK), 1)
    return jnp.where(t >= s, w_ref[g], 0.0).astype(BF16)


def _sgu_norm(v_s, ln_g, ln_b):
    cdf, pdf = _gelu_parts(v_s)
    gv = v_s * cdf
    xc = gv - jnp.mean(gv, axis=-1, keepdims=True)
    rstd = lax.rsqrt(jnp.mean(xc * xc, axis=-1, keepdims=True) + LN_EPS)
    nhat = xc * rstd
    return nhat * ln_g + ln_b, nhat, rstd, cdf + v_s * pdf


def _sgu_gate(vn, w_ref, bb_ref, nch):
    vnb = vn.astype(BF16)
    return _from_group_lanes(
        [_dot(_tril_bf16(w_ref, g), _to_group_lanes(vnb, g, nch)) + jnp.tile(bb_ref[g], (1, nch))
         for g in range(SGU_GROUPS)], nch)


def mix_fwd_out(x, z, ya, post_g, ln_g, ln_b, sgu_w, sgu_bb, wa, ws, wo, copies=None):
    T = x.shape[0]
    tm = min(TM_MIX, T)
    nch = tm // BLK

    def body(x_ref, us_ref, vs_ref, ga_ref, gb_ref, ya_ref, qg_ref, lg_ref, lb_ref, w_ref, bb_ref,
             wa_ref, ws_ref, wo_ref, xo_ref, ysg_ref, pa_ref, pb_ref, o_ref):
        vn, _, _, _ = _sgu_norm(vs_ref[...].astype(F32), lg_ref[...], lb_ref[...])
        gate = _sgu_gate(vn, w_ref, bb_ref, nch)
        us = us_ref[...].astype(F32)
        cdf, _ = _gelu_parts(us)
        ysg = (us * cdf * gate).astype(BF16)
        ysg_ref[...] = ysg
        pa = _dot(ya_ref[...], wa_ref[...])
        pb = _dot(ysg, ws_ref[...])
        pa_ref[...] = pa.astype(BF16)
        pb_ref[...] = pb.astype(BF16)
        merged = _sigmoid(ga_ref[...].astype(F32)) * pa + _sigmoid(gb_ref[...].astype(F32)) * pb
        o = _dot(merged.astype(BF16), wo_ref[...])
        o_ref[...] = o.astype(BF16)
        on, _, _ = _rms(o, qg_ref[...])
        xo_ref[...] = x_ref[...] + on

    zspec = lambda start: _row_spec(tm, D, start // D)
    act = jax.ShapeDtypeStruct((T, D), BF16)
    return _call(
        body, (T // tm,),
        [_row_spec(tm, D), zspec(ZU), zspec(ZV), zspec(ZGA), zspec(ZGB), _row_spec(tm, D),
         _full_spec((1, D)), _full_spec((1, D)), _full_spec((1, D)),
         _full_spec((SGU_GROUPS, BLK, BLK)), _full_spec((SGU_GROUPS, BLK, BLK)),
         _whole_spec(wa), _whole_spec(ws), _whole_spec(wo)],
        (x, z, z, z, z, ya, post_g, ln_g, ln_b, sgu_w, sgu_bb, wa, ws, wo),
        [_row_spec(tm, D)] * 5, [jax.ShapeDtypeStruct((T, D), F32), act, act, act, act],
        "mix_fwd_out", copies=copies)


def mix_bwd_out(dxo, z, o, pa, pb, post_g, ln_g, ln_b, sgu_w, sgu_bb, wa, ws, wo, copies=None):
    T = dxo.shape[0]
    tm = min(TM_MIX_BWD, T)
    nch = tm // BLK
    mid = ZKV - ZU

    def body(dxo_ref, us_ref, vs_ref, ga_ref, gb_ref, o_ref, pa_ref, pb_ref, qg_ref, lg_ref, lb_ref, w_ref, bb_ref,
             wa_ref, ws_ref, wo_ref,
             dzb_ref, dya_ref, mg_ref, do_ref, dpa_ref, dpb_ref, dqg_ref, dlg_ref, dlb_ref, dw_ref, dbb_ref):
        @pl.when(pl.program_id(0) == 0)
        def _():
            for r in (dqg_ref, dlg_ref, dlb_ref, dw_ref, dbb_ref):
                r[...] = jnp.zeros_like(r)

        qg = qg_ref[...]
        dxo = dxo_ref[...]
        _, no, ro = _rms(o_ref[...].astype(F32), qg)
        dqg_ref[...] += _colsum(dxo * no)
        dob = _rms_bwd(no, ro, qg, dxo).astype(BF16)
        do_ref[...] = dob
        dmerged = _dot_nt(dob, wo_ref[...])
        sa = _sigmoid(ga_ref[...].astype(F32))
        sb = _sigmoid(gb_ref[...].astype(F32))
        pa = pa_ref[...].astype(F32)
        pb = pb_ref[...].astype(F32)
        ta = sa * pa
        tb = sb * pb
        mg_ref[...] = (ta + tb).astype(BF16)
        dpa = (dmerged * sa).astype(BF16)
        dpb = (dmerged * sb).astype(BF16)
        dpa_ref[...] = dpa
        dpb_ref[...] = dpb
        dzb_ref[:, ZGA - ZU:ZGA - ZU + D] = (dmerged * (ta - ta * sa)).astype(BF16)
        dzb_ref[:, ZGB - ZU:ZGB - ZU + D] = (dmerged * (tb - tb * sb)).astype(BF16)
        dya_ref[...] = _dot_nt(dpa, wa_ref[...]).astype(BF16)
        dysg = _dot_nt(dpb, ws_ref[...])

        lg = lg_ref[...]
        vn, nhat, rstd, dgelu_v = _sgu_norm(vs_ref[...].astype(F32), lg, lb_ref[...])
        gate = _sgu_gate(vn, w_ref, bb_ref, nch)
        us = us_ref[...].astype(F32)
        cdf, pdf = _gelu_parts(us)
        dzb_ref[:, 0:D] = (dysg * gate * (cdf + us * pdf)).astype(BF16)
        dgate = (dysg * (us * cdf)).astype(BF16)
        vnb = vn.astype(BF16)
        t = lax.broadcasted_iota(jnp.int32, (BLK, BLK), 0)
        s = lax.broadcasted_iota(jnp.int32, (BLK, BLK), 1)
        dvn_groups = []
        for g in range(SGU_GROUPS):
            dgl = _to_group_lanes(dgate, g, nch)
            dbb_ref[g] += jnp.broadcast_to(jnp.sum(dgl.astype(F32), axis=-1, keepdims=True), (BLK, BLK))
            dw_ref[g] += jnp.where(t >= s, _dot_nt(dgl, _to_group_lanes(vnb, g, nch)), 0.0)
            dvn_groups.append(_dot_tn(_tril_bf16(w_ref, g), dgl))
        dvn = _from_group_lanes(dvn_groups, nch)
        dlg_ref[...] += _colsum(dvn * nhat)
        dlb_ref[...] += _colsum(dvn)
        dnh = dvn * lg
        dgv = rstd * (dnh - jnp.mean(dnh, axis=-1, keepdims=True) - nhat * jnp.mean(dnh * nhat, axis=-1, keepdims=True))
        dzb_ref[:, ZV - ZU:ZV - ZU + D] = (dgv * dgelu_v).astype(BF16)

    zspec = lambda start: _row_spec(tm, D, start // D)
    act = jax.ShapeDtypeStruct((T, D), BF16)
    grp = jax.ShapeDtypeStruct((SGU_GROUPS, BLK, BLK), F32)
    vec = jax.ShapeDtypeStruct((1, D), F32)
    return _call(
        body, (T // tm,),
        [_row_spec(tm, D), zspec(ZU), zspec(ZV), zspec(ZGA), zspec(ZGB),
         _row_spec(tm, D), _row_spec(tm, D), _row_spec(tm, D),
         _full_spec((1, D)), _full_spec((1, D)), _full_spec((1, D)),
         _full_spec((SGU_GROUPS, BLK, BLK)), _full_spec((SGU_GROUPS, BLK, BLK)),
         _whole_spec(wa), _whole_spec(ws), _whole_spec(wo)],
        (dxo, z, z, z, z, o, pa, pb, post_g, ln_g, ln_b, sgu_w, sgu_bb, wa, ws, wo),
        [_row_spec(tm, mid)] + [_row_spec(tm, D)] * 5 + [_full_spec((1, D))] * 3
        + [_full_spec((SGU_GROUPS, BLK, BLK))] * 2,
        [jax.ShapeDtypeStruct((T, mid), BF16), act, act, act, act, act, vec, vec, vec, grp, grp],
        "mix_bwd_out", copies=copies)


def mix_in_bwd(dxo, x, dz, pre_g, win):
    T = x.shape[0]
    tm = min(TM_MIX, T)

    def body(dxo_ref, x_ref, dz_ref, pg_ref, w_ref, dx_ref, hb_ref, dpg_ref):
        @pl.when(pl.program_id(0) == 0)
        def _():
            dpg_ref[...] = jnp.zeros_like(dpg_ref)

        pg = pg_ref[...]
        h, nx, rx = _rms(x_ref[...], pg)
        hb_ref[...] = h.astype(BF16)
        dh = jnp.zeros((tm, D), F32)
        for c0, cw in WIN_CHUNKS:
            dh = dh + _dot(dz_ref[:, c0:c0 + cw], w_ref[c0:c0 + cw, :])
        dpg_ref[...] += _colsum(dh * nx)
        dx_ref[...] = dxo_ref[...] + _rms_bwd(nx, rx, pg, dh)

    return pl.pallas_call(
        body, grid=(T // tm,),
        in_specs=[_row_spec(tm, D), _row_spec(tm, D), _row_spec(tm, IN_WIDTH), _full_spec((1, D)), _whole_spec(win)],
        out_specs=[_row_spec(tm, D), _row_spec(tm, D), _full_spec((1, D))],
        out_shape=[jax.ShapeDtypeStruct((T, D), F32), jax.ShapeDtypeStruct((T, D), BF16),
                   jax.ShapeDtypeStruct((1, D), F32)],
        name="mix_in_bwd", compiler_params=_params(1),
    )(dxo, x, dz, pre_g, win)


def wgrad(wide, narrow, name, tmo, copies=None, tk=TK_WGRAD):
    T, N = wide.shape
    M = narrow.shape[1]
    tk = min(tk, T)
    nk = T // tk
    chunk = max(c for c in range(BLK, 1792 + 1, BLK) if N % c == 0)

    def body(a_ref, b_ref, o_ref, acc_ref):
        k = pl.program_id(1)

        @pl.when(k == 0)
        def _():
            acc_ref[...] = jnp.zeros_like(acc_ref)

        acc_ref[...] += _dot_tn(a_ref[...], b_ref[...])

        @pl.when(k == nk - 1)
        def _():
            for c0 in range(0, N, chunk):
                o_ref[c0:c0 + chunk, :] = acc_ref[:, c0:c0 + chunk].T.astype(BF16)

    (out,), got = _call(
        body, (M // tmo, nk),
        [pl.BlockSpec((tk, tmo), lambda m, k: (k, m)), pl.BlockSpec((tk, N), lambda m, k: (k, 0))], (narrow, wide),
        [pl.BlockSpec((N, tmo), lambda m, k: (0, m))], [jax.ShapeDtypeStruct((N, M), BF16)], name,
        scratch=[pltpu.VMEM((tmo, N), F32)], copies=copies)
    return out if copies is None else (out, got)


def adamw_sum(parts, w, m, v, name):
    layers, rows, cols = w.shape
    tr = _row_tile(rows, TR_ADAM)
    nr = rows // tr
    c1 = 1.0 - ADAM_B1 ** ADAM_STEP
    c2 = 1.0 - ADAM_B2 ** ADAM_STEP

    def body(*refs):
        p_refs = refs[:layers]
        w_ref, m_ref, v_ref, g_ref, d_ref, nm_ref, nv_ref = refs[layers:]
        for k in range(layers):
            @pl.when(pl.program_id(0) == k)
            def _(p_ref=p_refs[k]):
                g = p_ref[0].astype(F32)
                for j in range(1, N_DEV):
                    g = g + p_ref[j].astype(F32)
                nm = ADAM_B1 * m_ref[...] + (1.0 - ADAM_B1) * g
                nv = ADAM_B2 * v_ref[...] + (1.0 - ADAM_B2) * (g * g)
                g_ref[...] = g
                nm_ref[...] = nm
                nv_ref[...] = nv
                d_ref[...] = -ADAM_LR * ((nm * (1.0 / c1)) / (jnp.sqrt(nv * (1.0 / c2)) + ADAM_EPS)
                                         + ADAM_WD * w_ref[...])

    def part_spec(k):
        return pl.BlockSpec((N_DEV, tr, cols),
                            lambda l, i: (0, jnp.where(l < k, 0, jnp.where(l == k, i, nr - 1)), 0))

    spec = pl.BlockSpec((None, tr, cols), lambda l, i: (l, i, 0))
    out = jax.ShapeDtypeStruct((layers, rows, cols), F32)
    return pl.pallas_call(
        body, grid=(layers, nr),
        in_specs=[part_spec(k) for k in range(layers)] + [spec, spec, spec],
        out_specs=[spec] * 4, out_shape=[out] * 4, name=name, compiler_params=_params(2),
    )(*parts, w, m, v)


def _pack_small(p):
    layers = []
    for l in range(DEPTH):
        rows = [p[n][l].reshape(1, D) for n in SMALL_VEC]
        rows.append(p["sgu_b"][l].reshape(1, D))
        rows.append(jnp.pad(p["attn_sinks"][l].reshape(1, N_Q_HEADS), ((0, 0), (0, D - N_Q_HEADS))))
        rows.append(jnp.zeros((SMALL_ROWS - len(SMALL_VEC) - 2, D), F32))
        layers.append(jnp.concatenate(rows, axis=0))
    return jnp.concatenate(layers, axis=0)


def _unpack_small(packed):
    a = packed.reshape(DEPTH, SMALL_ROWS, D)
    out = {n: a[:, i, :] for i, n in enumerate(SMALL_VEC)}
    k = len(SMALL_VEC)
    out["sgu_b"] = a[:, k, :].reshape(DEPTH, SGU_GROUPS, BLK)
    out["attn_sinks"] = a[:, k + 1, :N_Q_HEADS]
    return out


WEIGHT_NAMES = ("ffn1_pre_g", "ffn1_w1", "ffn1_w2", "ffn1_post_g", "mix_pre_g", "w_in", "attn_sinks", "sgu_ln_g",
                "sgu_ln_b", "sgu_w", "sgu_b", "w_attn_branch", "w_sgu_branch", "w_out", "mix_post_g", "ffn2_pre_g",
                "ffn2_w1", "ffn2_w2", "ffn2_post_g")
COL_SHARDED = ("ffn1_w1", "ffn2_w1", "w_in")
ROW_SHARDED = ("ffn1_w2", "ffn2_w2", "w_attn_branch", "w_sgu_branch", "w_out")
MATRICES = COL_SHARDED + ROW_SHARDED


def kernel(x, ffn1_pre_g, ffn1_w1, ffn1_w2, ffn1_post_g, mix_pre_g, w_in, attn_sinks, sgu_ln_g, sgu_ln_b, sgu_w, sgu_b, w_attn_branch, w_sgu_branch, w_out, mix_post_g, ffn2_pre_g, ffn2_w1, ffn2_w2, ffn2_post_g, loss_target, m_ffn1_pre_g, m_ffn1_w1, m_ffn1_w2, m_ffn1_post_g, m_mix_pre_g, m_w_in, m_attn_sinks, m_sgu_ln_g, m_sgu_ln_b, m_sgu_w, m_sgu_b, m_w_attn_branch, m_w_sgu_branch, m_w_out, m_mix_post_g, m_ffn2_pre_g, m_ffn2_w1, m_ffn2_w2, m_ffn2_post_g, v_ffn1_pre_g, v_ffn1_w1, v_ffn1_w2, v_ffn1_post_g, v_mix_pre_g, v_w_in, v_attn_sinks, v_sgu_ln_g, v_sgu_ln_b, v_sgu_w, v_sgu_b, v_w_attn_branch, v_w_sgu_branch, v_w_out, v_mix_post_g, v_ffn2_pre_g, v_ffn2_w1, v_ffn2_w2, v_ffn2_post_g):
    w = dict(zip(WEIGHT_NAMES, (ffn1_pre_g, ffn1_w1, ffn1_w2, ffn1_post_g, mix_pre_g, w_in, attn_sinks, sgu_ln_g,
                                sgu_ln_b, sgu_w, sgu_b, w_attn_branch, w_sgu_branch, w_out, mix_post_g, ffn2_pre_g,
                                ffn2_w1, ffn2_w2, ffn2_post_g)))
    mom = dict(zip(WEIGHT_NAMES, (m_ffn1_pre_g, m_ffn1_w1, m_ffn1_w2, m_ffn1_post_g, m_mix_pre_g, m_w_in,
                                  m_attn_sinks, m_sgu_ln_g, m_sgu_ln_b, m_sgu_w, m_sgu_b, m_w_attn_branch,
                                  m_w_sgu_branch, m_w_out, m_mix_post_g, m_ffn2_pre_g, m_ffn2_w1, m_ffn2_w2,
                                  m_ffn2_post_g)))
    var = dict(zip(WEIGHT_NAMES, (v_ffn1_pre_g, v_ffn1_w1, v_ffn1_w2, v_ffn1_post_g, v_mix_pre_g, v_w_in,
                                  v_attn_sinks, v_sgu_ln_g, v_sgu_ln_b, v_sgu_w, v_sgu_b, v_w_attn_branch,
                                  v_w_sgu_branch, v_w_out, v_mix_post_g, v_ffn2_pre_g, v_ffn2_w1, v_ffn2_w2,
                                  v_ffn2_post_g)))
    T = x.shape[1]
    xs = x.reshape(T, D)
    target = loss_target.reshape(T, D)

    for n in COL_SHARDED:
        w[n], mom[n], var[n] = (jnp.swapaxes(t[n], 1, 2) for t in (w, mom, var))

    per_layer = [cast_layer([w[n] for n in MATRICES], l) for l in range(DEPTH)]
    shard = {n: [per_layer[l][i] for l in range(DEPTH)] for i, n in enumerate(MATRICES)}
    weights = [{} for _ in range(DEPTH)]
    ffn1, ffn2, squares = ("ffn1_w1", "ffn1_w2"), ("ffn2_w1", "ffn2_w2"), ("w_attn_branch", "w_sgu_branch", "w_out")
    for n, g in zip(ffn1, all_gather_weights([shard[n][0] for n in ffn1])):
        weights[0][n] = g.reshape(-1, D)

    def fetch(groups):
        jobs = [("all", shard[n][l]) for l, names in groups if l < DEPTH for n in names]
        return PeerCopies(jobs) if jobs else None

    def landed(groups, arrivals):
        slots = [(l, n) for l, names in groups if l < DEPTH for n in names]
        for (l, n), g in zip(slots, arrivals):
            weights[l][n] = g.reshape(-1, D)

    vec = lambda n, l: w[n][l].reshape(1, D)
    sgu_bb = [jnp.broadcast_to(w["sgu_b"][l][:, :, None], (SGU_GROUPS, BLK, BLK)) for l in range(DEPTH)]

    saved = []
    h = xs
    for l in range(DEPTH):
        wl = weights[l]
        ahead = lambda names: [(0, names)] if l == 0 else []
        x0 = h
        groups = [(l + 1, ffn1)] + ahead(("w_in",))
        (x1, a1, s1, y1), got = ffn_fwd(x0, vec("ffn1_pre_g", l), vec("ffn1_post_g", l), wl["ffn1_w1"], wl["ffn1_w2"],
                                        fetch(groups))
        landed(groups, got)
        groups = [(l + 1, ("w_in",))] + ahead(squares)
        (z,), got = mix_in_fwd(x1, vec("mix_pre_g", l), wl["w_in"], fetch(groups))
        landed(groups, got)
        groups = ahead(("ffn2_w1",))
        (ya,), got = attn_fwd(z, w["attn_sinks"][l], fetch(groups))
        landed(groups, got)
        groups = [(l + 1, squares)] + ahead(("ffn2_w2",))
        (x2, ysg, pa, pb, o), got = mix_fwd_out(
            x1, z, ya, vec("mix_post_g", l), vec("sgu_ln_g", l), vec("sgu_ln_b", l), w["sgu_w"][l], sgu_bb[l],
            wl["w_attn_branch"], wl["w_sgu_branch"], wl["w_out"], fetch(groups))
        landed(groups, got)
        groups = [(l + 1, ffn2)]
        last = (vec("ffn2_pre_g", l), vec("ffn2_post_g", l), wl["ffn2_w1"], wl["ffn2_w2"])
        if l < DEPTH - 1:
            (h, a2, s2, y2), got = ffn_fwd(x2, *last, fetch(groups))
            landed(groups, got)
        else:
            (a2, s2, y2, sq, dx), _ = ffn_fwd(x2, *last, target=target)
        saved.append((x0, a1, s1, y1, x1, z, ya, ysg, pa, pb, o, x2, a2, s2, y2))

    loss = lax.psum(0.5 / D * jnp.sum(sq), ("x", "y", "c"))

    def ffn_wgrads(hb, da, s, dy):
        return [("own", wgrad(da, hb, "wgrad_ffn_w1", D // 2).reshape(N_DEV, W1_SHARD, D)),
                ("own", wgrad(s, dy, "wgrad_ffn_w2", D).reshape(N_DEV, W2_SHARD, D))]

    parts = {n: [None] * DEPTH for n in MATRICES}
    small = {n: [None] * DEPTH for n in WEIGHT_NAMES if n not in MATRICES + ("sgu_w",)}
    sgu_w_parts = [None] * DEPTH
    waiting = None
    for l in reversed(range(DEPTH)):
        wl = weights[l]
        x0, a1, s1, y1, x1, z, ya, ysg, pa, pb, o, x2, a2, s2, y2 = saved[l]
        (dx, da, dy, hb, dpg, dqg), got = ffn_bwd(
            dx, x2, y2, a2, vec("ffn2_pre_g", l), vec("ffn2_post_g", l), wl["ffn2_w1"], wl["ffn2_w2"],
            PeerCopies(waiting) if waiting else None)
        if waiting:
            parts["ffn1_w1"][l + 1], parts["ffn1_w2"][l + 1] = got
        small["ffn2_pre_g"][l], small["ffn2_post_g"][l] = dpg, dqg

        (dzb, dya, mg, dob, dpa, dpb, dqg, dlg, dlb, dsw, dsb), got = mix_bwd_out(
            dx, z, o, pa, pb, vec("mix_post_g", l), vec("sgu_ln_g", l), vec("sgu_ln_b", l), w["sgu_w"][l], sgu_bb[l],
            wl["w_attn_branch"], wl["w_sgu_branch"], wl["w_out"], PeerCopies(ffn_wgrads(hb, da, s2, dy)))
        parts["ffn2_w1"][l], parts["ffn2_w2"][l] = got
        dz, dsink = attn_bwd(z, ya, dya, dzb, w["attn_sinks"][l])
        dx, hb, dpg = mix_in_bwd(dx, x1, dz, vec("mix_pre_g", l), wl["w_in"])
        small["mix_pre_g"][l], small["mix_post_g"][l] = dpg, dqg
        small["sgu_ln_g"][l], small["sgu_ln_b"][l] = dlg, dlb
        small["sgu_b"][l] = dsb[:, :, 0]
        small["attn_sinks"][l] = dsink[0, :N_Q_HEADS]
        mixer = [("own", wgrad(dz, hb, "wgrad_w_in", D // 2).reshape(N_DEV, WIN_SHARD, D))]
        mixer += [("own", wgrad(act, cot, "wgrad_square", D, tk=2 * TK_WGRAD).reshape(N_DEV, SQ_SHARD, D))
                  for act, cot in ((ya, dpa), (ysg, dpb), (mg, dob))]
        mixer.append(("all", dsw.reshape(SGU_GROUPS * BLK, BLK)))
        if l == 0:
            small["ffn1_pre_g"][0] = small["ffn1_post_g"][0] = jnp.zeros((1, D), F32)
            mixer.append(("all", _pack_small({n: jnp.stack(v, axis=0) for n, v in small.items()})))

        (dx, da, dy, hb, dpg, dqg), got = ffn_bwd(
            dx, x0, y1, a1, vec("ffn1_pre_g", l), vec("ffn1_post_g", l), wl["ffn1_w1"], wl["ffn1_w2"],
            PeerCopies(mixer))
        parts["w_in"][l], parts["w_attn_branch"][l], parts["w_sgu_branch"][l], parts["w_out"][l] = got[:4]
        sgu_w_parts[l] = got[4]
        if l > 0:
            small["ffn1_pre_g"][l], small["ffn1_post_g"][l] = dpg, dqg
            waiting = ffn_wgrads(hb, da, s1, dy)
    grad_x = dx.reshape(x.shape)
    small_parts = got[5]

    def late_rows(pre, post):
        return jnp.concatenate([pre.reshape(1, D), post.reshape(1, D), jnp.zeros((LATE_ROWS - 2, D), F32)], axis=0)

    g_w1 = wgrad(da, hb, "wgrad_ffn_w1", D // 2).reshape(N_DEV, W1_SHARD, D)
    g_w2, got = wgrad(s1, dy, "wgrad_ffn_w2", D, PeerCopies([("own", g_w1)]))
    parts["ffn1_w1"][0] = got[0]
    parts["ffn1_w2"][0], late_parts = exchange_last([("own", g_w2.reshape(N_DEV, W2_SHARD, D)),
                                                     ("all", late_rows(dpg, dqg))])

    grads, deltas, new_m, new_v = {}, {}, {}, {}
    for n in MATRICES:
        grads[n], deltas[n], new_m[n], new_v[n] = adamw_sum(parts[n], w[n], mom[n], var[n], "adamw_" + n)
    for n in COL_SHARDED:
        for out in (grads, deltas, new_m, new_v):
            out[n] = jnp.swapaxes(out[n], 1, 2)
    flat = lambda t: t["sgu_w"].reshape(DEPTH, SGU_GROUPS * BLK, BLK)
    for out, r in zip((grads, deltas, new_m, new_v),
                      adamw_sum(sgu_w_parts, flat(w), flat(mom), flat(var), "adamw_sgu_w")):
        out["sgu_w"] = r.reshape(w["sgu_w"].shape)
    res = adamw_sum([small_parts], _pack_small(w)[None], _pack_small(mom)[None], _pack_small(var)[None],
                    "adamw_small")
    late = adamw_sum([late_parts], *[late_rows(t["ffn1_pre_g"][0], t["ffn1_post_g"][0])[None] for t in (w, mom, var)],
                     "adamw_late")
    for out, packed, late_out in zip((grads, deltas, new_m, new_v), res, late):
        out.update(_unpack_small(packed[0]))
        for row, n in enumerate(("ffn1_pre_g", "ffn1_post_g")):
            out[n] = jnp.concatenate([late_out[0, row:row + 1], out[n][1:]], axis=0)

    return (loss, grad_x, *[grads[n] for n in WEIGHT_NAMES], *[deltas[n] for n in WEIGHT_NAMES],
            *[new_m[n] for n in WEIGHT_NAMES], *[new_v[n] for n in WEIGHT_NAMES])
```

```python
import math

import jax
import jax.numpy as jnp
from jax import lax
from jax.experimental import pallas as pl
from jax.experimental.pallas import tpu as pltpu

F32 = jnp.float32
BF16 = jnp.bfloat16

N_DEV = 8
D = 1024
FF = 2816
DEPTH = 4
HEAD_DIM = 64
N_Q_HEADS = 16
N_KV_HEADS = 2
Q_PER_KV = N_Q_HEADS // N_KV_HEADS
BLK = 128
SGU_GROUPS = 8
IN_WIDTH = 5376
W1_SHARD = 2 * FF // N_DEV
WIN_SHARD = IN_WIDTH // N_DEV
W2_SHARD = FF // N_DEV
SQ_SHARD = D // N_DEV

RMS_EPS = 1e-6
LN_EPS = 1e-5
MASK_VALUE = -1e30
ATTN_SCALE = 1.0 / math.sqrt(HEAD_DIM)

ADAM_LR = 0.001
ADAM_B1 = 0.9
ADAM_B2 = 0.999
ADAM_EPS = 1e-08
ADAM_WD = 0.01
ADAM_STEP = 10

VMEM_LIMIT_V7X = 56 * 1024 * 1024

WIN_SEGMENTS = ((0, 1024, 0), (1024, 256, 5120), (1280, 1024, 1024), (2304, 1024, 2048), (3328, 1024, 3072),
                (4352, 1024, 4096))
ZQ, ZU, ZV, ZGA, ZGB, ZKV = 0, 1024, 2048, 3072, 4096, 5120
DZ_Q, DZ_KV, DZ_MID = 0, 1024, 1280
DZ_U, DZ_V, DZ_GA, DZ_GB = 1280, 2304, 3328, 4352

FF_CHUNKS = ((0, 1024), (1024, 1024), (2048, 768))
WIN_CHUNKS = ((0, 1792), (1792, 1792), (3584, 1792))

SMALL_ROWS = 16
LATE_ROWS = 16
SMALL_VEC = ("ffn1_pre_g", "ffn1_post_g", "mix_pre_g", "mix_post_g", "ffn2_pre_g", "ffn2_post_g",
             "sgu_ln_g", "sgu_ln_b")

TM_FFN_FWD = 512
TM_FFN_BWD = 256
TM_MIX_IN = 1024
TM_MIX = 512
TM_MIX_BWD = 256
TQ_ATTN = 512
TK_WGRAD = 1024
TR_ADAM = 256


def _params(n_grid, vmem=VMEM_LIMIT_V7X):
    return pltpu.CompilerParams(dimension_semantics=("arbitrary",) * n_grid, vmem_limit_bytes=vmem)


def _dot(a, b):
    return jnp.dot(a, b, preferred_element_type=F32)


def _dot_nt(a, b):
    return lax.dot_general(a, b, (((1,), (1,)), ((), ())), preferred_element_type=F32)


def _dot_tn(a, b):
    return lax.dot_general(a, b, (((0,), (0,)), ((), ())), preferred_element_type=F32)


def _rms(x, g):
    r = lax.rsqrt(jnp.mean(x * x, axis=-1, keepdims=True) + RMS_EPS)
    n = x * r
    return n * g, n, r


def _rms_bwd(n, r, g, dy):
    dn = dy * g
    return r * (dn - n * jnp.mean(dn * n, axis=-1, keepdims=True))


def _colsum(v):
    return jnp.sum(v, axis=0, keepdims=True)


def _sigmoid(v):
    return 0.5 * jnp.tanh(0.5 * v) + 0.5


def _gelu_parts(v):
    cdf = 0.5 * lax.erf(v * (1.0 / math.sqrt(2.0))) + 0.5
    return cdf, jnp.exp2(v * v * (-0.5 / math.log(2.0))) * (1.0 / math.sqrt(2.0 * math.pi))


def _row_tile(rows, cap):
    return max(t for t in range(16, min(rows, cap) + 1, 16) if rows % t == 0)


def _row_spec(tm, width, col_block=0):
    return pl.BlockSpec((tm, width), lambda i, cb=col_block: (i, cb))


def _full_spec(shape):
    nd = len(shape)
    return pl.BlockSpec(tuple(shape), lambda *_: (0,) * nd)


def _whole_spec(arr):
    nd = arr.ndim
    return pl.BlockSpec(tuple(arr.shape), lambda *_: (0,) * nd, pipeline_mode=pl.Buffered(1))


HBM_SPEC = pl.BlockSpec(memory_space=pltpu.HBM)
MESH_ID = pl.DeviceIdType.MESH
RELATIONS = tuple((rx, ry, rc) for rx in (0, 1) for ry in (0, 1) for rc in (0, 1))[1:]


class PeerCopies:
    def __init__(self, jobs):
        self.kinds = [k for k, _ in jobs]
        self.arrays = [a for _, a in jobs]
        self.n = len(jobs)
        self.out_shape = [jax.ShapeDtypeStruct((N_DEV,) + a.shape if k == "all" else a.shape, a.dtype)
                          for k, a in jobs]
        self.scratch = [pltpu.SemaphoreType.DMA((7 * self.n,)), pltpu.SemaphoreType.DMA((7 * self.n,)),
                        pltpu.SemaphoreType.DMA((self.n,))]

    def _copies(self, ins, outs, sems, arriving):
        send_sems, recv_sems, local_sems = sems
        x, y, c = lax.axis_index("x"), lax.axis_index("y"), lax.axis_index("c")
        me = 4 * x + 2 * y + c
        src = lambda a, d: ins[a] if self.kinds[a] == "all" else ins[a].at[d]
        if not arriving:
            local = [pltpu.make_async_copy(src(a, me), outs[a].at[me], local_sems.at[a]) for a in range(self.n)]
        remote = []
        for k, (rx, ry, rc) in enumerate(RELATIONS):
            tx, ty, tc = (1 - x if rx else x), (1 - y if ry else y), (1 - c if rc else c)
            peer = 4 * tx + 2 * ty + tc
            for a in range(self.n):
                from_slot, to_slot = (me, peer) if arriving else (peer, me)
                remote.append(pltpu.make_async_remote_copy(
                    src_ref=src(a, from_slot), dst_ref=outs[a].at[to_slot],
                    send_sem=send_sems.at[a * 7 + k], recv_sem=recv_sems.at[a * 7 + k],
                    device_id=(tx, ty, tc), device_id_type=MESH_ID))
        return remote if arriving else (local, remote)

    def start(self, ins, outs, sems):
        local, sends = self._copies(ins, outs, sems, False)
        for cp in local + sends:
            cp.start()

    def wait(self, ins, outs, sems):
        for cp in self._copies(ins, outs, sems, True):
            cp.wait_recv()
        local, sends = self._copies(ins, outs, sems, False)
        for cp in sends:
            cp.wait_send()
        for cp in local:
            cp.wait()


def _call(body, grid, in_specs, args, out_specs, out_shape, name, scratch=(), copies=None):
    if copies is None:
        outs = pl.pallas_call(body, grid=grid, in_specs=list(in_specs), out_specs=list(out_specs),
                              out_shape=list(out_shape), scratch_shapes=list(scratch), name=name,
                              compiler_params=_params(len(grid)))(*args)
        return outs, []
    n_in, n_out, n_scr, nc = len(in_specs), len(out_specs), len(scratch), copies.n

    def at_step(steps):
        hit = pl.program_id(0) == steps[0]
        for axis in range(1, len(grid)):
            hit = jnp.logical_and(hit, pl.program_id(axis) == steps[axis])
        return hit

    def hosted(*refs):
        ins, refs = refs[:n_in], refs[n_in:]
        c_in, refs = refs[:nc], refs[nc:]
        outs, refs = refs[:n_out], refs[n_out:]
        c_out, refs = refs[:nc], refs[nc:]
        scr, sems = refs[:n_scr], refs[n_scr:]

        @pl.when(at_step([0] * len(grid)))
        def _():
            copies.start(c_in, c_out, sems)

        body(*ins, *outs, *scr)

        @pl.when(at_step([g - 1 for g in grid]))
        def _():
            copies.wait(c_in, c_out, sems)

    outs = pl.pallas_call(hosted, grid=grid, in_specs=list(in_specs) + [HBM_SPEC] * nc,
                          out_specs=list(out_specs) + [HBM_SPEC] * nc,
                          out_shape=list(out_shape) + copies.out_shape,
                          scratch_shapes=list(scratch) + copies.scratch, name=name,
                          compiler_params=_params(len(grid)))(*args, *copies.arrays)
    return outs[:n_out], outs[n_out:]


def cast_layer(ws, l):
    n = len(ws)

    def body(*refs):
        for w_ref, o_ref in zip(refs[:n], refs[n:]):
            o_ref[...] = w_ref[...].astype(BF16)

    halves = [w.shape[1] // 2 for w in ws]
    return pl.pallas_call(
        body, grid=(2,),
        in_specs=[pl.BlockSpec((None, h, D), lambda i: (l, i, 0)) for h in halves],
        out_specs=[_row_spec(h, D) for h in halves],
        out_shape=[jax.ShapeDtypeStruct(w.shape[1:], BF16) for w in ws],
        name="cast_layer", compiler_params=_params(1),
    )(*ws)


def all_gather_weights(shards):
    n = len(shards)

    def body(*refs):
        ins, outs = refs[:n], refs[n:2 * n]
        send_sems, recv_sems, local_sems = refs[2 * n:]
        x, y, c = lax.axis_index("x"), lax.axis_index("y"), lax.axis_index("c")
        me, sibling = (x, y, c), (x, y, 1 - c)
        chips = [(1 - x, y), (x, 1 - y), (1 - x, 1 - y)]

        def slot(a, owner):
            return outs[a].at[4 * owner[0] + 2 * owner[1] + owner[2]]

        def copy(a, k, owner, to, src=None):
            return pltpu.make_async_remote_copy(
                src_ref=slot(a, owner) if src is None else src, dst_ref=slot(a, owner),
                send_sem=send_sems.at[a * 7 + k], recv_sem=recv_sems.at[a * 7 + k],
                device_id=to, device_id_type=MESH_ID)

        mine = [pltpu.make_async_copy(ins[a], slot(a, me), local_sems.at[a]) for a in range(n)]
        for cp in mine:
            cp.start()
        first = []
        for a in range(n):
            first.append(copy(a, 0, me, sibling, src=ins[a]))
            first += [copy(a, 1 + j, me, (*chip, c), src=ins[a]) for j, chip in enumerate(chips)]
        for cp in first:
            cp.start()
        passed = []
        for j, chip in enumerate(chips):
            for a in range(n):
                copy(a, 1 + j, (*chip, c), me).wait_recv()
                fwd = copy(a, 4 + j, (*chip, c), sibling)
                fwd.start()
                passed.append(fwd)
        for a in range(n):
            copy(a, 0, sibling, me).wait_recv()
            for j, chip in enumerate(chips):
                copy(a, 4 + j, (*chip, 1 - c), me).wait_recv()
        for cp in first + passed:
            cp.wait_send()
        for cp in mine:
            cp.wait()

    out_shape = [jax.ShapeDtypeStruct((N_DEV,) + s.shape, s.dtype) for s in shards]
    return pl.pallas_call(
        body, in_specs=[HBM_SPEC] * n, out_specs=[HBM_SPEC] * n, out_shape=out_shape,
        scratch_shapes=[pltpu.SemaphoreType.DMA((7 * n,)), pltpu.SemaphoreType.DMA((7 * n,)),
                        pltpu.SemaphoreType.DMA((n,))],
        name="all_gather_weights",
    )(*shards)


def exchange_last(jobs):
    copies = PeerCopies(jobs)
    n = copies.n

    def body(*refs):
        ins, outs, sems = refs[:n], refs[n:2 * n], refs[2 * n:]
        copies.start(ins, outs, sems)
        copies.wait(ins, outs, sems)

    return pl.pallas_call(
        body, in_specs=[HBM_SPEC] * n, out_specs=[HBM_SPEC] * n, out_shape=copies.out_shape,
        scratch_shapes=copies.scratch, name="exchange_last",
    )(*copies.arrays)


def ffn_fwd(x, pre_g, post_g, w1, w2, copies=None, target=None):
    T = x.shape[0]
    tm = min(TM_FFN_FWD, T)

    def body(*refs):
        if target is None:
            x_ref, pg_ref, qg_ref, w1_ref, w2_ref, xo_ref, a_ref, s_ref, y_ref = refs
        else:
            x_ref, pg_ref, qg_ref, w1_ref, w2_ref, t_ref, a_ref, s_ref, y_ref, sq_ref, dy_ref = refs
        xv = x_ref[...]
        h, _, _ = _rms(xv, pg_ref[...])
        hb = h.astype(BF16)
        acc = jnp.zeros((tm, D), F32)
        for c0, cw in FF_CHUNKS:
            g = _dot_nt(hb, w1_ref[c0:c0 + cw, :])
            u = _dot_nt(hb, w1_ref[FF + c0:FF + c0 + cw, :])
            sg = _sigmoid(g)
            si = g * sg
            a_ref[:, c0:c0 + cw] = (u * (sg + si - si * sg)).astype(BF16)
            a_ref[:, FF + c0:FF + c0 + cw] = si.astype(BF16)
            s = (si * u).astype(BF16)
            s_ref[:, c0:c0 + cw] = s
            acc = acc + _dot(s, w2_ref[c0:c0 + cw, :])
        y_ref[...] = acc.astype(BF16)
        o, _, _ = _rms(acc, qg_ref[...])
        xo = xv + 0.5 * o
        if target is None:
            xo_ref[...] = xo
        else:
            @pl.when(pl.program_id(0) == 0)
            def _():
                sq_ref[...] = jnp.zeros_like(sq_ref)

            err = xo - t_ref[...]
            sq_ref[...] += _colsum(err * err)
            dy_ref[...] = err * (1.0 / D)

    in_specs = [_row_spec(tm, D), _full_spec((1, D)), _full_spec((1, D)), _whole_spec(w1), _whole_spec(w2)]
    saved_specs = [_row_spec(tm, 2 * FF), _row_spec(tm, FF), _row_spec(tm, D)]
    saved_shapes = [jax.ShapeDtypeStruct((T, 2 * FF), BF16), jax.ShapeDtypeStruct((T, FF), BF16),
                    jax.ShapeDtypeStruct((T, D), BF16)]
    row_f32 = jax.ShapeDtypeStruct((T, D), F32)
    if target is None:
        return _call(body, (T // tm,), in_specs, (x, pre_g, post_g, w1, w2), [_row_spec(tm, D)] + saved_specs,
                     [row_f32] + saved_shapes, "ffn_fwd", copies=copies)
    return _call(body, (T // tm,), in_specs + [_row_spec(tm, D)], (x, pre_g, post_g, w1, w2, target),
                 saved_specs + [_full_spec((1, D)), _row_spec(tm, D)],
                 saved_shapes + [jax.ShapeDtypeStruct((1, D), F32), row_f32], "ffn_fwd_loss", copies=copies)


def ffn_bwd(dxo, x, y, a, pre_g, post_g, w1, w2, copies=None):
    T = x.shape[0]
    tm = min(TM_FFN_BWD, T)

    def body(dxo_ref, x_ref, y_ref, a_ref, pg_ref, qg_ref, w1_ref, w2_ref,
             dx_ref, da_ref, dy_ref, hb_ref, dpg_ref, dqg_ref):
        @pl.when(pl.program_id(0) == 0)
        def _():
            dpg_ref[...] = jnp.zeros_like(dpg_ref)
            dqg_ref[...] = jnp.zeros_like(dqg_ref)

        dxo = dxo_ref[...]
        qg = qg_ref[...]
        _, ny, ry = _rms(y_ref[...].astype(F32), qg)
        dn = 0.5 * dxo
        dqg_ref[...] += _colsum(dn * ny)
        dyb = _rms_bwd(ny, ry, qg, dn).astype(BF16)
        dy_ref[...] = dyb
        pg = pg_ref[...]
        h, nx, rx = _rms(x_ref[...], pg)
        hb_ref[...] = h.astype(BF16)
        dh = jnp.zeros((tm, D), F32)
        for c0, cw in FF_CHUNKS:
            ds = _dot_nt(dyb, w2_ref[c0:c0 + cw, :])
            dg = (ds * a_ref[:, c0:c0 + cw].astype(F32)).astype(BF16)
            du = (ds * a_ref[:, FF + c0:FF + c0 + cw].astype(F32)).astype(BF16)
            da_ref[:, c0:c0 + cw] = dg
            da_ref[:, FF + c0:FF + c0 + cw] = du
            dh = dh + _dot(dg, w1_ref[c0:c0 + cw, :]) + _dot(du, w1_ref[FF + c0:FF + c0 + cw, :])
        dpg_ref[...] += _colsum(dh * nx)
        dx_ref[...] = dxo + _rms_bwd(nx, rx, pg, dh)

    return _call(
        body, (T // tm,),
        [_row_spec(tm, D), _row_spec(tm, D), _row_spec(tm, D), _row_spec(tm, 2 * FF),
         _full_spec((1, D)), _full_spec((1, D)), _whole_spec(w1), _whole_spec(w2)],
        (dxo, x, y, a, pre_g, post_g, w1, w2),
        [_row_spec(tm, D), _row_spec(tm, 2 * FF), _row_spec(tm, D), _row_spec(tm, D),
         _full_spec((1, D)), _full_spec((1, D))],
        [jax.ShapeDtypeStruct((T, D), F32), jax.ShapeDtypeStruct((T, 2 * FF), BF16),
         jax.ShapeDtypeStruct((T, D), BF16), jax.ShapeDtypeStruct((T, D), BF16),
         jax.ShapeDtypeStruct((1, D), F32), jax.ShapeDtypeStruct((1, D), F32)],
        "ffn_bwd", copies=copies)


def mix_in_fwd(x, pre_g, win, copies=None):
    T = x.shape[0]
    tm = min(TM_MIX_IN, T)

    def body(x_ref, pg_ref, w_ref, z_ref):
        h, _, _ = _rms(x_ref[...], pg_ref[...])
        hb = h.astype(BF16)
        for w0, n, z0 in WIN_SEGMENTS:
            z_ref[:, z0:z0 + n] = _dot_nt(hb, w_ref[w0:w0 + n, :]).astype(BF16)

    return _call(
        body, (T // tm,), [_row_spec(tm, D), _full_spec((1, D)), _whole_spec(win)], (x, pre_g, win),
        [_row_spec(tm, IN_WIDTH)], [jax.ShapeDtypeStruct((T, IN_WIDTH), BF16)], "mix_in_fwd", copies=copies)


STACK = Q_PER_KV // 2
SROWS = STACK * BLK


def _sees_own():
    qi = lax.broadcasted_iota(jnp.int32, (SROWS, BLK), 0) & (BLK - 1)
    return lax.broadcasted_iota(jnp.int32, (SROWS, BLK), 1) <= qi


def _both_blocks(picked, own):
    return jnp.concatenate([jnp.where(own, 0.0, picked), jnp.where(own, picked, 0.0)], axis=1)


def _lane_half(rows):
    return lax.broadcasted_iota(jnp.int32, (rows, BLK), 1) // HEAD_DIM


def _stack(ref, b, g):
    return jnp.concatenate([ref[b * BLK:(b + 1) * BLK, (STACK * g + j) * BLK:(STACK * g + j + 1) * BLK]
                            for j in range(STACK)], axis=0)


def _placed(pair, g, hp, fill):
    src = pair if hp == g else pltpu.roll(pair, HEAD_DIM, 1)
    return jnp.where(_lane_half(2 * BLK) == hp, src, fill).astype(BF16)


def _sink_column(sink_ref, g, hp):
    rb = lax.broadcasted_iota(jnp.int32, (SROWS, 1), 0) // BLK
    col = jnp.full((SROWS, 1), sink_ref[Q_PER_KV * g + hp], F32)
    for j in range(1, STACK):
        col = jnp.where(rb == j, sink_ref[Q_PER_KV * g + 2 * j + hp], col)
    return col


def _attn_scores(qs, kz, sink_col, own, no_previous):
    raw = _dot_nt(qs, kz)
    prev = raw[:, 0:BLK]
    if no_previous is not None:
        prev = prev + no_previous.astype(F32) * MASK_VALUE
    picked = jnp.where(own, raw[:, BLK:2 * BLK], prev)
    sink_raw = sink_col * (1.0 / ATTN_SCALE)
    m = jnp.maximum(jnp.max(picked, axis=-1, keepdims=True), sink_raw)
    factor = ATTN_SCALE / math.log(2.0)
    return jnp.exp2((picked - m) * factor), jnp.exp2((sink_raw - m) * factor)


def _kv_specs(tq, nb):
    kv_blk = ZKV // (2 * BLK)
    return [pl.BlockSpec((tq, 2 * BLK), lambda i: (i, kv_blk)),
            pl.BlockSpec((BLK, 2 * BLK), lambda i: (jnp.maximum(i * nb - 1, 0), kv_blk))]


def attn_fwd(z, sinks, copies=None):
    T = z.shape[0]
    tq = min(TQ_ATTN, T)
    nb = tq // BLK

    def body(sink_ref, q_ref, kv_ref, kvh_ref, o_ref):
        i = pl.program_id(0)
        low = _lane_half(SROWS) == 0
        own = _sees_own()
        for b in range(nb):
            kvp = kvh_ref[...] if b == 0 else kv_ref[(b - 1) * BLK:b * BLK, :]
            kv2 = jnp.concatenate([kvp, kv_ref[b * BLK:(b + 1) * BLK, :]], axis=0).astype(F32)
            no_previous = (i == 0) if b == 0 else None
            for g in range(N_KV_HEADS):
                qs = _stack(q_ref, b, g)
                r, e = [], []
                for hp in range(2):
                    p, e_sink = _attn_scores(qs, _placed(kv2[:, 0:BLK], g, hp, 0.0), _sink_column(sink_ref, g, hp),
                                             own, no_previous)
                    r.append(_dot(_both_blocks(p, own).astype(BF16), _placed(kv2[:, BLK:2 * BLK], g, hp, 1.0)))
                    e.append(e_sink)
                den = pltpu.roll(jnp.where(low, r[1], r[0]), HEAD_DIM, 1) + jnp.where(low, e[0], e[1])
                out = (jnp.where(low, r[0], r[1]) * (1.0 / den)).astype(BF16)
                for j in range(STACK):
                    o_ref[b * BLK:(b + 1) * BLK, (STACK * g + j) * BLK:(STACK * g + j + 1) * BLK] = \
                        out[j * BLK:(j + 1) * BLK, :]

    return _call(
        body, (T // tq,), [pl.BlockSpec(memory_space=pltpu.SMEM), _row_spec(tq, D)] + _kv_specs(tq, nb),
        (sinks, z, z, z), [_row_spec(tq, D)], [jax.ShapeDtypeStruct((T, D), BF16)], "attn_fwd", copies=copies)


def attn_bwd(z, ya, dya, dz, sinks):
    T = z.shape[0]
    tq = min(TQ_ATTN, T)
    nb = tq // BLK
    nt = T // tq

    def body(sink_ref, q_ref, kv_ref, kvh_ref, y_ref, dy_ref, dz_whole_ref, dz_ref, dsink_ref, acc_ref, carry_ref):
        i = pl.program_id(0)
        first_tile = i == nt - 1

        @pl.when(i == 0)
        def _():
            carry_ref[...] = jnp.zeros_like(carry_ref)
            dsink_ref[...] = jnp.zeros_like(dsink_ref)

        acc_ref[...] = jnp.zeros_like(acc_ref)
        lane = lax.broadcasted_iota(jnp.int32, (1, BLK), 1)
        dsink = jnp.zeros((1, BLK), F32)
        half = _lane_half(SROWS)
        own = _sees_own()
        for b in range(nb):
            kvp = kvh_ref[...] if b == 0 else kv_ref[(b - 1) * BLK:b * BLK, :]
            kv2 = jnp.concatenate([kvp, kv_ref[b * BLK:(b + 1) * BLK, :]], axis=0).astype(F32)
            no_previous = first_tile if b == 0 else None
            dk_groups, dv_groups = [], []
            for g in range(N_KV_HEADS):
                qs = _stack(q_ref, b, g)
                dys = _stack(dy_ref, b, g)
                dyy = dys.astype(F32) * _stack(y_ref, b, g).astype(F32)
                dq = jnp.zeros((SROWS, BLK), F32)
                ds_both, pn_both = [], []
                for hp in range(2):
                    kz = _placed(kv2[:, 0:BLK], g, hp, 0.0)
                    p, e_sink = _attn_scores(qs, kz, _sink_column(sink_ref, g, hp), own, no_previous)
                    inv = 1.0 / (jnp.sum(p, axis=-1, keepdims=True) + e_sink)
                    p = p * inv
                    delta = jnp.sum(jnp.where(half == hp, dyy, 0.0), axis=-1, keepdims=True)
                    dp = _dot_nt(dys, _placed(kv2[:, BLK:2 * BLK], g, hp, 0.0))
                    ds = p * (jnp.where(own, dp[:, BLK:2 * BLK], dp[:, 0:BLK]) - delta)
                    ds = _both_blocks(ds, own).astype(BF16)
                    pn = _both_blocks(p, own)
                    sink_term = e_sink * inv * delta
                    for j in range(STACK):
                        dsink = dsink + jnp.where(lane == Q_PER_KV * g + 2 * j + hp,
                                                  -_colsum(sink_term[j * BLK:(j + 1) * BLK, :]), 0.0)
                    dq = dq + _dot(ds, kz)
                    ds_both.append(ds)
                    pn_both.append(pn.astype(BF16))
                dk_t = _dot_tn(qs, jnp.concatenate(ds_both, axis=1))
                dv_t = _dot_tn(dys, jnp.concatenate(pn_both, axis=1))
                for t, groups in ((dk_t, dk_groups), (dv_t, dv_groups)):
                    groups.append(t[0:HEAD_DIM, 0:2 * BLK] + t[HEAD_DIM:BLK, 2 * BLK:4 * BLK])
                dqb = (dq * ATTN_SCALE).astype(BF16)
                for j in range(STACK):
                    dz_ref[b * BLK:(b + 1) * BLK, DZ_Q + (STACK * g + j) * BLK:DZ_Q + (STACK * g + j + 1) * BLK] = \
                        dqb[j * BLK:(j + 1) * BLK, :]
            acc_ref[b * BLK:(b + 2) * BLK, 0:BLK] += jnp.concatenate(dk_groups, axis=0).T * ATTN_SCALE
            acc_ref[b * BLK:(b + 2) * BLK, BLK:2 * BLK] += jnp.concatenate(dv_groups, axis=0).T
        dsink_ref[...] += dsink
        if nb > 1:
            dz_ref[0:tq - BLK, DZ_KV:DZ_MID] = acc_ref[BLK:tq, :].astype(BF16)
        dz_ref[tq - BLK:tq, DZ_KV:DZ_MID] = (acc_ref[tq:tq + BLK, :] + carry_ref[...]).astype(BF16)
        carry_ref[...] = acc_ref[0:BLK, :]

    kv_blk = ZKV // (2 * BLK)
    return pl.pallas_call(
        body, grid=(nt,),
        in_specs=[pl.BlockSpec(memory_space=pltpu.SMEM),
                  pl.BlockSpec((tq, D), lambda i: (nt - 1 - i, 0)),
                  pl.BlockSpec((tq, 2 * BLK), lambda i: (nt - 1 - i, kv_blk)),
                  pl.BlockSpec((BLK, 2 * BLK), lambda i: (jnp.maximum((nt - 1 - i) * nb - 1, 0), kv_blk)),
                  pl.BlockSpec((tq, D), lambda i: (nt - 1 - i, 0)),
                  pl.BlockSpec((tq, D), lambda i: (nt - 1 - i, 0)),
                  pl.BlockSpec(memory_space=pl.ANY)],
        out_specs=[pl.BlockSpec((tq, DZ_MID), lambda i: (nt - 1 - i, 0)), _full_spec((1, BLK))],
        out_shape=[jax.ShapeDtypeStruct((T, IN_WIDTH), BF16), jax.ShapeDtypeStruct((1, BLK), F32)],
        scratch_shapes=[pltpu.VMEM((tq + BLK, 2 * BLK), F32), pltpu.VMEM((BLK, 2 * BLK), F32)],
        input_output_aliases={6: 0},
        name="attn_bwd", compiler_params=_params(1),
    )(sinks, z, z, z, ya, dya, dz)


def _to_group_lanes(v, g, nch):
    return jnp.concatenate([v[n * BLK:(n + 1) * BLK, g * BLK:(g + 1) * BLK] for n in range(nch)], axis=1)


def _from_group_lanes(per_group, nch):
    rows = [jnp.concatenate([per_group[g][:, n * BLK:(n + 1) * BLK] for g in range(SGU_GROUPS)], axis=1)
            for n in range(nch)]
    return jnp.concatenate(rows, axis=0)


def _tril_bf16(w_ref, g):
    t = lax.broadcasted_iota(jnp.int32, (BLK, BLK), 0)
    s = lax.broadcasted_iota(jnp.int32, (BLK, BLK), 1)
    return jnp.where(t >= s, w_ref[g], 0.0).astype(BF16)


def _sgu_norm(v_s, ln_g, ln_b):
    cdf, pdf = _gelu_parts(v_s)
    gv = v_s * cdf
    xc = gv - jnp.mean(gv, axis=-1, keepdims=True)
    rstd = lax.rsqrt(jnp.mean(xc * xc, axis=-1, keepdims=True) + LN_EPS)
    nhat = xc * rstd
    return nhat * ln_g + ln_b, nhat, rstd, cdf + v_s * pdf


def _sgu_gate(vn, w_ref, bb_ref, nch):
    vnb = vn.astype(BF16)
    return _from_group_lanes(
        [_dot(_tril_bf16(w_ref, g), _to_group_lanes(vnb, g, nch)) + jnp.tile(bb_ref[g], (1, nch))
         for g in range(SGU_GROUPS)], nch)


def mix_fwd_out(x, z, ya, post_g, ln_g, ln_b, sgu_w, sgu_bb, wa, ws, wo, copies=None):
    T = x.shape[0]
    tm = min(TM_MIX, T)
    nch = tm // BLK

    def body(x_ref, us_ref, vs_ref, ga_ref, gb_ref, ya_ref, qg_ref, lg_ref, lb_ref, w_ref, bb_ref,
             wa_ref, ws_ref, wo_ref, xo_ref, ysg_ref, pa_ref, pb_ref, o_ref):
        vn, _, _, _ = _sgu_norm(vs_ref[...].astype(F32), lg_ref[...], lb_ref[...])
        gate = _sgu_gate(vn, w_ref, bb_ref, nch)
        us = us_ref[...].astype(F32)
        cdf, _ = _gelu_parts(us)
        ysg = (us * cdf * gate).astype(BF16)
        ysg_ref[...] = ysg
        pa = _dot(ya_ref[...], wa_ref[...])
        pb = _dot(ysg, ws_ref[...])
        pa_ref[...] = pa.astype(BF16)
        pb_ref[...] = pb.astype(BF16)
        merged = _sigmoid(ga_ref[...].astype(F32)) * pa + _sigmoid(gb_ref[...].astype(F32)) * pb
        o = _dot(merged.astype(BF16), wo_ref[...])
        o_ref[...] = o.astype(BF16)
        on, _, _ = _rms(o, qg_ref[...])
        xo_ref[...] = x_ref[...] + on

    zspec = lambda start: _row_spec(tm, D, start // D)
    act = jax.ShapeDtypeStruct((T, D), BF16)
    return _call(
        body, (T // tm,),
        [_row_spec(tm, D), zspec(ZU), zspec(ZV), zspec(ZGA), zspec(ZGB), _row_spec(tm, D),
         _full_spec((1, D)), _full_spec((1, D)), _full_spec((1, D)),
         _full_spec((SGU_GROUPS, BLK, BLK)), _full_spec((SGU_GROUPS, BLK, BLK)),
         _whole_spec(wa), _whole_spec(ws), _whole_spec(wo)],
        (x, z, z, z, z, ya, post_g, ln_g, ln_b, sgu_w, sgu_bb, wa, ws, wo),
        [_row_spec(tm, D)] * 5, [jax.ShapeDtypeStruct((T, D), F32), act, act, act, act],
        "mix_fwd_out", copies=copies)


def mix_bwd_out(dxo, z, o, pa, pb, post_g, ln_g, ln_b, sgu_w, sgu_bb, wa, ws, wo, copies=None):
    T = dxo.shape[0]
    tm = min(TM_MIX_BWD, T)
    nch = tm // BLK

    def body(dxo_ref, us_ref, vs_ref, ga_ref, gb_ref, o_ref, pa_ref, pb_ref, qg_ref, lg_ref, lb_ref, w_ref, bb_ref,
             wa_ref, ws_ref, wo_ref,
             dz_ref, dya_ref, mg_ref, do_ref, dpa_ref, dpb_ref, dqg_ref, dlg_ref, dlb_ref, dw_ref, dbb_ref):
        @pl.when(pl.program_id(0) == 0)
        def _():
            for r in (dqg_ref, dlg_ref, dlb_ref, dw_ref, dbb_ref):
                r[...] = jnp.zeros_like(r)

        qg = qg_ref[...]
        dxo = dxo_ref[...]
        _, no, ro = _rms(o_ref[...].astype(F32), qg)
        dqg_ref[...] += _colsum(dxo * no)
        dob = _rms_bwd(no, ro, qg, dxo).astype(BF16)
        do_ref[...] = dob
        dmerged = _dot_nt(dob, wo_ref[...])
        sa = _sigmoid(ga_ref[...].astype(F32))
        sb = _sigmoid(gb_ref[...].astype(F32))
        pa = pa_ref[...].astype(F32)
        pb = pb_ref[...].astype(F32)
        ta = sa * pa
        tb = sb * pb
        mg_ref[...] = (ta + tb).astype(BF16)
        dpa = (dmerged * sa).astype(BF16)
        dpb = (dmerged * sb).astype(BF16)
        dpa_ref[...] = dpa
        dpb_ref[...] = dpb
        dz_ref[:, DZ_GA:DZ_GA + D] = (dmerged * (ta - ta * sa)).astype(BF16)
        dz_ref[:, DZ_GB:DZ_GB + D] = (dmerged * (tb - tb * sb)).astype(BF16)
        dya_ref[...] = _dot_nt(dpa, wa_ref[...]).astype(BF16)
        dysg = _dot_nt(dpb, ws_ref[...])

        lg = lg_ref[...]
        vn, nhat, rstd, dgelu_v = _sgu_norm(vs_ref[...].astype(F32), lg, lb_ref[...])
        gate = _sgu_gate(vn, w_ref, bb_ref, nch)
        us = us_ref[...].astype(F32)
        cdf, pdf = _gelu_parts(us)
        dz_ref[:, DZ_U:DZ_U + D] = (dysg * gate * (cdf + us * pdf)).astype(BF16)
        dgate = (dysg * (us * cdf)).astype(BF16)
        vnb = vn.astype(BF16)
        t = lax.broadcasted_iota(jnp.int32, (BLK, BLK), 0)
        s = lax.broadcasted_iota(jnp.int32, (BLK, BLK), 1)
        dvn_groups = []
        for g in range(SGU_GROUPS):
            dgl = _to_group_lanes(dgate, g, nch)
            dbb_ref[g] += jnp.broadcast_to(jnp.sum(dgl.astype(F32), axis=-1, keepdims=True), (BLK, BLK))
            dw_ref[g] += jnp.where(t >= s, _dot_nt(dgl, _to_group_lanes(vnb, g, nch)), 0.0)
            dvn_groups.append(_dot_tn(_tril_bf16(w_ref, g), dgl))
        dvn = _from_group_lanes(dvn_groups, nch)
        dlg_ref[...] += _colsum(dvn * nhat)
        dlb_ref[...] += _colsum(dvn)
        dnh = dvn * lg
        dgv = rstd * (dnh - jnp.mean(dnh, axis=-1, keepdims=True) - nhat * jnp.mean(dnh * nhat, axis=-1, keepdims=True))
        dz_ref[:, DZ_V:DZ_V + D] = (dgv * dgelu_v).astype(BF16)

    zspec = lambda start: _row_spec(tm, D, start // D)
    act = jax.ShapeDtypeStruct((T, D), BF16)
    grp = jax.ShapeDtypeStruct((SGU_GROUPS, BLK, BLK), F32)
    vec = jax.ShapeDtypeStruct((1, D), F32)
    return _call(
        body, (T // tm,),
        [_row_spec(tm, D), zspec(ZU), zspec(ZV), zspec(ZGA), zspec(ZGB),
         _row_spec(tm, D), _row_spec(tm, D), _row_spec(tm, D),
         _full_spec((1, D)), _full_spec((1, D)), _full_spec((1, D)),
         _full_spec((SGU_GROUPS, BLK, BLK)), _full_spec((SGU_GROUPS, BLK, BLK)),
         _whole_spec(wa), _whole_spec(ws), _whole_spec(wo)],
        (dxo, z, z, z, z, o, pa, pb, post_g, ln_g, ln_b, sgu_w, sgu_bb, wa, ws, wo),
        [_row_spec(tm, IN_WIDTH)] + [_row_spec(tm, D)] * 5 + [_full_spec((1, D))] * 3
        + [_full_spec((SGU_GROUPS, BLK, BLK))] * 2,
        [jax.ShapeDtypeStruct((T, IN_WIDTH), BF16), act, act, act, act, act, vec, vec, vec, grp, grp],
        "mix_bwd_out", copies=copies)


def mix_in_bwd(dxo, x, dz, pre_g, win):
    T = x.shape[0]
    tm = min(TM_MIX, T)

    def body(dxo_ref, x_ref, dz_ref, pg_ref, w_ref, dx_ref, hb_ref, dpg_ref):
        @pl.when(pl.program_id(0) == 0)
        def _():
            dpg_ref[...] = jnp.zeros_like(dpg_ref)

        pg = pg_ref[...]
        h, nx, rx = _rms(x_ref[...], pg)
        hb_ref[...] = h.astype(BF16)
        dh = jnp.zeros((tm, D), F32)
        for c0, cw in WIN_CHUNKS:
            dh = dh + _dot(dz_ref[:, c0:c0 + cw], w_ref[c0:c0 + cw, :])
        dpg_ref[...] += _colsum(dh * nx)
        dx_ref[...] = dxo_ref[...] + _rms_bwd(nx, rx, pg, dh)

    return pl.pallas_call(
        body, grid=(T // tm,),
        in_specs=[_row_spec(tm, D), _row_spec(tm, D), _row_spec(tm, IN_WIDTH), _full_spec((1, D)), _whole_spec(win)],
        out_specs=[_row_spec(tm, D), _row_spec(tm, D), _full_spec((1, D))],
        out_shape=[jax.ShapeDtypeStruct((T, D), F32), jax.ShapeDtypeStruct((T, D), BF16),
                   jax.ShapeDtypeStruct((1, D), F32)],
        name="mix_in_bwd", compiler_params=_params(1),
    )(dxo, x, dz, pre_g, win)


def wgrad(wide, narrow, name, tmo, copies=None, tk=TK_WGRAD):
    T, N = wide.shape
    M = narrow.shape[1]
    tk = min(tk, T)
    nk = T // tk
    chunk = max(c for c in range(BLK, 1792 + 1, BLK) if N % c == 0)

    def body(a_ref, b_ref, o_ref, acc_ref):
        k = pl.program_id(1)

        @pl.when(k == 0)
        def _():
            acc_ref[...] = jnp.zeros_like(acc_ref)

        acc_ref[...] += _dot_tn(a_ref[...], b_ref[...])

        @pl.when(k == nk - 1)
        def _():
            for c0 in range(0, N, chunk):
                o_ref[c0:c0 + chunk, :] = acc_ref[:, c0:c0 + chunk].T.astype(BF16)

    (out,), got = _call(
        body, (M // tmo, nk),
        [pl.BlockSpec((tk, tmo), lambda m, k: (k, m)), pl.BlockSpec((tk, N), lambda m, k: (k, 0))], (narrow, wide),
        [pl.BlockSpec((N, tmo), lambda m, k: (0, m))], [jax.ShapeDtypeStruct((N, M), BF16)], name,
        scratch=[pltpu.VMEM((tmo, N), F32)], copies=copies)
    return out if copies is None else (out, got)


def adamw_sum(parts, w, m, v, name):
    layers, rows, cols = w.shape
    tr = _row_tile(rows, TR_ADAM)
    nr = rows // tr
    c1 = 1.0 - ADAM_B1 ** ADAM_STEP
    c2 = 1.0 - ADAM_B2 ** ADAM_STEP

    def body(*refs):
        p_refs = refs[:layers]
        w_ref, m_ref, v_ref, g_ref, d_ref, nm_ref, nv_ref = refs[layers:]
        for k in range(layers):
            @pl.when(pl.program_id(0) == k)
            def _(p_ref=p_refs[k]):
                g = p_ref[0].astype(F32)
                for j in range(1, N_DEV):
                    g = g + p_ref[j].astype(F32)
                nm = ADAM_B1 * m_ref[...] + (1.0 - ADAM_B1) * g
                nv = ADAM_B2 * v_ref[...] + (1.0 - ADAM_B2) * (g * g)
                g_ref[...] = g
                nm_ref[...] = nm
                nv_ref[...] = nv
                d_ref[...] = -ADAM_LR * ((nm * (1.0 / c1)) / (jnp.sqrt(nv * (1.0 / c2)) + ADAM_EPS)
                                         + ADAM_WD * w_ref[...])

    def part_spec(k):
        return pl.BlockSpec((N_DEV, tr, cols),
                            lambda l, i: (0, jnp.where(l < k, 0, jnp.where(l == k, i, nr - 1)), 0))

    spec = pl.BlockSpec((None, tr, cols), lambda l, i: (l, i, 0))
    out = jax.ShapeDtypeStruct((layers, rows, cols), F32)
    return pl.pallas_call(
        body, grid=(layers, nr),
        in_specs=[part_spec(k) for k in range(layers)] + [spec, spec, spec],
        out_specs=[spec] * 4, out_shape=[out] * 4, name=name, compiler_params=_params(2),
    )(*parts, w, m, v)


def _pack_small(p):
    layers = []
    for l in range(DEPTH):
        rows = [p[n][l].reshape(1, D) for n in SMALL_VEC]
        rows.append(p["sgu_b"][l].reshape(1, D))
        rows.append(jnp.pad(p["attn_sinks"][l].reshape(1, N_Q_HEADS), ((0, 0), (0, D - N_Q_HEADS))))
        rows.append(jnp.zeros((SMALL_ROWS - len(SMALL_VEC) - 2, D), F32))
        layers.append(jnp.concatenate(rows, axis=0))
    return jnp.concatenate(layers, axis=0)


def _unpack_small(packed):
    a = packed.reshape(DEPTH, SMALL_ROWS, D)
    out = {n: a[:, i, :] for i, n in enumerate(SMALL_VEC)}
    k = len(SMALL_VEC)
    out["sgu_b"] = a[:, k, :].reshape(DEPTH, SGU_GROUPS, BLK)
    out["attn_sinks"] = a[:, k + 1, :N_Q_HEADS]
    return out


WEIGHT_NAMES = ("ffn1_pre_g", "ffn1_w1", "ffn1_w2", "ffn1_post_g", "mix_pre_g", "w_in", "attn_sinks", "sgu_ln_g",
                "sgu_ln_b", "sgu_w", "sgu_b", "w_attn_branch", "w_sgu_branch", "w_out", "mix_post_g", "ffn2_pre_g",
                "ffn2_w1", "ffn2_w2", "ffn2_post_g")
COL_SHARDED = ("ffn1_w1", "ffn2_w1", "w_in")
ROW_SHARDED = ("ffn1_w2", "ffn2_w2", "w_attn_branch", "w_sgu_branch", "w_out")
MATRICES = COL_SHARDED + ROW_SHARDED


def kernel(x, ffn1_pre_g, ffn1_w1, ffn1_w2, ffn1_post_g, mix_pre_g, w_in, attn_sinks, sgu_ln_g, sgu_ln_b, sgu_w, sgu_b, w_attn_branch, w_sgu_branch, w_out, mix_post_g, ffn2_pre_g, ffn2_w1, ffn2_w2, ffn2_post_g, loss_target, m_ffn1_pre_g, m_ffn1_w1, m_ffn1_w2, m_ffn1_post_g, m_mix_pre_g, m_w_in, m_attn_sinks, m_sgu_ln_g, m_sgu_ln_b, m_sgu_w, m_sgu_b, m_w_attn_branch, m_w_sgu_branch, m_w_out, m_mix_post_g, m_ffn2_pre_g, m_ffn2_w1, m_ffn2_w2, m_ffn2_post_g, v_ffn1_pre_g, v_ffn1_w1, v_ffn1_w2, v_ffn1_post_g, v_mix_pre_g, v_w_in, v_attn_sinks, v_sgu_ln_g, v_sgu_ln_b, v_sgu_w, v_sgu_b, v_w_attn_branch, v_w_sgu_branch, v_w_out, v_mix_post_g, v_ffn2_pre_g, v_ffn2_w1, v_ffn2_w2, v_ffn2_post_g):
    w = dict(zip(WEIGHT_NAMES, (ffn1_pre_g, ffn1_w1, ffn1_w2, ffn1_post_g, mix_pre_g, w_in, attn_sinks, sgu_ln_g,
                                sgu_ln_b, sgu_w, sgu_b, w_attn_branch, w_sgu_branch, w_out, mix_post_g, ffn2_pre_g,
                                ffn2_w1, ffn2_w2, ffn2_post_g)))
    mom = dict(zip(WEIGHT_NAMES, (m_ffn1_pre_g, m_ffn1_w1, m_ffn1_w2, m_ffn1_post_g, m_mix_pre_g, m_w_in,
                                  m_attn_sinks, m_sgu_ln_g, m_sgu_ln_b, m_sgu_w, m_sgu_b, m_w_attn_branch,
                                  m_w_sgu_branch, m_w_out, m_mix_post_g, m_ffn2_pre_g, m_ffn2_w1, m_ffn2_w2,
                                  m_ffn2_post_g)))
    var = dict(zip(WEIGHT_NAMES, (v_ffn1_pre_g, v_ffn1_w1, v_ffn1_w2, v_ffn1_post_g, v_mix_pre_g, v_w_in,
                                  v_attn_sinks, v_sgu_ln_g, v_sgu_ln_b, v_sgu_w, v_sgu_b, v_w_attn_branch,
                                  v_w_sgu_branch, v_w_out, v_mix_post_g, v_ffn2_pre_g, v_ffn2_w1, v_ffn2_w2,
                                  v_ffn2_post_g)))
    T = x.shape[1]
    xs = x.reshape(T, D)
    target = loss_target.reshape(T, D)

    for n in COL_SHARDED:
        w[n], mom[n], var[n] = (jnp.swapaxes(t[n], 1, 2) for t in (w, mom, var))

    per_layer = [cast_layer([w[n] for n in MATRICES], l) for l in range(DEPTH)]
    shard = {n: [per_layer[l][i] for l in range(DEPTH)] for i, n in enumerate(MATRICES)}
    weights = [{} for _ in range(DEPTH)]
    ffn1, ffn2, squares = ("ffn1_w1", "ffn1_w2"), ("ffn2_w1", "ffn2_w2"), ("w_attn_branch", "w_sgu_branch", "w_out")
    for n, g in zip(ffn1, all_gather_weights([shard[n][0] for n in ffn1])):
        weights[0][n] = g.reshape(-1, D)

    def fetch(groups):
        jobs = [("all", shard[n][l]) for l, names in groups if l < DEPTH for n in names]
        return PeerCopies(jobs) if jobs else None

    def landed(groups, arrivals):
        slots = [(l, n) for l, names in groups if l < DEPTH for n in names]
        for (l, n), g in zip(slots, arrivals):
            weights[l][n] = g.reshape(-1, D)

    vec = lambda n, l: w[n][l].reshape(1, D)
    sgu_bb = [jnp.broadcast_to(w["sgu_b"][l][:, :, None], (SGU_GROUPS, BLK, BLK)) for l in range(DEPTH)]

    saved = []
    h = xs
    for l in range(DEPTH):
        wl = weights[l]
        ahead = lambda names: [(0, names)] if l == 0 else []
        x0 = h
        groups = [(l + 1, ffn1)] + ahead(("w_in",))
        (x1, a1, s1, y1), got = ffn_fwd(x0, vec("ffn1_pre_g", l), vec("ffn1_post_g", l), wl["ffn1_w1"], wl["ffn1_w2"],
                                        fetch(groups))
        landed(groups, got)
        groups = [(l + 1, ("w_in",))] + ahead(squares)
        (z,), got = mix_in_fwd(x1, vec("mix_pre_g", l), wl["w_in"], fetch(groups))
        landed(groups, got)
        groups = ahead(("ffn2_w1",))
        (ya,), got = attn_fwd(z, w["attn_sinks"][l], fetch(groups))
        landed(groups, got)
        groups = [(l + 1, squares)] + ahead(("ffn2_w2",))
        (x2, ysg, pa, pb, o), got = mix_fwd_out(
            x1, z, ya, vec("mix_post_g", l), vec("sgu_ln_g", l), vec("sgu_ln_b", l), w["sgu_w"][l], sgu_bb[l],
            wl["w_attn_branch"], wl["w_sgu_branch"], wl["w_out"], fetch(groups))
        landed(groups, got)
        groups = [(l + 1, ffn2)]
        last = (vec("ffn2_pre_g", l), vec("ffn2_post_g", l), wl["ffn2_w1"], wl["ffn2_w2"])
        if l < DEPTH - 1:
            (h, a2, s2, y2), got = ffn_fwd(x2, *last, fetch(groups))
            landed(groups, got)
        else:
            (a2, s2, y2, sq, dx), _ = ffn_fwd(x2, *last, target=target)
        saved.append((x0, a1, s1, y1, x1, z, ya, ysg, pa, pb, o, x2, a2, s2, y2))

    loss = lax.psum(0.5 / D * jnp.sum(sq), ("x", "y", "c"))

    def ffn_wgrads(hb, da, s, dy):
        return [("own", wgrad(da, hb, "wgrad_ffn_w1", D // 2).reshape(N_DEV, W1_SHARD, D)),
                ("own", wgrad(s, dy, "wgrad_ffn_w2", D).reshape(N_DEV, W2_SHARD, D))]

    parts = {n: [None] * DEPTH for n in MATRICES}
    small = {n: [None] * DEPTH for n in WEIGHT_NAMES if n not in MATRICES + ("sgu_w",)}
    sgu_w_parts = [None] * DEPTH
    waiting = None
    for l in reversed(range(DEPTH)):
        wl = weights[l]
        x0, a1, s1, y1, x1, z, ya, ysg, pa, pb, o, x2, a2, s2, y2 = saved[l]
        (dx, da, dy, hb, dpg, dqg), got = ffn_bwd(
            dx, x2, y2, a2, vec("ffn2_pre_g", l), vec("ffn2_post_g", l), wl["ffn2_w1"], wl["ffn2_w2"],
            PeerCopies(waiting) if waiting else None)
        if waiting:
            parts["ffn1_w1"][l + 1], parts["ffn1_w2"][l + 1] = got
        small["ffn2_pre_g"][l], small["ffn2_post_g"][l] = dpg, dqg

        (dzb, dya, mg, dob, dpa, dpb, dqg, dlg, dlb, dsw, dsb), got = mix_bwd_out(
            dx, z, o, pa, pb, vec("mix_post_g", l), vec("sgu_ln_g", l), vec("sgu_ln_b", l), w["sgu_w"][l], sgu_bb[l],
            wl["w_attn_branch"], wl["w_sgu_branch"], wl["w_out"], PeerCopies(ffn_wgrads(hb, da, s2, dy)))
        parts["ffn2_w1"][l], parts["ffn2_w2"][l] = got
        dz, dsink = attn_bwd(z, ya, dya, dzb, w["attn_sinks"][l])
        dx, hb, dpg = mix_in_bwd(dx, x1, dz, vec("mix_pre_g", l), wl["w_in"])
        small["mix_pre_g"][l], small["mix_post_g"][l] = dpg, dqg
        small["sgu_ln_g"][l], small["sgu_ln_b"][l] = dlg, dlb
        small["sgu_b"][l] = dsb[:, :, 0]
        small["attn_sinks"][l] = dsink[0, :N_Q_HEADS]
        mixer = [("own", wgrad(dz, hb, "wgrad_w_in", D // 2).reshape(N_DEV, WIN_SHARD, D))]
        mixer += [("own", wgrad(act, cot, "wgrad_square", D, tk=2 * TK_WGRAD).reshape(N_DEV, SQ_SHARD, D))
                  for act, cot in ((ya, dpa), (ysg, dpb), (mg, dob))]
        mixer.append(("all", dsw.reshape(SGU_GROUPS * BLK, BLK)))
        if l == 0:
            small["ffn1_pre_g"][0] = small["ffn1_post_g"][0] = jnp.zeros((1, D), F32)
            mixer.append(("all", _pack_small({n: jnp.stack(v, axis=0) for n, v in small.items()})))

        (dx, da, dy, hb, dpg, dqg), got = ffn_bwd(
            dx, x0, y1, a1, vec("ffn1_pre_g", l), vec("ffn1_post_g", l), wl["ffn1_w1"], wl["ffn1_w2"],
            PeerCopies(mixer))
        parts["w_in"][l], parts["w_attn_branch"][l], parts["w_sgu_branch"][l], parts["w_out"][l] = got[:4]
        sgu_w_parts[l] = got[4]
        if l > 0:
            small["ffn1_pre_g"][l], small["ffn1_post_g"][l] = dpg, dqg
            waiting = ffn_wgrads(hb, da, s1, dy)
    grad_x = dx.reshape(x.shape)
    small_parts = got[5]

    def late_rows(pre, post):
        return jnp.concatenate([pre.reshape(1, D), post.reshape(1, D), jnp.zeros((LATE_ROWS - 2, D), F32)], axis=0)

    g_w1 = wgrad(da, hb, "wgrad_ffn_w1", D // 2).reshape(N_DEV, W1_SHARD, D)
    g_w2, got = wgrad(s1, dy, "wgrad_ffn_w2", D, PeerCopies([("own", g_w1)]))
    parts["ffn1_w1"][0] = got[0]
    parts["ffn1_w2"][0], late_parts = exchange_last([("own", g_w2.reshape(N_DEV, W2_SHARD, D)),
                                                     ("all", late_rows(dpg, dqg))])

    grads, deltas, new_m, new_v = {}, {}, {}, {}
    for n in MATRICES:
        grads[n], deltas[n], new_m[n], new_v[n] = adamw_sum(parts[n], w[n], mom[n], var[n], "adamw_" + n)
    for n in COL_SHARDED:
        for out in (grads, deltas, new_m, new_v):
            out[n] = jnp.swapaxes(out[n], 1, 2)
    flat = lambda t: t["sgu_w"].reshape(DEPTH, SGU_GROUPS * BLK, BLK)
    for out, r in zip((grads, deltas, new_m, new_v),
                      adamw_sum(sgu_w_parts, flat(w), flat(mom), flat(var), "adamw_sgu_w")):
        out["sgu_w"] = r.reshape(w["sgu_w"].shape)
    res = adamw_sum([small_parts], _pack_small(w)[None], _pack_small(mom)[None], _pack_small(var)[None],
                    "adamw_small")
    late = adamw_sum([late_parts], *[late_rows(t["ffn1_pre_g"][0], t["ffn1_post_g"][0])[None] for t in (w, mom, var)],
                     "adamw_late")
    for out, packed, late_out in zip((grads, deltas, new_m, new_v), res, late):
        out.update(_unpack_small(packed[0]))
        for row, n in enumerate(("ffn1_pre_g", "ffn1_post_g")):
            out[n] = jnp.concatenate([late_out[0, row:row + 1], out[n][1:]], axis=0)

    return (loss, grad_x, *[grads[n] for n in WEIGHT_NAMES], *[deltas[n] for n in WEIGHT_NAMES],
            *[new_m[n] for n in WEIGHT_NAMES], *[new_v[n] for n in WEIGHT_NAMES])
```

```python
import math

import jax
import jax.numpy as jnp
from jax import lax
from jax.experimental import pallas as pl
from jax.experimental.pallas import tpu as pltpu

F32 = jnp.float32
BF16 = jnp.bfloat16

N_DEV = 8
D = 1024
FF = 2816
DEPTH = 4
HEAD_DIM = 64
N_Q_HEADS = 16
N_KV_HEADS = 2
Q_PER_KV = N_Q_HEADS // N_KV_HEADS
BLK = 128
SGU_GROUPS = 8
IN_WIDTH = 5376
W1_SHARD = 2 * FF // N_DEV
WIN_SHARD = IN_WIDTH // N_DEV
W2_SHARD = FF // N_DEV
SQ_SHARD = D // N_DEV

RMS_EPS = 1e-6
LN_EPS = 1e-5
MASK_VALUE = -1e30
ATTN_SCALE = 1.0 / math.sqrt(HEAD_DIM)

ADAM_LR = 0.001
ADAM_B1 = 0.9
ADAM_B2 = 0.999
ADAM_EPS = 1e-08
ADAM_WD = 0.01
ADAM_STEP = 10

VMEM_LIMIT_V7X = 56 * 1024 * 1024

WIN_SEGMENTS = ((0, 1024, 0), (1024, 256, 5120), (1280, 1024, 1024), (2304, 1024, 2048), (3328, 1024, 3072),
                (4352, 1024, 4096))
ZQ, ZU, ZV, ZGA, ZGB, ZKV = 0, 1024, 2048, 3072, 4096, 5120
DZ_Q, DZ_KV, DZ_MID = 0, 1024, 1280
DZ_U, DZ_V, DZ_GA, DZ_GB = 1280, 2304, 3328, 4352

FF_CHUNKS = ((0, 1024), (1024, 1024), (2048, 768))
WIN_CHUNKS = ((0, 1792), (1792, 1792), (3584, 1792))

SMALL_ROWS = 16
LATE_ROWS = 16
SMALL_VEC = ("ffn1_pre_g", "ffn1_post_g", "mix_pre_g", "mix_post_g", "ffn2_pre_g", "ffn2_post_g",
             "sgu_ln_g", "sgu_ln_b")

TM_FFN_FWD = 512
TM_FFN_BWD = 256
TM_MIX_IN = 1024
TM_MIX = 512
TM_MIX_BWD = 256
TQ_ATTN = 512
TK_WGRAD = 1024
TR_ADAM = 256


def _params(n_grid, vmem=VMEM_LIMIT_V7X):
    return pltpu.CompilerParams(dimension_semantics=("arbitrary",) * n_grid, vmem_limit_bytes=vmem)


def _dot(a, b):
    return jnp.dot(a, b, preferred_element_type=F32)


def _dot_nt(a, b):
    return lax.dot_general(a, b, (((1,), (1,)), ((), ())), preferred_element_type=F32)


def _dot_tn(a, b):
    return lax.dot_general(a, b, (((0,), (0,)), ((), ())), preferred_element_type=F32)


def _rms(x, g):
    r = lax.rsqrt(jnp.mean(x * x, axis=-1, keepdims=True) + RMS_EPS)
    n = x * r
    return n * g, n, r


def _rms_bwd(n, r, g, dy):
    dn = dy * g
    return r * (dn - n * jnp.mean(dn * n, axis=-1, keepdims=True))


def _colsum(v):
    return jnp.sum(v, axis=0, keepdims=True)


def _sigmoid(v):
    return 0.5 * jnp.tanh(0.5 * v) + 0.5


def _gelu_parts(v):
    cdf = 0.5 * lax.erf(v * (1.0 / math.sqrt(2.0))) + 0.5
    return cdf, jnp.exp2(v * v * (-0.5 / math.log(2.0))) * (1.0 / math.sqrt(2.0 * math.pi))


def _row_tile(rows, cap):
    return max(t for t in range(16, min(rows, cap) + 1, 16) if rows % t == 0)


def _row_spec(tm, width, col_block=0):
    return pl.BlockSpec((tm, width), lambda i, cb=col_block: (i, cb))


def _full_spec(shape):
    nd = len(shape)
    return pl.BlockSpec(tuple(shape), lambda *_: (0,) * nd)


def _whole_spec(arr):
    nd = arr.ndim
    return pl.BlockSpec(tuple(arr.shape), lambda *_: (0,) * nd, pipeline_mode=pl.Buffered(1))


HBM_SPEC = pl.BlockSpec(memory_space=pltpu.HBM)
MESH_ID = pl.DeviceIdType.MESH
RELATIONS = tuple((rx, ry, rc) for rx in (0, 1) for ry in (0, 1) for rc in (0, 1))[1:]


class PeerCopies:
    def __init__(self, jobs):
        self.kinds = [k for k, _ in jobs]
        self.arrays = [a for _, a in jobs]
        self.n = len(jobs)
        self.out_shape = [jax.ShapeDtypeStruct((N_DEV,) + a.shape if k == "all" else a.shape, a.dtype)
                          for k, a in jobs]
        self.scratch = [pltpu.SemaphoreType.DMA((7 * self.n,)), pltpu.SemaphoreType.DMA((7 * self.n,)),
                        pltpu.SemaphoreType.DMA((self.n,))]

    def _copies(self, ins, outs, sems, arriving):
        send_sems, recv_sems, local_sems = sems
        x, y, c = lax.axis_index("x"), lax.axis_index("y"), lax.axis_index("c")
        me = 4 * x + 2 * y + c
        src = lambda a, d: ins[a] if self.kinds[a] == "all" else ins[a].at[d]
        if not arriving:
            local = [pltpu.make_async_copy(src(a, me), outs[a].at[me], local_sems.at[a]) for a in range(self.n)]
        remote = []
        for k, (rx, ry, rc) in enumerate(RELATIONS):
            tx, ty, tc = (1 - x if rx else x), (1 - y if ry else y), (1 - c if rc else c)
            peer = 4 * tx + 2 * ty + tc
            for a in range(self.n):
                from_slot, to_slot = (me, peer) if arriving else (peer, me)
                remote.append(pltpu.make_async_remote_copy(
                    src_ref=src(a, from_slot), dst_ref=outs[a].at[to_slot],
                    send_sem=send_sems.at[a * 7 + k], recv_sem=recv_sems.at[a * 7 + k],
                    device_id=(tx, ty, tc), device_id_type=MESH_ID))
        return remote if arriving else (local, remote)

    def start(self, ins, outs, sems):
        local, sends = self._copies(ins, outs, sems, False)
        for cp in local + sends:
            cp.start()

    def wait(self, ins, outs, sems):
        for cp in self._copies(ins, outs, sems, True):
            cp.wait_recv()
        local, sends = self._copies(ins, outs, sems, False)
        for cp in sends:
            cp.wait_send()
        for cp in local:
            cp.wait()


def _call(body, grid, in_specs, args, out_specs, out_shape, name, scratch=(), copies=None):
    if copies is None:
        outs = pl.pallas_call(body, grid=grid, in_specs=list(in_specs), out_specs=list(out_specs),
                              out_shape=list(out_shape), scratch_shapes=list(scratch), name=name,
                              compiler_params=_params(len(grid)))(*args)
        return outs, []
    n_in, n_out, n_scr, nc = len(in_specs), len(out_specs), len(scratch), copies.n

    def at_step(steps):
        hit = pl.program_id(0) == steps[0]
        for axis in range(1, len(grid)):
            hit = jnp.logical_and(hit, pl.program_id(axis) == steps[axis])
        return hit

    def hosted(*refs):
        ins, refs = refs[:n_in], refs[n_in:]
        c_in, refs = refs[:nc], refs[nc:]
        outs, refs = refs[:n_out], refs[n_out:]
        c_out, refs = refs[:nc], refs[nc:]
        scr, sems = refs[:n_scr], refs[n_scr:]

        @pl.when(at_step([0] * len(grid)))
        def _():
            copies.start(c_in, c_out, sems)

        body(*ins, *outs, *scr)

        @pl.when(at_step([g - 1 for g in grid]))
        def _():
            copies.wait(c_in, c_out, sems)

    outs = pl.pallas_call(hosted, grid=grid, in_specs=list(in_specs) + [HBM_SPEC] * nc,
                          out_specs=list(out_specs) + [HBM_SPEC] * nc,
                          out_shape=list(out_shape) + copies.out_shape,
                          scratch_shapes=list(scratch) + copies.scratch, name=name,
                          compiler_params=_params(len(grid)))(*args, *copies.arrays)
    return outs[:n_out], outs[n_out:]


def cast_layer(ws, l):
    n = len(ws)

    def body(*refs):
        for w_ref, o_ref in zip(refs[:n], refs[n:]):
            o_ref[...] = w_ref[...].astype(BF16)

    halves = [w.shape[1] // 2 for w in ws]
    return pl.pallas_call(
        body, grid=(2,),
        in_specs=[pl.BlockSpec((None, h, D), lambda i: (l, i, 0)) for h in halves],
        out_specs=[_row_spec(h, D) for h in halves],
        out_shape=[jax.ShapeDtypeStruct(w.shape[1:], BF16) for w in ws],
        name="cast_layer", compiler_params=_params(1),
    )(*ws)


def all_gather_weights(shards):
    n = len(shards)

    def body(*refs):
        ins, outs = refs[:n], refs[n:2 * n]
        send_sems, recv_sems, local_sems = refs[2 * n:]
        x, y, c = lax.axis_index("x"), lax.axis_index("y"), lax.axis_index("c")
        me, sibling = (x, y, c), (x, y, 1 - c)
        chips = [(1 - x, y), (x, 1 - y), (1 - x, 1 - y)]

        def slot(a, owner):
            return outs[a].at[4 * owner[0] + 2 * owner[1] + owner[2]]

        def copy(a, k, owner, to, src=None):
            return pltpu.make_async_remote_copy(
                src_ref=slot(a, owner) if src is None else src, dst_ref=slot(a, owner),
                send_sem=send_sems.at[a * 7 + k], recv_sem=recv_sems.at[a * 7 + k],
                device_id=to, device_id_type=MESH_ID)

        mine = [pltpu.make_async_copy(ins[a], slot(a, me), local_sems.at[a]) for a in range(n)]
        for cp in mine:
            cp.start()
        first = []
        for a in range(n):
            first.append(copy(a, 0, me, sibling, src=ins[a]))
            first += [copy(a, 1 + j, me, (*chip, c), src=ins[a]) for j, chip in enumerate(chips)]
        for cp in first:
            cp.start()
        passed = []
        for j, chip in enumerate(chips):
            for a in range(n):
                copy(a, 1 + j, (*chip, c), me).wait_recv()
                fwd = copy(a, 4 + j, (*chip, c), sibling)
                fwd.start()
                passed.append(fwd)
        for a in range(n):
            copy(a, 0, sibling, me).wait_recv()
            for j, chip in enumerate(chips):
                copy(a, 4 + j, (*chip, 1 - c), me).wait_recv()
        for cp in first + passed:
            cp.wait_send()
        for cp in mine:
            cp.wait()

    out_shape = [jax.ShapeDtypeStruct((N_DEV,) + s.shape, s.dtype) for s in shards]
    return pl.pallas_call(
        body, in_specs=[HBM_SPEC] * n, out_specs=[HBM_SPEC] * n, out_shape=out_shape,
        scratch_shapes=[pltpu.SemaphoreType.DMA((7 * n,)), pltpu.SemaphoreType.DMA((7 * n,)),
                        pltpu.SemaphoreType.DMA((n,))],
        name="all_gather_weights",
    )(*shards)


def exchange_last(jobs):
    copies = PeerCopies(jobs)
    n = copies.n

    def body(*refs):
        ins, outs, sems = refs[:n], refs[n:2 * n], refs[2 * n:]
        copies.start(ins, outs, sems)
        copies.wait(ins, outs, sems)

    return pl.pallas_call(
        body, in_specs=[HBM_SPEC] * n, out_specs=[HBM_SPEC] * n, out_shape=copies.out_shape,
        scratch_shapes=copies.scratch, name="exchange_last",
    )(*copies.arrays)


def ffn_fwd(x, pre_g, post_g, w1, w2, copies=None, target=None):
    T = x.shape[0]
    tm = min(TM_FFN_FWD, T)

    def body(*refs):
        if target is None:
            x_ref, pg_ref, qg_ref, w1_ref, w2_ref, xo_ref, a_ref, s_ref, y_ref = refs
        else:
            x_ref, pg_ref, qg_ref, w1_ref, w2_ref, t_ref, a_ref, s_ref, y_ref, sq_ref, dy_ref = refs
        xv = x_ref[...]
        h, _, _ = _rms(xv, pg_ref[...])
        hb = h.astype(BF16)
        acc = jnp.zeros((tm, D), F32)
        for c0, cw in FF_CHUNKS:
            g = _dot_nt(hb, w1_ref[c0:c0 + cw, :])
            u = _dot_nt(hb, w1_ref[FF + c0:FF + c0 + cw, :])
            sg = _sigmoid(g)
            si = g * sg
            a_ref[:, c0:c0 + cw] = (u * (sg + si - si * sg)).astype(BF16)
            a_ref[:, FF + c0:FF + c0 + cw] = si.astype(BF16)
            s = (si * u).astype(BF16)
            s_ref[:, c0:c0 + cw] = s
            acc = acc + _dot(s, w2_ref[c0:c0 + cw, :])
        y_ref[...] = acc.astype(BF16)
        o, _, _ = _rms(acc, qg_ref[...])
        xo = xv + 0.5 * o
        if target is None:
            xo_ref[...] = xo
        else:
            @pl.when(pl.program_id(0) == 0)
            def _():
                sq_ref[...] = jnp.zeros_like(sq_ref)

            err = xo - t_ref[...]
            sq_ref[...] += _colsum(err * err)
            dy_ref[...] = err * (1.0 / D)

    in_specs = [_row_spec(tm, D), _full_spec((1, D)), _full_spec((1, D)), _whole_spec(w1), _whole_spec(w2)]
    saved_specs = [_row_spec(tm, 2 * FF), _row_spec(tm, FF), _row_spec(tm, D)]
    saved_shapes = [jax.ShapeDtypeStruct((T, 2 * FF), BF16), jax.ShapeDtypeStruct((T, FF), BF16),
                    jax.ShapeDtypeStruct((T, D), BF16)]
    row_f32 = jax.ShapeDtypeStruct((T, D), F32)
    if target is None:
        return _call(body, (T // tm,), in_specs, (x, pre_g, post_g, w1, w2), [_row_spec(tm, D)] + saved_specs,
                     [row_f32] + saved_shapes, "ffn_fwd", copies=copies)
    return _call(body, (T // tm,), in_specs + [_row_spec(tm, D)], (x, pre_g, post_g, w1, w2, target),
                 saved_specs + [_full_spec((1, D)), _row_spec(tm, D)],
                 saved_shapes + [jax.ShapeDtypeStruct((1, D), F32), row_f32], "ffn_fwd_loss", copies=copies)


def ffn_bwd(dxo, x, y, a, pre_g, post_g, w1, w2, copies=None):
    T = x.shape[0]
    tm = min(TM_FFN_BWD, T)

    def body(dxo_ref, x_ref, y_ref, a_ref, pg_ref, qg_ref, w1_ref, w2_ref,
             dx_ref, da_ref, dy_ref, hb_ref, dpg_ref, dqg_ref):
        @pl.when(pl.program_id(0) == 0)
        def _():
            dpg_ref[...] = jnp.zeros_like(dpg_ref)
            dqg_ref[...] = jnp.zeros_like(dqg_ref)

        dxo = dxo_ref[...]
        qg = qg_ref[...]
        _, ny, ry = _rms(y_ref[...].astype(F32), qg)
        dn = 0.5 * dxo
        dqg_ref[...] += _colsum(dn * ny)
        dyb = _rms_bwd(ny, ry, qg, dn).astype(BF16)
        dy_ref[...] = dyb
        pg = pg_ref[...]
        h, nx, rx = _rms(x_ref[...], pg)
        hb_ref[...] = h.astype(BF16)
        dh = jnp.zeros((tm, D), F32)
        for c0, cw in FF_CHUNKS:
            ds = _dot_nt(dyb, w2_ref[c0:c0 + cw, :])
            dg = (ds * a_ref[:, c0:c0 + cw].astype(F32)).astype(BF16)
            du = (ds * a_ref[:, FF + c0:FF + c0 + cw].astype(F32)).astype(BF16)
            da_ref[:, c0:c0 + cw] = dg
            da_ref[:, FF + c0:FF + c0 + cw] = du
            dh = dh + _dot(dg, w1_ref[c0:c0 + cw, :]) + _dot(du, w1_ref[FF + c0:FF + c0 + cw, :])
        dpg_ref[...] += _colsum(dh * nx)
        dx_ref[...] = dxo + _rms_bwd(nx, rx, pg, dh)

    return _call(
        body, (T // tm,),
        [_row_spec(tm, D), _row_spec(tm, D), _row_spec(tm, D), _row_spec(tm, 2 * FF),
         _full_spec((1, D)), _full_spec((1, D)), _whole_spec(w1), _whole_spec(w2)],
        (dxo, x, y, a, pre_g, post_g, w1, w2),
        [_row_spec(tm, D), _row_spec(tm, 2 * FF), _row_spec(tm, D), _row_spec(tm, D),
         _full_spec((1, D)), _full_spec((1, D))],
        [jax.ShapeDtypeStruct((T, D), F32), jax.ShapeDtypeStruct((T, 2 * FF), BF16),
         jax.ShapeDtypeStruct((T, D), BF16), jax.ShapeDtypeStruct((T, D), BF16),
         jax.ShapeDtypeStruct((1, D), F32), jax.ShapeDtypeStruct((1, D), F32)],
        "ffn_bwd", copies=copies)


def mix_in_fwd(x, pre_g, win, copies=None):
    T = x.shape[0]
    tm = min(TM_MIX_IN, T)

    def body(x_ref, pg_ref, w_ref, z_ref):
        h, _, _ = _rms(x_ref[...], pg_ref[...])
        hb = h.astype(BF16)
        for w0, n, z0 in WIN_SEGMENTS:
            z_ref[:, z0:z0 + n] = _dot_nt(hb, w_ref[w0:w0 + n, :]).astype(BF16)

    return _call(
        body, (T // tm,), [_row_spec(tm, D), _full_spec((1, D)), _whole_spec(win)], (x, pre_g, win),
        [_row_spec(tm, IN_WIDTH)], [jax.ShapeDtypeStruct((T, IN_WIDTH), BF16)], "mix_in_fwd", copies=copies)


STACK = Q_PER_KV // 2
SROWS = STACK * BLK


def _sees_own():
    qi = lax.broadcasted_iota(jnp.int32, (SROWS, BLK), 0) & (BLK - 1)
    return lax.broadcasted_iota(jnp.int32, (SROWS, BLK), 1) <= qi


def _both_blocks(picked, own):
    return jnp.concatenate([jnp.where(own, 0.0, picked), jnp.where(own, picked, 0.0)], axis=1)


def _lane_half(rows):
    return lax.broadcasted_iota(jnp.int32, (rows, BLK), 1) // HEAD_DIM


def _stack(ref, b, g):
    return jnp.concatenate([ref[b * BLK:(b + 1) * BLK, (STACK * g + j) * BLK:(STACK * g + j + 1) * BLK]
                            for j in range(STACK)], axis=0)


def _placed(pair, g, hp, fill):
    src = pair if hp == g else pltpu.roll(pair, HEAD_DIM, 1)
    return jnp.where(_lane_half(2 * BLK) == hp, src, fill).astype(BF16)


def _sink_column(sink_ref, g, hp):
    rb = lax.broadcasted_iota(jnp.int32, (SROWS, 1), 0) // BLK
    col = jnp.full((SROWS, 1), sink_ref[Q_PER_KV * g + hp], F32)
    for j in range(1, STACK):
        col = jnp.where(rb == j, sink_ref[Q_PER_KV * g + 2 * j + hp], col)
    return col


def _attn_scores(qs, kz, sink_col, own, no_previous):
    raw = _dot_nt(qs, kz)
    prev = raw[:, 0:BLK]
    if no_previous is not None:
        prev = prev + no_previous.astype(F32) * MASK_VALUE
    picked = jnp.where(own, raw[:, BLK:2 * BLK], prev)
    sink_raw = sink_col * (1.0 / ATTN_SCALE)
    m = jnp.maximum(jnp.max(picked, axis=-1, keepdims=True), sink_raw)
    factor = ATTN_SCALE / math.log(2.0)
    return jnp.exp2((picked - m) * factor), jnp.exp2((sink_raw - m) * factor)


def _kv_specs(tq, nb):
    kv_blk = ZKV // (2 * BLK)
    return [pl.BlockSpec((tq, 2 * BLK), lambda i: (i, kv_blk)),
            pl.BlockSpec((BLK, 2 * BLK), lambda i: (jnp.maximum(i * nb - 1, 0), kv_blk))]


def attn_fwd(z, sinks, copies=None):
    T = z.shape[0]
    tq = min(TQ_ATTN, T)
    nb = tq // BLK

    def body(sink_ref, q_ref, kv_ref, kvh_ref, o_ref):
        i = pl.program_id(0)
        low = _lane_half(SROWS) == 0
        own = _sees_own()
        for b in range(nb):
            kvp = kvh_ref[...] if b == 0 else kv_ref[(b - 1) * BLK:b * BLK, :]
            kv2 = jnp.concatenate([kvp, kv_ref[b * BLK:(b + 1) * BLK, :]], axis=0).astype(F32)
            no_previous = (i == 0) if b == 0 else None
            for g in range(N_KV_HEADS):
                qs = _stack(q_ref, b, g)
                r, e = [], []
                for hp in range(2):
                    p, e_sink = _attn_scores(qs, _placed(kv2[:, 0:BLK], g, hp, 0.0), _sink_column(sink_ref, g, hp),
                                             own, no_previous)
                    r.append(_dot(_both_blocks(p, own).astype(BF16), _placed(kv2[:, BLK:2 * BLK], g, hp, 1.0)))
                    e.append(e_sink)
                den = pltpu.roll(jnp.where(low, r[1], r[0]), HEAD_DIM, 1) + jnp.where(low, e[0], e[1])
                out = (jnp.where(low, r[0], r[1]) * (1.0 / den)).astype(BF16)
                for j in range(STACK):
                    o_ref[b * BLK:(b + 1) * BLK, (STACK * g + j) * BLK:(STACK * g + j + 1) * BLK] = \
                        out[j * BLK:(j + 1) * BLK, :]

    return _call(
        body, (T // tq,), [pl.BlockSpec(memory_space=pltpu.SMEM), _row_spec(tq, D)] + _kv_specs(tq, nb),
        (sinks, z, z, z), [_row_spec(tq, D)], [jax.ShapeDtypeStruct((T, D), BF16)], "attn_fwd", copies=copies)


def attn_bwd(z, ya, dya, dz, sinks):
    T = z.shape[0]
    tq = min(TQ_ATTN, T)
    nb = tq // BLK
    nt = T // tq

    def body(sink_ref, q_ref, kv_ref, kvh_ref, y_ref, dy_ref, dz_whole_ref, dz_ref, dsink_ref, acc_ref, carry_ref):
        i = pl.program_id(0)
        first_tile = i == nt - 1

        @pl.when(i == 0)
        def _():
            carry_ref[...] = jnp.zeros_like(carry_ref)
            dsink_ref[...] = jnp.zeros_like(dsink_ref)

        acc_ref[...] = jnp.zeros_like(acc_ref)
        lane = lax.broadcasted_iota(jnp.int32, (1, BLK), 1)
        dsink = jnp.zeros((1, BLK), F32)
        half = _lane_half(SROWS)
        own = _sees_own()
        for b in range(nb):
            kvp = kvh_ref[...] if b == 0 else kv_ref[(b - 1) * BLK:b * BLK, :]
            kv2 = jnp.concatenate([kvp, kv_ref[b * BLK:(b + 1) * BLK, :]], axis=0).astype(F32)
            no_previous = first_tile if b == 0 else None
            dk_groups, dv_groups = [], []
            for g in range(N_KV_HEADS):
                qs = _stack(q_ref, b, g)
                dys = _stack(dy_ref, b, g)
                dyy = dys.astype(F32) * _stack(y_ref, b, g).astype(F32)
                dq = jnp.zeros((SROWS, BLK), F32)
                ds_both, pn_both = [], []
                for hp in range(2):
                    kz = _placed(kv2[:, 0:BLK], g, hp, 0.0)
                    p, e_sink = _attn_scores(qs, kz, _sink_column(sink_ref, g, hp), own, no_previous)
                    inv = 1.0 / (jnp.sum(p, axis=-1, keepdims=True) + e_sink)
                    p = p * inv
                    delta = jnp.sum(jnp.where(half == hp, dyy, 0.0), axis=-1, keepdims=True)
                    dp = _dot_nt(dys, _placed(kv2[:, BLK:2 * BLK], g, hp, 0.0))
                    ds = p * (jnp.where(own, dp[:, BLK:2 * BLK], dp[:, 0:BLK]) - delta)
                    ds = _both_blocks(ds, own).astype(BF16)
                    pn = _both_blocks(p, own)
                    sink_term = e_sink * inv * delta
                    for j in range(STACK):
                        dsink = dsink + jnp.where(lane == Q_PER_KV * g + 2 * j + hp,
                                                  -_colsum(sink_term[j * BLK:(j + 1) * BLK, :]), 0.0)
                    dq = dq + _dot(ds, kz)
                    ds_both.append(ds)
                    pn_both.append(pn.astype(BF16))
                dk_t = _dot_tn(qs, jnp.concatenate(ds_both, axis=1))
                dv_t = _dot_tn(dys, jnp.concatenate(pn_both, axis=1))
                for t, groups in ((dk_t, dk_groups), (dv_t, dv_groups)):
                    groups.append(t[0:HEAD_DIM, 0:2 * BLK] + t[HEAD_DIM:BLK, 2 * BLK:4 * BLK])
                dqb = (dq * ATTN_SCALE).astype(BF16)
                for j in range(STACK):
                    dz_ref[b * BLK:(b + 1) * BLK, DZ_Q + (STACK * g + j) * BLK:DZ_Q + (STACK * g + j + 1) * BLK] = \
                        dqb[j * BLK:(j + 1) * BLK, :]
            acc_ref[b * BLK:(b + 2) * BLK, 0:BLK] += jnp.concatenate(dk_groups, axis=0).T * ATTN_SCALE
            acc_ref[b * BLK:(b + 2) * BLK, BLK:2 * BLK] += jnp.concatenate(dv_groups, axis=0).T
        dsink_ref[...] += dsink
        if nb > 1:
            dz_ref[0:tq - BLK, DZ_KV:DZ_MID] = acc_ref[BLK:tq, :].astype(BF16)
        dz_ref[tq - BLK:tq, DZ_KV:DZ_MID] = (acc_ref[tq:tq + BLK, :] + carry_ref[...]).astype(BF16)
        carry_ref[...] = acc_ref[0:BLK, :]

    kv_blk = ZKV // (2 * BLK)
    return pl.pallas_call(
        body, grid=(nt,),
        in_specs=[pl.BlockSpec(memory_space=pltpu.SMEM),
                  pl.BlockSpec((tq, D), lambda i: (nt - 1 - i, 0)),
                  pl.BlockSpec((tq, 2 * BLK), lambda i: (nt - 1 - i, kv_blk)),
                  pl.BlockSpec((BLK, 2 * BLK), lambda i: (jnp.maximum((nt - 1 - i) * nb - 1, 0), kv_blk)),
                  pl.BlockSpec((tq, D), lambda i: (nt - 1 - i, 0)),
                  pl.BlockSpec((tq, D), lambda i: (nt - 1 - i, 0)),
                  pl.BlockSpec(memory_space=pl.ANY)],
        out_specs=[pl.BlockSpec((tq, DZ_MID), lambda i: (nt - 1 - i, 0)), _full_spec((1, BLK))],
        out_shape=[jax.ShapeDtypeStruct((T, IN_WIDTH), BF16), jax.ShapeDtypeStruct((1, BLK), F32)],
        scratch_shapes=[pltpu.VMEM((tq + BLK, 2 * BLK), F32), pltpu.VMEM((BLK, 2 * BLK), F32)],
        input_output_aliases={6: 0},
        name="attn_bwd", compiler_params=_params(1),
    )(sinks, z, z, z, ya, dya, dz)


def _to_group_lanes(v, g, nch):
    return jnp.concatenate([v[n * BLK:(n + 1) * BLK, g * BLK:(g + 1) * BLK] for n in range(nch)], axis=1)


def _from_group_lanes(per_group, nch):
    rows = [jnp.concatenate([per_group[g][:, n * BLK:(n + 1) * BLK] for g in range(SGU_GROUPS)], axis=1)
            for n in range(nch)]
    return jnp.concatenate(rows, axis=0)


def _tril_bf16(w_ref, g):
    t = lax.broadcasted_iota(jnp.int32, (BLK, BLK), 0)
    s = lax.broadcasted_iota(jnp.int32, (BLK, BLK), 1)
    return jnp.where(t >= s, w_ref[g], 0.0).astype(BF16)


def _sgu_norm(v_s, ln_g, ln_b):
    cdf, pdf = _gelu_parts(v_s)
    gv = v_s * cdf
    xc = gv - jnp.mean(gv, axis=-1, keepdims=True)
    rstd = lax.rsqrt(jnp.mean(xc * xc, axis=-1, keepdims=True) + LN_EPS)
    nhat = xc * rstd
    return nhat * ln_g + ln_b, nhat, rstd, cdf + v_s * pdf


def _sgu_gate(vn, w_ref, bb_ref, nch):
    vnb = vn.astype(BF16)
    return _from_group_lanes(
        [_dot(_tril_bf16(w_ref, g), _to_group_lanes(vnb, g, nch)) + jnp.tile(bb_ref[g], (1, nch))
         for g in range(SGU_GROUPS)], nch)


def mix_fwd_out(x, z, ya, post_g, ln_g, ln_b, sgu_w, sgu_bb, wa, ws, wo, copies=None):
    T = x.shape[0]
    tm = min(TM_MIX, T)
    nch = tm // BLK

    def body(x_ref, us_ref, vs_ref, ga_ref, gb_ref, ya_ref, qg_ref, lg_ref, lb_ref, w_ref, bb_ref,
             wa_ref, ws_ref, wo_ref, xo_ref, ysg_ref, pa_ref, pb_ref, o_ref):
        vn, _, _, _ = _sgu_norm(vs_ref[...].astype(F32), lg_ref[...], lb_ref[...])
        gate = _sgu_gate(vn, w_ref, bb_ref, nch)
        us = us_ref[...].astype(F32)
        cdf, _ = _gelu_parts(us)
        ysg = (us * cdf * gate).astype(BF16)
        ysg_ref[...] = ysg
        pa = _dot(ya_ref[...], wa_ref[...])
        pb = _dot(ysg, ws_ref[...])
        pa_ref[...] = pa.astype(BF16)
        pb_ref[...] = pb.astype(BF16)
        merged = _sigmoid(ga_ref[...].astype(F32)) * pa + _sigmoid(gb_ref[...].astype(F32)) * pb
        o = _dot(merged.astype(BF16), wo_ref[...])
        o_ref[...] = o.astype(BF16)
        on, _, _ = _rms(o, qg_ref[...])
        xo_ref[...] = x_ref[...] + on

    zspec = lambda start: _row_spec(tm, D, start // D)
    act = jax.ShapeDtypeStruct((T, D), BF16)
    return _call(
        body, (T // tm,),
        [_row_spec(tm, D), zspec(ZU), zspec(ZV), zspec(ZGA), zspec(ZGB), _row_spec(tm, D),
         _full_spec((1, D)), _full_spec((1, D)), _full_spec((1, D)),
         _full_spec((SGU_GROUPS, BLK, BLK)), _full_spec((SGU_GROUPS, BLK, BLK)),
         _whole_spec(wa), _whole_spec(ws), _whole_spec(wo)],
        (x, z, z, z, z, ya, post_g, ln_g, ln_b, sgu_w, sgu_bb, wa, ws, wo),
        [_row_spec(tm, D)] * 5, [jax.ShapeDtypeStruct((T, D), F32), act, act, act, act],
        "mix_fwd_out", copies=copies)


def mix_bwd_out(dxo, z, o, pa, pb, post_g, ln_g, ln_b, sgu_w, sgu_bb, wa, ws, wo, copies=None):
    T = dxo.shape[0]
    tm = min(TM_MIX_BWD, T)
    nch = tm // BLK

    def body(dxo_ref, us_ref, vs_ref, ga_ref, gb_ref, o_ref, pa_ref, pb_ref, qg_ref, lg_ref, lb_ref, w_ref, bb_ref,
             wa_ref, ws_ref, wo_ref,
             dz_ref, dya_ref, mg_ref, do_ref, dpa_ref, dpb_ref, dqg_ref, dlg_ref, dlb_ref, dw_ref, dbb_ref):
        @pl.when(pl.program_id(0) == 0)
        def _():
            for r in (dqg_ref, dlg_ref, dlb_ref, dw_ref, dbb_ref):
                r[...] = jnp.zeros_like(r)

        qg = qg_ref[...]
        dxo = dxo_ref[...]
        _, no, ro = _rms(o_ref[...].astype(F32), qg)
        dqg_ref[...] += _colsum(dxo * no)
        dob = _rms_bwd(no, ro, qg, dxo).astype(BF16)
        do_ref[...] = dob
        dmerged = _dot_nt(dob, wo_ref[...])
        sa = _sigmoid(ga_ref[...].astype(F32))
        sb = _sigmoid(gb_ref[...].astype(F32))
        pa = pa_ref[...].astype(F32)
        pb = pb_ref[...].astype(F32)
        ta = sa * pa
        tb = sb * pb
        mg_ref[...] = (ta + tb).astype(BF16)
        dpa = (dmerged * sa).astype(BF16)
        dpb = (dmerged * sb).astype(BF16)
        dpa_ref[...] = dpa
        dpb_ref[...] = dpb
        dz_ref[:, DZ_GA:DZ_GA + D] = (dmerged * (ta - ta * sa)).astype(BF16)
        dz_ref[:, DZ_GB:DZ_GB + D] = (dmerged * (tb - tb * sb)).astype(BF16)
        dya_ref[...] = _dot_nt(dpa, wa_ref[...]).astype(BF16)
        dysg = _dot_nt(dpb, ws_ref[...])

        lg = lg_ref[...]
        vn, nhat, rstd, dgelu_v = _sgu_norm(vs_ref[...].astype(F32), lg, lb_ref[...])
        gate = _sgu_gate(vn, w_ref, bb_ref, nch)
        us = us_ref[...].astype(F32)
        cdf, pdf = _gelu_parts(us)
        dz_ref[:, DZ_U:DZ_U + D] = (dysg * gate * (cdf + us * pdf)).astype(BF16)
        dgate = (dysg * (us * cdf)).astype(BF16)
        vnb = vn.astype(BF16)
        t = lax.broadcasted_iota(jnp.int32, (BLK, BLK), 0)
        s = lax.broadcasted_iota(jnp.int32, (BLK, BLK), 1)
        dvn_groups = []
        for g in range(SGU_GROUPS):
            dgl = _to_group_lanes(dgate, g, nch)
            dbb_ref[g] += _dot(dgl, jnp.ones((nch * BLK, BLK), BF16))
            dw_ref[g] += jnp.where(t >= s, _dot_nt(dgl, _to_group_lanes(vnb, g, nch)), 0.0)
            dvn_groups.append(_dot_tn(_tril_bf16(w_ref, g), dgl))
        dvn = _from_group_lanes(dvn_groups, nch)
        dlg_ref[...] += _colsum(dvn * nhat)
        dlb_ref[...] += _colsum(dvn)
        dnh = dvn * lg
        dgv = rstd * (dnh - jnp.mean(dnh, axis=-1, keepdims=True) - nhat * jnp.mean(dnh * nhat, axis=-1, keepdims=True))
        dz_ref[:, DZ_V:DZ_V + D] = (dgv * dgelu_v).astype(BF16)

    zspec = lambda start: _row_spec(tm, D, start // D)
    act = jax.ShapeDtypeStruct((T, D), BF16)
    grp = jax.ShapeDtypeStruct((SGU_GROUPS, BLK, BLK), F32)
    vec = jax.ShapeDtypeStruct((1, D), F32)
    return _call(
        body, (T // tm,),
        [_row_spec(tm, D), zspec(ZU), zspec(ZV), zspec(ZGA), zspec(ZGB),
         _row_spec(tm, D), _row_spec(tm, D), _row_spec(tm, D),
         _full_spec((1, D)), _full_spec((1, D)), _full_spec((1, D)),
         _full_spec((SGU_GROUPS, BLK, BLK)), _full_spec((SGU_GROUPS, BLK, BLK)),
         _whole_spec(wa), _whole_spec(ws), _whole_spec(wo)],
        (dxo, z, z, z, z, o, pa, pb, post_g, ln_g, ln_b, sgu_w, sgu_bb, wa, ws, wo),
        [_row_spec(tm, IN_WIDTH)] + [_row_spec(tm, D)] * 5 + [_full_spec((1, D))] * 3
        + [_full_spec((SGU_GROUPS, BLK, BLK))] * 2,
        [jax.ShapeDtypeStruct((T, IN_WIDTH), BF16), act, act, act, act, act, vec, vec, vec, grp, grp],
        "mix_bwd_out", copies=copies)


def mix_in_bwd(dxo, x, dz, pre_g, win):
    T = x.shape[0]
    tm = min(TM_MIX, T)

    def body(dxo_ref, x_ref, dz_ref, pg_ref, w_ref, dx_ref, hb_ref, dpg_ref):
        @pl.when(pl.program_id(0) == 0)
        def _():
            dpg_ref[...] = jnp.zeros_like(dpg_ref)

        pg = pg_ref[...]
        h, nx, rx = _rms(x_ref[...], pg)
        hb_ref[...] = h.astype(BF16)
        dh = jnp.zeros((tm, D), F32)
        for c0, cw in WIN_CHUNKS:
            dh = dh + _dot(dz_ref[:, c0:c0 + cw], w_ref[c0:c0 + cw, :])
        dpg_ref[...] += _colsum(dh * nx)
        dx_ref[...] = dxo_ref[...] + _rms_bwd(nx, rx, pg, dh)

    return pl.pallas_call(
        body, grid=(T // tm,),
        in_specs=[_row_spec(tm, D), _row_spec(tm, D), _row_spec(tm, IN_WIDTH), _full_spec((1, D)), _whole_spec(win)],
        out_specs=[_row_spec(tm, D), _row_spec(tm, D), _full_spec((1, D))],
        out_shape=[jax.ShapeDtypeStruct((T, D), F32), jax.ShapeDtypeStruct((T, D), BF16),
                   jax.ShapeDtypeStruct((1, D), F32)],
        name="mix_in_bwd", compiler_params=_params(1),
    )(dxo, x, dz, pre_g, win)


def wgrad(wide, narrow, name, tmo, copies=None, tk=TK_WGRAD):
    T, N = wide.shape
    M = narrow.shape[1]
    tk = min(tk, T)
    nk = T // tk
    chunk = max(c for c in range(BLK, 1792 + 1, BLK) if N % c == 0)

    def body(a_ref, b_ref, o_ref, acc_ref):
        k = pl.program_id(1)

        @pl.when(k == 0)
        def _():
            acc_ref[...] = jnp.zeros_like(acc_ref)

        acc_ref[...] += _dot_tn(a_ref[...], b_ref[...])

        @pl.when(k == nk - 1)
        def _():
            for c0 in range(0, N, chunk):
                o_ref[c0:c0 + chunk, :] = acc_ref[:, c0:c0 + chunk].T.astype(BF16)

    (out,), got = _call(
        body, (M // tmo, nk),
        [pl.BlockSpec((tk, tmo), lambda m, k: (k, m)), pl.BlockSpec((tk, N), lambda m, k: (k, 0))], (narrow, wide),
        [pl.BlockSpec((N, tmo), lambda m, k: (0, m))], [jax.ShapeDtypeStruct((N, M), BF16)], name,
        scratch=[pltpu.VMEM((tmo, N), F32)], copies=copies)
    return out if copies is None else (out, got)


def adamw_sum(parts, w, m, v, name):
    layers, rows, cols = w.shape
    tr = _row_tile(rows, TR_ADAM)
    nr = rows // tr
    c1 = 1.0 - ADAM_B1 ** ADAM_STEP
    c2 = 1.0 - ADAM_B2 ** ADAM_STEP

    def body(*refs):
        p_refs = refs[:layers]
        w_ref, m_ref, v_ref, g_ref, d_ref, nm_ref, nv_ref = refs[layers:]
        for k in range(layers):
            @pl.when(pl.program_id(0) == k)
            def _(p_ref=p_refs[k]):
                g = p_ref[0].astype(F32)
                for j in range(1, N_DEV):
                    g = g + p_ref[j].astype(F32)
                nm = ADAM_B1 * m_ref[...] + (1.0 - ADAM_B1) * g
                nv = ADAM_B2 * v_ref[...] + (1.0 - ADAM_B2) * (g * g)
                g_ref[...] = g
                nm_ref[...] = nm
                nv_ref[...] = nv
                d_ref[...] = -ADAM_LR * ((nm * (1.0 / c1)) / (jnp.sqrt(nv * (1.0 / c2)) + ADAM_EPS)
                                         + ADAM_WD * w_ref[...])

    def part_spec(k):
        return pl.BlockSpec((N_DEV, tr, cols),
                            lambda l, i: (0, jnp.where(l < k, 0, jnp.where(l == k, i, nr - 1)), 0))

    spec = pl.BlockSpec((None, tr, cols), lambda l, i: (l, i, 0))
    out = jax.ShapeDtypeStruct((layers, rows, cols), F32)
    return pl.pallas_call(
        body, grid=(layers, nr),
        in_specs=[part_spec(k) for k in range(layers)] + [spec, spec, spec],
        out_specs=[spec] * 4, out_shape=[out] * 4, name=name, compiler_params=_params(2),
    )(*parts, w, m, v)


def _pack_small(p):
    layers = []
    for l in range(DEPTH):
        rows = [p[n][l].reshape(1, D) for n in SMALL_VEC]
        rows.append(p["sgu_b"][l].reshape(1, D))
        rows.append(jnp.pad(p["attn_sinks"][l].reshape(1, N_Q_HEADS), ((0, 0), (0, D - N_Q_HEADS))))
        rows.append(jnp.zeros((SMALL_ROWS - len(SMALL_VEC) - 2, D), F32))
        layers.append(jnp.concatenate(rows, axis=0))
    return jnp.concatenate(layers, axis=0)


def _unpack_small(packed):
    a = packed.reshape(DEPTH, SMALL_ROWS, D)
    out = {n: a[:, i, :] for i, n in enumerate(SMALL_VEC)}
    k = len(SMALL_VEC)
    out["sgu_b"] = a[:, k, :].reshape(DEPTH, SGU_GROUPS, BLK)
    out["attn_sinks"] = a[:, k + 1, :N_Q_HEADS]
    return out


WEIGHT_NAMES = ("ffn1_pre_g", "ffn1_w1", "ffn1_w2", "ffn1_post_g", "mix_pre_g", "w_in", "attn_sinks", "sgu_ln_g",
                "sgu_ln_b", "sgu_w", "sgu_b", "w_attn_branch", "w_sgu_branch", "w_out", "mix_post_g", "ffn2_pre_g",
                "ffn2_w1", "ffn2_w2", "ffn2_post_g")
COL_SHARDED = ("ffn1_w1", "ffn2_w1", "w_in")
ROW_SHARDED = ("ffn1_w2", "ffn2_w2", "w_attn_branch", "w_sgu_branch", "w_out")
MATRICES = COL_SHARDED + ROW_SHARDED


def kernel(x, ffn1_pre_g, ffn1_w1, ffn1_w2, ffn1_post_g, mix_pre_g, w_in, attn_sinks, sgu_ln_g, sgu_ln_b, sgu_w, sgu_b, w_attn_branch, w_sgu_branch, w_out, mix_post_g, ffn2_pre_g, ffn2_w1, ffn2_w2, ffn2_post_g, loss_target, m_ffn1_pre_g, m_ffn1_w1, m_ffn1_w2, m_ffn1_post_g, m_mix_pre_g, m_w_in, m_attn_sinks, m_sgu_ln_g, m_sgu_ln_b, m_sgu_w, m_sgu_b, m_w_attn_branch, m_w_sgu_branch, m_w_out, m_mix_post_g, m_ffn2_pre_g, m_ffn2_w1, m_ffn2_w2, m_ffn2_post_g, v_ffn1_pre_g, v_ffn1_w1, v_ffn1_w2, v_ffn1_post_g, v_mix_pre_g, v_w_in, v_attn_sinks, v_sgu_ln_g, v_sgu_ln_b, v_sgu_w, v_sgu_b, v_w_attn_branch, v_w_sgu_branch, v_w_out, v_mix_post_g, v_ffn2_pre_g, v_ffn2_w1, v_ffn2_w2, v_ffn2_post_g):
    w = dict(zip(WEIGHT_NAMES, (ffn1_pre_g, ffn1_w1, ffn1_w2, ffn1_post_g, mix_pre_g, w_in, attn_sinks, sgu_ln_g,
                                sgu_ln_b, sgu_w, sgu_b, w_attn_branch, w_sgu_branch, w_out, mix_post_g, ffn2_pre_g,
                                ffn2_w1, ffn2_w2, ffn2_post_g)))
    mom = dict(zip(WEIGHT_NAMES, (m_ffn1_pre_g, m_ffn1_w1, m_ffn1_w2, m_ffn1_post_g, m_mix_pre_g, m_w_in,
                                  m_attn_sinks, m_sgu_ln_g, m_sgu_ln_b, m_sgu_w, m_sgu_b, m_w_attn_branch,
                                  m_w_sgu_branch, m_w_out, m_mix_post_g, m_ffn2_pre_g, m_ffn2_w1, m_ffn2_w2,
                                  m_ffn2_post_g)))
    var = dict(zip(WEIGHT_NAMES, (v_ffn1_pre_g, v_ffn1_w1, v_ffn1_w2, v_ffn1_post_g, v_mix_pre_g, v_w_in,
                                  v_attn_sinks, v_sgu_ln_g, v_sgu_ln_b, v_sgu_w, v_sgu_b, v_w_attn_branch,
                                  v_w_sgu_branch, v_w_out, v_mix_post_g, v_ffn2_pre_g, v_ffn2_w1, v_ffn2_w2,
                                  v_ffn2_post_g)))
    T = x.shape[1]
    xs = x.reshape(T, D)
    target = loss_target.reshape(T, D)

    for n in COL_SHARDED:
        w[n], mom[n], var[n] = (jnp.swapaxes(t[n], 1, 2) for t in (w, mom, var))

    per_layer = [cast_layer([w[n] for n in MATRICES], l) for l in range(DEPTH)]
    shard = {n: [per_layer[l][i] for l in range(DEPTH)] for i, n in enumerate(MATRICES)}
    weights = [{} for _ in range(DEPTH)]
    ffn1, ffn2, squares = ("ffn1_w1", "ffn1_w2"), ("ffn2_w1", "ffn2_w2"), ("w_attn_branch", "w_sgu_branch", "w_out")
    for n, g in zip(ffn1, all_gather_weights([shard[n][0] for n in ffn1])):
        weights[0][n] = g.reshape(-1, D)

    def fetch(groups):
        jobs = [("all", shard[n][l]) for l, names in groups if l < DEPTH for n in names]
        return PeerCopies(jobs) if jobs else None

    def landed(groups, arrivals):
        slots = [(l, n) for l, names in groups if l < DEPTH for n in names]
        for (l, n), g in zip(slots, arrivals):
            weights[l][n] = g.reshape(-1, D)

    vec = lambda n, l: w[n][l].reshape(1, D)
    sgu_bb = [jnp.broadcast_to(w["sgu_b"][l][:, :, None], (SGU_GROUPS, BLK, BLK)) for l in range(DEPTH)]

    saved = []
    h = xs
    for l in range(DEPTH):
        wl = weights[l]
        ahead = lambda names: [(0, names)] if l == 0 else []
        x0 = h
        groups = [(l + 1, ffn1)] + ahead(("w_in",))
        (x1, a1, s1, y1), got = ffn_fwd(x0, vec("ffn1_pre_g", l), vec("ffn1_post_g", l), wl["ffn1_w1"], wl["ffn1_w2"],
                                        fetch(groups))
        landed(groups, got)
        groups = [(l + 1, ("w_in",))] + ahead(squares)
        (z,), got = mix_in_fwd(x1, vec("mix_pre_g", l), wl["w_in"], fetch(groups))
        landed(groups, got)
        groups = ahead(("ffn2_w1",))
        (ya,), got = attn_fwd(z, w["attn_sinks"][l], fetch(groups))
        landed(groups, got)
        groups = [(l + 1, squares)] + ahead(("ffn2_w2",))
        (x2, ysg, pa, pb, o), got = mix_fwd_out(
            x1, z, ya, vec("mix_post_g", l), vec("sgu_ln_g", l), vec("sgu_ln_b", l), w["sgu_w"][l], sgu_bb[l],
            wl["w_attn_branch"], wl["w_sgu_branch"], wl["w_out"], fetch(groups))
        landed(groups, got)
        groups = [(l + 1, ffn2)]
        last = (vec("ffn2_pre_g", l), vec("ffn2_post_g", l), wl["ffn2_w1"], wl["ffn2_w2"])
        if l < DEPTH - 1:
            (h, a2, s2, y2), got = ffn_fwd(x2, *last, fetch(groups))
            landed(groups, got)
        else:
            (a2, s2, y2, sq, dx), _ = ffn_fwd(x2, *last, target=target)
        saved.append((x0, a1, s1, y1, x1, z, ya, ysg, pa, pb, o, x2, a2, s2, y2))

    loss = lax.psum(0.5 / D * jnp.sum(sq), ("x", "y", "c"))

    def ffn_wgrads(hb, da, s, dy):
        return [("own", wgrad(da, hb, "wgrad_ffn_w1", D // 2).reshape(N_DEV, W1_SHARD, D)),
                ("own", wgrad(s, dy, "wgrad_ffn_w2", D).reshape(N_DEV, W2_SHARD, D))]

    parts = {n: [None] * DEPTH for n in MATRICES}
    small = {n: [None] * DEPTH for n in WEIGHT_NAMES if n not in MATRICES + ("sgu_w",)}
    sgu_w_parts = [None] * DEPTH
    waiting = None
    for l in reversed(range(DEPTH)):
        wl = weights[l]
        x0, a1, s1, y1, x1, z, ya, ysg, pa, pb, o, x2, a2, s2, y2 = saved[l]
        (dx, da, dy, hb, dpg, dqg), got = ffn_bwd(
            dx, x2, y2, a2, vec("ffn2_pre_g", l), vec("ffn2_post_g", l), wl["ffn2_w1"], wl["ffn2_w2"],
            PeerCopies(waiting) if waiting else None)
        if waiting:
            parts["ffn1_w1"][l + 1], parts["ffn1_w2"][l + 1] = got
        small["ffn2_pre_g"][l], small["ffn2_post_g"][l] = dpg, dqg

        (dzb, dya, mg, dob, dpa, dpb, dqg, dlg, dlb, dsw, dsb), got = mix_bwd_out(
            dx, z, o, pa, pb, vec("mix_post_g", l), vec("sgu_ln_g", l), vec("sgu_ln_b", l), w["sgu_w"][l], sgu_bb[l],
            wl["w_attn_branch"], wl["w_sgu_branch"], wl["w_out"], PeerCopies(ffn_wgrads(hb, da, s2, dy)))
        parts["ffn2_w1"][l], parts["ffn2_w2"][l] = got
        dz, dsink = attn_bwd(z, ya, dya, dzb, w["attn_sinks"][l])
        dx, hb, dpg = mix_in_bwd(dx, x1, dz, vec("mix_pre_g", l), wl["w_in"])
        small["mix_pre_g"][l], small["mix_post_g"][l] = dpg, dqg
        small["sgu_ln_g"][l], small["sgu_ln_b"][l] = dlg, dlb
        small["sgu_b"][l] = dsb[:, :, 0]
        small["attn_sinks"][l] = dsink[0, :N_Q_HEADS]
        mixer = [("own", wgrad(dz, hb, "wgrad_w_in", D // 2).reshape(N_DEV, WIN_SHARD, D))]
        mixer += [("own", wgrad(act, cot, "wgrad_square", D, tk=2 * TK_WGRAD).reshape(N_DEV, SQ_SHARD, D))
                  for act, cot in ((ya, dpa), (ysg, dpb), (mg, dob))]
        mixer.append(("all", dsw.reshape(SGU_GROUPS * BLK, BLK)))
        if l == 0:
            small["ffn1_pre_g"][0] = small["ffn1_post_g"][0] = jnp.zeros((1, D), F32)
            mixer.append(("all", _pack_small({n: jnp.stack(v, axis=0) for n, v in small.items()})))

        (dx, da, dy, hb, dpg, dqg), got = ffn_bwd(
            dx, x0, y1, a1, vec("ffn1_pre_g", l), vec("ffn1_post_g", l), wl["ffn1_w1"], wl["ffn1_w2"],
            PeerCopies(mixer))
        parts["w_in"][l], parts["w_attn_branch"][l], parts["w_sgu_branch"][l], parts["w_out"][l] = got[:4]
        sgu_w_parts[l] = got[4]
        if l > 0:
            small["ffn1_pre_g"][l], small["ffn1_post_g"][l] = dpg, dqg
            waiting = ffn_wgrads(hb, da, s1, dy)
    grad_x = dx.reshape(x.shape)
    small_parts = got[5]

    def late_rows(pre, post):
        return jnp.concatenate([pre.reshape(1, D), post.reshape(1, D), jnp.zeros((LATE_ROWS - 2, D), F32)], axis=0)

    g_w1 = wgrad(da, hb, "wgrad_ffn_w1", D // 2).reshape(N_DEV, W1_SHARD, D)
    g_w2, got = wgrad(s1, dy, "wgrad_ffn_w2", D, PeerCopies([("own", g_w1)]))
    parts["ffn1_w1"][0] = got[0]
    parts["ffn1_w2"][0], late_parts = exchange_last([("own", g_w2.reshape(N_DEV, W2_SHARD, D)),
                                                     ("all", late_rows(dpg, dqg))])

    grads, deltas, new_m, new_v = {}, {}, {}, {}
    for n in MATRICES:
        grads[n], deltas[n], new_m[n], new_v[n] = adamw_sum(parts[n], w[n], mom[n], var[n], "adamw_" + n)
    for n in COL_SHARDED:
        for out in (grads, deltas, new_m, new_v):
            out[n] = jnp.swapaxes(out[n], 1, 2)
    flat = lambda t: t["sgu_w"].reshape(DEPTH, SGU_GROUPS * BLK, BLK)
    for out, r in zip((grads, deltas, new_m, new_v),
                      adamw_sum(sgu_w_parts, flat(w), flat(mom), flat(var), "adamw_sgu_w")):
        out["sgu_w"] = r.reshape(w["sgu_w"].shape)
    res = adamw_sum([small_parts], _pack_small(w)[None], _pack_small(mom)[None], _pack_small(var)[None],
                    "adamw_small")
    late = adamw_sum([late_parts], *[late_rows(t["ffn1_pre_g"][0], t["ffn1_post_g"][0])[None] for t in (w, mom, var)],
                     "adamw_late")
    for out, packed, late_out in zip((grads, deltas, new_m, new_v), res, late):
        out.update(_unpack_small(packed[0]))
        for row, n in enumerate(("ffn1_pre_g", "ffn1_post_g")):
            out[n] = jnp.concatenate([late_out[0, row:row + 1], out[n][1:]], axis=0)

    return (loss, grad_x, *[grads[n] for n in WEIGHT_NAMES], *[deltas[n] for n in WEIGHT_NAMES],
            *[new_m[n] for n in WEIGHT_NAMES], *[new_v[n] for n in WEIGHT_NAMES])
```

```python
import math

import jax
import jax.numpy as jnp
from jax import lax
from jax.experimental import pallas as pl
from jax.experimental.pallas import tpu as pltpu

F32 = jnp.float32
BF16 = jnp.bfloat16

N_DEV = 8
D = 1024
FF = 2816
DEPTH = 4
HEAD_DIM = 64
N_Q_HEADS = 16
N_KV_HEADS = 2
Q_PER_KV = N_Q_HEADS // N_KV_HEADS
BLK = 128
SGU_GROUPS = 8
IN_WIDTH = 5376
W1_SHARD = 2 * FF // N_DEV
WIN_SHARD = IN_WIDTH // N_DEV
W2_SHARD = FF // N_DEV
SQ_SHARD = D // N_DEV

RMS_EPS = 1e-6
LN_EPS = 1e-5
MASK_VALUE = -1e30
ATTN_SCALE = 1.0 / math.sqrt(HEAD_DIM)

ADAM_LR = 0.001
ADAM_B1 = 0.9
ADAM_B2 = 0.999
ADAM_EPS = 1e-08
ADAM_WD = 0.01
ADAM_STEP = 10

VMEM_LIMIT_V7X = 56 * 1024 * 1024

WIN_SEGMENTS = ((0, 1024, 0), (1024, 256, 5120), (1280, 1024, 1024), (2304, 1024, 2048), (3328, 1024, 3072),
                (4352, 1024, 4096))
ZQ, ZU, ZV, ZGA, ZGB, ZKV = 0, 1024, 2048, 3072, 4096, 5120
DZ_Q, DZ_KV, DZ_MID = 0, 1024, 1280
DZ_U, DZ_V, DZ_GA, DZ_GB = 1280, 2304, 3328, 4352

FF_CHUNKS = ((0, 1024), (1024, 1024), (2048, 768))
WIN_CHUNKS = ((0, 1792), (1792, 1792), (3584, 1792))

SMALL_ROWS = 16
LATE_ROWS = 16
SMALL_VEC = ("ffn1_pre_g", "ffn1_post_g", "mix_pre_g", "mix_post_g", "ffn2_pre_g", "ffn2_post_g",
             "sgu_ln_g", "sgu_ln_b")

TM_FFN_FWD = 512
TM_FFN_BWD = 256
TM_MIX_IN = 1024
TM_MIX = 512
TM_MIX_BWD = 256
TQ_ATTN = 512
TK_WGRAD = 1024
TR_ADAM = 256


def _params(n_grid, vmem=VMEM_LIMIT_V7X):
    return pltpu.CompilerParams(dimension_semantics=("arbitrary",) * n_grid, vmem_limit_bytes=vmem)


def _dot(a, b):
    return jnp.dot(a, b, preferred_element_type=F32)


def _dot_nt(a, b):
    return lax.dot_general(a, b, (((1,), (1,)), ((), ())), preferred_element_type=F32)


def _dot_tn(a, b):
    return lax.dot_general(a, b, (((0,), (0,)), ((), ())), preferred_element_type=F32)


def _rms(x, g):
    r = lax.rsqrt(jnp.mean(x * x, axis=-1, keepdims=True) + RMS_EPS)
    n = x * r
    return n * g, n, r


def _rms_bwd(n, r, g, dy):
    dn = dy * g
    return r * (dn - n * jnp.mean(dn * n, axis=-1, keepdims=True))


def _colsum(v):
    return jnp.sum(v, axis=0, keepdims=True)


def _sigmoid(v):
    return 0.5 * jnp.tanh(0.5 * v) + 0.5


def _gelu_parts(v):
    cdf = 0.5 * lax.erf(v * (1.0 / math.sqrt(2.0))) + 0.5
    return cdf, jnp.exp2(v * v * (-0.5 / math.log(2.0))) * (1.0 / math.sqrt(2.0 * math.pi))


def _row_tile(rows, cap):
    return max(t for t in range(16, min(rows, cap) + 1, 16) if rows % t == 0)


def _row_spec(tm, width, col_block=0):
    return pl.BlockSpec((tm, width), lambda i, cb=col_block: (i, cb))


def _full_spec(shape):
    nd = len(shape)
    return pl.BlockSpec(tuple(shape), lambda *_: (0,) * nd)


def _whole_spec(arr):
    nd = arr.ndim
    return pl.BlockSpec(tuple(arr.shape), lambda *_: (0,) * nd, pipeline_mode=pl.Buffered(1))


HBM_SPEC = pl.BlockSpec(memory_space=pltpu.HBM)
MESH_ID = pl.DeviceIdType.MESH
RELATIONS = tuple((rx, ry, rc) for rx in (0, 1) for ry in (0, 1) for rc in (0, 1))[1:]


class PeerCopies:
    def __init__(self, jobs):
        self.kinds = [k for k, _ in jobs]
        self.arrays = [a for _, a in jobs]
        self.n = len(jobs)
        self.out_shape = [jax.ShapeDtypeStruct((N_DEV,) + a.shape if k == "all" else a.shape, a.dtype)
                          for k, a in jobs]
        self.scratch = [pltpu.SemaphoreType.DMA((7 * self.n,)), pltpu.SemaphoreType.DMA((7 * self.n,)),
                        pltpu.SemaphoreType.DMA((self.n,))]

    def _copies(self, ins, outs, sems, arriving):
        send_sems, recv_sems, local_sems = sems
        x, y, c = lax.axis_index("x"), lax.axis_index("y"), lax.axis_index("c")
        me = 4 * x + 2 * y + c
        src = lambda a, d: ins[a] if self.kinds[a] == "all" else ins[a].at[d]
        if not arriving:
            local = [pltpu.make_async_copy(src(a, me), outs[a].at[me], local_sems.at[a]) for a in range(self.n)]
        remote = []
        for k, (rx, ry, rc) in enumerate(RELATIONS):
            tx, ty, tc = (1 - x if rx else x), (1 - y if ry else y), (1 - c if rc else c)
            peer = 4 * tx + 2 * ty + tc
            for a in range(self.n):
                from_slot, to_slot = (me, peer) if arriving else (peer, me)
                remote.append(pltpu.make_async_remote_copy(
                    src_ref=src(a, from_slot), dst_ref=outs[a].at[to_slot],
                    send_sem=send_sems.at[a * 7 + k], recv_sem=recv_sems.at[a * 7 + k],
                    device_id=(tx, ty, tc), device_id_type=MESH_ID))
        return remote if arriving else (local, remote)

    def start(self, ins, outs, sems):
        local, sends = self._copies(ins, outs, sems, False)
        for cp in local + sends:
            cp.start()

    def wait(self, ins, outs, sems):
        for cp in self._copies(ins, outs, sems, True):
            cp.wait_recv()
        local, sends = self._copies(ins, outs, sems, False)
        for cp in sends:
            cp.wait_send()
        for cp in local:
            cp.wait()


def _call(body, grid, in_specs, args, out_specs, out_shape, name, scratch=(), copies=None):
    if copies is None:
        outs = pl.pallas_call(body, grid=grid, in_specs=list(in_specs), out_specs=list(out_specs),
                              out_shape=list(out_shape), scratch_shapes=list(scratch), name=name,
                              compiler_params=_params(len(grid)))(*args)
        return outs, []
    n_in, n_out, n_scr, nc = len(in_specs), len(out_specs), len(scratch), copies.n

    def at_step(steps):
        hit = pl.program_id(0) == steps[0]
        for axis in range(1, len(grid)):
            hit = jnp.logical_and(hit, pl.program_id(axis) == steps[axis])
        return hit

    def hosted(*refs):
        ins, refs = refs[:n_in], refs[n_in:]
        c_in, refs = refs[:nc], refs[nc:]
        outs, refs = refs[:n_out], refs[n_out:]
        c_out, refs = refs[:nc], refs[nc:]
        scr, sems = refs[:n_scr], refs[n_scr:]

        @pl.when(at_step([0] * len(grid)))
        def _():
            copies.start(c_in, c_out, sems)

        body(*ins, *outs, *scr)

        @pl.when(at_step([g - 1 for g in grid]))
        def _():
            copies.wait(c_in, c_out, sems)

    outs = pl.pallas_call(hosted, grid=grid, in_specs=list(in_specs) + [HBM_SPEC] * nc,
                          out_specs=list(out_specs) + [HBM_SPEC] * nc,
                          out_shape=list(out_shape) + copies.out_shape,
                          scratch_shapes=list(scratch) + copies.scratch, name=name,
                          compiler_params=_params(len(grid)))(*args, *copies.arrays)
    return outs[:n_out], outs[n_out:]


def cast_layer(ws, l):
    n = len(ws)

    def body(*refs):
        for w_ref, o_ref in zip(refs[:n], refs[n:]):
            o_ref[...] = w_ref[...].astype(BF16)

    halves = [w.shape[1] // 2 for w in ws]
    return pl.pallas_call(
        body, grid=(2,),
        in_specs=[pl.BlockSpec((None, h, D), lambda i: (l, i, 0)) for h in halves],
        out_specs=[_row_spec(h, D) for h in halves],
        out_shape=[jax.ShapeDtypeStruct(w.shape[1:], BF16) for w in ws],
        name="cast_layer", compiler_params=_params(1),
    )(*ws)


def all_gather_weights(shards):
    n = len(shards)

    def body(*refs):
        ins, outs = refs[:n], refs[n:2 * n]
        send_sems, recv_sems, local_sems = refs[2 * n:]
        x, y, c = lax.axis_index("x"), lax.axis_index("y"), lax.axis_index("c")
        me, sibling = (x, y, c), (x, y, 1 - c)
        chips = [(1 - x, y), (x, 1 - y), (1 - x, 1 - y)]

        def slot(a, owner):
            return outs[a].at[4 * owner[0] + 2 * owner[1] + owner[2]]

        def copy(a, k, owner, to, src=None):
            return pltpu.make_async_remote_copy(
                src_ref=slot(a, owner) if src is None else src, dst_ref=slot(a, owner),
                send_sem=send_sems.at[a * 7 + k], recv_sem=recv_sems.at[a * 7 + k],
                device_id=to, device_id_type=MESH_ID)

        mine = [pltpu.make_async_copy(ins[a], slot(a, me), local_sems.at[a]) for a in range(n)]
        for cp in mine:
            cp.start()
        first = []
        for a in range(n):
            first.append(copy(a, 0, me, sibling, src=ins[a]))
            first += [copy(a, 1 + j, me, (*chip, c), src=ins[a]) for j, chip in enumerate(chips)]
        for cp in first:
            cp.start()
        passed = []
        for j, chip in enumerate(chips):
            for a in range(n):
                copy(a, 1 + j, (*chip, c), me).wait_recv()
                fwd = copy(a, 4 + j, (*chip, c), sibling)
                fwd.start()
                passed.append(fwd)
        for a in range(n):
            copy(a, 0, sibling, me).wait_recv()
            for j, chip in enumerate(chips):
                copy(a, 4 + j, (*chip, 1 - c), me).wait_recv()
        for cp in first + passed:
            cp.wait_send()
        for cp in mine:
            cp.wait()

    out_shape = [jax.ShapeDtypeStruct((N_DEV,) + s.shape, s.dtype) for s in shards]
    return pl.pallas_call(
        body, in_specs=[HBM_SPEC] * n, out_specs=[HBM_SPEC] * n, out_shape=out_shape,
        scratch_shapes=[pltpu.SemaphoreType.DMA((7 * n,)), pltpu.SemaphoreType.DMA((7 * n,)),
                        pltpu.SemaphoreType.DMA((n,))],
        name="all_gather_weights",
    )(*shards)


def exchange_last(jobs):
    copies = PeerCopies(jobs)
    n = copies.n

    def body(*refs):
        ins, outs, sems = refs[:n], refs[n:2 * n], refs[2 * n:]
        copies.start(ins, outs, sems)
        copies.wait(ins, outs, sems)

    return pl.pallas_call(
        body, in_specs=[HBM_SPEC] * n, out_specs=[HBM_SPEC] * n, out_shape=copies.out_shape,
        scratch_shapes=copies.scratch, name="exchange_last",
    )(*copies.arrays)


def ffn_fwd(x, pre_g, post_g, w1, w2, copies=None, target=None):
    T = x.shape[0]
    tm = min(TM_FFN_FWD, T)

    def body(*refs):
        if target is None:
            x_ref, pg_ref, qg_ref, w1_ref, w2_ref, xo_ref, a_ref, s_ref, y_ref = refs
        else:
            x_ref, pg_ref, qg_ref, w1_ref, w2_ref, t_ref, a_ref, s_ref, y_ref, sq_ref, dy_ref = refs
        xv = x_ref[...]
        h, _, _ = _rms(xv, pg_ref[...])
        hb = h.astype(BF16)
        acc = jnp.zeros((tm, D), F32)
        for c0, cw in FF_CHUNKS:
            g = _dot_nt(hb, w1_ref[c0:c0 + cw, :])
            u = _dot_nt(hb, w1_ref[FF + c0:FF + c0 + cw, :])
            sg = _sigmoid(g)
            si = g * sg
            a_ref[:, c0:c0 + cw] = (u * (sg + si - si * sg)).astype(BF16)
            a_ref[:, FF + c0:FF + c0 + cw] = si.astype(BF16)
            s = (si * u).astype(BF16)
            s_ref[:, c0:c0 + cw] = s
            acc = acc + _dot(s, w2_ref[c0:c0 + cw, :])
        y_ref[...] = acc.astype(BF16)
        o, _, _ = _rms(acc, qg_ref[...])
        xo = xv + 0.5 * o
        if target is None:
            xo_ref[...] = xo
        else:
            @pl.when(pl.program_id(0) == 0)
            def _():
                sq_ref[...] = jnp.zeros_like(sq_ref)

            err = xo - t_ref[...]
            sq_ref[...] += _colsum(err * err)
            dy_ref[...] = err * (1.0 / D)

    in_specs = [_row_spec(tm, D), _full_spec((1, D)), _full_spec((1, D)), _whole_spec(w1), _whole_spec(w2)]
    saved_specs = [_row_spec(tm, 2 * FF), _row_spec(tm, FF), _row_spec(tm, D)]
    saved_shapes = [jax.ShapeDtypeStruct((T, 2 * FF), BF16), jax.ShapeDtypeStruct((T, FF), BF16),
                    jax.ShapeDtypeStruct((T, D), BF16)]
    row_f32 = jax.ShapeDtypeStruct((T, D), F32)
    if target is None:
        return _call(body, (T // tm,), in_specs, (x, pre_g, post_g, w1, w2), [_row_spec(tm, D)] + saved_specs,
                     [row_f32] + saved_shapes, "ffn_fwd", copies=copies)
    return _call(body, (T // tm,), in_specs + [_row_spec(tm, D)], (x, pre_g, post_g, w1, w2, target),
                 saved_specs + [_full_spec((1, D)), _row_spec(tm, D)],
                 saved_shapes + [jax.ShapeDtypeStruct((1, D), F32), row_f32], "ffn_fwd_loss", copies=copies)


def ffn_bwd(dxo, x, y, a, pre_g, post_g, w1, w2, copies=None):
    T = x.shape[0]
    tm = min(TM_FFN_BWD, T)

    def body(dxo_ref, x_ref, y_ref, a_ref, pg_ref, qg_ref, w1_ref, w2_ref,
             dx_ref, da_ref, dy_ref, hb_ref, dpg_ref, dqg_ref):
        @pl.when(pl.program_id(0) == 0)
        def _():
            dpg_ref[...] = jnp.zeros_like(dpg_ref)
            dqg_ref[...] = jnp.zeros_like(dqg_ref)

        dxo = dxo_ref[...]
        qg = qg_ref[...]
        _, ny, ry = _rms(y_ref[...].astype(F32), qg)
        dn = 0.5 * dxo
        dqg_ref[...] += _colsum(dn * ny)
        dyb = _rms_bwd(ny, ry, qg, dn).astype(BF16)
        dy_ref[...] = dyb
        pg = pg_ref[...]
        h, nx, rx = _rms(x_ref[...], pg)
        hb_ref[...] = h.astype(BF16)
        dh = jnp.zeros((tm, D), F32)
        for c0, cw in FF_CHUNKS:
            ds = _dot_nt(dyb, w2_ref[c0:c0 + cw, :])
            dg = (ds * a_ref[:, c0:c0 + cw].astype(F32)).astype(BF16)
            du = (ds * a_ref[:, FF + c0:FF + c0 + cw].astype(F32)).astype(BF16)
            da_ref[:, c0:c0 + cw] = dg
            da_ref[:, FF + c0:FF + c0 + cw] = du
            dh = dh + _dot(dg, w1_ref[c0:c0 + cw, :]) + _dot(du, w1_ref[FF + c0:FF + c0 + cw, :])
        dpg_ref[...] += _colsum(dh * nx)
        dx_ref[...] = dxo + _rms_bwd(nx, rx, pg, dh)

    return _call(
        body, (T // tm,),
        [_row_spec(tm, D), _row_spec(tm, D), _row_spec(tm, D), _row_spec(tm, 2 * FF),
         _full_spec((1, D)), _full_spec((1, D)), _whole_spec(w1), _whole_spec(w2)],
        (dxo, x, y, a, pre_g, post_g, w1, w2),
        [_row_spec(tm, D), _row_spec(tm, 2 * FF), _row_spec(tm, D), _row_spec(tm, D),
         _full_spec((1, D)), _full_spec((1, D))],
        [jax.ShapeDtypeStruct((T, D), F32), jax.ShapeDtypeStruct((T, 2 * FF), BF16),
         jax.ShapeDtypeStruct((T, D), BF16), jax.ShapeDtypeStruct((T, D), BF16),
         jax.ShapeDtypeStruct((1, D), F32), jax.ShapeDtypeStruct((1, D), F32)],
        "ffn_bwd", copies=copies)


def mix_in_fwd(x, pre_g, win, copies=None):
    T = x.shape[0]
    tm = min(TM_MIX_IN, T)

    def body(x_ref, pg_ref, w_ref, z_ref):
        h, _, _ = _rms(x_ref[...], pg_ref[...])
        hb = h.astype(BF16)
        for w0, n, z0 in WIN_SEGMENTS:
            z_ref[:, z0:z0 + n] = _dot_nt(hb, w_ref[w0:w0 + n, :]).astype(BF16)

    return _call(
        body, (T // tm,), [_row_spec(tm, D), _full_spec((1, D)), _whole_spec(win)], (x, pre_g, win),
        [_row_spec(tm, IN_WIDTH)], [jax.ShapeDtypeStruct((T, IN_WIDTH), BF16)], "mix_in_fwd", copies=copies)


STACK = Q_PER_KV // 2
SROWS = STACK * BLK


def _sees_own():
    qi = lax.broadcasted_iota(jnp.int32, (SROWS, BLK), 0) & (BLK - 1)
    return lax.broadcasted_iota(jnp.int32, (SROWS, BLK), 1) <= qi


def _both_blocks(picked, own):
    return jnp.concatenate([jnp.where(own, 0.0, picked), jnp.where(own, picked, 0.0)], axis=1)


def _lane_half(rows):
    return lax.broadcasted_iota(jnp.int32, (rows, BLK), 1) // HEAD_DIM


def _stack(ref, b, g):
    return jnp.concatenate([ref[b * BLK:(b + 1) * BLK, (STACK * g + j) * BLK:(STACK * g + j + 1) * BLK]
                            for j in range(STACK)], axis=0)


def _placed(pair, g, hp, fill):
    src = pair if hp == g else pltpu.roll(pair, HEAD_DIM, 1)
    return jnp.where(_lane_half(2 * BLK) == hp, src, fill).astype(BF16)


def _sink_column(sink_ref, g, hp):
    rb = lax.broadcasted_iota(jnp.int32, (SROWS, 1), 0) // BLK
    col = jnp.full((SROWS, 1), sink_ref[Q_PER_KV * g + hp], F32)
    for j in range(1, STACK):
        col = jnp.where(rb == j, sink_ref[Q_PER_KV * g + 2 * j + hp], col)
    return col


def _attn_scores(qs, kz, sink_col, own, no_previous):
    raw = _dot_nt(qs, kz)
    prev = raw[:, 0:BLK]
    if no_previous is not None:
        prev = prev + no_previous.astype(F32) * MASK_VALUE
    picked = jnp.where(own, raw[:, BLK:2 * BLK], prev)
    sink_raw = sink_col * (1.0 / ATTN_SCALE)
    m = jnp.maximum(jnp.max(picked, axis=-1, keepdims=True), sink_raw)
    factor = ATTN_SCALE / math.log(2.0)
    return jnp.exp2((picked - m) * factor), jnp.exp2((sink_raw - m) * factor)


def _kv_specs(tq, nb):
    kv_blk = ZKV // (2 * BLK)
    return [pl.BlockSpec((tq, 2 * BLK), lambda i: (i, kv_blk)),
            pl.BlockSpec((BLK, 2 * BLK), lambda i: (jnp.maximum(i * nb - 1, 0), kv_blk))]


def attn_fwd(z, sinks, copies=None):
    T = z.shape[0]
    tq = min(TQ_ATTN, T)
    nb = tq // BLK

    def body(sink_ref, q_ref, kv_ref, kvh_ref, o_ref):
        i = pl.program_id(0)
        low = _lane_half(SROWS) == 0
        own = _sees_own()
        for b in range(nb):
            kvp = kvh_ref[...] if b == 0 else kv_ref[(b - 1) * BLK:b * BLK, :]
            kv2 = jnp.concatenate([kvp, kv_ref[b * BLK:(b + 1) * BLK, :]], axis=0).astype(F32)
            no_previous = (i == 0) if b == 0 else None
            for g in range(N_KV_HEADS):
                qs = _stack(q_ref, b, g)
                r, e = [], []
                for hp in range(2):
                    p, e_sink = _attn_scores(qs, _placed(kv2[:, 0:BLK], g, hp, 0.0), _sink_column(sink_ref, g, hp),
                                             own, no_previous)
                    r.append(_dot(_both_blocks(p, own).astype(BF16), _placed(kv2[:, BLK:2 * BLK], g, hp, 1.0)))
                    e.append(e_sink)
                den = pltpu.roll(jnp.where(low, r[1], r[0]), HEAD_DIM, 1) + jnp.where(low, e[0], e[1])
                out = (jnp.where(low, r[0], r[1]) * (1.0 / den)).astype(BF16)
                for j in range(STACK):
                    o_ref[b * BLK:(b + 1) * BLK, (STACK * g + j) * BLK:(STACK * g + j + 1) * BLK] = \
                        out[j * BLK:(j + 1) * BLK, :]

    return _call(
        body, (T // tq,), [pl.BlockSpec(memory_space=pltpu.SMEM), _row_spec(tq, D)] + _kv_specs(tq, nb),
        (sinks, z, z, z), [_row_spec(tq, D)], [jax.ShapeDtypeStruct((T, D), BF16)], "attn_fwd", copies=copies)


def attn_bwd(z, ya, dya, dz, sinks):
    T = z.shape[0]
    tq = min(TQ_ATTN, T)
    nb = tq // BLK
    nt = T // tq

    def body(sink_ref, q_ref, kv_ref, kvh_ref, y_ref, dy_ref, dz_whole_ref, dz_ref, dsink_ref, acc_ref, carry_ref):
        i = pl.program_id(0)
        first_tile = i == nt - 1

        @pl.when(i == 0)
        def _():
            carry_ref[...] = jnp.zeros_like(carry_ref)
            dsink_ref[...] = jnp.zeros_like(dsink_ref)

        acc_ref[...] = jnp.zeros_like(acc_ref)
        lane = lax.broadcasted_iota(jnp.int32, (1, BLK), 1)
        dsink = jnp.zeros((1, BLK), F32)
        half = _lane_half(SROWS)
        own = _sees_own()
        for b in range(nb):
            kvp = kvh_ref[...] if b == 0 else kv_ref[(b - 1) * BLK:b * BLK, :]
            kv2 = jnp.concatenate([kvp, kv_ref[b * BLK:(b + 1) * BLK, :]], axis=0).astype(F32)
            no_previous = first_tile if b == 0 else None
            dk_groups, dv_groups = [], []
            for g in range(N_KV_HEADS):
                qs = _stack(q_ref, b, g)
                dys = _stack(dy_ref, b, g)
                dyy = dys.astype(F32) * _stack(y_ref, b, g).astype(F32)
                dq = jnp.zeros((SROWS, BLK), F32)
                ds_both, pn_both = [], []
                for hp in range(2):
                    kz = _placed(kv2[:, 0:BLK], g, hp, 0.0)
                    p, e_sink = _attn_scores(qs, kz, _sink_column(sink_ref, g, hp), own, no_previous)
                    inv = 1.0 / (jnp.sum(p, axis=-1, keepdims=True) + e_sink)
                    p = p * inv
                    delta = jnp.sum(jnp.where(half == hp, dyy, 0.0), axis=-1, keepdims=True)
                    dp = _dot_nt(dys, _placed(kv2[:, BLK:2 * BLK], g, hp, 0.0))
                    ds = p * (jnp.where(own, dp[:, BLK:2 * BLK], dp[:, 0:BLK]) - delta)
                    ds = _both_blocks(ds, own).astype(BF16)
                    pn = _both_blocks(p, own)
                    sink_term = e_sink * inv * delta
                    for j in range(STACK):
                        dsink = dsink + jnp.where(lane == Q_PER_KV * g + 2 * j + hp,
                                                  -_colsum(sink_term[j * BLK:(j + 1) * BLK, :]), 0.0)
                    dq = dq + _dot(ds, kz)
                    ds_both.append(ds)
                    pn_both.append(pn.astype(BF16))
                dk_t = _dot_tn(qs, jnp.concatenate(ds_both, axis=1))
                dv_t = _dot_tn(dys, jnp.concatenate(pn_both, axis=1))
                for t, groups in ((dk_t, dk_groups), (dv_t, dv_groups)):
                    groups.append(t[0:HEAD_DIM, 0:2 * BLK] + t[HEAD_DIM:BLK, 2 * BLK:4 * BLK])
                dqb = (dq * ATTN_SCALE).astype(BF16)
                for j in range(STACK):
                    dz_ref[b * BLK:(b + 1) * BLK, DZ_Q + (STACK * g + j) * BLK:DZ_Q + (STACK * g + j + 1) * BLK] = \
                        dqb[j * BLK:(j + 1) * BLK, :]
            acc_ref[b * BLK:(b + 2) * BLK, 0:BLK] += jnp.concatenate(dk_groups, axis=0).T * ATTN_SCALE
            acc_ref[b * BLK:(b + 2) * BLK, BLK:2 * BLK] += jnp.concatenate(dv_groups, axis=0).T
        dsink_ref[...] += dsink
        if nb > 1:
            dz_ref[0:tq - BLK, DZ_KV:DZ_MID] = acc_ref[BLK:tq, :].astype(BF16)
        dz_ref[tq - BLK:tq, DZ_KV:DZ_MID] = (acc_ref[tq:tq + BLK, :] + carry_ref[...]).astype(BF16)
        carry_ref[...] = acc_ref[0:BLK, :]

    kv_blk = ZKV // (2 * BLK)
    return pl.pallas_call(
        body, grid=(nt,),
        in_specs=[pl.BlockSpec(memory_space=pltpu.SMEM),
                  pl.BlockSpec((tq, D), lambda i: (nt - 1 - i, 0)),
                  pl.BlockSpec((tq, 2 * BLK), lambda i: (nt - 1 - i, kv_blk)),
                  pl.BlockSpec((BLK, 2 * BLK), lambda i: (jnp.maximum((nt - 1 - i) * nb - 1, 0), kv_blk)),
                  pl.BlockSpec((tq, D), lambda i: (nt - 1 - i, 0)),
                  pl.BlockSpec((tq, D), lambda i: (nt - 1 - i, 0)),
                  pl.BlockSpec(memory_space=pl.ANY)],
        out_specs=[pl.BlockSpec((tq, DZ_MID), lambda i: (nt - 1 - i, 0)), _full_spec((1, BLK))],
        out_shape=[jax.ShapeDtypeStruct((T, IN_WIDTH), BF16), jax.ShapeDtypeStruct((1, BLK), F32)],
        scratch_shapes=[pltpu.VMEM((tq + BLK, 2 * BLK), F32), pltpu.VMEM((BLK, 2 * BLK), F32)],
        input_output_aliases={6: 0},
        name="attn_bwd", compiler_params=_params(1),
    )(sinks, z, z, z, ya, dya, dz)


def _to_group_lanes(v, g, nch):
    return jnp.concatenate([v[n * BLK:(n + 1) * BLK, g * BLK:(g + 1) * BLK] for n in range(nch)], axis=1)


def _from_group_lanes(per_group, nch):
    rows = [jnp.concatenate([per_group[g][:, n * BLK:(n + 1) * BLK] for g in range(SGU_GROUPS)], axis=1)
            for n in range(nch)]
    return jnp.concatenate(rows, axis=0)


def _tril_bf16(w_ref, g):
    t = lax.broadcasted_iota(jnp.int32, (BLK, BLK), 0)
    s = lax.broadcasted_iota(jnp.int32, (BLK, BLK), 1)
    return jnp.where(t >= s, w_ref[g], 0.0).astype(BF16)


def _sgu_norm(v_s, ln_g, ln_b):
    cdf, pdf = _gelu_parts(v_s)
    gv = v_s * cdf
    xc = gv - jnp.mean(gv, axis=-1, keepdims=True)
    rstd = lax.rsqrt(jnp.mean(xc * xc, axis=-1, keepdims=True) + LN_EPS)
    nhat = xc * rstd
    return nhat * ln_g + ln_b, nhat, rstd, cdf + v_s * pdf


def _sgu_gate(vn, w_ref, bb_ref, nch):
    vnb = vn.astype(BF16)
    return _from_group_lanes(
        [_dot(_tril_bf16(w_ref, g), _to_group_lanes(vnb, g, nch)) + jnp.tile(bb_ref[g], (1, nch))
         for g in range(SGU_GROUPS)], nch)


def mix_fwd_out(x, z, ya, post_g, ln_g, ln_b, sgu_w, sgu_bb, wa, ws, wo, copies=None):
    T = x.shape[0]
    tm = min(TM_MIX, T)
    nch = tm // BLK

    def body(x_ref, us_ref, vs_ref, ga_ref, gb_ref, ya_ref, qg_ref, lg_ref, lb_ref, w_ref, bb_ref,
             wa_ref, ws_ref, wo_ref, xo_ref, ysg_ref, pa_ref, pb_ref, o_ref):
        vn, _, _, _ = _sgu_norm(vs_ref[...].astype(F32), lg_ref[...], lb_ref[...])
        gate = _sgu_gate(vn, w_ref, bb_ref, nch)
        us = us_ref[...].astype(F32)
        cdf, _ = _gelu_parts(us)
        ysg = (us * cdf * gate).astype(BF16)
        ysg_ref[...] = ysg
        pa = _dot(ya_ref[...], wa_ref[...])
        pb = _dot(ysg, ws_ref[...])
        pa_ref[...] = pa.astype(BF16)
        pb_ref[...] = pb.astype(BF16)
        merged = _sigmoid(ga_ref[...].astype(F32)) * pa + _sigmoid(gb_ref[...].astype(F32)) * pb
        o = _dot(merged.astype(BF16), wo_ref[...])
        o_ref[...] = o.astype(BF16)
        on, _, _ = _rms(o, qg_ref[...])
        xo_ref[...] = x_ref[...] + on

    zspec = lambda start: _row_spec(tm, D, start // D)
    act = jax.ShapeDtypeStruct((T, D), BF16)
    return _call(
        body, (T // tm,),
        [_row_spec(tm, D), zspec(ZU), zspec(ZV), zspec(ZGA), zspec(ZGB), _row_spec(tm, D),
         _full_spec((1, D)), _full_spec((1, D)), _full_spec((1, D)),
         _full_spec((SGU_GROUPS, BLK, BLK)), _full_spec((SGU_GROUPS, BLK, BLK)),
         _whole_spec(wa), _whole_spec(ws), _whole_spec(wo)],
        (x, z, z, z, z, ya, post_g, ln_g, ln_b, sgu_w, sgu_bb, wa, ws, wo),
        [_row_spec(tm, D)] * 5, [jax.ShapeDtypeStruct((T, D), F32), act, act, act, act],
        "mix_fwd_out", copies=copies)


def mix_bwd_out(dxo, z, o, pa, pb, post_g, ln_g, ln_b, sgu_w, sgu_bb, wa, ws, wo, copies=None):
    T = dxo.shape[0]
    tm = min(TM_MIX_BWD, T)
    nch = tm // BLK

    def body(dxo_ref, us_ref, vs_ref, ga_ref, gb_ref, o_ref, pa_ref, pb_ref, qg_ref, lg_ref, lb_ref, w_ref, bb_ref,
             wa_ref, ws_ref, wo_ref,
             dz_ref, dya_ref, mg_ref, do_ref, dpa_ref, dpb_ref, dqg_ref, dlg_ref, dlb_ref, dw_ref, dbb_ref):
        @pl.when(pl.program_id(0) == 0)
        def _():
            for r in (dqg_ref, dlg_ref, dlb_ref, dw_ref, dbb_ref):
                r[...] = jnp.zeros_like(r)

        qg = qg_ref[...]
        dxo = dxo_ref[...]
        _, no, ro = _rms(o_ref[...].astype(F32), qg)
        dqg_ref[...] += _colsum(dxo * no)
        dob = _rms_bwd(no, ro, qg, dxo).astype(BF16)
        do_ref[...] = dob
        dmerged = _dot_nt(dob, wo_ref[...])
        sa = _sigmoid(ga_ref[...].astype(F32))
        sb = _sigmoid(gb_ref[...].astype(F32))
        pa = pa_ref[...].astype(F32)
        pb = pb_ref[...].astype(F32)
        ta = sa * pa
        tb = sb * pb
        mg_ref[...] = (ta + tb).astype(BF16)
        dpa = (dmerged * sa).astype(BF16)
        dpb = (dmerged * sb).astype(BF16)
        dpa_ref[...] = dpa
        dpb_ref[...] = dpb
        dz_ref[:, DZ_GA:DZ_GA + D] = (dmerged * (ta - ta * sa)).astype(BF16)
        dz_ref[:, DZ_GB:DZ_GB + D] = (dmerged * (tb - tb * sb)).astype(BF16)
        dya_ref[...] = _dot_nt(dpa, wa_ref[...]).astype(BF16)
        dysg = _dot_nt(dpb, ws_ref[...])

        lg = lg_ref[...]
        vn, nhat, rstd, dgelu_v = _sgu_norm(vs_ref[...].astype(F32), lg, lb_ref[...])
        gate = _sgu_gate(vn, w_ref, bb_ref, nch)
        us = us_ref[...].astype(F32)
        cdf, pdf = _gelu_parts(us)
        dz_ref[:, DZ_U:DZ_U + D] = (dysg * gate * (cdf + us * pdf)).astype(BF16)
        dgate = (dysg * (us * cdf)).astype(BF16)
        vnb = vn.astype(BF16)
        t = lax.broadcasted_iota(jnp.int32, (BLK, BLK), 0)
        s = lax.broadcasted_iota(jnp.int32, (BLK, BLK), 1)
        dvn_groups = []
        for g in range(SGU_GROUPS):
            dgl = _to_group_lanes(dgate, g, nch)
            dbb_ref[g] += jnp.broadcast_to(jnp.sum(dgl.astype(F32), axis=-1, keepdims=True), (BLK, BLK))
            dw_ref[g] += jnp.where(t >= s, _dot_nt(dgl, _to_group_lanes(vnb, g, nch)), 0.0)
            dvn_groups.append(_dot_tn(_tril_bf16(w_ref, g), dgl))
        dvn = _from_group_lanes(dvn_groups, nch)
        dlg_ref[...] += _colsum(dvn * nhat)
        dlb_ref[...] += _colsum(dvn)
        dnh = dvn * lg
        dgv = rstd * (dnh - jnp.mean(dnh, axis=-1, keepdims=True) - nhat * jnp.mean(dnh * nhat, axis=-1, keepdims=True))
        dz_ref[:, DZ_V:DZ_V + D] = (dgv * dgelu_v).astype(BF16)

    zspec = lambda start: _row_spec(tm, D, start // D)
    act = jax.ShapeDtypeStruct((T, D), BF16)
    grp = jax.ShapeDtypeStruct((SGU_GROUPS, BLK, BLK), F32)
    vec = jax.ShapeDtypeStruct((1, D), F32)
    return _call(
        body, (T // tm,),
        [_row_spec(tm, D), zspec(ZU), zspec(ZV), zspec(ZGA), zspec(ZGB),
         _row_spec(tm, D), _row_spec(tm, D), _row_spec(tm, D),
         _full_spec((1, D)), _full_spec((1, D)), _full_spec((1, D)),
         _full_spec((SGU_GROUPS, BLK, BLK)), _full_spec((SGU_GROUPS, BLK, BLK)),
         _whole_spec(wa), _whole_spec(ws), _whole_spec(wo)],
        (dxo, z, z, z, z, o, pa, pb, post_g, ln_g, ln_b, sgu_w, sgu_bb, wa, ws, wo),
        [_row_spec(tm, IN_WIDTH)] + [_row_spec(tm, D)] * 5 + [_full_spec((1, D))] * 3
        + [_full_spec((SGU_GROUPS, BLK, BLK))] * 2,
        [jax.ShapeDtypeStruct((T, IN_WIDTH), BF16), act, act, act, act, act, vec, vec, vec, grp, grp],
        "mix_bwd_out", copies=copies)


def mix_in_bwd(dxo, x, dz, pre_g, win):
    T = x.shape[0]
    tm = min(TM_MIX, T)

    def body(dxo_ref, x_ref, dz_ref, pg_ref, w_ref, dx_ref, hb_ref, dpg_ref):
        @pl.when(pl.program_id(0) == 0)
        def _():
            dpg_ref[...] = jnp.zeros_like(dpg_ref)

        pg = pg_ref[...]
        h, nx, rx = _rms(x_ref[...], pg)
        hb_ref[...] = h.astype(BF16)
        dh = jnp.zeros((tm, D), F32)
        for c0, cw in WIN_CHUNKS:
            dh = dh + _dot(dz_ref[:, c0:c0 + cw], w_ref[c0:c0 + cw, :])
        dpg_ref[...] += _colsum(dh * nx)
        dx_ref[...] = dxo_ref[...] + _rms_bwd(nx, rx, pg, dh)

    return pl.pallas_call(
        body, grid=(T // tm,),
        in_specs=[_row_spec(tm, D), _row_spec(tm, D), _row_spec(tm, IN_WIDTH), _full_spec((1, D)), _whole_spec(win)],
        out_specs=[_row_spec(tm, D), _row_spec(tm, D), _full_spec((1, D))],
        out_shape=[jax.ShapeDtypeStruct((T, D), F32), jax.ShapeDtypeStruct((T, D), BF16),
                   jax.ShapeDtypeStruct((1, D), F32)],
        name="mix_in_bwd", compiler_params=_params(1),
    )(dxo, x, dz, pre_g, win)


def wgrad(wide, narrow, name, tmo, copies=None, tk=TK_WGRAD):
    T, N = wide.shape
    M = narrow.shape[1]
    tk = min(tk, T)
    nk = T // tk
    chunk = max(c for c in range(BLK, 1792 + 1, BLK) if N % c == 0)

    def body(a_ref, b_ref, o_ref, acc_ref):
        k = pl.program_id(1)

        @pl.when(k == 0)
        def _():
            acc_ref[...] = jnp.zeros_like(acc_ref)

        acc_ref[...] += _dot_tn(a_ref[...], b_ref[...])

        @pl.when(k == nk - 1)
        def _():
            for c0 in range(0, N, chunk):
                o_ref[c0:c0 + chunk, :] = acc_ref[:, c0:c0 + chunk].T.astype(BF16)

    (out,), got = _call(
        body, (M // tmo, nk),
        [pl.BlockSpec((tk, tmo), lambda m, k: (k, m)), pl.BlockSpec((tk, N), lambda m, k: (k, 0))], (narrow, wide),
        [pl.BlockSpec((N, tmo), lambda m, k: (0, m))], [jax.ShapeDtypeStruct((N, M), BF16)], name,
        scratch=[pltpu.VMEM((tmo, N), F32)], copies=copies)
    return out if copies is None else (out, got)


def adamw_sum(parts, w, m, v, name):
    layers, rows, cols = w.shape
    tr = _row_tile(rows, TR_ADAM)
    nr = rows // tr
    c1 = 1.0 - ADAM_B1 ** ADAM_STEP
    c2 = 1.0 - ADAM_B2 ** ADAM_STEP

    def body(*refs):
        p_refs = refs[:layers]
        w_ref, m_ref, v_ref, g_ref, d_ref, nm_ref, nv_ref = refs[layers:]
        for k in range(layers):
            @pl.when(pl.program_id(0) == k)
            def _(p_ref=p_refs[k]):
                g = p_ref[0].astype(F32)
                for j in range(1, N_DEV):
                    g = g + p_ref[j].astype(F32)
                nm = ADAM_B1 * m_ref[...] + (1.0 - ADAM_B1) * g
                nv = ADAM_B2 * v_ref[...] + (1.0 - ADAM_B2) * (g * g)
                g_ref[...] = g
                nm_ref[...] = nm
                nv_ref[...] = nv
                d_ref[...] = -ADAM_LR * ((nm * (1.0 / c1)) / (jnp.sqrt(nv * (1.0 / c2)) + ADAM_EPS)
                                         + ADAM_WD * w_ref[...])

    def part_spec(k):
        return pl.BlockSpec((N_DEV, tr, cols),
                            lambda l, i: (0, jnp.where(l < k, 0, jnp.where(l == k, i, nr - 1)), 0))

    spec = pl.BlockSpec((None, tr, cols), lambda l, i: (l, i, 0))
    out = jax.ShapeDtypeStruct((layers, rows, cols), F32)
    return pl.pallas_call(
        body, grid=(layers, nr),
        in_specs=[part_spec(k) for k in range(layers)] + [spec, spec, spec],
        out_specs=[spec] * 4, out_shape=[out] * 4, name=name, compiler_params=_params(2),
    )(*parts, w, m, v)


def _pack_small(p):
    layers = []
    for l in range(DEPTH):
        rows = [p[n][l].reshape(1, D) for n in SMALL_VEC]
        rows.append(p["sgu_b"][l].reshape(1, D))
        rows.append(jnp.pad(p["attn_sinks"][l].reshape(1, N_Q_HEADS), ((0, 0), (0, D - N_Q_HEADS))))
        rows.append(jnp.zeros((SMALL_ROWS - len(SMALL_VEC) - 2, D), F32))
        layers.append(jnp.concatenate(rows, axis=0))
    return jnp.concatenate(layers, axis=0)


def _unpack_small(packed):
    a = packed.reshape(DEPTH, SMALL_ROWS, D)
    out = {n: a[:, i, :] for i, n in enumerate(SMALL_VEC)}
    k = len(SMALL_VEC)
    out["sgu_b"] = a[:, k, :].reshape(DEPTH, SGU_GROUPS, BLK)
    out["attn_sinks"] = a[:, k + 1, :N_Q_HEADS]
    return out


WEIGHT_NAMES = ("ffn1_pre_g", "ffn1_w1", "ffn1_w2", "ffn1_post_g", "mix_pre_g", "w_in", "attn_sinks", "sgu_ln_g",
                "sgu_ln_b", "sgu_w", "sgu_b", "w_attn_branch", "w_sgu_branch", "w_out", "mix_post_g", "ffn2_pre_g",
                "ffn2_w1", "ffn2_w2", "ffn2_post_g")
COL_SHARDED = ("ffn1_w1", "ffn2_w1", "w_in")
ROW_SHARDED = ("ffn1_w2", "ffn2_w2", "w_attn_branch", "w_sgu_branch", "w_out")
MATRICES = COL_SHARDED + ROW_SHARDED


def kernel(x, ffn1_pre_g, ffn1_w1, ffn1_w2, ffn1_post_g, mix_pre_g, w_in, attn_sinks, sgu_ln_g, sgu_ln_b, sgu_w, sgu_b, w_attn_branch, w_sgu_branch, w_out, mix_post_g, ffn2_pre_g, ffn2_w1, ffn2_w2, ffn2_post_g, loss_target, m_ffn1_pre_g, m_ffn1_w1, m_ffn1_w2, m_ffn1_post_g, m_mix_pre_g, m_w_in, m_attn_sinks, m_sgu_ln_g, m_sgu_ln_b, m_sgu_w, m_sgu_b, m_w_attn_branch, m_w_sgu_branch, m_w_out, m_mix_post_g, m_ffn2_pre_g, m_ffn2_w1, m_ffn2_w2, m_ffn2_post_g, v_ffn1_pre_g, v_ffn1_w1, v_ffn1_w2, v_ffn1_post_g, v_mix_pre_g, v_w_in, v_attn_sinks, v_sgu_ln_g, v_sgu_ln_b, v_sgu_w, v_sgu_b, v_w_attn_branch, v_w_sgu_branch, v_w_out, v_mix_post_g, v_ffn2_pre_g, v_ffn2_w1, v_ffn2_w2, v_ffn2_post_g):
    w = dict(zip(WEIGHT_NAMES, (ffn1_pre_g, ffn1_w1, ffn1_w2, ffn1_post_g, mix_pre_g, w_in, attn_sinks, sgu_ln_g,
                                sgu_ln_b, sgu_w, sgu_b, w_attn_branch, w_sgu_branch, w_out, mix_post_g, ffn2_pre_g,
                                ffn2_w1, ffn2_w2, ffn2_post_g)))
    mom = dict(zip(WEIGHT_NAMES, (m_ffn1_pre_g, m_ffn1_w1, m_ffn1_w2, m_ffn1_post_g, m_mix_pre_g, m_w_in,
                                  m_attn_sinks, m_sgu_ln_g, m_sgu_ln_b, m_sgu_w, m_sgu_b, m_w_attn_branch,
                                  m_w_sgu_branch, m_w_out, m_mix_post_g, m_ffn2_pre_g, m_ffn2_w1, m_ffn2_w2,
                                  m_ffn2_post_g)))
    var = dict(zip(WEIGHT_NAMES, (v_ffn1_pre_g, v_ffn1_w1, v_ffn1_w2, v_ffn1_post_g, v_mix_pre_g, v_w_in,
                                  v_attn_sinks, v_sgu_ln_g, v_sgu_ln_b, v_sgu_w, v_sgu_b, v_w_attn_branch,
                                  v_w_sgu_branch, v_w_out, v_mix_post_g, v_ffn2_pre_g, v_ffn2_w1, v_ffn2_w2,
                                  v_ffn2_post_g)))
    T = x.shape[1]
    xs = x.reshape(T, D)
    target = loss_target.reshape(T, D)

    for n in COL_SHARDED:
        w[n], mom[n], var[n] = (jnp.swapaxes(t[n], 1, 2) for t in (w, mom, var))

    per_layer = [cast_layer([w[n] for n in MATRICES], l) for l in range(DEPTH)]
    shard = {n: [per_layer[l][i] for l in range(DEPTH)] for i, n in enumerate(MATRICES)}
    weights = [{} for _ in range(DEPTH)]
    ffn1, ffn2, squares = ("ffn1_w1", "ffn1_w2"), ("ffn2_w1", "ffn2_w2"), ("w_attn_branch", "w_sgu_branch", "w_out")
    for n, g in zip(ffn1, all_gather_weights([shard[n][0] for n in ffn1])):
        weights[0][n] = g.reshape(-1, D)

    def fetch(groups):
        jobs = [("all", shard[n][l]) for l, names in groups if l < DEPTH for n in names]
        return PeerCopies(jobs) if jobs else None

    def landed(groups, arrivals):
        slots = [(l, n) for l, names in groups if l < DEPTH for n in names]
        for (l, n), g in zip(slots, arrivals):
            weights[l][n] = g.reshape(-1, D)

    vec = lambda n, l: w[n][l].reshape(1, D)
    sgu_bb = [jnp.broadcast_to(w["sgu_b"][l][:, :, None], (SGU_GROUPS, BLK, BLK)) for l in range(DEPTH)]

    saved = []
    h = xs
    for l in range(DEPTH):
        wl = weights[l]
        ahead = lambda names: [(0, names)] if l == 0 else []
        with_ffn = lambda names: [(l + 1, names)] if l > 0 else []
        with_mixer = lambda names: [(l + 1, names)] if l == 0 else []
        x0 = h
        groups = [(l + 1, ffn1)] + ahead(("w_in",)) + with_ffn(("w_in",))
        (x1, a1, s1, y1), got = ffn_fwd(x0, vec("ffn1_pre_g", l), vec("ffn1_post_g", l), wl["ffn1_w1"], wl["ffn1_w2"],
                                        fetch(groups))
        landed(groups, got)
        groups = with_mixer(("w_in",)) + ahead(squares)
        (z,), got = mix_in_fwd(x1, vec("mix_pre_g", l), wl["w_in"], fetch(groups))
        landed(groups, got)
        groups = ahead(("ffn2_w1",))
        (ya,), got = attn_fwd(z, w["attn_sinks"][l], fetch(groups))
        landed(groups, got)
        groups = with_mixer(squares) + ahead(("ffn2_w2",))
        (x2, ysg, pa, pb, o), got = mix_fwd_out(
            x1, z, ya, vec("mix_post_g", l), vec("sgu_ln_g", l), vec("sgu_ln_b", l), w["sgu_w"][l], sgu_bb[l],
            wl["w_attn_branch"], wl["w_sgu_branch"], wl["w_out"], fetch(groups))
        landed(groups, got)
        groups = [(l + 1, ffn2)] + with_ffn(squares)
        last = (vec("ffn2_pre_g", l), vec("ffn2_post_g", l), wl["ffn2_w1"], wl["ffn2_w2"])
        if l < DEPTH - 1:
            (h, a2, s2, y2), got = ffn_fwd(x2, *last, fetch(groups))
            landed(groups, got)
        else:
            (a2, s2, y2, sq, dx), _ = ffn_fwd(x2, *last, target=target)
        saved.append((x0, a1, s1, y1, x1, z, ya, ysg, pa, pb, o, x2, a2, s2, y2))

    loss = lax.psum(0.5 / D * jnp.sum(sq), ("x", "y", "c"))

    def ffn_wgrads(hb, da, s, dy):
        return [("own", wgrad(da, hb, "wgrad_ffn_w1", D // 2).reshape(N_DEV, W1_SHARD, D)),
                ("own", wgrad(s, dy, "wgrad_ffn_w2", D).reshape(N_DEV, W2_SHARD, D))]

    parts = {n: [None] * DEPTH for n in MATRICES}
    small = {n: [None] * DEPTH for n in WEIGHT_NAMES if n not in MATRICES + ("sgu_w",)}
    sgu_w_parts = [None] * DEPTH
    waiting = None
    for l in reversed(range(DEPTH)):
        wl = weights[l]
        x0, a1, s1, y1, x1, z, ya, ysg, pa, pb, o, x2, a2, s2, y2 = saved[l]
        (dx, da, dy, hb, dpg, dqg), got = ffn_bwd(
            dx, x2, y2, a2, vec("ffn2_pre_g", l), vec("ffn2_post_g", l), wl["ffn2_w1"], wl["ffn2_w2"],
            PeerCopies(waiting) if waiting else None)
        if waiting:
            parts["ffn1_w1"][l + 1], parts["ffn1_w2"][l + 1] = got
        small["ffn2_pre_g"][l], small["ffn2_post_g"][l] = dpg, dqg

        (dzb, dya, mg, dob, dpa, dpb, dqg, dlg, dlb, dsw, dsb), got = mix_bwd_out(
            dx, z, o, pa, pb, vec("mix_post_g", l), vec("sgu_ln_g", l), vec("sgu_ln_b", l), w["sgu_w"][l], sgu_bb[l],
            wl["w_attn_branch"], wl["w_sgu_branch"], wl["w_out"], PeerCopies(ffn_wgrads(hb, da, s2, dy)))
        parts["ffn2_w1"][l], parts["ffn2_w2"][l] = got
        dz, dsink = attn_bwd(z, ya, dya, dzb, w["attn_sinks"][l])
        dx, hb, dpg = mix_in_bwd(dx, x1, dz, vec("mix_pre_g", l), wl["w_in"])
        small["mix_pre_g"][l], small["mix_post_g"][l] = dpg, dqg
        small["sgu_ln_g"][l], small["sgu_ln_b"][l] = dlg, dlb
        small["sgu_b"][l] = dsb[:, :, 0]
        small["attn_sinks"][l] = dsink[0, :N_Q_HEADS]
        mixer = [("own", wgrad(dz, hb, "wgrad_w_in", D // 2).reshape(N_DEV, WIN_SHARD, D))]
        mixer += [("own", wgrad(act, cot, "wgrad_square", D, tk=2 * TK_WGRAD).reshape(N_DEV, SQ_SHARD, D))
                  for act, cot in ((ya, dpa), (ysg, dpb), (mg, dob))]
        mixer.append(("all", dsw.reshape(SGU_GROUPS * BLK, BLK)))
        if l == 0:
            small["ffn1_pre_g"][0] = small["ffn1_post_g"][0] = jnp.zeros((1, D), F32)
            mixer.append(("all", _pack_small({n: jnp.stack(v, axis=0) for n, v in small.items()})))

        (dx, da, dy, hb, dpg, dqg), got = ffn_bwd(
            dx, x0, y1, a1, vec("ffn1_pre_g", l), vec("ffn1_post_g", l), wl["ffn1_w1"], wl["ffn1_w2"],
            PeerCopies(mixer))
        parts["w_in"][l], parts["w_attn_branch"][l], parts["w_sgu_branch"][l], parts["w_out"][l] = got[:4]
        sgu_w_parts[l] = got[4]
        if l > 0:
            small["ffn1_pre_g"][l], small["ffn1_post_g"][l] = dpg, dqg
            waiting = ffn_wgrads(hb, da, s1, dy)
    grad_x = dx.reshape(x.shape)
    small_parts = got[5]

    def late_rows(pre, post):
        return jnp.concatenate([pre.reshape(1, D), post.reshape(1, D), jnp.zeros((LATE_ROWS - 2, D), F32)], axis=0)

    g_w1 = wgrad(da, hb, "wgrad_ffn_w1", D // 2).reshape(N_DEV, W1_SHARD, D)
    g_w2, got = wgrad(s1, dy, "wgrad_ffn_w2", D, PeerCopies([("own", g_w1)]))
    parts["ffn1_w1"][0] = got[0]
    parts["ffn1_w2"][0], late_parts = exchange_last([("own", g_w2.reshape(N_DEV, W2_SHARD, D)),
                                                     ("all", late_rows(dpg, dqg))])

    grads, deltas, new_m, new_v = {}, {}, {}, {}
    for n in MATRICES:
        grads[n], deltas[n], new_m[n], new_v[n] = adamw_sum(parts[n], w[n], mom[n], var[n], "adamw_" + n)
    for n in COL_SHARDED:
        for out in (grads, deltas, new_m, new_v):
            out[n] = jnp.swapaxes(out[n], 1, 2)
    flat = lambda t: t["sgu_w"].reshape(DEPTH, SGU_GROUPS * BLK, BLK)
    for out, r in zip((grads, deltas, new_m, new_v),
                      adamw_sum(sgu_w_parts, flat(w), flat(mom), flat(var), "adamw_sgu_w")):
        out["sgu_w"] = r.reshape(w["sgu_w"].shape)
    res = adamw_sum([small_parts], _pack_small(w)[None], _pack_small(mom)[None], _pack_small(var)[None],
                    "adamw_small")
    late = adamw_sum([late_parts], *[late_rows(t["ffn1_pre_g"][0], t["ffn1_post_g"][0])[None] for t in (w, mom, var)],
                     "adamw_late")
    for out, packed, late_out in zip((grads, deltas, new_m, new_v), res, late):
        out.update(_unpack_small(packed[0]))
        for row, n in enumerate(("ffn1_pre_g", "ffn1_post_g")):
            out[n] = jnp.concatenate([late_out[0, row:row + 1], out[n][1:]], axis=0)

    return (loss, grad_x, *[grads[n] for n in WEIGHT_NAMES], *[deltas[n] for n in WEIGHT_NAMES],
            *[new_m[n] for n in WEIGHT_NAMES], *[new_v[n] for n in WEIGHT_NAMES])
```
